```python
import math
import jax, jax.numpy as jnp
from jax import lax
import numpy as np

D_MODEL = 1024
BATCH = 4
SEQ = 8192
DEPTH = 1

CONV_DIM = 1024
CONV_K = 31
HG_HEADS = 8
HG_DK = 128
HG_DV = 128
HG_CHUNK = 64
HG_KDIM = HG_HEADS * HG_DK
HG_VDIM = HG_HEADS * HG_DV
OFF_CA = 0
OFF_CB = OFF_CA + CONV_DIM
OFF_Q = OFF_CB + CONV_DIM
OFF_F = OFF_Q + HG_KDIM
OFF_I = OFF_F + HG_KDIM
OFF_G = OFF_I + HG_VDIM
OFF_GC = OFF_G + HG_VDIM
OFF_GH = OFF_GC + D_MODEL
IN_COLS = OFF_GH + D_MODEL
N_EXPERTS = 32
TOP_K = 4
D_FF = 1024
SWIGLU_ALPHA = 1.702
SWIGLU_LIMIT = 7.0
MOE_BLOCK = 256
EPS = 1e-6
N_MOD = 6

kernel_name = "hybrid_conv_hgrn2_moe_adaln_block"


def rmsnorm(x, g):
    xf = x.astype(jnp.float32)
    y = xf * lax.rsqrt(jnp.mean(xf * xf, axis=-1, keepdims=True) + EPS) * g.astype(jnp.float32)
    return y.astype(x.dtype)


def layernorm(x, g, b):
    xf = x.astype(jnp.float32)
    mu = jnp.mean(xf, axis=-1, keepdims=True)
    var = jnp.mean(jnp.square(xf - mu), axis=-1, keepdims=True)
    y = (xf - mu) * lax.rsqrt(var + EPS) * g.astype(jnp.float32) + b.astype(jnp.float32)
    return y.astype(x.dtype)


def conformer_conv(a, b, dw, dw_bias, ln_g, ln_b, w_proj):
    u = a * jax.nn.sigmoid(b)
    kern = dw.astype(u.dtype).reshape(CONV_K, 1, CONV_DIM)
    u = lax.conv_general_dilated(u, kern, window_strides=(1,), padding=[(CONV_K - 1, 0)],
                                 dimension_numbers=("NWC", "WIO", "NWC"),
                                 feature_group_count=CONV_DIM)
    u = u + dw_bias.astype(u.dtype)
    u = jax.nn.silu(layernorm(u, ln_g, ln_b))
    return u @ w_proj


def hgrn2_step(state, inp):
    q, k, v, logf = inp
    G = jnp.cumsum(logf, axis=2)
    C = q.shape[2]
    causal = jnp.tril(jnp.ones((C, C), dtype=bool))
    diff = G[:, :, :, None, :] - G[:, :, None, :, :]
    decay = jnp.exp(jnp.where(causal[None, None, :, :, None], diff, -jnp.inf))
    A = jnp.einsum("bhtk,bhsk,bhtsk->bhts", q, k, decay)
    o = jnp.einsum("bhts,bhsv->bhtv", A, v) + jnp.einsum("bhtk,bhkv->bhtv", q * jnp.exp(G), state)
    G_last = G[:, :, -1, :]
    k_dec = k * jnp.exp(G_last[:, :, None, :] - G)
    state = jnp.exp(G_last)[..., None] * state + jnp.einsum("bhsk,bhsv->bhkv", k_dec, v)
    return state, o


def hgrn2(q, fz, iv, og, lb, norm_g, w_proj):
    B, S, _ = q.shape
    n_chunks = S // HG_CHUNK
    qf = q.astype(jnp.float32).reshape(B, S, HG_HEADS, HG_DK) * (HG_DK ** -0.5)
    z = fz.astype(jnp.float32).reshape(B, S, HG_HEADS, HG_DK)
    lbh = lb.reshape(HG_HEADS, HG_DK)
    logf = jnp.logaddexp(jnp.log(lbh), jnp.log1p(-lbh) + jax.nn.log_sigmoid(z))
    k = (1.0 - lbh) * jax.nn.sigmoid(-z)
    v = iv.astype(jnp.float32).reshape(B, S, HG_HEADS, HG_DV)

    def to_chunks(t):
        return t.reshape(B, n_chunks, HG_CHUNK, HG_HEADS, t.shape[-1]).transpose(1, 0, 3, 2, 4)

    s0 = jnp.zeros((B, HG_HEADS, HG_DK, HG_DV), jnp.float32)
    _, o = lax.scan(hgrn2_step, s0, (to_chunks(qf), to_chunks(k), to_chunks(v), to_chunks(logf)))
    o = o.transpose(1, 0, 3, 2, 4).reshape(B, S, HG_HEADS, HG_DV)
    o = o * lax.rsqrt(jnp.mean(o * o, axis=-1, keepdims=True) + EPS) * norm_g.astype(jnp.float32)
    o = o * jax.nn.silu(og.astype(jnp.float32).reshape(B, S, HG_HEADS, HG_DV))
    return o.reshape(B, S, HG_VDIM).astype(q.dtype) @ w_proj


def clamped_swiglu(u):
    gl, lin = u[:, :D_FF], u[:, D_FF:]
    gl = jnp.minimum(gl, SWIGLU_LIMIT)
    lin = jnp.clip(lin, -SWIGLU_LIMIT, SWIGLU_LIMIT)
    return gl * jax.nn.sigmoid(SWIGLU_ALPHA * gl) * (lin + 1.0)


def moe(h, w_router, b_router, w1, b1, w2, b2):
    B, S, D = h.shape
    T = B * S
    TK = T * TOP_K
    hf = h.reshape(T, D)
    logits = hf.astype(jnp.float32) @ w_router.astype(jnp.float32) + b_router.astype(jnp.float32)
    top_val, top_idx = lax.top_k(logits, TOP_K)
    top_w = jax.nn.softmax(top_val, axis=-1)
    flat_e = top_idx.reshape(TK).astype(jnp.int32)
    flat_tok = jnp.arange(TK, dtype=jnp.int32) // TOP_K
    flat_w = top_w.reshape(TK)
    order = jnp.argsort(flat_e)
    se, stok, sw = flat_e[order], flat_tok[order], flat_w[order]
    counts = jnp.bincount(flat_e, length=N_EXPERTS).astype(jnp.int32)
    padded = (counts + MOE_BLOCK - 1) // MOE_BLOCK * MOE_BLOCK
    start = jnp.cumsum(counts) - counts
    pend = jnp.cumsum(padded)
    pstart = pend - padded
    dest = pstart[se] + jnp.arange(TK, dtype=jnp.int32) - start[se]
    n_blocks = -(-(TK + N_EXPERTS * (MOE_BLOCK - 1)) // MOE_BLOCK)
    R = n_blocks * MOE_BLOCK
    row_tok = jnp.zeros((R,), jnp.int32).at[dest].set(stok)
    row_w = jnp.zeros((R,), jnp.float32).at[dest].set(sw)
    block_e = jnp.minimum(
        jnp.searchsorted(pend, jnp.arange(n_blocks, dtype=jnp.int32) * MOE_BLOCK, side="right"),
        N_EXPERTS - 1).astype(jnp.int32)

    def block_fn(args):
        tok, e = args
        xb = hf[tok]
        u = xb @ w1[e] + b1[e]
        return clamped_swiglu(u) @ w2[e] + b2[e]

    ys = lax.map(block_fn, (row_tok.reshape(n_blocks, MOE_BLOCK), block_e))
    out = jnp.zeros((T, D), jnp.float32).at[row_tok].add(
        ys.reshape(R, D).astype(jnp.float32) * row_w[:, None])
    return out.reshape(B, S, D).astype(h.dtype)


def setup_inputs(seed: int = 0) -> dict:
    key = jax.random.key(seed)
    ks = jax.random.split(key, 24)
    n = jax.random.normal
    f32 = jnp.float32
    D, L = D_MODEL, DEPTH
    return {
        "x": n(ks[0], (BATCH, SEQ, D), f32),
        "c": n(ks[1], (BATCH, D), f32),
        "w_ada": n(ks[2], (L, D, N_MOD * D), f32) * (0.5 * D ** -0.5),
        "b_ada": n(ks[3], (L, N_MOD * D), f32) * 0.01,
        "g_mix": 1.0 + 0.02 * n(ks[4], (L, D), f32),
        "w_in": n(ks[5], (L, D, IN_COLS), f32) * D ** -0.5,
        "conv_dw": n(ks[6], (L, CONV_K, CONV_DIM), f32) * CONV_K ** -0.5,
        "conv_dw_bias": n(ks[7], (L, CONV_DIM), f32) * 0.01,
        "conv_ln_g": 1.0 + 0.02 * n(ks[8], (L, CONV_DIM), f32),
        "conv_ln_b": n(ks[9], (L, CONV_DIM), f32) * 0.01,
        "w_conv_out": n(ks[10], (L, CONV_DIM, D), f32) * CONV_DIM ** -0.5,
        "lb_param": n(ks[11], (L + 1, HG_KDIM), f32) * 0.5,
        "hgrn_norm_g": 1.0 + 0.02 * n(ks[12], (L, HG_DV), f32),
        "w_hgrn_out": n(ks[13], (L, HG_VDIM, D), f32) * HG_VDIM ** -0.5,
        "w_out": n(ks[14], (L, D, D), f32) * D ** -0.5,
        "g_ffn": 1.0 + 0.02 * n(ks[15], (L, D), f32),
        "w_router": n(ks[16], (L, D, N_EXPERTS), f32) * D ** -0.5,
        "b_router": n(ks[17], (L, N_EXPERTS), f32) * 0.01,
        "w1": n(ks[18], (L, N_EXPERTS, D, 2 * D_FF), f32) * D ** -0.5,
        "b1": n(ks[19], (L, N_EXPERTS, 2 * D_FF), f32) * 0.01,
        "w2": n(ks[20], (L, N_EXPERTS, D_FF, D), f32) * D_FF ** -0.5,
        "b2": n(ks[21], (L, N_EXPERTS, D), f32) * 0.01,
        "g_final": 1.0 + 0.02 * n(ks[22], (D,), f32),
    }


def reference(x, c, w_ada, b_ada, g_mix, w_in, conv_dw, conv_dw_bias, conv_ln_g, conv_ln_b,
              w_conv_out, lb_param, hgrn_norm_g, w_hgrn_out, w_out, g_ffn, w_router, b_router,
              w1, b1, w2, b2, g_final):
    dt = x.dtype
    lb_all = jnp.cumsum(jax.nn.softmax(lb_param.astype(jnp.float32), axis=0), axis=0)
    c_act = jax.nn.silu(c.astype(jnp.float32))
    for l in range(DEPTH):
        mod = c_act @ w_ada[l].astype(jnp.float32) + b_ada[l].astype(jnp.float32)
        sh1, sc1, ga1, sh2, sc2, ga2 = [m[:, None, :].astype(dt) for m in jnp.split(mod, N_MOD, axis=-1)]

        h = rmsnorm(x, g_mix[l]) * (1.0 + sc1) + sh1
        p = h @ w_in[l]
        y_conv = conformer_conv(p[..., OFF_CA:OFF_CB], p[..., OFF_CB:OFF_Q], conv_dw[l],
                                conv_dw_bias[l], conv_ln_g[l], conv_ln_b[l], w_conv_out[l])
        y_hg = hgrn2(p[..., OFF_Q:OFF_F], p[..., OFF_F:OFF_I], p[..., OFF_I:OFF_G],
                     p[..., OFF_G:OFF_GC], lb_all[l], hgrn_norm_g[l], w_hgrn_out[l])
        merged = (jax.nn.sigmoid(p[..., OFF_GC:OFF_GH]) * y_conv
                  + jax.nn.sigmoid(p[..., OFF_GH:IN_COLS]) * y_hg)
        x = x + ga1 * (merged @ w_out[l])

        h2 = rmsnorm(x, g_ffn[l]) * (1.0 + sc2) + sh2
        x = x + ga2 * moe(h2, w_router[l], b_router[l], w1[l], b1[l], w2[l], b2[l])
    return rmsnorm(x, g_final)
```

```python
import functools

import jax
import jax.numpy as jnp
import numpy as np
from jax import lax
from jax.experimental import pallas as pl
from jax.experimental.pallas import tpu as pltpu

F32 = jnp.float32
BF16 = jnp.bfloat16

D_MODEL = 1024
CONV_K = 31
HG_HEADS = 8
HG_DK = 128
N_EXPERTS = 32
TOP_K = 4
D_FF = 1024
SWIGLU_ALPHA = 1.702
SWIGLU_LIMIT = 7.0
MOE_BLOCK = 256
EPS = 1e-6
N_MOD = 6
COL_CA, COL_CB, COL_Q, COL_F, COL_I, COL_G, COL_GC, COL_GH = range(8)
N_COLS = 8

HG_CHUNK = 128
CONV_HALO = 32
VMEM_LIMIT = 56 * 1024 * 1024


def _sigmoid(x):
    return 1.0 / (1.0 + jnp.exp(-x))


def _params(sem, vmem=VMEM_LIMIT):
    return pltpu.CompilerParams(dimension_semantics=sem, vmem_limit_bytes=vmem)


def _ada_kernel(c_ref, w_ref, b_ref, lbp_ref, mod_ref, lb_ref):
    c = c_ref[...]
    c_act = c * _sigmoid(c)
    mod_ref[...] = jnp.dot(c_act, w_ref[...], preferred_element_type=F32,
                           precision=lax.Precision.HIGHEST) + b_ref[...]
    p = lbp_ref[...]
    e = jnp.exp(p - jnp.max(p, axis=0, keepdims=True))
    lb_ref[...] = e[0:1, :] / jnp.sum(e, axis=0, keepdims=True)


def _ada(c_pad, w_ada, b_ada, lb_param):
    nb, d = c_pad.shape
    n = w_ada.shape[1]
    tn = 1536
    return pl.pallas_call(
        _ada_kernel,
        grid=(n // tn,),
        in_specs=[
            pl.BlockSpec((nb, d), lambda j: (0, 0)),
            pl.BlockSpec((d, tn), lambda j: (0, j)),
            pl.BlockSpec((1, tn), lambda j: (0, j)),
            pl.BlockSpec(lb_param.shape, lambda j: (0, 0)),
        ],
        out_specs=[
            pl.BlockSpec((nb, tn), lambda j: (0, j)),
            pl.BlockSpec((1, d), lambda j: (0, 0)),
        ],
        out_shape=[
            jax.ShapeDtypeStruct((nb, n), F32),
            jax.ShapeDtypeStruct((1, d), F32),
        ],
        compiler_params=_params(("arbitrary",)),
        name="ada_mod",
    )(c_pad, w_ada, b_ada, lb_param)


def _inproj_kernel(x_ref, g_ref, sc_ref, sh_ref, w_ref, o_ref):
    x = x_ref[0]
    ms = jnp.mean(x * x, axis=-1, keepdims=True)
    h = x * lax.rsqrt(ms + EPS) * g_ref[...]
    h = h * (1.0 + sc_ref[0]) + sh_ref[0]
    o_ref[0] = jnp.dot(h.astype(BF16), w_ref[...], preferred_element_type=F32)


def _inproj(x, g_mix, mod3, w_in_bf):
    b, s, d = x.shape
    n = w_in_bf.shape[1]
    tm, tn = 512, 2048
    return pl.pallas_call(
        _inproj_kernel,
        grid=(n // tn, b, s // tm),
        in_specs=[
            pl.BlockSpec((1, tm, d), lambda j, bi, i: (bi, i, 0)),
            pl.BlockSpec((1, d), lambda j, bi, i: (0, 0)),
            pl.BlockSpec((1, 1, d), lambda j, bi, i: (bi * N_MOD + 1, 0, 0)),
            pl.BlockSpec((1, 1, d), lambda j, bi, i: (bi * N_MOD + 0, 0, 0)),
            pl.BlockSpec((d, tn), lambda j, bi, i: (0, j)),
        ],
        out_specs=pl.BlockSpec((1, tm, tn), lambda j, bi, i: (bi, i, j)),
        out_shape=jax.ShapeDtypeStruct((b, s, n), F32),
        compiler_params=_params(("arbitrary", "arbitrary", "arbitrary")),
        name="in_proj",
    )(x, g_mix, mod3, mod3, w_in_bf)


CONV_TS = 256
CONV_RG = 32


def _conv_kernel(a_ref, ah_ref, b_ref, bh_ref, gc_ref, dw_ref, bias_ref, lng_ref, lnb_ref,
                 w_ref, o_ref, buf, cv):
    i = pl.program_id(1)
    ts = CONV_TS
    uh = ah_ref[0] * _sigmoid(bh_ref[0])
    buf[0:CONV_HALO, :] = jnp.where(i > 0, uh, 0.0)
    buf[CONV_HALO:CONV_HALO + ts, :] = a_ref[0] * _sigmoid(b_ref[0])
    off = CONV_HALO - (CONV_K - 1)
    for r in range(ts // CONV_RG):
        base = r * CONV_RG
        acc = jnp.zeros((CONV_RG, D_MODEL), F32)
        for j in range(CONV_K):
            acc = acc + dw_ref[j:j + 1, :] * buf[base + off + j:base + off + j + CONV_RG, :]
        cv[base:base + CONV_RG, :] = acc + bias_ref[...]
    u = cv[...]
    mu = jnp.mean(u, axis=-1, keepdims=True)
    uc = u - mu
    var = jnp.mean(uc * uc, axis=-1, keepdims=True)
    y = uc * lax.rsqrt(var + EPS) * lng_ref[...] + lnb_ref[...]
    y = y * _sigmoid(y)
    yc = jnp.dot(y.astype(BF16), w_ref[...], preferred_element_type=F32)
    o_ref[0] = _sigmoid(gc_ref[0]) * yc


def _conv_branch(p, dw_pad, bias, ln_g, ln_b, w_bf):
    b, s, _ = p.shape
    d = D_MODEL
    ts = CONV_TS
    hb = ts // CONV_HALO

    def halo_map(col):
        return lambda bi, i: (bi, jnp.maximum(i * hb - 1, 0), col)

    vec = pl.BlockSpec((1, d), lambda bi, i: (0, 0))
    return pl.pallas_call(
        _conv_kernel,
        grid=(b, s // ts),
        in_specs=[
            pl.BlockSpec((1, ts, d), lambda bi, i: (bi, i, COL_CA)),
            pl.BlockSpec((1, CONV_HALO, d), halo_map(COL_CA)),
            pl.BlockSpec((1, ts, d), lambda bi, i: (bi, i, COL_CB)),
            pl.BlockSpec((1, CONV_HALO, d), halo_map(COL_CB)),
            pl.BlockSpec((1, ts, d), lambda bi, i: (bi, i, COL_GC)),
            pl.BlockSpec(dw_pad.shape, lambda bi, i: (0, 0)),
            vec, vec, vec,
            pl.BlockSpec((d, d), lambda bi, i: (0, 0)),
        ],
        out_specs=pl.BlockSpec((1, ts, d), lambda bi, i: (bi, i, 0)),
        out_shape=jax.ShapeDtypeStruct((b, s, d), F32),
        scratch_shapes=[pltpu.VMEM((ts + CONV_HALO, d), F32), pltpu.VMEM((ts, d), F32)],
        compiler_params=_params(("arbitrary", "arbitrary")),
        name="conv_branch",
    )(p, p, p, p, p, dw_pad, bias, ln_g, ln_b, w_bf)


def _hgrn_levels():
    c = HG_CHUNK
    levels = []
    m = c // 2
    while m >= 1:
        levels.append(m)
        m //= 2
    return levels


def _hgrn_sum_matrix():
    c = HG_CHUNK
    r = np.arange(c)[:, None]
    u = np.arange(c)[None, :]
    mats = [(u <= r), (u > r)]
    for m in _hgrn_levels():
        start = (r // (2 * m)) * (2 * m)
        mid = start + m
        upper = (r - start) >= m
        mq = upper & (u >= mid) & (u <= r)
        mk = (~upper) & (u > r) & (u < mid)
        mats.append(mq | mk)
    return np.concatenate(mats, axis=0).astype(np.float32)


def _hgrn_kernel(q_ref, z_ref, v_ref, og_ref, gh_ref, lb_ref, ng_ref, msum_ref, w_ref, o_ref,
                 st, obuf):
    c = HG_CHUNK
    dk = HG_DK
    levels = _hgrn_levels()

    @pl.when(pl.program_id(1) == 0)
    def _():
        st[...] = jnp.zeros_like(st)

    z = z_ref[0]
    lb = lb_ref[...]
    sig = _sigmoid(z)
    f = lb + (1.0 - lb) * sig
    logf = jnp.log(f)
    kk = (1.0 - lb) * (1.0 - sig)
    q = q_ref[0] * (dk ** -0.5)
    v = v_ref[0]
    og = og_ref[0]

    hi = logf.astype(BF16)
    lo = (logf - hi.astype(F32)).astype(BF16)
    msum = msum_ref[...]
    seg = (jnp.dot(msum, hi, preferred_element_type=F32)
           + jnp.dot(msum, lo, preferred_element_type=F32))

    g_inc = seg[0:c]
    g_sfx = seg[c:2 * c]
    q_st = (q * jnp.exp(g_inc)).astype(BF16)
    k_st = (kk * jnp.exp(g_sfx)).astype(BF16)
    dec_all = jnp.exp(g_inc[c - 1:c, :])
    v_bf = v.astype(BF16)
    q_bf = q.astype(BF16)
    k_bf = kk.astype(BF16)

    row = lax.broadcasted_iota(jnp.int32, (c, c), 0)
    col = lax.broadcasted_iota(jnp.int32, (c, c), 1)
    rr = lax.broadcasted_iota(jnp.int32, (c, 1), 0)

    qs, ks, masks = [], [], []
    for li, m in enumerate(levels):
        e = jnp.exp(seg[(2 + li) * c:(3 + li) * c])
        upper = (rr & m) != 0
        qs.append(jnp.where(upper, q * e, 0.0).astype(BF16))
        ks.append(jnp.where(upper, 0.0, kk * e).astype(BF16))
        sh = int(np.log2(2 * m))
        masks.append((row >> sh) == (col >> sh))
    diag = row == col

    nt = (((1,), (1,)), ((), ()))
    tn = (((0,), (0,)), ((), ()))
    for h in range(HG_HEADS):
        sl = slice(h * dk, (h + 1) * dk)
        a = jnp.where(diag, lax.dot_general(q_bf[:, sl], k_bf[:, sl], nt,
                                            preferred_element_type=F32), 0.0)
        for li in range(len(levels)):
            a = a + jnp.where(masks[li],
                              lax.dot_general(qs[li][:, sl], ks[li][:, sl], nt,
                                              preferred_element_type=F32), 0.0)
        s_t = st[h]
        o = jnp.dot(a.astype(BF16), v_bf[:, sl], preferred_element_type=F32)
        o = o + lax.dot_general(q_st[:, sl], s_t.astype(BF16), nt, preferred_element_type=F32)
        st[h] = s_t * dec_all[:, sl] + lax.dot_general(v_bf[:, sl], k_st[:, sl], tn,
                                                       preferred_element_type=F32)
        ms = jnp.mean(o * o, axis=-1, keepdims=True)
        o = o * lax.rsqrt(ms + EPS) * ng_ref[...]
        g = og[:, sl]
        obuf[:, sl] = (o * (g * _sigmoid(g))).astype(BF16)

    y = jnp.dot(obuf[...], w_ref[...], preferred_element_type=F32)
    o_ref[0] = _sigmoid(gh_ref[0]) * y


def _hgrn_branch(p, lb, norm_g, w_bf):
    b, s, _ = p.shape
    d = D_MODEL
    c = HG_CHUNK
    msum = jnp.asarray(_hgrn_sum_matrix(), dtype=BF16)

    def col_spec(col):
        return pl.BlockSpec((1, c, d), lambda bi, i: (bi, i, col))

    return pl.pallas_call(
        _hgrn_kernel,
        grid=(b, s // c),
        in_specs=[
            col_spec(COL_Q), col_spec(COL_F), col_spec(COL_I), col_spec(COL_G), col_spec(COL_GH),
            pl.BlockSpec((1, d), lambda bi, i: (0, 0)),
            pl.BlockSpec((1, HG_DK), lambda bi, i: (0, 0)),
            pl.BlockSpec(msum.shape, lambda bi, i: (0, 0)),
            pl.BlockSpec((d, d), lambda bi, i: (0, 0)),
        ],
        out_specs=pl.BlockSpec((1, c, d), lambda bi, i: (bi, i, 0)),
        out_shape=jax.ShapeDtypeStruct((b, s, d), F32),
        scratch_shapes=[pltpu.VMEM((HG_HEADS, HG_DK, HG_DK), F32), pltpu.VMEM((c, d), BF16)],
        compiler_params=_params(("arbitrary", "arbitrary")),
        name="hgrn_branch",
    )(p, p, p, p, p, lb, norm_g, msum, w_bf)


def _merge_kernel(yc_ref, yh_ref, x_ref, ga_ref, sc_ref, sh_ref, g_ref, w_ref, wr_ref, br_ref,
                  x1_ref, h2_ref, lg_ref):
    merged = (yc_ref[0] + yh_ref[0]).astype(BF16)
    x1 = x_ref[0] + ga_ref[0] * jnp.dot(merged, w_ref[...], preferred_element_type=F32)
    x1_ref[0] = x1
    ms = jnp.mean(x1 * x1, axis=-1, keepdims=True)
    h2 = x1 * lax.rsqrt(ms + EPS) * g_ref[...]
    h2 = h2 * (1.0 + sc_ref[0]) + sh_ref[0]
    h2_ref[0] = h2
    lg_ref[0] = jnp.dot(h2, wr_ref[...], preferred_element_type=F32,
                        precision=lax.Precision.HIGHEST) + br_ref[...]


def _merge(yc, yh, x, mod3, g_ffn, w_out_bf, w_router, b_router):
    b, s, d = x.shape
    tm = 512
    ne = w_router.shape[1]

    def mod_spec(k):
        return pl.BlockSpec((1, 1, d), lambda bi, i: (bi * N_MOD + k, 0, 0))

    tile = pl.BlockSpec((1, tm, d), lambda bi, i: (bi, i, 0))
    return pl.pallas_call(
        _merge_kernel,
        grid=(b, s // tm),
        in_specs=[
            tile, tile, tile, mod_spec(2), mod_spec(4), mod_spec(3),
            pl.BlockSpec((1, d), lambda bi, i: (0, 0)),
            pl.BlockSpec((d, d), lambda bi, i: (0, 0)),
            pl.BlockSpec((d, ne), lambda bi, i: (0, 0)),
            pl.BlockSpec((1, ne), lambda bi, i: (0, 0)),
        ],
        out_specs=[tile, tile, pl.BlockSpec((1, tm, ne), lambda bi, i: (bi, i, 0))],
        out_shape=[
            jax.ShapeDtypeStruct((b, s, d), F32),
            jax.ShapeDtypeStruct((b, s, d), F32),
            jax.ShapeDtypeStruct((b, s, ne), F32),
        ],
        compiler_params=_params(("arbitrary", "arbitrary")),
        name="merge_router",
    )(yc, yh, x, mod3, mod3, mod3, g_ffn, w_out_bf, w_router, b_router)


def _ffn_kernel(be_ref, x_ref, rw_ref, w1_ref, b1_ref, w2_ref, b2_ref, o_ref, w1s, w2s):
    i = pl.program_id(0)
    prev = be_ref[jnp.maximum(i - 1, 0)]

    @pl.when((i == 0) | (be_ref[i] != prev))
    def _():
        w1s[...] = w1_ref[0].astype(BF16)
        w2s[...] = w2_ref[0].astype(BF16)

    u = jnp.dot(x_ref[...].astype(BF16), w1s[...], preferred_element_type=F32) + b1_ref[0]
    gl = jnp.minimum(u[:, :D_FF], SWIGLU_LIMIT)
    lin = jnp.clip(u[:, D_FF:], -SWIGLU_LIMIT, SWIGLU_LIMIT)
    act = gl * _sigmoid(SWIGLU_ALPHA * gl) * (lin + 1.0)
    y = jnp.dot(act.astype(BF16), w2s[...], preferred_element_type=F32) + b2_ref[0]
    o_ref[...] = y * rw_ref[...]


def _ffn(block_e, xg, row_w, w1, b1, w2, b2):
    r, d = xg.shape
    nb = r // MOE_BLOCK
    grid_spec = pltpu.PrefetchScalarGridSpec(
        num_scalar_prefetch=1,
        grid=(nb,),
        in_specs=[
            pl.BlockSpec((MOE_BLOCK, d), lambda i, be: (i, 0)),
            pl.BlockSpec((MOE_BLOCK, 1), lambda i, be: (i, 0)),
            pl.BlockSpec((1, d, 2 * D_FF), lambda i, be: (be[i], 0, 0)),
            pl.BlockSpec((1, 1, 2 * D_FF), lambda i, be: (be[i], 0, 0)),
            pl.BlockSpec((1, D_FF, d), lambda i, be: (be[i], 0, 0)),
            pl.BlockSpec((1, 1, d), lambda i, be: (be[i], 0, 0)),
        ],
        out_specs=pl.BlockSpec((MOE_BLOCK, d), lambda i, be: (i, 0)),
        scratch_shapes=[pltpu.VMEM((d, 2 * D_FF), BF16), pltpu.VMEM((D_FF, d), BF16)],
    )
    return pl.pallas_call(
        _ffn_kernel,
        grid_spec=grid_spec,
        out_shape=jax.ShapeDtypeStruct((r, d), F32),
        compiler_params=_params(("arbitrary",)),
        name="expert_ffn",
    )(block_e, xg, row_w, w1, b1, w2, b2)


def _final_kernel(x1_ref, moe_ref, ga_ref, g_ref, o_ref):
    x2 = x1_ref[0] + ga_ref[0] * moe_ref[0]
    ms = jnp.mean(x2 * x2, axis=-1, keepdims=True)
    o_ref[0] = x2 * lax.rsqrt(ms + EPS) * g_ref[...]


def _final(x1, moe_out, mod3, g_final):
    b, s, d = x1.shape
    tm = 512
    tile = pl.BlockSpec((1, tm, d), lambda bi, i: (bi, i, 0))
    return pl.pallas_call(
        _final_kernel,
        grid=(b, s // tm),
        in_specs=[tile, tile,
                  pl.BlockSpec((1, 1, d), lambda bi, i: (bi * N_MOD + 5, 0, 0)),
                  pl.BlockSpec((1, d), lambda bi, i: (0, 0))],
        out_specs=tile,
        out_shape=jax.ShapeDtypeStruct((b, s, d), F32),
        compiler_params=_params(("arbitrary", "arbitrary")),
        name="final_norm",
    )(x1, moe_out, mod3, g_final)


def _route(logits):
    t = logits.shape[0]
    tk = t * TOP_K
    top_val, top_idx = lax.top_k(logits, TOP_K)
    top_w = jax.nn.softmax(top_val, axis=-1)
    flat_e = top_idx.reshape(tk).astype(jnp.int32)
    flat_tok = jnp.arange(tk, dtype=jnp.int32) // TOP_K
    flat_w = top_w.reshape(tk)
    order = jnp.argsort(flat_e)
    se, stok, sw = flat_e[order], flat_tok[order], flat_w[order]
    counts = jnp.bincount(flat_e, length=N_EXPERTS).astype(jnp.int32)
    padded = (counts + MOE_BLOCK - 1) // MOE_BLOCK * MOE_BLOCK
    start = jnp.cumsum(counts) - counts
    pend = jnp.cumsum(padded)
    pstart = pend - padded
    dest = pstart[se] + jnp.arange(tk, dtype=jnp.int32) - start[se]
    n_blocks = -(-(tk + N_EXPERTS * (MOE_BLOCK - 1)) // MOE_BLOCK)
    r = n_blocks * MOE_BLOCK
    row_tok = jnp.zeros((r,), jnp.int32).at[dest].set(stok)
    row_w = jnp.zeros((r,), F32).at[dest].set(sw)
    block_e = jnp.minimum(
        jnp.searchsorted(pend, jnp.arange(n_blocks, dtype=jnp.int32) * MOE_BLOCK, side="right"),
        N_EXPERTS - 1).astype(jnp.int32)
    return row_tok, row_w, block_e


def kernel(x, c, w_ada, b_ada, g_mix, w_in, conv_dw, conv_dw_bias, conv_ln_g, conv_ln_b,
           w_conv_out, lb_param, hgrn_norm_g, w_hgrn_out, w_out, g_ffn, w_router, b_router,
           w1, b1, w2, b2, g_final):
    b, s, d = x.shape
    assert w_ada.shape[0] == 1, "single-layer block"
    t = b * s

    c_pad = jnp.zeros((8, d), F32).at[:b].set(c.astype(F32))
    mod, lb = _ada(c_pad, w_ada[0], b_ada, lb_param)
    mod3 = mod[:b].reshape(b * N_MOD, 1, d)

    p = _inproj(x, g_mix, mod3, w_in[0].astype(BF16))

    dw_pad = jnp.zeros((32, d), F32).at[:CONV_K].set(conv_dw[0])
    yc = _conv_branch(p, dw_pad, conv_dw_bias, conv_ln_g, conv_ln_b, w_conv_out[0].astype(BF16))
    yh = _hgrn_branch(p, lb, hgrn_norm_g, w_hgrn_out[0].astype(BF16))

    x1, h2, logits = _merge(yc, yh, x, mod3, g_ffn, w_out[0].astype(BF16), w_router[0], b_router)

    row_tok, row_w, block_e = _route(logits.reshape(t, N_EXPERTS))
    hf = h2.reshape(t, d)
    ys = _ffn(block_e, hf[row_tok], row_w[:, None], w1[0], b1[0][:, None, :], w2[0],
              b2[0][:, None, :])
    moe_out = jnp.zeros((t, d), F32).at[row_tok].add(ys)

    return _final(x1, moe_out.reshape(b, s, d), mod3, g_final.reshape(1, d))
```

```python
import functools

import jax
import jax.numpy as jnp
import numpy as np
from jax import lax
from jax.experimental import pallas as pl
from jax.experimental.pallas import tpu as pltpu

F32 = jnp.float32
BF16 = jnp.bfloat16

D_MODEL = 1024
CONV_K = 31
HG_HEADS = 8
HG_DK = 128
N_EXPERTS = 32
TOP_K = 4
D_FF = 1024
SWIGLU_ALPHA = 1.702
SWIGLU_LIMIT = 7.0
MOE_BLOCK = 256
EPS = 1e-6
N_MOD = 6
COL_CA, COL_CB, COL_Q, COL_F, COL_I, COL_G, COL_GC, COL_GH = range(8)
N_COLS = 8

HG_CHUNK = 128
CONV_HALO = 32
VMEM_LIMIT = 56 * 1024 * 1024


def _sigmoid(x):
    return 1.0 / (1.0 + jnp.exp(-x))


def _params(sem, vmem=VMEM_LIMIT):
    return pltpu.CompilerParams(dimension_semantics=sem, vmem_limit_bytes=vmem)


def _ada_kernel(c_ref, w_ref, b_ref, lbp_ref, mod_ref, lb_ref):
    c = c_ref[...]
    c_act = c * _sigmoid(c)
    mod_ref[...] = jnp.dot(c_act, w_ref[...], preferred_element_type=F32,
                           precision=lax.Precision.HIGHEST) + b_ref[...]
    p = lbp_ref[...]
    e = jnp.exp(p - jnp.max(p, axis=0, keepdims=True))
    lb_ref[...] = e[0:1, :] / jnp.sum(e, axis=0, keepdims=True)


def _ada(c_pad, w_ada, b_ada, lb_param):
    nb, d = c_pad.shape
    n = w_ada.shape[1]
    tn = 1536
    return pl.pallas_call(
        _ada_kernel,
        grid=(n // tn,),
        in_specs=[
            pl.BlockSpec((nb, d), lambda j: (0, 0)),
            pl.BlockSpec((d, tn), lambda j: (0, j)),
            pl.BlockSpec((1, tn), lambda j: (0, j)),
            pl.BlockSpec(lb_param.shape, lambda j: (0, 0)),
        ],
        out_specs=[
            pl.BlockSpec((nb, tn), lambda j: (0, j)),
            pl.BlockSpec((1, d), lambda j: (0, 0)),
        ],
        out_shape=[
            jax.ShapeDtypeStruct((nb, n), F32),
            jax.ShapeDtypeStruct((1, d), F32),
        ],
        compiler_params=_params(("arbitrary",)),
        name="ada_mod",
    )(c_pad, w_ada, b_ada, lb_param)


def _inproj_kernel(x_ref, g_ref, sc_ref, sh_ref, w_ref, o_ref):
    x = x_ref[0]
    ms = jnp.mean(x * x, axis=-1, keepdims=True)
    h = x * lax.rsqrt(ms + EPS) * g_ref[...]
    h = h * (1.0 + sc_ref[0]) + sh_ref[0]
    o_ref[0] = jnp.dot(h.astype(BF16), w_ref[...], preferred_element_type=F32)


def _inproj(x, g_mix, mod3, w_in_bf):
    b, s, d = x.shape
    n = w_in_bf.shape[1]
    tm, tn = 512, 2048
    return pl.pallas_call(
        _inproj_kernel,
        grid=(n // tn, b, s // tm),
        in_specs=[
            pl.BlockSpec((1, tm, d), lambda j, bi, i: (bi, i, 0)),
            pl.BlockSpec((1, d), lambda j, bi, i: (0, 0)),
            pl.BlockSpec((1, 1, d), lambda j, bi, i: (bi * N_MOD + 1, 0, 0)),
            pl.BlockSpec((1, 1, d), lambda j, bi, i: (bi * N_MOD + 0, 0, 0)),
            pl.BlockSpec((d, tn), lambda j, bi, i: (0, j)),
        ],
        out_specs=pl.BlockSpec((1, tm, tn), lambda j, bi, i: (bi, i, j)),
        out_shape=jax.ShapeDtypeStruct((b, s, n), F32),
        compiler_params=_params(("arbitrary", "arbitrary", "arbitrary")),
        name="in_proj",
    )(x, g_mix, mod3, mod3, w_in_bf)


CONV_TS = 256
CONV_RG = 32


def _conv_kernel(a_ref, ah_ref, b_ref, bh_ref, gc_ref, dw_ref, bias_ref, lng_ref, lnb_ref,
                 w_ref, o_ref, buf, cv):
    i = pl.program_id(1)
    ts = CONV_TS
    uh = ah_ref[0] * _sigmoid(bh_ref[0])
    buf[0:CONV_HALO, :] = jnp.where(i > 0, uh, 0.0)
    buf[CONV_HALO:CONV_HALO + ts, :] = a_ref[0] * _sigmoid(b_ref[0])
    off = CONV_HALO - (CONV_K - 1)
    for r in range(ts // CONV_RG):
        base = r * CONV_RG
        acc = jnp.zeros((CONV_RG, D_MODEL), F32)
        for j in range(CONV_K):
            acc = acc + dw_ref[j:j + 1, :] * buf[base + off + j:base + off + j + CONV_RG, :]
        cv[base:base + CONV_RG, :] = acc + bias_ref[...]
    u = cv[...]
    mu = jnp.mean(u, axis=-1, keepdims=True)
    uc = u - mu
    var = jnp.mean(uc * uc, axis=-1, keepdims=True)
    y = uc * lax.rsqrt(var + EPS) * lng_ref[...] + lnb_ref[...]
    y = y * _sigmoid(y)
    yc = jnp.dot(y.astype(BF16), w_ref[...], preferred_element_type=F32)
    o_ref[0] = _sigmoid(gc_ref[0]) * yc


def _conv_branch(p, dw_pad, bias, ln_g, ln_b, w_bf):
    b, s, _ = p.shape
    d = D_MODEL
    ts = CONV_TS
    hb = ts // CONV_HALO

    def halo_map(col):
        return lambda bi, i: (bi, jnp.maximum(i * hb - 1, 0), col)

    vec = pl.BlockSpec((1, d), lambda bi, i: (0, 0))
    return pl.pallas_call(
        _conv_kernel,
        grid=(b, s // ts),
        in_specs=[
            pl.BlockSpec((1, ts, d), lambda bi, i: (bi, i, COL_CA)),
            pl.BlockSpec((1, CONV_HALO, d), halo_map(COL_CA)),
            pl.BlockSpec((1, ts, d), lambda bi, i: (bi, i, COL_CB)),
            pl.BlockSpec((1, CONV_HALO, d), halo_map(COL_CB)),
            pl.BlockSpec((1, ts, d), lambda bi, i: (bi, i, COL_GC)),
            pl.BlockSpec(dw_pad.shape, lambda bi, i: (0, 0)),
            vec, vec, vec,
            pl.BlockSpec((d, d), lambda bi, i: (0, 0)),
        ],
        out_specs=pl.BlockSpec((1, ts, d), lambda bi, i: (bi, i, 0)),
        out_shape=jax.ShapeDtypeStruct((b, s, d), F32),
        scratch_shapes=[pltpu.VMEM((ts + CONV_HALO, d), F32), pltpu.VMEM((ts, d), F32)],
        compiler_params=_params(("arbitrary", "arbitrary")),
        name="conv_branch",
    )(p, p, p, p, p, dw_pad, bias, ln_g, ln_b, w_bf)


def _hgrn_levels():
    c = HG_CHUNK
    levels = []
    m = c // 2
    while m >= 1:
        levels.append(m)
        m //= 2
    return levels


def _hgrn_sum_matrix():
    c = HG_CHUNK
    r = np.arange(c)[:, None]
    u = np.arange(c)[None, :]
    mats = [(u <= r), (u > r)]
    for m in _hgrn_levels():
        start = (r // (2 * m)) * (2 * m)
        mid = start + m
        upper = (r - start) >= m
        mq = upper & (u >= mid) & (u <= r)
        mk = (~upper) & (u > r) & (u < mid)
        mats.append(mq | mk)
    return np.concatenate(mats, axis=0).astype(np.float32)


def _hgrn_kernel(q_ref, z_ref, v_ref, og_ref, gh_ref, lb_ref, ng_ref, msum_ref, w_ref, o_ref,
                 st, obuf):
    c = HG_CHUNK
    dk = HG_DK
    levels = _hgrn_levels()

    @pl.when(pl.program_id(1) == 0)
    def _():
        st[...] = jnp.zeros_like(st)

    z = z_ref[0]
    lb = lb_ref[...]
    sig = _sigmoid(z)
    f = lb + (1.0 - lb) * sig
    logf = jnp.log(f)
    kk = (1.0 - lb) * (1.0 - sig)
    q = q_ref[0] * (dk ** -0.5)
    v = v_ref[0]
    og = og_ref[0]

    hi = logf.astype(BF16)
    lo = (logf - hi.astype(F32)).astype(BF16)
    msum = msum_ref[...]
    seg = (jnp.dot(msum, hi, preferred_element_type=F32)
           + jnp.dot(msum, lo, preferred_element_type=F32))

    g_inc = seg[0:c]
    g_sfx = seg[c:2 * c]
    q_st = (q * jnp.exp(g_inc)).astype(BF16)
    k_st = (kk * jnp.exp(g_sfx)).astype(BF16)
    dec_all = jnp.exp(g_inc[c - 1:c, :])
    v_bf = v.astype(BF16)
    q_bf = q.astype(BF16)
    k_bf = kk.astype(BF16)

    row = lax.broadcasted_iota(jnp.int32, (c, c), 0)
    col = lax.broadcasted_iota(jnp.int32, (c, c), 1)
    rr = lax.broadcasted_iota(jnp.int32, (c, 1), 0)

    qs, ks, masks = [], [], []
    for li, m in enumerate(levels):
        e = jnp.exp(seg[(2 + li) * c:(3 + li) * c])
        upper = (rr & m) != 0
        qs.append(jnp.where(upper, q * e, 0.0).astype(BF16))
        ks.append(jnp.where(upper, 0.0, kk * e).astype(BF16))
        sh = int(np.log2(2 * m))
        masks.append((row >> sh) == (col >> sh))
    diag = row == col

    nt = (((1,), (1,)), ((), ()))
    tn = (((0,), (0,)), ((), ()))
    for h in range(HG_HEADS):
        sl = slice(h * dk, (h + 1) * dk)
        a = jnp.where(diag, lax.dot_general(q_bf[:, sl], k_bf[:, sl], nt,
                                            preferred_element_type=F32), 0.0)
        for li in range(len(levels)):
            a = a + jnp.where(masks[li],
                              lax.dot_general(qs[li][:, sl], ks[li][:, sl], nt,
                                              preferred_element_type=F32), 0.0)
        s_t = st[h]
        o = jnp.dot(a.astype(BF16), v_bf[:, sl], preferred_element_type=F32)
        o = o + lax.dot_general(q_st[:, sl], s_t.astype(BF16), nt, preferred_element_type=F32)
        st[h] = s_t * dec_all[:, sl] + lax.dot_general(v_bf[:, sl], k_st[:, sl], tn,
                                                       preferred_element_type=F32)
        ms = jnp.mean(o * o, axis=-1, keepdims=True)
        o = o * lax.rsqrt(ms + EPS) * ng_ref[...]
        g = og[:, sl]
        obuf[:, sl] = (o * (g * _sigmoid(g))).astype(BF16)

    y = jnp.dot(obuf[...], w_ref[...], preferred_element_type=F32)
    o_ref[0] = _sigmoid(gh_ref[0]) * y


def _hgrn_branch(p, lb, norm_g, w_bf):
    b, s, _ = p.shape
    d = D_MODEL
    c = HG_CHUNK
    msum = jnp.asarray(_hgrn_sum_matrix(), dtype=BF16)

    def col_spec(col):
        return pl.BlockSpec((1, c, d), lambda bi, i: (bi, i, col))

    return pl.pallas_call(
        _hgrn_kernel,
        grid=(b, s // c),
        in_specs=[
            col_spec(COL_Q), col_spec(COL_F), col_spec(COL_I), col_spec(COL_G), col_spec(COL_GH),
            pl.BlockSpec((1, d), lambda bi, i: (0, 0)),
            pl.BlockSpec((1, HG_DK), lambda bi, i: (0, 0)),
            pl.BlockSpec(msum.shape, lambda bi, i: (0, 0)),
            pl.BlockSpec((d, d), lambda bi, i: (0, 0)),
        ],
        out_specs=pl.BlockSpec((1, c, d), lambda bi, i: (bi, i, 0)),
        out_shape=jax.ShapeDtypeStruct((b, s, d), F32),
        scratch_shapes=[pltpu.VMEM((HG_HEADS, HG_DK, HG_DK), F32), pltpu.VMEM((c, d), BF16)],
        compiler_params=_params(("arbitrary", "arbitrary")),
        name="hgrn_branch",
    )(p, p, p, p, p, lb, norm_g, msum, w_bf)


ROW_TILE = (8, 128)


def _rows_to_tiles(rows):
    st = jnp.stack([rows[:, j * 128:(j + 1) * 128] for j in range(ROW_TILE[0])], axis=0)
    return pltpu.einshape("jrl->rjl", st)


def _tiles_to_rows(tiles):
    y = pltpu.einshape("rjl->jrl", tiles)
    return jnp.concatenate([y[j] for j in range(ROW_TILE[0])], axis=-1)


_NT = (((1,), (1,)), ((), ()))


def _merge_kernel(yc_ref, yh_ref, x_ref, ga_ref, sc_ref, sh_ref, g_ref, w_ref, wrt_ref, br_ref,
                  x1_ref, h3_ref, lg_ref):
    merged = (yc_ref[0] + yh_ref[0]).astype(BF16)
    x1 = x_ref[0] + ga_ref[0] * jnp.dot(merged, w_ref[...], preferred_element_type=F32)
    x1_ref[0] = x1
    ms = jnp.mean(x1 * x1, axis=-1, keepdims=True)
    h2 = x1 * lax.rsqrt(ms + EPS) * g_ref[...]
    h2 = h2 * (1.0 + sc_ref[0]) + sh_ref[0]
    h3_ref[...] = _rows_to_tiles(h2)
    lg_ref[...] = lax.dot_general(wrt_ref[...], h2, _NT, preferred_element_type=F32,
                                  precision=lax.Precision.HIGHEST) + br_ref[...]


def _merge(yc, yh, x, mod3, g_ffn, w_out_bf, w_router_t, b_router_col):
    b, s, d = x.shape
    tm = 512
    nt = s // tm
    ne = w_router_t.shape[0]

    def mod_spec(k):
        return pl.BlockSpec((1, 1, d), lambda bi, i: (bi * N_MOD + k, 0, 0))

    tile = pl.BlockSpec((1, tm, d), lambda bi, i: (bi, i, 0))
    return pl.pallas_call(
        _merge_kernel,
        grid=(b, nt),
        in_specs=[
            tile, tile, tile, mod_spec(2), mod_spec(4), mod_spec(3),
            pl.BlockSpec((1, d), lambda bi, i: (0, 0)),
            pl.BlockSpec((d, d), lambda bi, i: (0, 0)),
            pl.BlockSpec((ne, d), lambda bi, i: (0, 0)),
            pl.BlockSpec((ne, 1), lambda bi, i: (0, 0)),
        ],
        out_specs=[
            tile,
            pl.BlockSpec((tm,) + ROW_TILE, lambda bi, i: (bi * nt + i, 0, 0)),
            pl.BlockSpec((ne, tm), lambda bi, i: (0, bi * nt + i)),
        ],
        out_shape=[
            jax.ShapeDtypeStruct((b, s, d), F32),
            jax.ShapeDtypeStruct((b * s,) + ROW_TILE, F32),
            jax.ShapeDtypeStruct((ne, b * s), F32),
        ],
        compiler_params=_params(("arbitrary", "arbitrary")),
        name="merge_router",
    )(yc, yh, x, mod3, mod3, mod3, g_ffn, w_out_bf, w_router_t, b_router_col)


MOE_TL = 1024
MOE_GRP = 16
MOE_CAP = MOE_TL * TOP_K + N_EXPERTS * MOE_GRP
MOE_GPB = MOE_BLOCK // MOE_GRP
RELAY_ROWS = 256
assert MOE_CAP % RELAY_ROWS == 0 and MOE_TL % RELAY_ROWS == 0


def _route_kernel(lg_ref, u_ref, ld_ref, w_ref, cnt_ref):
    tl = MOE_TL
    l = lg_ref[...]
    eio = lax.broadcasted_iota(jnp.int32, l.shape, 0)
    vals, hots = [], []
    for _ in range(TOP_K):
        m = jnp.max(l, axis=0, keepdims=True)
        idx = jnp.min(jnp.where(l == m, eio, N_EXPERTS), axis=0, keepdims=True)
        hot = eio == idx
        vals.append(m)
        hots.append(hot)
        l = jnp.where(hot, -jnp.inf, l)
    ex = [jnp.exp(v - vals[0]) for v in vals]
    den = ex[0] + ex[1] + ex[2] + ex[3]
    cnt = hots[0].astype(F32)
    for k in range(1, TOP_K):
        cnt = cnt + hots[k].astype(F32)
    prefix = jnp.dot(cnt.astype(BF16), u_ref[...], preferred_element_type=F32)
    n_e = jnp.sum(cnt, axis=1, keepdims=True)
    pad_e = jnp.floor((n_e + (MOE_GRP - 1)) * (1.0 / MOE_GRP)) * MOE_GRP
    scan = jnp.broadcast_to(pad_e, (N_EXPERTS, 128))
    ei = lax.broadcasted_iota(jnp.int32, scan.shape, 0)
    dist = 1
    while dist < N_EXPERTS:
        scan = scan + jnp.where(ei >= dist, pltpu.roll(scan, dist, axis=0), 0.0)
        dist *= 2
    base = prefix + (scan[:, 0:1] - pad_e)
    for k in range(TOP_K):
        dest = jnp.sum(jnp.where(hots[k], base, 0.0), axis=0, keepdims=True)
        ld_ref[0, :, k * tl:(k + 1) * tl] = dest.astype(jnp.int32)
        w_ref[0, :, k * tl:(k + 1) * tl] = ex[k] / den
    cnt_ref[0] = jnp.broadcast_to(n_e, (N_EXPERTS, 128)).astype(jnp.int32)


def _route(logits_t):
    ne, t = logits_t.shape
    tl = MOE_TL
    n_tiles = t // tl
    upper = jnp.asarray(np.triu(np.ones((tl, tl), np.float32), k=1), dtype=BF16)
    slot = pl.BlockSpec((1, 1, TOP_K * tl), lambda i: (i, 0, 0))
    return pl.pallas_call(
        _route_kernel,
        grid=(n_tiles,),
        in_specs=[pl.BlockSpec((ne, tl), lambda i: (0, i)),
                  pl.BlockSpec((tl, tl), lambda i: (0, 0))],
        out_specs=[slot, slot, pl.BlockSpec((1, ne, 128), lambda i: (i, 0, 0))],
        out_shape=[
            jax.ShapeDtypeStruct((n_tiles, 1, TOP_K * tl), jnp.int32),
            jax.ShapeDtypeStruct((n_tiles, 1, TOP_K * tl), F32),
            jax.ShapeDtypeStruct((n_tiles, ne, 128), jnp.int32),
        ],
        compiler_params=_params(("arbitrary",)),
        name="route",
    )(logits_t, upper)


def _gather_kernel(ld_ref, h3_ref, xs_ref, xs3):
    tl = MOE_TL
    xs3[...] = jnp.zeros_like(xs3)

    def push(t, carry):
        tile = h3_ref[t]
        for k in range(TOP_K):
            xs3[ld_ref[0, 0, k * tl + t]] = tile
        return carry

    lax.fori_loop(0, tl, push, 0, unroll=4)

    def relay(r, carry):
        r0 = pl.multiple_of(r * RELAY_ROWS, RELAY_ROWS)
        xs_ref[pl.ds(r0, RELAY_ROWS), :] = _tiles_to_rows(xs3[pl.ds(r0, RELAY_ROWS)]).astype(BF16)
        return carry

    lax.fori_loop(0, MOE_CAP // RELAY_ROWS, relay, 0)


def _gather(ld, h3):
    t = h3.shape[0]
    tl = MOE_TL
    n_tiles = t // tl
    return pl.pallas_call(
        _gather_kernel,
        grid=(n_tiles,),
        in_specs=[
            pl.BlockSpec((1, 1, TOP_K * tl), lambda i: (i, 0, 0), memory_space=pltpu.SMEM),
            pl.BlockSpec((tl,) + ROW_TILE, lambda i: (i, 0, 0)),
        ],
        out_specs=pl.BlockSpec((MOE_CAP, D_MODEL), lambda i: (i, 0)),
        out_shape=jax.ShapeDtypeStruct((n_tiles * MOE_CAP, D_MODEL), BF16),
        scratch_shapes=[pltpu.VMEM((MOE_CAP,) + ROW_TILE, F32)],
        compiler_params=_params(("arbitrary",)),
        name="moe_gather",
    )(ld, h3)


def _block_table(cnt, n_blocks):
    n_tiles = cnt.shape[0]
    g = (cnt + (MOE_GRP - 1)) // MOE_GRP
    loff = jnp.cumsum(g, axis=1) - g
    ctile = jnp.cumsum(g, axis=0)
    ge = ctile[-1]
    nbe = (ge + (MOE_GPB - 1)) // MOE_GPB
    bend = jnp.cumsum(nbe)
    bstart = bend - nbe
    blk = jnp.arange(n_blocks, dtype=jnp.int32)
    be = jnp.minimum(jnp.searchsorted(bend, blk, side="right"), N_EXPERTS - 1).astype(jnp.int32)
    slot = jnp.arange(n_blocks * MOE_GPB, dtype=jnp.int32)
    sb = slot // MOE_GPB
    se = be[sb]
    pos = (sb - bstart[se]) * MOE_GPB + slot % MOE_GPB
    valid = (sb < bend[-1]) & (pos < ge[se])
    ct = ctile[:, se]
    ti = jnp.minimum(jnp.sum(ct <= pos[None, :], axis=0), n_tiles - 1).astype(jnp.int32)
    prev = jnp.where(ti > 0, ctile[jnp.maximum(ti - 1, 0), se], 0)
    grp = ti * (MOE_CAP // MOE_GRP) + loff[ti, se] + (pos - prev)
    return be, jnp.where(valid, grp, -1).astype(jnp.int32)


def _ffn_kernel(be_ref, src_ref, xs_hbm, w1_ref, b1_ref, w2_ref, b2_ref, ys_hbm,
                xbuf, ybuf, w1s, w2s, sem_in, sem_out):
    i = pl.program_id(0)
    nb = pl.num_programs(0)
    slot = i % 2

    def rows(blk, g):
        grp = jnp.maximum(src_ref[blk * MOE_GPB + g], 0)
        return pl.ds(pl.multiple_of(grp * MOE_GRP, MOE_GRP), MOE_GRP)

    def in_copy(blk, sl, g):
        return pltpu.make_async_copy(xs_hbm.at[rows(blk, g), :],
                                     xbuf.at[sl, pl.ds(g * MOE_GRP, MOE_GRP), :], sem_in.at[sl])

    def out_copy(blk, sl, g):
        return pltpu.make_async_copy(ybuf.at[sl, pl.ds(g * MOE_GRP, MOE_GRP), :],
                                     ys_hbm.at[rows(blk, g), :], sem_out.at[sl])

    def each_group(blk, fn):
        for g in range(MOE_GPB):
            @pl.when(src_ref[blk * MOE_GPB + g] >= 0)
            def _():
                fn(g)

    @pl.when(i == 0)
    def _():
        xbuf[...] = jnp.zeros_like(xbuf)
        each_group(0, lambda g: in_copy(0, 0, g).start())

    @pl.when(i + 1 < nb)
    def _():
        each_group(i + 1, lambda g: in_copy(i + 1, 1 - slot, g).start())

    each_group(i, lambda g: in_copy(i, slot, g).wait())

    @pl.when(i >= 2)
    def _():
        each_group(i - 2, lambda g: out_copy(i - 2, slot, g).wait())

    prev = be_ref[jnp.maximum(i - 1, 0)]

    @pl.when((i == 0) | (be_ref[i] != prev))
    def _():
        w1s[...] = w1_ref[0].astype(BF16)
        w2s[...] = w2_ref[0].astype(BF16)

    u = jnp.dot(xbuf[slot], w1s[...], preferred_element_type=F32) + b1_ref[0]
    gl = jnp.minimum(u[:, :D_FF], SWIGLU_LIMIT)
    lin = jnp.clip(u[:, D_FF:], -SWIGLU_LIMIT, SWIGLU_LIMIT)
    act = gl * _sigmoid(SWIGLU_ALPHA * gl) * (lin + 1.0)
    y = jnp.dot(act.astype(BF16), w2s[...], preferred_element_type=F32) + b2_ref[0]
    ybuf[slot] = y.astype(BF16)

    each_group(i, lambda g: out_copy(i, slot, g).start())

    @pl.when(i == nb - 1)
    def _():
        each_group(i, lambda g: out_copy(i, slot, g).wait())

        @pl.when(i >= 1)
        def _():
            each_group(i - 1, lambda g: out_copy(i - 1, 1 - slot, g).wait())


def _ffn(block_e, src, xs, w1, b1, w2, b2):
    r, d = xs.shape
    nb = block_e.shape[0]
    grid_spec = pltpu.PrefetchScalarGridSpec(
        num_scalar_prefetch=2,
        grid=(nb,),
        in_specs=[
            pl.BlockSpec(memory_space=pl.ANY),
            pl.BlockSpec((1, d, 2 * D_FF), lambda i, be, sr: (be[i], 0, 0)),
            pl.BlockSpec((1, 1, 2 * D_FF), lambda i, be, sr: (be[i], 0, 0)),
            pl.BlockSpec((1, D_FF, d), lambda i, be, sr: (be[i], 0, 0)),
            pl.BlockSpec((1, 1, d), lambda i, be, sr: (be[i], 0, 0)),
        ],
        out_specs=pl.BlockSpec(memory_space=pl.ANY),
        scratch_shapes=[
            pltpu.VMEM((2, MOE_BLOCK, d), BF16), pltpu.VMEM((2, MOE_BLOCK, d), BF16),
            pltpu.VMEM((d, 2 * D_FF), BF16), pltpu.VMEM((D_FF, d), BF16),
            pltpu.SemaphoreType.DMA((2,)), pltpu.SemaphoreType.DMA((2,)),
        ],
    )
    return pl.pallas_call(
        _ffn_kernel,
        grid_spec=grid_spec,
        out_shape=jax.ShapeDtypeStruct((r, d), BF16),
        input_output_aliases={2: 0},
        compiler_params=_params(("arbitrary",)),
        name="expert_ffn",
    )(block_e, src, xs, w1, b1, w2, b2)


def _combine_kernel(ld_ref, w_ref, ys_ref, x1_ref, ga_ref, g_ref, o_ref, y3, o3):
    tl = MOE_TL

    def relay(r, carry):
        r0 = pl.multiple_of(r * RELAY_ROWS, RELAY_ROWS)
        y3[pl.ds(r0, RELAY_ROWS)] = _rows_to_tiles(ys_ref[pl.ds(r0, RELAY_ROWS), :].astype(F32))
        return carry

    lax.fori_loop(0, MOE_CAP // RELAY_ROWS, relay, 0)

    def pull(t, carry):
        acc = w_ref[0, 0, t] * y3[ld_ref[0, 0, t]]
        for k in range(1, TOP_K):
            acc = acc + w_ref[0, 0, k * tl + t] * y3[ld_ref[0, 0, k * tl + t]]
        o3[t] = acc
        return carry

    lax.fori_loop(0, tl, pull, 0, unroll=4)

    def finish(r, carry):
        r0 = pl.multiple_of(r * RELAY_ROWS, RELAY_ROWS)
        x2 = x1_ref[pl.ds(r0, RELAY_ROWS), :] + ga_ref[0] * _tiles_to_rows(o3[pl.ds(r0, RELAY_ROWS)])
        ms = jnp.mean(x2 * x2, axis=-1, keepdims=True)
        o_ref[pl.ds(r0, RELAY_ROWS), :] = x2 * lax.rsqrt(ms + EPS) * g_ref[...]
        return carry

    lax.fori_loop(0, tl // RELAY_ROWS, finish, 0)


def _combine(ld, wts, ys, x1f, mod3, g_final, tiles_per_batch):
    t, d = x1f.shape
    tl = MOE_TL
    n_tiles = t // tl
    smem = functools.partial(pl.BlockSpec, (1, 1, TOP_K * tl), lambda i: (i, 0, 0),
                             memory_space=pltpu.SMEM)
    return pl.pallas_call(
        _combine_kernel,
        grid=(n_tiles,),
        in_specs=[
            smem(), smem(),
            pl.BlockSpec((MOE_CAP, d), lambda i: (i, 0), pipeline_mode=pl.Buffered(1)),
            pl.BlockSpec((tl, d), lambda i: (i, 0)),
            pl.BlockSpec((1, 1, d), lambda i: ((i // tiles_per_batch) * N_MOD + 5, 0, 0)),
            pl.BlockSpec((1, d), lambda i: (0, 0)),
        ],
        out_specs=pl.BlockSpec((tl, d), lambda i: (i, 0)),
        out_shape=jax.ShapeDtypeStruct((t, d), F32),
        scratch_shapes=[pltpu.VMEM((MOE_CAP,) + ROW_TILE, F32), pltpu.VMEM((tl,) + ROW_TILE, F32)],
        compiler_params=_params(("arbitrary",)),
        name="moe_combine",
    )(ld, wts, ys, x1f, mod3, g_final)


def kernel(x, c, w_ada, b_ada, g_mix, w_in, conv_dw, conv_dw_bias, conv_ln_g, conv_ln_b,
           w_conv_out, lb_param, hgrn_norm_g, w_hgrn_out, w_out, g_ffn, w_router, b_router,
           w1, b1, w2, b2, g_final):
    b, s, d = x.shape
    assert w_ada.shape[0] == 1, "single-layer block"
    assert s % MOE_TL == 0
    t = b * s
    n_tiles = t // MOE_TL

    c_pad = jnp.zeros((8, d), F32).at[:b].set(c.astype(F32))
    mod, lb = _ada(c_pad, w_ada[0], b_ada, lb_param)
    mod3 = mod[:b].reshape(b * N_MOD, 1, d)

    p = _inproj(x, g_mix, mod3, w_in[0].astype(BF16))

    dw_pad = jnp.zeros((32, d), F32).at[:CONV_K].set(conv_dw[0])
    yc = _conv_branch(p, dw_pad, conv_dw_bias, conv_ln_g, conv_ln_b, w_conv_out[0].astype(BF16))
    yh = _hgrn_branch(p, lb, hgrn_norm_g, w_hgrn_out[0].astype(BF16))

    x1, h3, logits_t = _merge(yc, yh, x, mod3, g_ffn, w_out[0].astype(BF16),
                              w_router[0].T, b_router[0][:, None])

    ld, wts, cnt = _route(logits_t)
    xs = _gather(ld, h3)
    groups_max = t * TOP_K // MOE_GRP + n_tiles * N_EXPERTS
    n_blocks = groups_max // MOE_GPB + N_EXPERTS
    block_e, src = _block_table(cnt[:, :, 0], n_blocks)
    ys = _ffn(block_e, src, xs, w1[0], b1[0][:, None, :], w2[0], b2[0][:, None, :])
    out = _combine(ld, wts, ys, x1.reshape(t, d), mod3, g_final.reshape(1, d), s // MOE_TL)
    return out.reshape(b, s, d)
```

```python
import functools

import jax
import jax.numpy as jnp
import numpy as np
from jax import lax
from jax.experimental import pallas as pl
from jax.experimental.pallas import tpu as pltpu

F32 = jnp.float32
BF16 = jnp.bfloat16

D_MODEL = 1024
CONV_K = 31
HG_HEADS = 8
HG_DK = 128
N_EXPERTS = 32
TOP_K = 4
D_FF = 1024
SWIGLU_ALPHA = 1.702
SWIGLU_LIMIT = 7.0
MOE_BLOCK = 256
EPS = 1e-6
N_MOD = 6
COL_CA, COL_CB, COL_Q, COL_F, COL_I, COL_G, COL_GC, COL_GH = range(8)
N_COLS = 8

HG_CHUNK = 128
CONV_HALO = 32
VMEM_LIMIT = 56 * 1024 * 1024


def _sigmoid(x):
    return 1.0 / (1.0 + jnp.exp(-x))


def _params(sem, vmem=VMEM_LIMIT):
    return pltpu.CompilerParams(dimension_semantics=sem, vmem_limit_bytes=vmem)


def _ada_kernel(c_ref, w_ref, b_ref, lbp_ref, mod_ref, lb_ref):
    c = c_ref[...]
    c_act = c * _sigmoid(c)
    mod_ref[...] = jnp.dot(c_act, w_ref[...], preferred_element_type=F32,
                           precision=lax.Precision.HIGHEST) + b_ref[...]
    p = lbp_ref[...]
    e = jnp.exp(p - jnp.max(p, axis=0, keepdims=True))
    lb_ref[...] = e[0:1, :] / jnp.sum(e, axis=0, keepdims=True)


def _ada(c_pad, w_ada, b_ada, lb_param):
    nb, d = c_pad.shape
    n = w_ada.shape[1]
    tn = 1536
    return pl.pallas_call(
        _ada_kernel,
        grid=(n // tn,),
        in_specs=[
            pl.BlockSpec((nb, d), lambda j: (0, 0)),
            pl.BlockSpec((d, tn), lambda j: (0, j)),
            pl.BlockSpec((1, tn), lambda j: (0, j)),
            pl.BlockSpec(lb_param.shape, lambda j: (0, 0)),
        ],
        out_specs=[
            pl.BlockSpec((nb, tn), lambda j: (0, j)),
            pl.BlockSpec((1, d), lambda j: (0, 0)),
        ],
        out_shape=[
            jax.ShapeDtypeStruct((nb, n), F32),
            jax.ShapeDtypeStruct((1, d), F32),
        ],
        compiler_params=_params(("arbitrary",)),
        name="ada_mod",
    )(c_pad, w_ada, b_ada, lb_param)


def _inproj_kernel(x_ref, g_ref, sc_ref, sh_ref, w_ref, o_ref):
    x = x_ref[0]
    ms = jnp.mean(x * x, axis=-1, keepdims=True)
    h = x * lax.rsqrt(ms + EPS) * g_ref[...]
    h = h * (1.0 + sc_ref[0]) + sh_ref[0]
    o_ref[0] = jnp.dot(h.astype(BF16), w_ref[...], preferred_element_type=F32)


def _inproj(x, g_mix, mod3, w_in_bf):
    b, s, d = x.shape
    n = w_in_bf.shape[1]
    tm, tn = 512, 2048
    return pl.pallas_call(
        _inproj_kernel,
        grid=(n // tn, b, s // tm),
        in_specs=[
            pl.BlockSpec((1, tm, d), lambda j, bi, i: (bi, i, 0)),
            pl.BlockSpec((1, d), lambda j, bi, i: (0, 0)),
            pl.BlockSpec((1, 1, d), lambda j, bi, i: (bi * N_MOD + 1, 0, 0)),
            pl.BlockSpec((1, 1, d), lambda j, bi, i: (bi * N_MOD + 0, 0, 0)),
            pl.BlockSpec((d, tn), lambda j, bi, i: (0, j)),
        ],
        out_specs=pl.BlockSpec((1, tm, tn), lambda j, bi, i: (bi, i, j)),
        out_shape=jax.ShapeDtypeStruct((b, s, n), F32),
        compiler_params=_params(("arbitrary", "arbitrary", "arbitrary")),
        name="in_proj",
    )(x, g_mix, mod3, mod3, w_in_bf)


CONV_TS = 256
CONV_RG = 32


def _conv_kernel(a_ref, ah_ref, b_ref, bh_ref, gc_ref, dw_ref, bias_ref, lng_ref, lnb_ref,
                 w_ref, o_ref, buf, cv):
    i = pl.program_id(1)
    ts = CONV_TS
    uh = ah_ref[0] * _sigmoid(bh_ref[0])
    buf[0:CONV_HALO, :] = jnp.where(i > 0, uh, 0.0)
    buf[CONV_HALO:CONV_HALO + ts, :] = a_ref[0] * _sigmoid(b_ref[0])
    off = CONV_HALO - (CONV_K - 1)
    for r in range(ts // CONV_RG):
        base = r * CONV_RG
        acc = jnp.zeros((CONV_RG, D_MODEL), F32)
        for j in range(CONV_K):
            acc = acc + dw_ref[j:j + 1, :] * buf[base + off + j:base + off + j + CONV_RG, :]
        cv[base:base + CONV_RG, :] = acc + bias_ref[...]
    u = cv[...]
    mu = jnp.mean(u, axis=-1, keepdims=True)
    uc = u - mu
    var = jnp.mean(uc * uc, axis=-1, keepdims=True)
    y = uc * lax.rsqrt(var + EPS) * lng_ref[...] + lnb_ref[...]
    y = y * _sigmoid(y)
    yc = jnp.dot(y.astype(BF16), w_ref[...], preferred_element_type=F32)
    o_ref[0] = _sigmoid(gc_ref[0]) * yc


def _conv_branch(p, dw_pad, bias, ln_g, ln_b, w_bf):
    b, s, _ = p.shape
    d = D_MODEL
    ts = CONV_TS
    hb = ts // CONV_HALO

    def halo_map(col):
        return lambda bi, i: (bi, jnp.maximum(i * hb - 1, 0), col)

    vec = pl.BlockSpec((1, d), lambda bi, i: (0, 0))
    return pl.pallas_call(
        _conv_kernel,
        grid=(b, s // ts),
        in_specs=[
            pl.BlockSpec((1, ts, d), lambda bi, i: (bi, i, COL_CA)),
            pl.BlockSpec((1, CONV_HALO, d), halo_map(COL_CA)),
            pl.BlockSpec((1, ts, d), lambda bi, i: (bi, i, COL_CB)),
            pl.BlockSpec((1, CONV_HALO, d), halo_map(COL_CB)),
            pl.BlockSpec((1, ts, d), lambda bi, i: (bi, i, COL_GC)),
            pl.BlockSpec(dw_pad.shape, lambda bi, i: (0, 0)),
            vec, vec, vec,
            pl.BlockSpec((d, d), lambda bi, i: (0, 0)),
        ],
        out_specs=pl.BlockSpec((1, ts, d), lambda bi, i: (bi, i, 0)),
        out_shape=jax.ShapeDtypeStruct((b, s, d), F32),
        scratch_shapes=[pltpu.VMEM((ts + CONV_HALO, d), F32), pltpu.VMEM((ts, d), F32)],
        compiler_params=_params(("arbitrary", "arbitrary")),
        name="conv_branch",
    )(p, p, p, p, p, dw_pad, bias, ln_g, ln_b, w_bf)


def _hgrn_levels():
    c = HG_CHUNK
    levels = []
    m = c // 2
    while m >= 1:
        levels.append(m)
        m //= 2
    return levels


def _hgrn_sum_matrix():
    c = HG_CHUNK
    r = np.arange(c)[:, None]
    u = np.arange(c)[None, :]
    mats = [(u <= r), (u > r)]
    for m in _hgrn_levels():
        start = (r // (2 * m)) * (2 * m)
        mid = start + m
        upper = (r - start) >= m
        mq = upper & (u >= mid) & (u <= r)
        mk = (~upper) & (u > r) & (u < mid)
        mats.append(mq | mk)
    return np.concatenate(mats, axis=0).astype(np.float32)


def _hgrn_kernel(q_ref, z_ref, v_ref, og_ref, gh_ref, lb_ref, ng_ref, msum_ref, w_ref, o_ref,
                 st, obuf):
    c = HG_CHUNK
    dk = HG_DK
    levels = _hgrn_levels()

    @pl.when(pl.program_id(1) == 0)
    def _():
        st[...] = jnp.zeros_like(st)

    z = z_ref[0]
    lb = lb_ref[...]
    sig = _sigmoid(z)
    f = lb + (1.0 - lb) * sig
    logf = jnp.log(f)
    kk = (1.0 - lb) * (1.0 - sig)
    q = q_ref[0] * (dk ** -0.5)
    v = v_ref[0]
    og = og_ref[0]

    hi = logf.astype(BF16)
    lo = (logf - hi.astype(F32)).astype(BF16)
    msum = msum_ref[...]
    seg = (jnp.dot(msum, hi, preferred_element_type=F32)
           + jnp.dot(msum, lo, preferred_element_type=F32))

    g_inc = seg[0:c]
    g_sfx = seg[c:2 * c]
    q_st = (q * jnp.exp(g_inc)).astype(BF16)
    k_st = (kk * jnp.exp(g_sfx)).astype(BF16)
    dec_all = jnp.exp(g_inc[c - 1:c, :])
    v_bf = v.astype(BF16)
    q_bf = q.astype(BF16)
    k_bf = kk.astype(BF16)

    row = lax.broadcasted_iota(jnp.int32, (c, c), 0)
    col = lax.broadcasted_iota(jnp.int32, (c, c), 1)
    rr = lax.broadcasted_iota(jnp.int32, (c, 1), 0)

    qs, ks, masks = [], [], []
    for li, m in enumerate(levels):
        e = jnp.exp(seg[(2 + li) * c:(3 + li) * c])
        upper = (rr & m) != 0
        qs.append(jnp.where(upper, q * e, 0.0).astype(BF16))
        ks.append(jnp.where(upper, 0.0, kk * e).astype(BF16))
        sh = int(np.log2(2 * m))
        masks.append((row >> sh) == (col >> sh))
    diag = row == col

    nt = (((1,), (1,)), ((), ()))
    tn = (((0,), (0,)), ((), ()))
    for h in range(HG_HEADS):
        sl = slice(h * dk, (h + 1) * dk)
        a = jnp.where(diag, lax.dot_general(q_bf[:, sl], k_bf[:, sl], nt,
                                            preferred_element_type=F32), 0.0)
        for li in range(len(levels)):
            a = a + jnp.where(masks[li],
                              lax.dot_general(qs[li][:, sl], ks[li][:, sl], nt,
                                              preferred_element_type=F32), 0.0)
        s_t = st[h]
        o = jnp.dot(a.astype(BF16), v_bf[:, sl], preferred_element_type=F32)
        o = o + lax.dot_general(q_st[:, sl], s_t.astype(BF16), nt, preferred_element_type=F32)
        st[h] = s_t * dec_all[:, sl] + lax.dot_general(v_bf[:, sl], k_st[:, sl], tn,
                                                       preferred_element_type=F32)
        ms = jnp.mean(o * o, axis=-1, keepdims=True)
        o = o * lax.rsqrt(ms + EPS) * ng_ref[...]
        g = og[:, sl]
        obuf[:, sl] = (o * (g * _sigmoid(g))).astype(BF16)

    y = jnp.dot(obuf[...], w_ref[...], preferred_element_type=F32)
    o_ref[0] = _sigmoid(gh_ref[0]) * y


def _hgrn_branch(p, lb, norm_g, w_bf):
    b, s, _ = p.shape
    d = D_MODEL
    c = HG_CHUNK
    msum = jnp.asarray(_hgrn_sum_matrix(), dtype=BF16)

    def col_spec(col):
        return pl.BlockSpec((1, c, d), lambda bi, i: (bi, i, col))

    return pl.pallas_call(
        _hgrn_kernel,
        grid=(b, s // c),
        in_specs=[
            col_spec(COL_Q), col_spec(COL_F), col_spec(COL_I), col_spec(COL_G), col_spec(COL_GH),
            pl.BlockSpec((1, d), lambda bi, i: (0, 0)),
            pl.BlockSpec((1, HG_DK), lambda bi, i: (0, 0)),
            pl.BlockSpec(msum.shape, lambda bi, i: (0, 0)),
            pl.BlockSpec((d, d), lambda bi, i: (0, 0)),
        ],
        out_specs=pl.BlockSpec((1, c, d), lambda bi, i: (bi, i, 0)),
        out_shape=jax.ShapeDtypeStruct((b, s, d), F32),
        scratch_shapes=[pltpu.VMEM((HG_HEADS, HG_DK, HG_DK), F32), pltpu.VMEM((c, d), BF16)],
        compiler_params=_params(("arbitrary", "arbitrary")),
        name="hgrn_branch",
    )(p, p, p, p, p, lb, norm_g, msum, w_bf)


ROW_TILE = (8, 128)


def _rows_to_tiles(rows):
    st = jnp.stack([rows[:, j * 128:(j + 1) * 128] for j in range(ROW_TILE[0])], axis=0)
    return pltpu.einshape("jrl->rjl", st)


def _tiles_to_rows(tiles):
    y = pltpu.einshape("rjl->jrl", tiles)
    return jnp.concatenate([y[j] for j in range(ROW_TILE[0])], axis=-1)


_NT = (((1,), (1,)), ((), ()))


def _merge_kernel(yc_ref, yh_ref, x_ref, ga_ref, sc_ref, sh_ref, g_ref, w_ref, wrt_ref, br_ref,
                  x1_ref, h3_ref, lg_ref):
    merged = (yc_ref[0] + yh_ref[0]).astype(BF16)
    x1 = x_ref[0] + ga_ref[0] * jnp.dot(merged, w_ref[...], preferred_element_type=F32)
    x1_ref[0] = x1
    ms = jnp.mean(x1 * x1, axis=-1, keepdims=True)
    h2 = x1 * lax.rsqrt(ms + EPS) * g_ref[...]
    h2 = h2 * (1.0 + sc_ref[0]) + sh_ref[0]
    h3_ref[...] = _rows_to_tiles(h2)
    lg_ref[...] = lax.dot_general(wrt_ref[...], h2, _NT, preferred_element_type=F32,
                                  precision=lax.Precision.HIGHEST) + br_ref[...]


def _merge(yc, yh, x, mod3, g_ffn, w_out_bf, w_router_t, b_router_col):
    b, s, d = x.shape
    tm = 512
    nt = s // tm
    ne = w_router_t.shape[0]

    def mod_spec(k):
        return pl.BlockSpec((1, 1, d), lambda bi, i: (bi * N_MOD + k, 0, 0))

    tile = pl.BlockSpec((1, tm, d), lambda bi, i: (bi, i, 0))
    return pl.pallas_call(
        _merge_kernel,
        grid=(b, nt),
        in_specs=[
            tile, tile, tile, mod_spec(2), mod_spec(4), mod_spec(3),
            pl.BlockSpec((1, d), lambda bi, i: (0, 0)),
            pl.BlockSpec((d, d), lambda bi, i: (0, 0)),
            pl.BlockSpec((ne, d), lambda bi, i: (0, 0)),
            pl.BlockSpec((ne, 1), lambda bi, i: (0, 0)),
        ],
        out_specs=[
            tile,
            pl.BlockSpec((tm,) + ROW_TILE, lambda bi, i: (bi * nt + i, 0, 0)),
            pl.BlockSpec((ne, tm), lambda bi, i: (0, bi * nt + i)),
        ],
        out_shape=[
            jax.ShapeDtypeStruct((b, s, d), F32),
            jax.ShapeDtypeStruct((b * s,) + ROW_TILE, F32),
            jax.ShapeDtypeStruct((ne, b * s), F32),
        ],
        compiler_params=_params(("arbitrary", "arbitrary")),
        name="merge_router",
    )(yc, yh, x, mod3, mod3, mod3, g_ffn, w_out_bf, w_router_t, b_router_col)


MOE_TL = 1024
MOE_GRP = 16
MOE_CAP = MOE_TL * TOP_K + N_EXPERTS * MOE_GRP
MOE_GPB = MOE_BLOCK // MOE_GRP
RELAY_ROWS = 256
assert MOE_CAP % RELAY_ROWS == 0 and MOE_TL % RELAY_ROWS == 0


def _route_kernel(lg_ref, u_ref, ld_ref, w_ref, cnt_ref):
    tl = MOE_TL
    l = lg_ref[...]
    eio = lax.broadcasted_iota(jnp.int32, l.shape, 0)
    vals, hots = [], []
    for _ in range(TOP_K):
        m = jnp.max(l, axis=0, keepdims=True)
        idx = jnp.min(jnp.where(l == m, eio, N_EXPERTS), axis=0, keepdims=True)
        hot = eio == idx
        vals.append(m)
        hots.append(hot)
        l = jnp.where(hot, -jnp.inf, l)
    ex = [jnp.exp(v - vals[0]) for v in vals]
    den = ex[0] + ex[1] + ex[2] + ex[3]
    cnt = hots[0].astype(F32)
    for k in range(1, TOP_K):
        cnt = cnt + hots[k].astype(F32)
    prefix = jnp.dot(cnt.astype(BF16), u_ref[...], preferred_element_type=F32)
    n_e = jnp.sum(cnt, axis=1, keepdims=True)
    pad_e = jnp.floor((n_e + (MOE_GRP - 1)) * (1.0 / MOE_GRP)) * MOE_GRP
    scan = jnp.broadcast_to(pad_e, (N_EXPERTS, 128))
    ei = lax.broadcasted_iota(jnp.int32, scan.shape, 0)
    dist = 1
    while dist < N_EXPERTS:
        scan = scan + jnp.where(ei >= dist, pltpu.roll(scan, dist, axis=0), 0.0)
        dist *= 2
    base = prefix + (scan[:, 0:1] - pad_e)
    for k in range(TOP_K):
        dest = jnp.sum(jnp.where(hots[k], base, 0.0), axis=0, keepdims=True)
        ld_ref[0, :, k * tl:(k + 1) * tl] = dest.astype(jnp.int32)
        w_ref[0, :, k * tl:(k + 1) * tl] = ex[k] / den
    cnt_ref[0] = jnp.broadcast_to(n_e, (N_EXPERTS, 128)).astype(jnp.int32)


def _route(logits_t):
    ne, t = logits_t.shape
    tl = MOE_TL
    n_tiles = t // tl
    upper = jnp.asarray(np.triu(np.ones((tl, tl), np.float32), k=1), dtype=BF16)
    slot = pl.BlockSpec((1, 1, TOP_K * tl), lambda i: (i, 0, 0))
    return pl.pallas_call(
        _route_kernel,
        grid=(n_tiles,),
        in_specs=[pl.BlockSpec((ne, tl), lambda i: (0, i)),
                  pl.BlockSpec((tl, tl), lambda i: (0, 0))],
        out_specs=[slot, slot, pl.BlockSpec((1, ne, 128), lambda i: (i, 0, 0))],
        out_shape=[
            jax.ShapeDtypeStruct((n_tiles, 1, TOP_K * tl), jnp.int32),
            jax.ShapeDtypeStruct((n_tiles, 1, TOP_K * tl), F32),
            jax.ShapeDtypeStruct((n_tiles, ne, 128), jnp.int32),
        ],
        compiler_params=_params(("arbitrary",)),
        name="route",
    )(logits_t, upper)


def _gather_kernel(ld_ref, h3_ref, xs_ref, xs3):
    tl = MOE_TL
    xs3[...] = jnp.zeros_like(xs3)

    def push(t, carry):
        tile = h3_ref[t]
        for k in range(TOP_K):
            xs3[ld_ref[0, 0, k * tl + t]] = tile
        return carry

    lax.fori_loop(0, tl, push, 0, unroll=4)

    def relay(r, carry):
        r0 = pl.multiple_of(r * RELAY_ROWS, RELAY_ROWS)
        xs_ref[pl.ds(r0, RELAY_ROWS), :] = _tiles_to_rows(xs3[pl.ds(r0, RELAY_ROWS)]).astype(BF16)
        return carry

    lax.fori_loop(0, MOE_CAP // RELAY_ROWS, relay, 0)


def _gather(ld, h3):
    t = h3.shape[0]
    tl = MOE_TL
    n_tiles = t // tl
    return pl.pallas_call(
        _gather_kernel,
        grid=(n_tiles,),
        in_specs=[
            pl.BlockSpec((1, 1, TOP_K * tl), lambda i: (i, 0, 0), memory_space=pltpu.SMEM),
            pl.BlockSpec((tl,) + ROW_TILE, lambda i: (i, 0, 0)),
        ],
        out_specs=pl.BlockSpec((MOE_CAP, D_MODEL), lambda i: (i, 0)),
        out_shape=jax.ShapeDtypeStruct((n_tiles * MOE_CAP, D_MODEL), BF16),
        scratch_shapes=[pltpu.VMEM((MOE_CAP,) + ROW_TILE, F32)],
        compiler_params=_params(("arbitrary",)),
        name="moe_gather",
    )(ld, h3)


def _table_kernel(cnt_ref, be_ref, src_ref, toff):
    n_tiles = cnt_ref.shape[0]
    groups_per_tile = MOE_CAP // MOE_GRP

    def fill_src(j, c):
        src_ref[j] = -1
        return c

    lax.fori_loop(0, src_ref.shape[0], fill_src, 0)

    def fill_be(j, c):
        be_ref[j] = N_EXPERTS - 1
        return c

    lax.fori_loop(0, be_ref.shape[0], fill_be, 0)

    def fill_off(i, c):
        toff[i] = 0
        return c

    lax.fori_loop(0, n_tiles, fill_off, 0)

    def per_expert(e, pos):
        def per_tile(i, p):
            g = (cnt_ref[i, e] + (MOE_GRP - 1)) // MOE_GRP
            base = i * groups_per_tile + toff[i]

            def put(j, c):
                src_ref[p + j] = base + j
                return c

            lax.fori_loop(0, g, put, 0)
            toff[i] = toff[i] + g
            return p + g

        end = lax.fori_loop(0, n_tiles, per_tile, pos)
        new_pos = (end + (MOE_GPB - 1)) // MOE_GPB * MOE_GPB

        def put_e(b, c):
            be_ref[b] = e
            return c

        lax.fori_loop(pos // MOE_GPB, new_pos // MOE_GPB, put_e, 0)
        return new_pos

    lax.fori_loop(0, N_EXPERTS, per_expert, 0)


def _block_table(cnt, n_blocks):
    smem = pl.BlockSpec(memory_space=pltpu.SMEM)
    return pl.pallas_call(
        _table_kernel,
        in_specs=[smem],
        out_specs=[smem, smem],
        out_shape=[jax.ShapeDtypeStruct((n_blocks,), jnp.int32),
                   jax.ShapeDtypeStruct((n_blocks * MOE_GPB,), jnp.int32)],
        scratch_shapes=[pltpu.SMEM((cnt.shape[0],), jnp.int32)],
        name="moe_block_table",
    )(cnt)


def _ffn_kernel(be_ref, src_ref, xs_hbm, w1_ref, b1_ref, w2_ref, b2_ref, ys_hbm,
                xbuf, ybuf, w1s, w2s, sem_in, sem_out):
    i = pl.program_id(0)
    nb = pl.num_programs(0)
    slot = i % 2

    def rows(blk, g):
        grp = jnp.maximum(src_ref[blk * MOE_GPB + g], 0)
        return pl.ds(pl.multiple_of(grp * MOE_GRP, MOE_GRP), MOE_GRP)

    def in_copy(blk, sl, g):
        return pltpu.make_async_copy(xs_hbm.at[rows(blk, g), :],
                                     xbuf.at[sl, pl.ds(g * MOE_GRP, MOE_GRP), :], sem_in.at[sl])

    def out_copy(blk, sl, g):
        return pltpu.make_async_copy(ybuf.at[sl, pl.ds(g * MOE_GRP, MOE_GRP), :],
                                     ys_hbm.at[rows(blk, g), :], sem_out.at[sl])

    def each_group(blk, fn):
        for g in range(MOE_GPB):
            @pl.when(src_ref[blk * MOE_GPB + g] >= 0)
            def _():
                fn(g)

    @pl.when(i == 0)
    def _():
        xbuf[...] = jnp.zeros_like(xbuf)
        each_group(0, lambda g: in_copy(0, 0, g).start())

    @pl.when(i + 1 < nb)
    def _():
        each_group(i + 1, lambda g: in_copy(i + 1, 1 - slot, g).start())

    each_group(i, lambda g: in_copy(i, slot, g).wait())

    @pl.when(i >= 2)
    def _():
        each_group(i - 2, lambda g: out_copy(i - 2, slot, g).wait())

    prev = be_ref[jnp.maximum(i - 1, 0)]

    @pl.when((i == 0) | (be_ref[i] != prev))
    def _():
        w1s[...] = w1_ref[0].astype(BF16)
        w2s[...] = w2_ref[0].astype(BF16)

    @pl.when(src_ref[i * MOE_GPB] >= 0)
    def _():
        u = jnp.dot(xbuf[slot], w1s[...], preferred_element_type=F32) + b1_ref[0]
        gl = jnp.minimum(u[:, :D_FF], SWIGLU_LIMIT)
        lin = jnp.clip(u[:, D_FF:], -SWIGLU_LIMIT, SWIGLU_LIMIT)
        act = gl * _sigmoid(SWIGLU_ALPHA * gl) * (lin + 1.0)
        y = jnp.dot(act.astype(BF16), w2s[...], preferred_element_type=F32) + b2_ref[0]
        ybuf[slot] = y.astype(BF16)

    each_group(i, lambda g: out_copy(i, slot, g).start())

    @pl.when(i == nb - 1)
    def _():
        each_group(i, lambda g: out_copy(i, slot, g).wait())

        @pl.when(i >= 1)
        def _():
            each_group(i - 1, lambda g: out_copy(i - 1, 1 - slot, g).wait())


def _ffn(block_e, src, xs, w1, b1, w2, b2):
    r, d = xs.shape
    nb = block_e.shape[0]
    grid_spec = pltpu.PrefetchScalarGridSpec(
        num_scalar_prefetch=2,
        grid=(nb,),
        in_specs=[
            pl.BlockSpec(memory_space=pl.ANY),
            pl.BlockSpec((1, d, 2 * D_FF), lambda i, be, sr: (be[i], 0, 0)),
            pl.BlockSpec((1, 1, 2 * D_FF), lambda i, be, sr: (be[i], 0, 0)),
            pl.BlockSpec((1, D_FF, d), lambda i, be, sr: (be[i], 0, 0)),
            pl.BlockSpec((1, 1, d), lambda i, be, sr: (be[i], 0, 0)),
        ],
        out_specs=pl.BlockSpec(memory_space=pl.ANY),
        scratch_shapes=[
            pltpu.VMEM((2, MOE_BLOCK, d), BF16), pltpu.VMEM((2, MOE_BLOCK, d), BF16),
            pltpu.VMEM((d, 2 * D_FF), BF16), pltpu.VMEM((D_FF, d), BF16),
            pltpu.SemaphoreType.DMA((2,)), pltpu.SemaphoreType.DMA((2,)),
        ],
    )
    return pl.pallas_call(
        _ffn_kernel,
        grid_spec=grid_spec,
        out_shape=jax.ShapeDtypeStruct((r, d), BF16),
        input_output_aliases={2: 0},
        compiler_params=_params(("arbitrary",)),
        name="expert_ffn",
    )(block_e, src, xs, w1, b1, w2, b2)


def _combine_kernel(ld_ref, w_ref, ys_ref, x1_ref, ga_ref, g_ref, o_ref, y3, o3):
    tl = MOE_TL

    def relay(r, carry):
        r0 = pl.multiple_of(r * RELAY_ROWS, RELAY_ROWS)
        y3[pl.ds(r0, RELAY_ROWS)] = _rows_to_tiles(ys_ref[pl.ds(r0, RELAY_ROWS), :].astype(F32))
        return carry

    lax.fori_loop(0, MOE_CAP // RELAY_ROWS, relay, 0)

    def pull(t, carry):
        acc = w_ref[0, 0, t] * y3[ld_ref[0, 0, t]]
        for k in range(1, TOP_K):
            acc = acc + w_ref[0, 0, k * tl + t] * y3[ld_ref[0, 0, k * tl + t]]
        o3[t] = acc
        return carry

    lax.fori_loop(0, tl, pull, 0, unroll=4)

    def finish(r, carry):
        r0 = pl.multiple_of(r * RELAY_ROWS, RELAY_ROWS)
        x2 = x1_ref[pl.ds(r0, RELAY_ROWS), :] + ga_ref[0] * _tiles_to_rows(o3[pl.ds(r0, RELAY_ROWS)])
        ms = jnp.mean(x2 * x2, axis=-1, keepdims=True)
        o_ref[pl.ds(r0, RELAY_ROWS), :] = x2 * lax.rsqrt(ms + EPS) * g_ref[...]
        return carry

    lax.fori_loop(0, tl // RELAY_ROWS, finish, 0)


def _combine(ld, wts, ys, x1f, mod3, g_final, tiles_per_batch):
    t, d = x1f.shape
    tl = MOE_TL
    n_tiles = t // tl
    smem = functools.partial(pl.BlockSpec, (1, 1, TOP_K * tl), lambda i: (i, 0, 0),
                             memory_space=pltpu.SMEM)
    return pl.pallas_call(
        _combine_kernel,
        grid=(n_tiles,),
        in_specs=[
            smem(), smem(),
            pl.BlockSpec((MOE_CAP, d), lambda i: (i, 0), pipeline_mode=pl.Buffered(1)),
            pl.BlockSpec((tl, d), lambda i: (i, 0)),
            pl.BlockSpec((1, 1, d), lambda i: ((i // tiles_per_batch) * N_MOD + 5, 0, 0)),
            pl.BlockSpec((1, d), lambda i: (0, 0)),
        ],
        out_specs=pl.BlockSpec((tl, d), lambda i: (i, 0)),
        out_shape=jax.ShapeDtypeStruct((t, d), F32),
        scratch_shapes=[pltpu.VMEM((MOE_CAP,) + ROW_TILE, F32), pltpu.VMEM((tl,) + ROW_TILE, F32)],
        compiler_params=_params(("arbitrary",)),
        name="moe_combine",
    )(ld, wts, ys, x1f, mod3, g_final)


def kernel(x, c, w_ada, b_ada, g_mix, w_in, conv_dw, conv_dw_bias, conv_ln_g, conv_ln_b,
           w_conv_out, lb_param, hgrn_norm_g, w_hgrn_out, w_out, g_ffn, w_router, b_router,
           w1, b1, w2, b2, g_final):
    b, s, d = x.shape
    assert w_ada.shape[0] == 1, "single-layer block"
    assert s % MOE_TL == 0
    t = b * s
    n_tiles = t // MOE_TL

    c_pad = jnp.zeros((8, d), F32).at[:b].set(c.astype(F32))
    mod, lb = _ada(c_pad, w_ada[0], b_ada, lb_param)
    mod3 = mod[:b].reshape(b * N_MOD, 1, d)

    p = _inproj(x, g_mix, mod3, w_in[0].astype(BF16))

    dw_pad = jnp.zeros((32, d), F32).at[:CONV_K].set(conv_dw[0])
    yc = _conv_branch(p, dw_pad, conv_dw_bias, conv_ln_g, conv_ln_b, w_conv_out[0].astype(BF16))
    yh = _hgrn_branch(p, lb, hgrn_norm_g, w_hgrn_out[0].astype(BF16))

    x1, h3, logits_t = _merge(yc, yh, x, mod3, g_ffn, w_out[0].astype(BF16),
                              w_router[0].T, b_router[0][:, None])

    ld, wts, cnt = _route(logits_t)
    xs = _gather(ld, h3)
    groups_max = t * TOP_K // MOE_GRP + n_tiles * N_EXPERTS
    n_blocks = groups_max // MOE_GPB + N_EXPERTS
    block_e, src = _block_table(cnt[:, :, 0], n_blocks)
    ys = _ffn(block_e, src, xs, w1[0], b1[0][:, None, :], w2[0], b2[0][:, None, :])
    out = _combine(ld, wts, ys, x1.reshape(t, d), mod3, g_final.reshape(1, d), s // MOE_TL)
    return out.reshape(b, s, d)
```

```python
import functools

import jax
import jax.numpy as jnp
import numpy as np
from jax import lax
from jax.experimental import pallas as pl
from jax.experimental.pallas import tpu as pltpu

F32 = jnp.float32
BF16 = jnp.bfloat16

D_MODEL = 1024
CONV_K = 31
HG_HEADS = 8
HG_DK = 128
N_EXPERTS = 32
TOP_K = 4
D_FF = 1024
SWIGLU_ALPHA = 1.702
SWIGLU_LIMIT = 7.0
MOE_BLOCK = 256
EPS = 1e-6
N_MOD = 6
COL_CA, COL_CB, COL_Q, COL_F, COL_I, COL_G, COL_GC, COL_GH = range(8)
N_COLS = 8

HG_CHUNK = 128
CONV_HALO = 32
VMEM_LIMIT = 56 * 1024 * 1024


def _sigmoid(x):
    return 1.0 / (1.0 + jnp.exp(-x))


def _params(sem, vmem=VMEM_LIMIT):
    return pltpu.CompilerParams(dimension_semantics=sem, vmem_limit_bytes=vmem)


def _ada_kernel(c_ref, w_ref, b_ref, lbp_ref, mod_ref, lb_ref):
    c = c_ref[...]
    c_act = c * _sigmoid(c)
    mod_ref[...] = jnp.dot(c_act, w_ref[...], preferred_element_type=F32,
                           precision=lax.Precision.HIGHEST) + b_ref[...]
    p = lbp_ref[...]
    e = jnp.exp(p - jnp.max(p, axis=0, keepdims=True))
    lb_ref[...] = e[0:1, :] / jnp.sum(e, axis=0, keepdims=True)


def _ada(c_pad, w_ada, b_ada, lb_param):
    nb, d = c_pad.shape
    n = w_ada.shape[1]
    tn = 1536
    return pl.pallas_call(
        _ada_kernel,
        grid=(n // tn,),
        in_specs=[
            pl.BlockSpec((nb, d), lambda j: (0, 0)),
            pl.BlockSpec((d, tn), lambda j: (0, j)),
            pl.BlockSpec((1, tn), lambda j: (0, j)),
            pl.BlockSpec(lb_param.shape, lambda j: (0, 0)),
        ],
        out_specs=[
            pl.BlockSpec((nb, tn), lambda j: (0, j)),
            pl.BlockSpec((1, d), lambda j: (0, 0)),
        ],
        out_shape=[
            jax.ShapeDtypeStruct((nb, n), F32),
            jax.ShapeDtypeStruct((1, d), F32),
        ],
        compiler_params=_params(("arbitrary",)),
        name="ada_mod",
    )(c_pad, w_ada, b_ada, lb_param)


def _inproj_kernel(x_ref, g_ref, sc_ref, sh_ref, w_ref, o_ref):
    x = x_ref[0]
    ms = jnp.mean(x * x, axis=-1, keepdims=True)
    h = x * lax.rsqrt(ms + EPS) * g_ref[...]
    h = h * (1.0 + sc_ref[0]) + sh_ref[0]
    o_ref[0] = jnp.dot(h.astype(BF16), w_ref[...], preferred_element_type=F32)


def _inproj(x, g_mix, mod3, w_in_bf):
    b, s, d = x.shape
    n = w_in_bf.shape[1]
    tm, tn = 512, 2048
    return pl.pallas_call(
        _inproj_kernel,
        grid=(n // tn, b, s // tm),
        in_specs=[
            pl.BlockSpec((1, tm, d), lambda j, bi, i: (bi, i, 0)),
            pl.BlockSpec((1, d), lambda j, bi, i: (0, 0)),
            pl.BlockSpec((1, 1, d), lambda j, bi, i: (bi * N_MOD + 1, 0, 0)),
            pl.BlockSpec((1, 1, d), lambda j, bi, i: (bi * N_MOD + 0, 0, 0)),
            pl.BlockSpec((d, tn), lambda j, bi, i: (0, j)),
        ],
        out_specs=pl.BlockSpec((1, tm, tn), lambda j, bi, i: (bi, i, j)),
        out_shape=jax.ShapeDtypeStruct((b, s, n), F32),
        compiler_params=_params(("arbitrary", "arbitrary", "arbitrary")),
        name="in_proj",
    )(x, g_mix, mod3, mod3, w_in_bf)


CONV_TS = 256
CONV_RG = 32
SUBLANES = 8
CONV_SPAN = CONV_TS + CONV_HALO
CONV_OFF = CONV_HALO - (CONV_K - 1)


def _conv_kernel(a_ref, ah_ref, b_ref, bh_ref, gc_ref, dw_ref, bias_ref, lng_ref, lnb_ref,
                 w_ref, o_ref, buf, sh, cv):
    i = pl.program_id(1)
    ts = CONV_TS
    uh = ah_ref[0] * _sigmoid(bh_ref[0])
    buf[0:CONV_HALO, :] = jnp.where(i > 0, uh, 0.0)
    buf[CONV_HALO:CONV_SPAN, :] = a_ref[0] * _sigmoid(b_ref[0])
    for s in range(1, SUBLANES):
        sh[s - 1] = buf[s:s + CONV_SPAN - SUBLANES, :]
    for r in range(ts // CONV_RG):
        base = r * CONV_RG
        acc = jnp.zeros((CONV_RG, D_MODEL), F32)
        for j in range(CONV_K):
            s = (CONV_OFF + j) % SUBLANES
            row = base + CONV_OFF + j - s
            src = buf if s == 0 else sh.at[s - 1]
            acc = acc + dw_ref[j:j + 1, :] * src[row:row + CONV_RG, :]
        cv[base:base + CONV_RG, :] = acc + bias_ref[...]
    u = cv[...]
    mu = jnp.mean(u, axis=-1, keepdims=True)
    uc = u - mu
    var = jnp.mean(uc * uc, axis=-1, keepdims=True)
    y = uc * lax.rsqrt(var + EPS) * lng_ref[...] + lnb_ref[...]
    y = y * _sigmoid(y)
    yc = jnp.dot(y.astype(BF16), w_ref[...], preferred_element_type=F32)
    o_ref[0] = _sigmoid(gc_ref[0]) * yc


def _conv_branch(p, dw_pad, bias, ln_g, ln_b, w_bf):
    b, s, _ = p.shape
    d = D_MODEL
    ts = CONV_TS
    hb = ts // CONV_HALO

    def halo_map(col):
        return lambda bi, i: (bi, jnp.maximum(i * hb - 1, 0), col)

    vec = pl.BlockSpec((1, d), lambda bi, i: (0, 0))
    return pl.pallas_call(
        _conv_kernel,
        grid=(b, s // ts),
        in_specs=[
            pl.BlockSpec((1, ts, d), lambda bi, i: (bi, i, COL_CA)),
            pl.BlockSpec((1, CONV_HALO, d), halo_map(COL_CA)),
            pl.BlockSpec((1, ts, d), lambda bi, i: (bi, i, COL_CB)),
            pl.BlockSpec((1, CONV_HALO, d), halo_map(COL_CB)),
            pl.BlockSpec((1, ts, d), lambda bi, i: (bi, i, COL_GC)),
            pl.BlockSpec(dw_pad.shape, lambda bi, i: (0, 0)),
            vec, vec, vec,
            pl.BlockSpec((d, d), lambda bi, i: (0, 0)),
        ],
        out_specs=pl.BlockSpec((1, ts, d), lambda bi, i: (bi, i, 0)),
        out_shape=jax.ShapeDtypeStruct((b, s, d), F32),
        scratch_shapes=[pltpu.VMEM((CONV_SPAN, d), F32),
                        pltpu.VMEM((SUBLANES - 1, CONV_SPAN - SUBLANES, d), F32),
                        pltpu.VMEM((ts, d), F32)],
        compiler_params=_params(("arbitrary", "arbitrary")),
        name="conv_branch",
    )(p, p, p, p, p, dw_pad, bias, ln_g, ln_b, w_bf)


def _hgrn_levels():
    c = HG_CHUNK
    levels = []
    m = c // 2
    while m >= 1:
        levels.append(m)
        m //= 2
    return levels


def _hgrn_sum_matrix():
    c = HG_CHUNK
    r = np.arange(c)[:, None]
    u = np.arange(c)[None, :]
    mats = [(u <= r), (u > r)]
    for m in _hgrn_levels():
        start = (r // (2 * m)) * (2 * m)
        mid = start + m
        upper = (r - start) >= m
        mq = upper & (u >= mid) & (u <= r)
        mk = (~upper) & (u > r) & (u < mid)
        mats.append(mq | mk)
    return np.concatenate(mats, axis=0).astype(np.float32)


def _hgrn_kernel(q_ref, z_ref, v_ref, og_ref, gh_ref, lb_ref, ng_ref, msum_ref, w_ref, o_ref,
                 st, obuf):
    c = HG_CHUNK
    dk = HG_DK
    levels = _hgrn_levels()

    @pl.when(pl.program_id(1) == 0)
    def _():
        st[...] = jnp.zeros_like(st)

    z = z_ref[0]
    lb = lb_ref[...]
    sig = _sigmoid(z)
    f = lb + (1.0 - lb) * sig
    logf = jnp.log(f)
    kk = (1.0 - lb) * (1.0 - sig)
    q = q_ref[0] * (dk ** -0.5)
    v = v_ref[0]
    og = og_ref[0]

    hi = logf.astype(BF16)
    lo = (logf - hi.astype(F32)).astype(BF16)
    msum = msum_ref[...]
    seg = (jnp.dot(msum, hi, preferred_element_type=F32)
           + jnp.dot(msum, lo, preferred_element_type=F32))

    g_inc = seg[0:c]
    g_sfx = seg[c:2 * c]
    q_st = (q * jnp.exp(g_inc)).astype(BF16)
    k_st = (kk * jnp.exp(g_sfx)).astype(BF16)
    dec_all = jnp.exp(g_inc[c - 1:c, :])
    v_bf = v.astype(BF16)
    q_bf = q.astype(BF16)
    k_bf = kk.astype(BF16)

    row = lax.broadcasted_iota(jnp.int32, (c, c), 0)
    col = lax.broadcasted_iota(jnp.int32, (c, c), 1)
    rr = lax.broadcasted_iota(jnp.int32, (c, 1), 0)

    qs, ks, masks = [], [], []
    for li, m in enumerate(levels):
        e = jnp.exp(seg[(2 + li) * c:(3 + li) * c])
        upper = (rr & m) != 0
        qs.append(jnp.where(upper, q * e, 0.0).astype(BF16))
        ks.append(jnp.where(upper, 0.0, kk * e).astype(BF16))
        sh = int(np.log2(2 * m))
        masks.append((row >> sh) == (col >> sh))
    diag = row == col

    nt = (((1,), (1,)), ((), ()))
    tn = (((0,), (0,)), ((), ()))
    for h in range(HG_HEADS):
        sl = slice(h * dk, (h + 1) * dk)
        a = jnp.where(diag, lax.dot_general(q_bf[:, sl], k_bf[:, sl], nt,
                                            preferred_element_type=F32), 0.0)
        for li in range(len(levels)):
            a = a + jnp.where(masks[li],
                              lax.dot_general(qs[li][:, sl], ks[li][:, sl], nt,
                                              preferred_element_type=F32), 0.0)
        s_t = st[h]
        o = jnp.dot(a.astype(BF16), v_bf[:, sl], preferred_element_type=F32)
        o = o + lax.dot_general(q_st[:, sl], s_t.astype(BF16), nt, preferred_element_type=F32)
        st[h] = s_t * dec_all[:, sl] + lax.dot_general(v_bf[:, sl], k_st[:, sl], tn,
                                                       preferred_element_type=F32)
        ms = jnp.mean(o * o, axis=-1, keepdims=True)
        o = o * lax.rsqrt(ms + EPS) * ng_ref[...]
        g = og[:, sl]
        obuf[:, sl] = (o * (g * _sigmoid(g))).astype(BF16)

    y = jnp.dot(obuf[...], w_ref[...], preferred_element_type=F32)
    o_ref[0] = _sigmoid(gh_ref[0]) * y


def _hgrn_branch(p, lb, norm_g, w_bf):
    b, s, _ = p.shape
    d = D_MODEL
    c = HG_CHUNK
    msum = jnp.asarray(_hgrn_sum_matrix(), dtype=BF16)

    def col_spec(col):
        return pl.BlockSpec((1, c, d), lambda bi, i: (bi, i, col))

    return pl.pallas_call(
        _hgrn_kernel,
        grid=(b, s // c),
        in_specs=[
            col_spec(COL_Q), col_spec(COL_F), col_spec(COL_I), col_spec(COL_G), col_spec(COL_GH),
            pl.BlockSpec((1, d), lambda bi, i: (0, 0)),
            pl.BlockSpec((1, HG_DK), lambda bi, i: (0, 0)),
            pl.BlockSpec(msum.shape, lambda bi, i: (0, 0)),
            pl.BlockSpec((d, d), lambda bi, i: (0, 0)),
        ],
        out_specs=pl.BlockSpec((1, c, d), lambda bi, i: (bi, i, 0)),
        out_shape=jax.ShapeDtypeStruct((b, s, d), F32),
        scratch_shapes=[pltpu.VMEM((HG_HEADS, HG_DK, HG_DK), F32), pltpu.VMEM((c, d), BF16)],
        compiler_params=_params(("arbitrary", "arbitrary")),
        name="hgrn_branch",
    )(p, p, p, p, p, lb, norm_g, msum, w_bf)


ROW_TILE = (8, 128)


def _rows_to_tiles(rows):
    st = jnp.stack([rows[:, j * 128:(j + 1) * 128] for j in range(ROW_TILE[0])], axis=0)
    return pltpu.einshape("jrl->rjl", st)


def _tiles_to_rows(tiles):
    y = pltpu.einshape("rjl->jrl", tiles)
    return jnp.concatenate([y[j] for j in range(ROW_TILE[0])], axis=-1)


_NT = (((1,), (1,)), ((), ()))


def _merge_kernel(yc_ref, yh_ref, x_ref, ga_ref, sc_ref, sh_ref, g_ref, w_ref, wrt_ref, br_ref,
                  x1_ref, h3_ref, lg_ref):
    merged = (yc_ref[0] + yh_ref[0]).astype(BF16)
    x1 = x_ref[0] + ga_ref[0] * jnp.dot(merged, w_ref[...], preferred_element_type=F32)
    x1_ref[0] = x1
    ms = jnp.mean(x1 * x1, axis=-1, keepdims=True)
    h2 = x1 * lax.rsqrt(ms + EPS) * g_ref[...]
    h2 = h2 * (1.0 + sc_ref[0]) + sh_ref[0]
    h3_ref[...] = _rows_to_tiles(h2)
    lg_ref[...] = lax.dot_general(wrt_ref[...], h2, _NT, preferred_element_type=F32,
                                  precision=lax.Precision.HIGHEST) + br_ref[...]


def _merge(yc, yh, x, mod3, g_ffn, w_out_bf, w_router_t, b_router_col):
    b, s, d = x.shape
    tm = 512
    nt = s // tm
    ne = w_router_t.shape[0]

    def mod_spec(k):
        return pl.BlockSpec((1, 1, d), lambda bi, i: (bi * N_MOD + k, 0, 0))

    tile = pl.BlockSpec((1, tm, d), lambda bi, i: (bi, i, 0))
    return pl.pallas_call(
        _merge_kernel,
        grid=(b, nt),
        in_specs=[
            tile, tile, tile, mod_spec(2), mod_spec(4), mod_spec(3),
            pl.BlockSpec((1, d), lambda bi, i: (0, 0)),
            pl.BlockSpec((d, d), lambda bi, i: (0, 0)),
            pl.BlockSpec((ne, d), lambda bi, i: (0, 0)),
            pl.BlockSpec((ne, 1), lambda bi, i: (0, 0)),
        ],
        out_specs=[
            tile,
            pl.BlockSpec((tm,) + ROW_TILE, lambda bi, i: (bi * nt + i, 0, 0)),
            pl.BlockSpec((ne, tm), lambda bi, i: (0, bi * nt + i)),
        ],
        out_shape=[
            jax.ShapeDtypeStruct((b, s, d), F32),
            jax.ShapeDtypeStruct((b * s,) + ROW_TILE, F32),
            jax.ShapeDtypeStruct((ne, b * s), F32),
        ],
        compiler_params=_params(("arbitrary", "arbitrary")),
        name="merge_router",
    )(yc, yh, x, mod3, mod3, mod3, g_ffn, w_out_bf, w_router_t, b_router_col)


MOE_TL = 1024
MOE_GRP = 16
MOE_CAP = MOE_TL * TOP_K + N_EXPERTS * MOE_GRP
MOE_GPB = MOE_BLOCK // MOE_GRP
RELAY_ROWS = 256
assert MOE_CAP % RELAY_ROWS == 0 and MOE_TL % RELAY_ROWS == 0


def _route_kernel(lg_ref, u_ref, ld_ref, w_ref, cnt_ref):
    tl = MOE_TL
    l = lg_ref[...]
    eio = lax.broadcasted_iota(jnp.int32, l.shape, 0)
    vals, hots = [], []
    for _ in range(TOP_K):
        m = jnp.max(l, axis=0, keepdims=True)
        idx = jnp.min(jnp.where(l == m, eio, N_EXPERTS), axis=0, keepdims=True)
        hot = eio == idx
        vals.append(m)
        hots.append(hot)
        l = jnp.where(hot, -jnp.inf, l)
    ex = [jnp.exp(v - vals[0]) for v in vals]
    den = ex[0] + ex[1] + ex[2] + ex[3]
    cnt = hots[0].astype(F32)
    for k in range(1, TOP_K):
        cnt = cnt + hots[k].astype(F32)
    prefix = jnp.dot(cnt.astype(BF16), u_ref[...], preferred_element_type=F32)
    n_e = jnp.sum(cnt, axis=1, keepdims=True)
    pad_e = jnp.floor((n_e + (MOE_GRP - 1)) * (1.0 / MOE_GRP)) * MOE_GRP
    scan = jnp.broadcast_to(pad_e, (N_EXPERTS, 128))
    ei = lax.broadcasted_iota(jnp.int32, scan.shape, 0)
    dist = 1
    while dist < N_EXPERTS:
        scan = scan + jnp.where(ei >= dist, pltpu.roll(scan, dist, axis=0), 0.0)
        dist *= 2
    base = prefix + (scan[:, 0:1] - pad_e)
    for k in range(TOP_K):
        dest = jnp.sum(jnp.where(hots[k], base, 0.0), axis=0, keepdims=True)
        ld_ref[0, :, k * tl:(k + 1) * tl] = dest.astype(jnp.int32)
        w_ref[0, :, k * tl:(k + 1) * tl] = ex[k] / den
    cnt_ref[0] = jnp.broadcast_to(n_e, (N_EXPERTS, 128)).astype(jnp.int32)


def _route(logits_t):
    ne, t = logits_t.shape
    tl = MOE_TL
    n_tiles = t // tl
    upper = jnp.asarray(np.triu(np.ones((tl, tl), np.float32), k=1), dtype=BF16)
    slot = pl.BlockSpec((1, 1, TOP_K * tl), lambda i: (i, 0, 0))
    return pl.pallas_call(
        _route_kernel,
        grid=(n_tiles,),
        in_specs=[pl.BlockSpec((ne, tl), lambda i: (0, i)),
                  pl.BlockSpec((tl, tl), lambda i: (0, 0))],
        out_specs=[slot, slot, pl.BlockSpec((1, ne, 128), lambda i: (i, 0, 0))],
        out_shape=[
            jax.ShapeDtypeStruct((n_tiles, 1, TOP_K * tl), jnp.int32),
            jax.ShapeDtypeStruct((n_tiles, 1, TOP_K * tl), F32),
            jax.ShapeDtypeStruct((n_tiles, ne, 128), jnp.int32),
        ],
        compiler_params=_params(("arbitrary",)),
        name="route",
    )(logits_t, upper)


def _gather_kernel(ld_ref, h3_ref, xs_ref, xs3):
    tl = MOE_TL
    xs3[...] = jnp.zeros_like(xs3)

    def push(t, carry):
        tile = h3_ref[t]
        for k in range(TOP_K):
            xs3[ld_ref[0, 0, k * tl + t]] = tile
        return carry

    @pl.when(pl.program_id(0) < pl.num_programs(0) - 1)
    def _():
        lax.fori_loop(0, tl, push, 0, unroll=4)

    def relay(r, carry):
        r0 = pl.multiple_of(r * RELAY_ROWS, RELAY_ROWS)
        xs_ref[pl.ds(r0, RELAY_ROWS), :] = _tiles_to_rows(xs3[pl.ds(r0, RELAY_ROWS)]).astype(BF16)
        return carry

    lax.fori_loop(0, MOE_CAP // RELAY_ROWS, relay, 0)


def _gather(ld, h3):
    t = h3.shape[0]
    tl = MOE_TL
    n_tiles = t // tl
    last = n_tiles - 1
    return pl.pallas_call(
        _gather_kernel,
        grid=(n_tiles + 1,),
        in_specs=[
            pl.BlockSpec((1, 1, TOP_K * tl), lambda i: (jnp.minimum(i, last), 0, 0),
                         memory_space=pltpu.SMEM),
            pl.BlockSpec((tl,) + ROW_TILE, lambda i: (jnp.minimum(i, last), 0, 0)),
        ],
        out_specs=pl.BlockSpec((MOE_CAP, D_MODEL), lambda i: (i, 0)),
        out_shape=jax.ShapeDtypeStruct(((n_tiles + 1) * MOE_CAP, D_MODEL), BF16),
        scratch_shapes=[pltpu.VMEM((MOE_CAP,) + ROW_TILE, F32)],
        compiler_params=_params(("arbitrary",)),
        name="moe_gather",
    )(ld, h3)


def _table_kernel(cnt_ref, be_ref, src_ref, toff):
    n_tiles = cnt_ref.shape[0]
    n_slots = src_ref.shape[0]
    groups_per_tile = MOE_CAP // MOE_GRP
    step = 8

    for i in range(n_tiles):
        toff[i] = 0

    def per_expert(e, pos):
        def per_tile(i, p):
            g = (cnt_ref[i, e] + (MOE_GRP - 1)) // MOE_GRP
            base = i * groups_per_tile + toff[i]

            def put(j, c):
                for u in range(step):
                    src_ref[p + j * step + u] = base + j * step + u
                return c

            lax.fori_loop(0, (g + (step - 1)) // step, put, 0)
            toff[i] = toff[i] + g
            return p + g

        end = lax.fori_loop(0, n_tiles, per_tile, pos)
        new_pos = (end + (MOE_GPB - 1)) // MOE_GPB * MOE_GPB
        for u in range(MOE_GPB):
            src_ref[end + u] = -1

        def put_e(b, c):
            be_ref[b] = e
            return c

        lax.fori_loop(pos // MOE_GPB, new_pos // MOE_GPB, put_e, 0)
        return new_pos

    used = lax.fori_loop(0, N_EXPERTS, per_expert, 0)

    def tail(b, c):
        be_ref[b] = N_EXPERTS - 1
        for u in range(MOE_GPB):
            src_ref[b * MOE_GPB + u] = -1
        return c

    lax.fori_loop(used // MOE_GPB, n_slots // MOE_GPB, tail, 0)


def _block_table(cnt, n_blocks):
    smem = pl.BlockSpec(memory_space=pltpu.SMEM)
    return pl.pallas_call(
        _table_kernel,
        in_specs=[smem],
        out_specs=[smem, smem],
        out_shape=[jax.ShapeDtypeStruct((n_blocks,), jnp.int32),
                   jax.ShapeDtypeStruct((n_blocks * MOE_GPB,), jnp.int32)],
        scratch_shapes=[pltpu.SMEM((cnt.shape[0],), jnp.int32)],
        name="moe_block_table",
    )(cnt)


def _ffn_kernel(be_ref, src_ref, xs_hbm, w1_ref, b1_ref, w2_ref, b2_ref, ys_hbm,
                xbuf, ybuf, w1s, w2s, sem_in, sem_out, *, spare_grp):
    i = pl.program_id(0)
    nb = pl.num_programs(0)
    slot = i % 2

    def live(blk):
        return src_ref[blk * MOE_GPB] >= 0

    def group_rows(grp):
        return pl.ds(pl.multiple_of(grp * MOE_GRP, MOE_GRP), MOE_GRP)

    def start_in(blk, sl):
        for g in range(MOE_GPB):
            grp = src_ref[blk * MOE_GPB + g]
            grp = jnp.where(grp >= 0, grp, spare_grp)
            pltpu.make_async_copy(xs_hbm.at[group_rows(grp), :],
                                  xbuf.at[sl, pl.ds(g * MOE_GRP, MOE_GRP), :], sem_in.at[sl]).start()

    def start_out(blk, sl):
        for g in range(MOE_GPB):
            grp = src_ref[blk * MOE_GPB + g]
            grp = jnp.where(grp >= 0, grp, spare_grp + 1 + sl * MOE_GPB + g)
            pltpu.make_async_copy(ybuf.at[sl, pl.ds(g * MOE_GRP, MOE_GRP), :],
                                  ys_hbm.at[group_rows(grp), :], sem_out.at[sl]).start()

    def wait_in(sl):
        pltpu.make_async_copy(xs_hbm.at[pl.ds(0, MOE_BLOCK), :], xbuf.at[sl], sem_in.at[sl]).wait()

    def wait_out(sl):
        pltpu.make_async_copy(ybuf.at[sl], ys_hbm.at[pl.ds(0, MOE_BLOCK), :], sem_out.at[sl]).wait()

    @pl.when((i == 0) & live(0))
    def _():
        start_in(0, 0)

    nxt = jnp.minimum(i + 1, nb - 1)

    @pl.when((i + 1 < nb) & live(nxt))
    def _():
        start_in(nxt, 1 - slot)

    @pl.when((i >= 2) & live(jnp.maximum(i - 2, 0)))
    def _():
        wait_out(slot)

    prev = be_ref[jnp.maximum(i - 1, 0)]

    @pl.when((i == 0) | (be_ref[i] != prev))
    def _():
        w1s[...] = w1_ref[0].astype(BF16)
        w2s[...] = w2_ref[0].astype(BF16)

    @pl.when(live(i))
    def _():
        wait_in(slot)
        u = jnp.dot(xbuf[slot], w1s[...], preferred_element_type=F32) + b1_ref[0]
        gl = jnp.minimum(u[:, :D_FF], SWIGLU_LIMIT)
        lin = jnp.clip(u[:, D_FF:], -SWIGLU_LIMIT, SWIGLU_LIMIT)
        act = gl * _sigmoid(SWIGLU_ALPHA * gl) * (lin + 1.0)
        y = jnp.dot(act.astype(BF16), w2s[...], preferred_element_type=F32) + b2_ref[0]
        ybuf[slot] = y.astype(BF16)
        start_out(i, slot)

    @pl.when(i == nb - 1)
    def _():
        @pl.when(live(i))
        def _():
            wait_out(slot)

        @pl.when((i >= 1) & live(jnp.maximum(i - 1, 0)))
        def _():
            wait_out(1 - slot)


def _ffn(block_e, src, xs, w1, b1, w2, b2):
    r, d = xs.shape
    nb = block_e.shape[0]
    spare_grp = (r - MOE_CAP) // MOE_GRP
    assert 1 + 2 * MOE_GPB <= MOE_CAP // MOE_GRP
    grid_spec = pltpu.PrefetchScalarGridSpec(
        num_scalar_prefetch=2,
        grid=(nb,),
        in_specs=[
            pl.BlockSpec(memory_space=pl.ANY),
            pl.BlockSpec((1, d, 2 * D_FF), lambda i, be, sr: (be[i], 0, 0)),
            pl.BlockSpec((1, 1, 2 * D_FF), lambda i, be, sr: (be[i], 0, 0)),
            pl.BlockSpec((1, D_FF, d), lambda i, be, sr: (be[i], 0, 0)),
            pl.BlockSpec((1, 1, d), lambda i, be, sr: (be[i], 0, 0)),
        ],
        out_specs=pl.BlockSpec(memory_space=pl.ANY),
        scratch_shapes=[
            pltpu.VMEM((2, MOE_BLOCK, d), BF16), pltpu.VMEM((2, MOE_BLOCK, d), BF16),
            pltpu.VMEM((d, 2 * D_FF), BF16), pltpu.VMEM((D_FF, d), BF16),
            pltpu.SemaphoreType.DMA((2,)), pltpu.SemaphoreType.DMA((2,)),
        ],
    )
    return pl.pallas_call(
        functools.partial(_ffn_kernel, spare_grp=spare_grp),
        grid_spec=grid_spec,
        out_shape=jax.ShapeDtypeStruct((r, d), BF16),
        input_output_aliases={2: 0},
        compiler_params=_params(("arbitrary",)),
        name="expert_ffn",
    )(block_e, src, xs, w1, b1, w2, b2)


def _combine_kernel(ld_ref, w_ref, ys_ref, x1_ref, ga_ref, g_ref, o_ref, y3, o3):
    tl = MOE_TL

    def relay(r, carry):
        r0 = pl.multiple_of(r * RELAY_ROWS, RELAY_ROWS)
        y3[pl.ds(r0, RELAY_ROWS)] = _rows_to_tiles(ys_ref[pl.ds(r0, RELAY_ROWS), :].astype(F32))
        return carry

    lax.fori_loop(0, MOE_CAP // RELAY_ROWS, relay, 0)

    def pull(t, carry):
        acc = w_ref[0, 0, t] * y3[ld_ref[0, 0, t]]
        for k in range(1, TOP_K):
            acc = acc + w_ref[0, 0, k * tl + t] * y3[ld_ref[0, 0, k * tl + t]]
        o3[t] = acc
        return carry

    lax.fori_loop(0, tl, pull, 0, unroll=4)

    def finish(r, carry):
        r0 = pl.multiple_of(r * RELAY_ROWS, RELAY_ROWS)
        x2 = x1_ref[pl.ds(r0, RELAY_ROWS), :] + ga_ref[0] * _tiles_to_rows(o3[pl.ds(r0, RELAY_ROWS)])
        ms = jnp.mean(x2 * x2, axis=-1, keepdims=True)
        o_ref[pl.ds(r0, RELAY_ROWS), :] = x2 * lax.rsqrt(ms + EPS) * g_ref[...]
        return carry

    lax.fori_loop(0, tl // RELAY_ROWS, finish, 0)


def _combine(ld, wts, ys, x1f, mod3, g_final, tiles_per_batch):
    t, d = x1f.shape
    tl = MOE_TL
    n_tiles = t // tl
    smem = functools.partial(pl.BlockSpec, (1, 1, TOP_K * tl), lambda i: (i, 0, 0),
                             memory_space=pltpu.SMEM)
    return pl.pallas_call(
        _combine_kernel,
        grid=(n_tiles,),
        in_specs=[
            smem(), smem(),
            pl.BlockSpec((MOE_CAP, d), lambda i: (i, 0), pipeline_mode=pl.Buffered(1)),
            pl.BlockSpec((tl, d), lambda i: (i, 0)),
            pl.BlockSpec((1, 1, d), lambda i: ((i // tiles_per_batch) * N_MOD + 5, 0, 0)),
            pl.BlockSpec((1, d), lambda i: (0, 0)),
        ],
        out_specs=pl.BlockSpec((tl, d), lambda i: (i, 0)),
        out_shape=jax.ShapeDtypeStruct((t, d), F32),
        scratch_shapes=[pltpu.VMEM((MOE_CAP,) + ROW_TILE, F32), pltpu.VMEM((tl,) + ROW_TILE, F32)],
        compiler_params=_params(("arbitrary",)),
        name="moe_combine",
    )(ld, wts, ys, x1f, mod3, g_final)


def kernel(x, c, w_ada, b_ada, g_mix, w_in, conv_dw, conv_dw_bias, conv_ln_g, conv_ln_b,
           w_conv_out, lb_param, hgrn_norm_g, w_hgrn_out, w_out, g_ffn, w_router, b_router,
           w1, b1, w2, b2, g_final):
    b, s, d = x.shape
    assert w_ada.shape[0] == 1, "single-layer block"
    assert s % MOE_TL == 0
    t = b * s
    n_tiles = t // MOE_TL

    c_pad = jnp.zeros((8, d), F32).at[:b].set(c.astype(F32))
    mod, lb = _ada(c_pad, w_ada[0], b_ada, lb_param)
    mod3 = mod[:b].reshape(b * N_MOD, 1, d)

    p = _inproj(x, g_mix, mod3, w_in[0].astype(BF16))

    dw_pad = jnp.zeros((32, d), F32).at[:CONV_K].set(conv_dw[0])
    yc = _conv_branch(p, dw_pad, conv_dw_bias, conv_ln_g, conv_ln_b, w_conv_out[0].astype(BF16))
    yh = _hgrn_branch(p, lb, hgrn_norm_g, w_hgrn_out[0].astype(BF16))

    x1, h3, logits_t = _merge(yc, yh, x, mod3, g_ffn, w_out[0].astype(BF16),
                              w_router[0].T, b_router[0][:, None])

    ld, wts, cnt = _route(logits_t)
    xs = _gather(ld, h3)
    groups_max = t * TOP_K // MOE_GRP + n_tiles * N_EXPERTS
    n_blocks = groups_max // MOE_GPB + N_EXPERTS
    block_e, src = _block_table(cnt[:, :, 0], n_blocks)
    ys = _ffn(block_e, src, xs, w1[0], b1[0][:, None, :], w2[0], b2[0][:, None, :])
    out = _combine(ld, wts, ys, x1.reshape(t, d), mod3, g_final.reshape(1, d), s // MOE_TL)
    return out.reshape(b, s, d)
```

```python
import functools

import jax
import jax.numpy as jnp
import numpy as np
from jax import lax
from jax.experimental import pallas as pl
from jax.experimental.pallas import tpu as pltpu

F32 = jnp.float32
BF16 = jnp.bfloat16

D_MODEL = 1024
CONV_K = 31
HG_HEADS = 8
HG_DK = 128
N_EXPERTS = 32
TOP_K = 4
D_FF = 1024
SWIGLU_ALPHA = 1.702
SWIGLU_LIMIT = 7.0
MOE_BLOCK = 256
EPS = 1e-6
N_MOD = 6
COL_CA, COL_CB, COL_Q, COL_F, COL_I, COL_G, COL_GC, COL_GH = range(8)
N_COLS = 8

HG_CHUNK = 128
CONV_HALO = 32
VMEM_LIMIT = 56 * 1024 * 1024


def _sigmoid(x):
    return 1.0 / (1.0 + jnp.exp(-x))


def _params(sem, vmem=VMEM_LIMIT):
    return pltpu.CompilerParams(dimension_semantics=sem, vmem_limit_bytes=vmem)


def _ada_kernel(c_ref, w_ref, b_ref, lbp_ref, mod_ref, lb_ref):
    c = c_ref[...]
    c_act = c * _sigmoid(c)
    mod_ref[...] = jnp.dot(c_act, w_ref[...], preferred_element_type=F32,
                           precision=lax.Precision.HIGHEST) + b_ref[...]
    p = lbp_ref[...]
    e = jnp.exp(p - jnp.max(p, axis=0, keepdims=True))
    lb_ref[...] = e[0:1, :] / jnp.sum(e, axis=0, keepdims=True)


def _ada(c_pad, w_ada, b_ada, lb_param):
    nb, d = c_pad.shape
    n = w_ada.shape[1]
    tn = 1536
    return pl.pallas_call(
        _ada_kernel,
        grid=(n // tn,),
        in_specs=[
            pl.BlockSpec((nb, d), lambda j: (0, 0)),
            pl.BlockSpec((d, tn), lambda j: (0, j)),
            pl.BlockSpec((1, tn), lambda j: (0, j)),
            pl.BlockSpec(lb_param.shape, lambda j: (0, 0)),
        ],
        out_specs=[
            pl.BlockSpec((nb, tn), lambda j: (0, j)),
            pl.BlockSpec((1, d), lambda j: (0, 0)),
        ],
        out_shape=[
            jax.ShapeDtypeStruct((nb, n), F32),
            jax.ShapeDtypeStruct((1, d), F32),
        ],
        compiler_params=_params(("arbitrary",)),
        name="ada_mod",
    )(c_pad, w_ada, b_ada, lb_param)


def _inproj_kernel(x_ref, g_ref, sc_ref, sh_ref, w_ref, o_ref):
    x = x_ref[0]
    ms = jnp.mean(x * x, axis=-1, keepdims=True)
    h = x * lax.rsqrt(ms + EPS) * g_ref[...]
    h = h * (1.0 + sc_ref[0]) + sh_ref[0]
    o_ref[0] = jnp.dot(h.astype(BF16), w_ref[...], preferred_element_type=F32)


def _inproj(x, g_mix, mod3, w_in_bf):
    b, s, d = x.shape
    n = w_in_bf.shape[1]
    tm, tn = 512, 2048
    return pl.pallas_call(
        _inproj_kernel,
        grid=(n // tn, b, s // tm),
        in_specs=[
            pl.BlockSpec((1, tm, d), lambda j, bi, i: (bi, i, 0)),
            pl.BlockSpec((1, d), lambda j, bi, i: (0, 0)),
            pl.BlockSpec((1, 1, d), lambda j, bi, i: (bi * N_MOD + 1, 0, 0)),
            pl.BlockSpec((1, 1, d), lambda j, bi, i: (bi * N_MOD + 0, 0, 0)),
            pl.BlockSpec((d, tn), lambda j, bi, i: (0, j)),
        ],
        out_specs=pl.BlockSpec((1, tm, tn), lambda j, bi, i: (bi, i, j)),
        out_shape=jax.ShapeDtypeStruct((b, s, n), F32),
        compiler_params=_params(("arbitrary", "arbitrary", "arbitrary")),
        name="in_proj",
    )(x, g_mix, mod3, mod3, w_in_bf)


CONV_TS = 256
CONV_RG = 32
SUBLANES = 8
CONV_SPAN = CONV_TS + CONV_HALO
CONV_OFF = CONV_HALO - (CONV_K - 1)


def _conv_kernel(a_ref, ah_ref, b_ref, bh_ref, dw_ref, bias_ref, lng_ref, lnb_ref,
                 o_ref, buf, sh, cv):
    i = pl.program_id(1)
    ts = CONV_TS
    uh = ah_ref[0] * _sigmoid(bh_ref[0])
    buf[0:CONV_HALO, :] = jnp.where(i > 0, uh, 0.0)
    buf[CONV_HALO:CONV_SPAN, :] = a_ref[0] * _sigmoid(b_ref[0])
    for s in range(1, SUBLANES):
        sh[s - 1] = buf[s:s + CONV_SPAN - SUBLANES, :]
    for r in range(ts // CONV_RG):
        base = r * CONV_RG
        acc = jnp.zeros((CONV_RG, D_MODEL), F32)
        for j in range(CONV_K):
            s = (CONV_OFF + j) % SUBLANES
            row = base + CONV_OFF + j - s
            src = buf if s == 0 else sh.at[s - 1]
            acc = acc + dw_ref[j:j + 1, :] * src[row:row + CONV_RG, :]
        cv[base:base + CONV_RG, :] = acc + bias_ref[...]
    u = cv[...]
    mu = jnp.mean(u, axis=-1, keepdims=True)
    uc = u - mu
    var = jnp.mean(uc * uc, axis=-1, keepdims=True)
    y = uc * lax.rsqrt(var + EPS) * lng_ref[...] + lnb_ref[...]
    o_ref[0] = (y * _sigmoid(y)).astype(BF16)


def _conv_branch(p, dw_pad, bias, ln_g, ln_b):
    b, s, _ = p.shape
    d = D_MODEL
    ts = CONV_TS
    hb = ts // CONV_HALO

    def halo_map(col):
        return lambda bi, i: (bi, jnp.maximum(i * hb - 1, 0), col)

    vec = pl.BlockSpec((1, d), lambda bi, i: (0, 0))
    return pl.pallas_call(
        _conv_kernel,
        grid=(b, s // ts),
        in_specs=[
            pl.BlockSpec((1, ts, d), lambda bi, i: (bi, i, COL_CA)),
            pl.BlockSpec((1, CONV_HALO, d), halo_map(COL_CA)),
            pl.BlockSpec((1, ts, d), lambda bi, i: (bi, i, COL_CB)),
            pl.BlockSpec((1, CONV_HALO, d), halo_map(COL_CB)),
            pl.BlockSpec(dw_pad.shape, lambda bi, i: (0, 0)),
            vec, vec, vec,
        ],
        out_specs=pl.BlockSpec((1, ts, d), lambda bi, i: (bi, i, 0)),
        out_shape=jax.ShapeDtypeStruct((b, s, d), BF16),
        scratch_shapes=[pltpu.VMEM((CONV_SPAN, d), F32),
                        pltpu.VMEM((SUBLANES - 1, CONV_SPAN - SUBLANES, d), F32),
                        pltpu.VMEM((ts, d), F32)],
        compiler_params=_params(("arbitrary", "arbitrary")),
        name="conv_branch",
    )(p, p, p, p, dw_pad, bias, ln_g, ln_b)


def _hgrn_levels():
    c = HG_CHUNK
    levels = []
    m = c // 2
    while m >= 1:
        levels.append(m)
        m //= 2
    return levels


def _level_exponent(g_inc, logf, m, row):
    c, d = g_inc.shape
    upper = (row & m) != 0
    if m == 1:
        return upper, jnp.where(upper, logf, 0.0)
    if m >= SUBLANES:
        parts = [jnp.broadcast_to(g_inc[b * 2 * m + m - 1:b * 2 * m + m, :], (2 * m, d))
                 for b in range(c // (2 * m))]
        gref = parts[0] if len(parts) == 1 else jnp.concatenate(parts, axis=0)
    else:
        g3 = g_inc.reshape(c // SUBLANES, SUBLANES, d)

        def bcast(j):
            return jnp.broadcast_to(g3[:, j:j + 1, :], g3.shape).reshape(c, d)

        if 2 * m == SUBLANES:
            gref = bcast(m - 1)
        else:
            assert 4 * m == SUBLANES
            gref = jnp.where((row & (SUBLANES - 1)) < 2 * m, bcast(m - 1), bcast(3 * m - 1))
    return upper, jnp.where(upper, g_inc - gref, gref - g_inc)


def _hgrn_kernel(q_ref, z_ref, v_ref, og_ref, lb_ref, ng_ref, tril_ref, o_ref, st):
    c = HG_CHUNK
    dk = HG_DK
    levels = _hgrn_levels()

    @pl.when(pl.program_id(1) == 0)
    def _():
        st[...] = jnp.zeros_like(st)

    z = z_ref[0]
    lb = lb_ref[...]
    sig = _sigmoid(z)
    f = lb + (1.0 - lb) * sig
    logf = jnp.log(f)
    kk = (1.0 - lb) * (1.0 - sig)
    q = q_ref[0] * (dk ** -0.5)
    v = v_ref[0]
    og = og_ref[0]

    hi = logf.astype(BF16)
    lo = (logf - hi.astype(F32)).astype(BF16)
    tril = tril_ref[...]
    g_inc = (jnp.dot(tril, hi, preferred_element_type=F32)
             + jnp.dot(tril, lo, preferred_element_type=F32))
    g_last = g_inc[c - 1:c, :]
    q_st = (q * jnp.exp(g_inc)).astype(BF16)
    k_st = (kk * jnp.exp(g_last - g_inc)).astype(BF16)
    dec_all = jnp.exp(g_last)
    v_bf = v.astype(BF16)
    q_bf = q.astype(BF16)
    k_bf = kk.astype(BF16)

    row = lax.broadcasted_iota(jnp.int32, (c, c), 0)
    col = lax.broadcasted_iota(jnp.int32, (c, c), 1)
    rr = lax.broadcasted_iota(jnp.int32, (c, 1), 0)

    qs, ks, masks = [], [], []
    for m in levels:
        upper, ex = _level_exponent(g_inc, logf, m, rr)
        e = jnp.exp(ex)
        qs.append(jnp.where(upper, q * e, 0.0).astype(BF16))
        ks.append(jnp.where(upper, 0.0, kk * e).astype(BF16))
        sh = int(np.log2(2 * m))
        masks.append((row >> sh) == (col >> sh))
    diag = row == col

    nt = (((1,), (1,)), ((), ()))
    tn = (((0,), (0,)), ((), ()))
    for h in range(HG_HEADS):
        sl = slice(h * dk, (h + 1) * dk)
        a = jnp.where(diag, lax.dot_general(q_bf[:, sl], k_bf[:, sl], nt,
                                            preferred_element_type=F32), 0.0)
        for li in range(len(levels)):
            a = a + jnp.where(masks[li],
                              lax.dot_general(qs[li][:, sl], ks[li][:, sl], nt,
                                              preferred_element_type=F32), 0.0)
        s_t = st[h]
        o = jnp.dot(a.astype(BF16), v_bf[:, sl], preferred_element_type=F32)
        o = o + lax.dot_general(q_st[:, sl], s_t.astype(BF16), nt, preferred_element_type=F32)
        st[h] = s_t * dec_all[:, sl] + lax.dot_general(v_bf[:, sl], k_st[:, sl], tn,
                                                       preferred_element_type=F32)
        ms = jnp.mean(o * o, axis=-1, keepdims=True)
        o = o * lax.rsqrt(ms + EPS) * ng_ref[...]
        g = og[:, sl]
        o_ref[0, :, sl] = (o * (g * _sigmoid(g))).astype(BF16)


def _hgrn_branch(p, lb, norm_g):
    b, s, _ = p.shape
    d = D_MODEL
    c = HG_CHUNK
    tril = jnp.asarray(np.tril(np.ones((c, c), np.float32)), dtype=BF16)

    def col_spec(col):
        return pl.BlockSpec((1, c, d), lambda bi, i: (bi, i, col))

    return pl.pallas_call(
        _hgrn_kernel,
        grid=(b, s // c),
        in_specs=[
            col_spec(COL_Q), col_spec(COL_F), col_spec(COL_I), col_spec(COL_G),
            pl.BlockSpec((1, d), lambda bi, i: (0, 0)),
            pl.BlockSpec((1, HG_DK), lambda bi, i: (0, 0)),
            pl.BlockSpec((c, c), lambda bi, i: (0, 0)),
        ],
        out_specs=pl.BlockSpec((1, c, d), lambda bi, i: (bi, i, 0)),
        out_shape=jax.ShapeDtypeStruct((b, s, d), BF16),
        scratch_shapes=[pltpu.VMEM((HG_HEADS, HG_DK, HG_DK), F32)],
        compiler_params=_params(("arbitrary", "arbitrary")),
        name="hgrn_branch",
    )(p, p, p, p, lb, norm_g, tril)


ROW_TILE = (8, 128)


def _rows_to_tiles(rows):
    st = jnp.stack([rows[:, j * 128:(j + 1) * 128] for j in range(ROW_TILE[0])], axis=0)
    return pltpu.einshape("jrl->rjl", st)


def _tiles_to_rows(tiles):
    y = pltpu.einshape("rjl->jrl", tiles)
    return jnp.concatenate([y[j] for j in range(ROW_TILE[0])], axis=-1)


_NT = (((1,), (1,)), ((), ()))


def _merge_kernel(uc_ref, uh_ref, gc_ref, gh_ref, x_ref, ga_ref, sc_ref, sh_ref, g_ref,
                  wc_ref, wh_ref, w_ref, wrt_ref, br_ref, x1_ref, h3_ref, lg_ref):
    yc = jnp.dot(uc_ref[0], wc_ref[...], preferred_element_type=F32)
    yh = jnp.dot(uh_ref[0], wh_ref[...], preferred_element_type=F32)
    merged = (_sigmoid(gc_ref[0]) * yc + _sigmoid(gh_ref[0]) * yh).astype(BF16)
    x1 = x_ref[0] + ga_ref[0] * jnp.dot(merged, w_ref[...], preferred_element_type=F32)
    x1_ref[0] = x1
    ms = jnp.mean(x1 * x1, axis=-1, keepdims=True)
    h2 = x1 * lax.rsqrt(ms + EPS) * g_ref[...]
    h2 = h2 * (1.0 + sc_ref[0]) + sh_ref[0]
    h3_ref[...] = _rows_to_tiles(h2)
    lg_ref[...] = lax.dot_general(wrt_ref[...], h2, _NT, preferred_element_type=F32,
                                  precision=lax.Precision.HIGHEST) + br_ref[...]


def _merge(uc, uh, p, x, mod3, g_ffn, w_conv_bf, w_hgrn_bf, w_out_bf, w_router_t, b_router_col):
    b, s, d = x.shape
    tm = 512
    nt = s // tm
    ne = w_router_t.shape[0]

    def mod_spec(k):
        return pl.BlockSpec((1, 1, d), lambda bi, i: (bi * N_MOD + k, 0, 0))

    def col_spec(col):
        return pl.BlockSpec((1, tm, d), lambda bi, i: (bi, i, col))

    tile = pl.BlockSpec((1, tm, d), lambda bi, i: (bi, i, 0))
    weight = pl.BlockSpec((d, d), lambda bi, i: (0, 0))
    return pl.pallas_call(
        _merge_kernel,
        grid=(b, nt),
        in_specs=[
            tile, tile, col_spec(COL_GC), col_spec(COL_GH), tile,
            mod_spec(2), mod_spec(4), mod_spec(3),
            pl.BlockSpec((1, d), lambda bi, i: (0, 0)),
            weight, weight, weight,
            pl.BlockSpec((ne, d), lambda bi, i: (0, 0)),
            pl.BlockSpec((ne, 1), lambda bi, i: (0, 0)),
        ],
        out_specs=[
            tile,
            pl.BlockSpec((tm,) + ROW_TILE, lambda bi, i: (bi * nt + i, 0, 0)),
            pl.BlockSpec((ne, tm), lambda bi, i: (0, bi * nt + i)),
        ],
        out_shape=[
            jax.ShapeDtypeStruct((b, s, d), F32),
            jax.ShapeDtypeStruct((b * s,) + ROW_TILE, F32),
            jax.ShapeDtypeStruct((ne, b * s), F32),
        ],
        compiler_params=_params(("arbitrary", "arbitrary")),
        name="merge_router",
    )(uc, uh, p, p, x, mod3, mod3, mod3, g_ffn, w_conv_bf, w_hgrn_bf, w_out_bf, w_router_t,
      b_router_col)


MOE_TL = 1024
MOE_GRP = 16
MOE_CAP = MOE_TL * TOP_K + N_EXPERTS * MOE_GRP
MOE_GPB = MOE_BLOCK // MOE_GRP
RELAY_ROWS = 256
assert MOE_CAP % RELAY_ROWS == 0 and MOE_TL % RELAY_ROWS == 0


def _route_kernel(lg_ref, u_ref, ld_ref, w_ref, cnt_ref):
    tl = MOE_TL
    l = lg_ref[...]
    eio = lax.broadcasted_iota(jnp.int32, l.shape, 0)
    vals, hots = [], []
    for _ in range(TOP_K):
        m = jnp.max(l, axis=0, keepdims=True)
        idx = jnp.min(jnp.where(l == m, eio, N_EXPERTS), axis=0, keepdims=True)
        hot = eio == idx
        vals.append(m)
        hots.append(hot)
        l = jnp.where(hot, -jnp.inf, l)
    ex = [jnp.exp(v - vals[0]) for v in vals]
    den = ex[0] + ex[1] + ex[2] + ex[3]
    cnt = hots[0].astype(F32)
    for k in range(1, TOP_K):
        cnt = cnt + hots[k].astype(F32)
    prefix = jnp.dot(cnt.astype(BF16), u_ref[...], preferred_element_type=F32)
    n_e = jnp.sum(cnt, axis=1, keepdims=True)
    pad_e = jnp.floor((n_e + (MOE_GRP - 1)) * (1.0 / MOE_GRP)) * MOE_GRP
    scan = jnp.broadcast_to(pad_e, (N_EXPERTS, 128))
    ei = lax.broadcasted_iota(jnp.int32, scan.shape, 0)
    dist = 1
    while dist < N_EXPERTS:
        scan = scan + jnp.where(ei >= dist, pltpu.roll(scan, dist, axis=0), 0.0)
        dist *= 2
    base = prefix + (scan[:, 0:1] - pad_e)
    for k in range(TOP_K):
        dest = jnp.sum(jnp.where(hots[k], base, 0.0), axis=0, keepdims=True)
        ld_ref[0, :, k * tl:(k + 1) * tl] = dest.astype(jnp.int32)
        w_ref[0, :, k * tl:(k + 1) * tl] = ex[k] / den
    cnt_ref[0] = jnp.broadcast_to(n_e, (N_EXPERTS, 128)).astype(jnp.int32)


def _route(logits_t):
    ne, t = logits_t.shape
    tl = MOE_TL
    n_tiles = t // tl
    upper = jnp.asarray(np.triu(np.ones((tl, tl), np.float32), k=1), dtype=BF16)
    slot = pl.BlockSpec((1, 1, TOP_K * tl), lambda i: (i, 0, 0))
    return pl.pallas_call(
        _route_kernel,
        grid=(n_tiles,),
        in_specs=[pl.BlockSpec((ne, tl), lambda i: (0, i)),
                  pl.BlockSpec((tl, tl), lambda i: (0, 0))],
        out_specs=[slot, slot, pl.BlockSpec((1, ne, 128), lambda i: (i, 0, 0))],
        out_shape=[
            jax.ShapeDtypeStruct((n_tiles, 1, TOP_K * tl), jnp.int32),
            jax.ShapeDtypeStruct((n_tiles, 1, TOP_K * tl), F32),
            jax.ShapeDtypeStruct((n_tiles, ne, 128), jnp.int32),
        ],
        compiler_params=_params(("arbitrary",)),
        name="route",
    )(logits_t, upper)


def _gather_kernel(ld_ref, h3_ref, xs_ref, xs3):
    tl = MOE_TL
    xs3[...] = jnp.zeros_like(xs3)

    def push(t, carry):
        tile = h3_ref[t]
        for k in range(TOP_K):
            xs3[ld_ref[0, 0, k * tl + t]] = tile
        return carry

    @pl.when(pl.program_id(0) < pl.num_programs(0) - 1)
    def _():
        lax.fori_loop(0, tl, push, 0, unroll=4)

    def relay(r, carry):
        r0 = pl.multiple_of(r * RELAY_ROWS, RELAY_ROWS)
        xs_ref[pl.ds(r0, RELAY_ROWS), :] = _tiles_to_rows(xs3[pl.ds(r0, RELAY_ROWS)]).astype(BF16)
        return carry

    lax.fori_loop(0, MOE_CAP // RELAY_ROWS, relay, 0)


def _gather(ld, h3):
    t = h3.shape[0]
    tl = MOE_TL
    n_tiles = t // tl
    last = n_tiles - 1
    return pl.pallas_call(
        _gather_kernel,
        grid=(n_tiles + 1,),
        in_specs=[
            pl.BlockSpec((1, 1, TOP_K * tl), lambda i: (jnp.minimum(i, last), 0, 0),
                         memory_space=pltpu.SMEM),
            pl.BlockSpec((tl,) + ROW_TILE, lambda i: (jnp.minimum(i, last), 0, 0)),
        ],
        out_specs=pl.BlockSpec((MOE_CAP, D_MODEL), lambda i: (i, 0)),
        out_shape=jax.ShapeDtypeStruct(((n_tiles + 1) * MOE_CAP, D_MODEL), BF16),
        scratch_shapes=[pltpu.VMEM((MOE_CAP,) + ROW_TILE, F32)],
        compiler_params=_params(("arbitrary",)),
        name="moe_gather",
    )(ld, h3)


def _table_kernel(cnt_ref, be_ref, src_ref, toff):
    n_tiles = cnt_ref.shape[0]
    n_slots = src_ref.shape[0]
    groups_per_tile = MOE_CAP // MOE_GRP
    step = 8

    for i in range(n_tiles):
        toff[i] = 0

    def per_expert(e, pos):
        def per_tile(i, p):
            g = (cnt_ref[i, e] + (MOE_GRP - 1)) // MOE_GRP
            base = i * groups_per_tile + toff[i]

            def put(j, c):
                for u in range(step):
                    src_ref[p + j * step + u] = base + j * step + u
                return c

            lax.fori_loop(0, (g + (step - 1)) // step, put, 0)
            toff[i] = toff[i] + g
            return p + g

        end = lax.fori_loop(0, n_tiles, per_tile, pos)
        new_pos = (end + (MOE_GPB - 1)) // MOE_GPB * MOE_GPB
        for u in range(MOE_GPB):
            src_ref[end + u] = -1

        def put_e(b, c):
            be_ref[b] = e
            return c

        lax.fori_loop(pos // MOE_GPB, new_pos // MOE_GPB, put_e, 0)
        return new_pos

    used = lax.fori_loop(0, N_EXPERTS, per_expert, 0)

    def tail(b, c):
        be_ref[b] = N_EXPERTS - 1
        for u in range(MOE_GPB):
            src_ref[b * MOE_GPB + u] = -1
        return c

    lax.fori_loop(used // MOE_GPB, n_slots // MOE_GPB, tail, 0)


def _block_table(cnt, n_blocks):
    smem = pl.BlockSpec(memory_space=pltpu.SMEM)
    return pl.pallas_call(
        _table_kernel,
        in_specs=[smem],
        out_specs=[smem, smem],
        out_shape=[jax.ShapeDtypeStruct((n_blocks,), jnp.int32),
                   jax.ShapeDtypeStruct((n_blocks * MOE_GPB,), jnp.int32)],
        scratch_shapes=[pltpu.SMEM((cnt.shape[0],), jnp.int32)],
        name="moe_block_table",
    )(cnt)


def _ffn_kernel(be_ref, src_ref, xs_hbm, w1_ref, b1_ref, w2_ref, b2_ref, ys_hbm,
                xbuf, ybuf, w1s, w2s, sem_in, sem_out, *, spare_grp):
    i = pl.program_id(0)
    nb = pl.num_programs(0)
    slot = i % 2

    def live(blk):
        return src_ref[blk * MOE_GPB] >= 0

    def group_rows(grp):
        return pl.ds(pl.multiple_of(grp * MOE_GRP, MOE_GRP), MOE_GRP)

    def start_in(blk, sl):
        for g in range(MOE_GPB):
            grp = src_ref[blk * MOE_GPB + g]
            grp = jnp.where(grp >= 0, grp, spare_grp)
            pltpu.make_async_copy(xs_hbm.at[group_rows(grp), :],
                                  xbuf.at[sl, pl.ds(g * MOE_GRP, MOE_GRP), :], sem_in.at[sl]).start()

    def start_out(blk, sl):
        for g in range(MOE_GPB):
            grp = src_ref[blk * MOE_GPB + g]
            grp = jnp.where(grp >= 0, grp, spare_grp + 1 + sl * MOE_GPB + g)
            pltpu.make_async_copy(ybuf.at[sl, pl.ds(g * MOE_GRP, MOE_GRP), :],
                                  ys_hbm.at[group_rows(grp), :], sem_out.at[sl]).start()

    def wait_in(sl):
        pltpu.make_async_copy(xs_hbm.at[pl.ds(0, MOE_BLOCK), :], xbuf.at[sl], sem_in.at[sl]).wait()

    def wait_out(sl):
        pltpu.make_async_copy(ybuf.at[sl], ys_hbm.at[pl.ds(0, MOE_BLOCK), :], sem_out.at[sl]).wait()

    @pl.when((i == 0) & live(0))
    def _():
        start_in(0, 0)

    nxt = jnp.minimum(i + 1, nb - 1)

    @pl.when((i + 1 < nb) & live(nxt))
    def _():
        start_in(nxt, 1 - slot)

    @pl.when((i >= 2) & live(jnp.maximum(i - 2, 0)))
    def _():
        wait_out(slot)

    prev = be_ref[jnp.maximum(i - 1, 0)]

    @pl.when((i == 0) | (be_ref[i] != prev))
    def _():
        w1s[...] = w1_ref[0].astype(BF16)
        w2s[...] = w2_ref[0].astype(BF16)

    @pl.when(live(i))
    def _():
        wait_in(slot)
        u = jnp.dot(xbuf[slot], w1s[...], preferred_element_type=F32) + b1_ref[0]
        gl = jnp.minimum(u[:, :D_FF], SWIGLU_LIMIT)
        lin = jnp.clip(u[:, D_FF:], -SWIGLU_LIMIT, SWIGLU_LIMIT)
        act = gl * _sigmoid(SWIGLU_ALPHA * gl) * (lin + 1.0)
        y = jnp.dot(act.astype(BF16), w2s[...], preferred_element_type=F32) + b2_ref[0]
        ybuf[slot] = y.astype(BF16)
        start_out(i, slot)

    @pl.when(i == nb - 1)
    def _():
        @pl.when(live(i))
        def _():
            wait_out(slot)

        @pl.when((i >= 1) & live(jnp.maximum(i - 1, 0)))
        def _():
            wait_out(1 - slot)


def _ffn(block_e, src, xs, w1, b1, w2, b2):
    r, d = xs.shape
    nb = block_e.shape[0]
    spare_grp = (r - MOE_CAP) // MOE_GRP
    assert 1 + 2 * MOE_GPB <= MOE_CAP // MOE_GRP
    grid_spec = pltpu.PrefetchScalarGridSpec(
        num_scalar_prefetch=2,
        grid=(nb,),
        in_specs=[
            pl.BlockSpec(memory_space=pl.ANY),
            pl.BlockSpec((1, d, 2 * D_FF), lambda i, be, sr: (be[i], 0, 0)),
            pl.BlockSpec((1, 1, 2 * D_FF), lambda i, be, sr: (be[i], 0, 0)),
            pl.BlockSpec((1, D_FF, d), lambda i, be, sr: (be[i], 0, 0)),
            pl.BlockSpec((1, 1, d), lambda i, be, sr: (be[i], 0, 0)),
        ],
        out_specs=pl.BlockSpec(memory_space=pl.ANY),
        scratch_shapes=[
            pltpu.VMEM((2, MOE_BLOCK, d), BF16), pltpu.VMEM((2, MOE_BLOCK, d), BF16),
            pltpu.VMEM((d, 2 * D_FF), BF16), pltpu.VMEM((D_FF, d), BF16),
            pltpu.SemaphoreType.DMA((2,)), pltpu.SemaphoreType.DMA((2,)),
        ],
    )
    return pl.pallas_call(
        functools.partial(_ffn_kernel, spare_grp=spare_grp),
        grid_spec=grid_spec,
        out_shape=jax.ShapeDtypeStruct((r, d), BF16),
        input_output_aliases={2: 0},
        compiler_params=_params(("arbitrary",)),
        name="expert_ffn",
    )(block_e, src, xs, w1, b1, w2, b2)


def _combine_kernel(ld_ref, w_ref, ys_ref, x1_ref, ga_ref, g_ref, o_ref, y3, o3):
    tl = MOE_TL

    def relay(r, carry):
        r0 = pl.multiple_of(r * RELAY_ROWS, RELAY_ROWS)
        y3[pl.ds(r0, RELAY_ROWS)] = _rows_to_tiles(ys_ref[pl.ds(r0, RELAY_ROWS), :].astype(F32))
        return carry

    lax.fori_loop(0, MOE_CAP // RELAY_ROWS, relay, 0)

    def pull(t, carry):
        acc = w_ref[0, 0, t] * y3[ld_ref[0, 0, t]]
        for k in range(1, TOP_K):
            acc = acc + w_ref[0, 0, k * tl + t] * y3[ld_ref[0, 0, k * tl + t]]
        o3[t] = acc
        return carry

    lax.fori_loop(0, tl, pull, 0, unroll=4)

    def finish(r, carry):
        r0 = pl.multiple_of(r * RELAY_ROWS, RELAY_ROWS)
        x2 = x1_ref[pl.ds(r0, RELAY_ROWS), :] + ga_ref[0] * _tiles_to_rows(o3[pl.ds(r0, RELAY_ROWS)])
        ms = jnp.mean(x2 * x2, axis=-1, keepdims=True)
        o_ref[pl.ds(r0, RELAY_ROWS), :] = x2 * lax.rsqrt(ms + EPS) * g_ref[...]
        return carry

    lax.fori_loop(0, tl // RELAY_ROWS, finish, 0)


def _combine(ld, wts, ys, x1f, mod3, g_final, tiles_per_batch):
    t, d = x1f.shape
    tl = MOE_TL
    n_tiles = t // tl
    smem = functools.partial(pl.BlockSpec, (1, 1, TOP_K * tl), lambda i: (i, 0, 0),
                             memory_space=pltpu.SMEM)
    return pl.pallas_call(
        _combine_kernel,
        grid=(n_tiles,),
        in_specs=[
            smem(), smem(),
            pl.BlockSpec((MOE_CAP, d), lambda i: (i, 0), pipeline_mode=pl.Buffered(1)),
            pl.BlockSpec((tl, d), lambda i: (i, 0)),
            pl.BlockSpec((1, 1, d), lambda i: ((i // tiles_per_batch) * N_MOD + 5, 0, 0)),
            pl.BlockSpec((1, d), lambda i: (0, 0)),
        ],
        out_specs=pl.BlockSpec((tl, d), lambda i: (i, 0)),
        out_shape=jax.ShapeDtypeStruct((t, d), F32),
        scratch_shapes=[pltpu.VMEM((MOE_CAP,) + ROW_TILE, F32), pltpu.VMEM((tl,) + ROW_TILE, F32)],
        compiler_params=_params(("arbitrary",)),
        name="moe_combine",
    )(ld, wts, ys, x1f, mod3, g_final)


def kernel(x, c, w_ada, b_ada, g_mix, w_in, conv_dw, conv_dw_bias, conv_ln_g, conv_ln_b,
           w_conv_out, lb_param, hgrn_norm_g, w_hgrn_out, w_out, g_ffn, w_router, b_router,
           w1, b1, w2, b2, g_final):
    b, s, d = x.shape
    assert w_ada.shape[0] == 1, "single-layer block"
    assert s % MOE_TL == 0
    t = b * s
    n_tiles = t // MOE_TL

    c_pad = jnp.zeros((8, d), F32).at[:b].set(c.astype(F32))
    mod, lb = _ada(c_pad, w_ada[0], b_ada, lb_param)
    mod3 = mod[:b].reshape(b * N_MOD, 1, d)

    p = _inproj(x, g_mix, mod3, w_in[0].astype(BF16))

    dw_pad = jnp.zeros((32, d), F32).at[:CONV_K].set(conv_dw[0])
    uc = _conv_branch(p, dw_pad, conv_dw_bias, conv_ln_g, conv_ln_b)
    uh = _hgrn_branch(p, lb, hgrn_norm_g)

    x1, h3, logits_t = _merge(uc, uh, p, x, mod3, g_ffn, w_conv_out[0].astype(BF16),
                              w_hgrn_out[0].astype(BF16), w_out[0].astype(BF16),
                              w_router[0].T, b_router[0][:, None])

    ld, wts, cnt = _route(logits_t)
    xs = _gather(ld, h3)
    groups_max = t * TOP_K // MOE_GRP + n_tiles * N_EXPERTS
    n_blocks = groups_max // MOE_GPB + N_EXPERTS
    block_e, src = _block_table(cnt[:, :, 0], n_blocks)
    ys = _ffn(block_e, src, xs, w1[0], b1[0][:, None, :], w2[0], b2[0][:, None, :])
    out = _combine(ld, wts, ys, x1.reshape(t, d), mod3, g_final.reshape(1, d), s // MOE_TL)
    return out.reshape(b, s, d)
```

```python
import functools

import jax
import jax.numpy as jnp
import numpy as np
from jax import lax
from jax.experimental import pallas as pl
from jax.experimental.pallas import tpu as pltpu

F32 = jnp.float32
BF16 = jnp.bfloat16

D_MODEL = 1024
CONV_K = 31
HG_HEADS = 8
HG_DK = 128
N_EXPERTS = 32
TOP_K = 4
D_FF = 1024
SWIGLU_ALPHA = 1.702
SWIGLU_LIMIT = 7.0
MOE_BLOCK = 256
EPS = 1e-6
N_MOD = 6
COL_CA, COL_CB, COL_Q, COL_F, COL_I, COL_G, COL_GC, COL_GH = range(8)
N_COLS = 8

HG_CHUNK = 128
CONV_HALO = 32
VMEM_LIMIT = 56 * 1024 * 1024


def _sigmoid(x):
    return 1.0 / (1.0 + jnp.exp(-x))


def _params(sem, vmem=VMEM_LIMIT):
    return pltpu.CompilerParams(dimension_semantics=sem, vmem_limit_bytes=vmem)


def _ada_kernel(c_ref, w_ref, b_ref, lbp_ref, mod_ref, lb_ref):
    c = c_ref[...]
    c_act = c * _sigmoid(c)
    mod_ref[...] = jnp.dot(c_act, w_ref[...], preferred_element_type=F32,
                           precision=lax.Precision.HIGHEST) + b_ref[...]
    p = lbp_ref[...]
    e = jnp.exp(p - jnp.max(p, axis=0, keepdims=True))
    lb_ref[...] = e[0:1, :] / jnp.sum(e, axis=0, keepdims=True)


def _ada(c_pad, w_ada, b_ada, lb_param):
    nb, d = c_pad.shape
    n = w_ada.shape[1]
    tn = 1536
    return pl.pallas_call(
        _ada_kernel,
        grid=(n // tn,),
        in_specs=[
            pl.BlockSpec((nb, d), lambda j: (0, 0)),
            pl.BlockSpec((d, tn), lambda j: (0, j)),
            pl.BlockSpec((1, tn), lambda j: (0, j)),
            pl.BlockSpec(lb_param.shape, lambda j: (0, 0)),
        ],
        out_specs=[
            pl.BlockSpec((nb, tn), lambda j: (0, j)),
            pl.BlockSpec((1, d), lambda j: (0, 0)),
        ],
        out_shape=[
            jax.ShapeDtypeStruct((nb, n), F32),
            jax.ShapeDtypeStruct((1, d), F32),
        ],
        compiler_params=_params(("arbitrary",)),
        name="ada_mod",
    )(c_pad, w_ada, b_ada, lb_param)


CONV_TS = 256
CONV_RG = 32
SUBLANES = 8
CONV_SPAN = CONV_TS + CONV_HALO
CONV_OFF = CONV_HALO - (CONV_K - 1)
N_PCOLS = N_COLS - 2
P_Q, P_F, P_I, P_G, P_GC, P_GH = range(N_PCOLS)


def _inproj_conv_kernel(x_ref, g_ref, sc_ref, sh_ref, w_ref, dwb_ref, bias_ref, lng_ref, lnb_ref,
                        p_ref, uc_ref, buf, sh, cv):
    i = pl.program_id(1)
    ts = CONV_TS
    d = D_MODEL

    @pl.when(i == 0)
    def _():
        buf[0:CONV_HALO, :] = jnp.zeros((CONV_HALO, d), F32)

    @pl.when(i > 0)
    def _():
        buf[0:CONV_HALO, :] = buf[ts:CONV_SPAN, :]

    x = x_ref[0]
    ms = jnp.mean(x * x, axis=-1, keepdims=True)
    h = x * lax.rsqrt(ms + EPS) * g_ref[...]
    h = (h * (1.0 + sc_ref[0]) + sh_ref[0]).astype(BF16)

    ab = jnp.dot(h, w_ref[:, 0:2 * d], preferred_element_type=F32)
    buf[CONV_HALO:CONV_SPAN, :] = ab[:, :d] * _sigmoid(ab[:, d:])
    p_ref[0] = jnp.dot(h, w_ref[:, 2 * d:], preferred_element_type=F32)

    for s in range(1, SUBLANES):
        sh[s - 1] = buf[s:s + CONV_SPAN - SUBLANES, :]
    for r in range(ts // CONV_RG):
        base = r * CONV_RG
        acc = jnp.zeros((CONV_RG // SUBLANES, SUBLANES, d), F32)
        for j in range(CONV_K):
            s = (CONV_OFF + j) % SUBLANES
            row = base + CONV_OFF + j - s
            src = buf if s == 0 else sh.at[s - 1]
            rows = src[row:row + CONV_RG, :].reshape(CONV_RG // SUBLANES, SUBLANES, d)
            acc = acc + dwb_ref[j] * rows
        cv[base:base + CONV_RG, :] = acc.reshape(CONV_RG, d) + bias_ref[...]
    u = cv[...]
    mu = jnp.mean(u, axis=-1, keepdims=True)
    uc = u - mu
    var = jnp.mean(uc * uc, axis=-1, keepdims=True)
    y = uc * lax.rsqrt(var + EPS) * lng_ref[...] + lnb_ref[...]
    uc_ref[0] = (y * _sigmoid(y)).astype(BF16)


def _inproj_conv(x, g_mix, mod3, w_in_bf, dw_rows, bias, ln_g, ln_b):
    b, s, d = x.shape
    n = w_in_bf.shape[1]
    ts = CONV_TS
    vec = pl.BlockSpec((1, d), lambda bi, i: (0, 0))
    return pl.pallas_call(
        _inproj_conv_kernel,
        grid=(b, s // ts),
        in_specs=[
            pl.BlockSpec((1, ts, d), lambda bi, i: (bi, i, 0)),
            vec,
            pl.BlockSpec((1, 1, d), lambda bi, i: (bi * N_MOD + 1, 0, 0)),
            pl.BlockSpec((1, 1, d), lambda bi, i: (bi * N_MOD + 0, 0, 0)),
            pl.BlockSpec((d, n), lambda bi, i: (0, 0), pipeline_mode=pl.Buffered(1)),
            pl.BlockSpec(dw_rows.shape, lambda bi, i: (0, 0, 0)),
            vec, vec, vec,
        ],
        out_specs=[
            pl.BlockSpec((1, ts, N_PCOLS * d), lambda bi, i: (bi, i, 0)),
            pl.BlockSpec((1, ts, d), lambda bi, i: (bi, i, 0)),
        ],
        out_shape=[
            jax.ShapeDtypeStruct((b, s, N_PCOLS * d), F32),
            jax.ShapeDtypeStruct((b, s, d), BF16),
        ],
        scratch_shapes=[pltpu.VMEM((CONV_SPAN, d), F32),
                        pltpu.VMEM((SUBLANES - 1, CONV_SPAN - SUBLANES, d), F32),
                        pltpu.VMEM((ts, d), F32)],
        compiler_params=_params(("arbitrary", "arbitrary")),
        name="inproj_conv",
    )(x, g_mix, mod3, mod3, w_in_bf, dw_rows, bias, ln_g, ln_b)


def _hgrn_levels():
    c = HG_CHUNK
    levels = []
    m = c // 2
    while m >= 1:
        levels.append(m)
        m //= 2
    return levels


def _level_exponent(g_inc, logf, m, row):
    c, d = g_inc.shape
    upper = (row & m) != 0
    if m == 1:
        return upper, jnp.where(upper, logf, 0.0)
    if m >= SUBLANES:
        parts = [jnp.broadcast_to(g_inc[b * 2 * m + m - 1:b * 2 * m + m, :], (2 * m, d))
                 for b in range(c // (2 * m))]
        gref = parts[0] if len(parts) == 1 else jnp.concatenate(parts, axis=0)
    else:
        g3 = g_inc.reshape(c // SUBLANES, SUBLANES, d)

        def bcast(j):
            return jnp.broadcast_to(g3[:, j:j + 1, :], g3.shape).reshape(c, d)

        if 2 * m == SUBLANES:
            gref = bcast(m - 1)
        else:
            assert 4 * m == SUBLANES
            gref = jnp.where((row & (SUBLANES - 1)) < 2 * m, bcast(m - 1), bcast(3 * m - 1))
    return upper, jnp.where(upper, g_inc - gref, gref - g_inc)


def _hgrn_kernel(q_ref, z_ref, v_ref, og_ref, lb_ref, ng_ref, tril_ref, o_ref, st):
    c = HG_CHUNK
    dk = HG_DK
    levels = _hgrn_levels()

    @pl.when(pl.program_id(1) == 0)
    def _():
        st[...] = jnp.zeros_like(st)

    z = z_ref[0]
    lb = lb_ref[...]
    sig = _sigmoid(z)
    f = lb + (1.0 - lb) * sig
    logf = jnp.log(f)
    kk = (1.0 - lb) * (1.0 - sig)
    q = q_ref[0] * (dk ** -0.5)
    v = v_ref[0]
    og = og_ref[0]

    hi = logf.astype(BF16)
    lo = (logf - hi.astype(F32)).astype(BF16)
    tril = tril_ref[...]
    g_inc = (jnp.dot(tril, hi, preferred_element_type=F32)
             + jnp.dot(tril, lo, preferred_element_type=F32))
    g_last = g_inc[c - 1:c, :]
    q_st = (q * jnp.exp(g_inc)).astype(BF16)
    k_st = (kk * jnp.exp(g_last - g_inc)).astype(BF16)
    dec_all = jnp.exp(g_last)
    v_bf = v.astype(BF16)
    q_bf = q.astype(BF16)
    k_bf = kk.astype(BF16)

    row = lax.broadcasted_iota(jnp.int32, (c, c), 0)
    col = lax.broadcasted_iota(jnp.int32, (c, c), 1)
    rr = lax.broadcasted_iota(jnp.int32, (c, 1), 0)

    qs, ks, masks = [], [], []
    for m in levels:
        upper, ex = _level_exponent(g_inc, logf, m, rr)
        e = jnp.exp(ex)
        qs.append(jnp.where(upper, q * e, 0.0).astype(BF16))
        ks.append(jnp.where(upper, 0.0, kk * e).astype(BF16))
        sh = int(np.log2(2 * m))
        masks.append((row >> sh) == (col >> sh))
    diag = row == col

    nt = (((1,), (1,)), ((), ()))
    tn = (((0,), (0,)), ((), ()))
    for h in range(HG_HEADS):
        sl = slice(h * dk, (h + 1) * dk)
        a = jnp.where(diag, lax.dot_general(q_bf[:, sl], k_bf[:, sl], nt,
                                            preferred_element_type=F32), 0.0)
        for li in range(len(levels)):
            a = a + jnp.where(masks[li],
                              lax.dot_general(qs[li][:, sl], ks[li][:, sl], nt,
                                              preferred_element_type=F32), 0.0)
        s_t = st[h]
        o = jnp.dot(a.astype(BF16), v_bf[:, sl], preferred_element_type=F32)
        o = o + lax.dot_general(q_st[:, sl], s_t.astype(BF16), nt, preferred_element_type=F32)
        st[h] = s_t * dec_all[:, sl] + lax.dot_general(v_bf[:, sl], k_st[:, sl], tn,
                                                       preferred_element_type=F32)
        ms = jnp.mean(o * o, axis=-1, keepdims=True)
        o = o * lax.rsqrt(ms + EPS) * ng_ref[...]
        g = og[:, sl]
        o_ref[0, :, sl] = (o * (g * _sigmoid(g))).astype(BF16)


def _hgrn_branch(p, lb, norm_g):
    b, s, _ = p.shape
    d = D_MODEL
    c = HG_CHUNK
    tril = jnp.asarray(np.tril(np.ones((c, c), np.float32)), dtype=BF16)

    def col_spec(col):
        return pl.BlockSpec((1, c, d), lambda bi, i: (bi, i, col))

    return pl.pallas_call(
        _hgrn_kernel,
        grid=(b, s // c),
        in_specs=[
            col_spec(P_Q), col_spec(P_F), col_spec(P_I), col_spec(P_G),
            pl.BlockSpec((1, d), lambda bi, i: (0, 0)),
            pl.BlockSpec((1, HG_DK), lambda bi, i: (0, 0)),
            pl.BlockSpec((c, c), lambda bi, i: (0, 0)),
        ],
        out_specs=pl.BlockSpec((1, c, d), lambda bi, i: (bi, i, 0)),
        out_shape=jax.ShapeDtypeStruct((b, s, d), BF16),
        scratch_shapes=[pltpu.VMEM((HG_HEADS, HG_DK, HG_DK), F32)],
        compiler_params=_params(("arbitrary", "arbitrary")),
        name="hgrn_branch",
    )(p, p, p, p, lb, norm_g, tril)


ROW_TILE = (8, 128)


def _rows_to_tiles(rows):
    st = jnp.stack([rows[:, j * 128:(j + 1) * 128] for j in range(ROW_TILE[0])], axis=0)
    return pltpu.einshape("jrl->rjl", st)


def _tiles_to_rows(tiles):
    y = pltpu.einshape("rjl->jrl", tiles)
    return jnp.concatenate([y[j] for j in range(ROW_TILE[0])], axis=-1)


_NT = (((1,), (1,)), ((), ()))
MERGE_SUB = 256


def _merge_kernel(uc_ref, uh_ref, gc_ref, gh_ref, x_ref, ga_ref, sc_ref, sh_ref, g_ref,
                  wc_ref, wh_ref, w_ref, wrt_ref, br_ref, x1_ref, h3_ref, lg_ref):
    wrt = wrt_ref[...]
    wrt_hi = wrt.astype(BF16)
    wrt_lo = (wrt - wrt_hi.astype(F32)).astype(BF16)
    for r in range(uc_ref.shape[1] // MERGE_SUB):
        rs = pl.ds(r * MERGE_SUB, MERGE_SUB)
        yc = jnp.dot(uc_ref[0, rs, :], wc_ref[...], preferred_element_type=F32)
        yh = jnp.dot(uh_ref[0, rs, :], wh_ref[...], preferred_element_type=F32)
        merged = (_sigmoid(gc_ref[0, rs, :]) * yc + _sigmoid(gh_ref[0, rs, :]) * yh).astype(BF16)
        x1 = x_ref[0, rs, :] + ga_ref[0] * jnp.dot(merged, w_ref[...],
                                                     preferred_element_type=F32)
        x1_ref[0, rs, :] = x1
        ms = jnp.mean(x1 * x1, axis=-1, keepdims=True)
        h2 = x1 * lax.rsqrt(ms + EPS) * g_ref[...]
        h2 = h2 * (1.0 + sc_ref[0]) + sh_ref[0]
        h3_ref[rs] = _rows_to_tiles(h2)
        h2_hi = h2.astype(BF16)
        h2_lo = (h2 - h2_hi.astype(F32)).astype(BF16)
        lg = (lax.dot_general(wrt_hi, h2_hi, _NT, preferred_element_type=F32)
              + lax.dot_general(wrt_hi, h2_lo, _NT, preferred_element_type=F32)
              + lax.dot_general(wrt_lo, h2_hi, _NT, preferred_element_type=F32))
        lg_ref[:, rs] = lg + br_ref[...]


def _merge(uc, uh, p, x, mod3, g_ffn, w_conv_bf, w_hgrn_bf, w_out_bf, w_router_t, b_router_col):
    b, s, d = x.shape
    tm = 512
    nt = s // tm
    ne = w_router_t.shape[0]

    def mod_spec(k):
        return pl.BlockSpec((1, 1, d), lambda bi, i: (bi * N_MOD + k, 0, 0))

    def col_spec(col):
        return pl.BlockSpec((1, tm, d), lambda bi, i: (bi, i, col))

    tile = pl.BlockSpec((1, tm, d), lambda bi, i: (bi, i, 0))
    weight = pl.BlockSpec((d, d), lambda bi, i: (0, 0))
    return pl.pallas_call(
        _merge_kernel,
        grid=(b, nt),
        in_specs=[
            tile, tile, col_spec(P_GC), col_spec(P_GH), tile,
            mod_spec(2), mod_spec(4), mod_spec(3),
            pl.BlockSpec((1, d), lambda bi, i: (0, 0)),
            weight, weight, weight,
            pl.BlockSpec((ne, d), lambda bi, i: (0, 0)),
            pl.BlockSpec((ne, 1), lambda bi, i: (0, 0)),
        ],
        out_specs=[
            tile,
            pl.BlockSpec((tm,) + ROW_TILE, lambda bi, i: (bi * nt + i, 0, 0)),
            pl.BlockSpec((ne, tm), lambda bi, i: (0, bi * nt + i)),
        ],
        out_shape=[
            jax.ShapeDtypeStruct((b, s, d), F32),
            jax.ShapeDtypeStruct((b * s,) + ROW_TILE, F32),
            jax.ShapeDtypeStruct((ne, b * s), F32),
        ],
        compiler_params=_params(("arbitrary", "arbitrary")),
        name="merge_router",
    )(uc, uh, p, p, x, mod3, mod3, mod3, g_ffn, w_conv_bf, w_hgrn_bf, w_out_bf, w_router_t,
      b_router_col)


MOE_TL = 1024
MOE_GRP = 16
MOE_CAP = MOE_TL * TOP_K + N_EXPERTS * MOE_GRP
MOE_GPB = MOE_BLOCK // MOE_GRP
RELAY_ROWS = 256
assert MOE_CAP % RELAY_ROWS == 0 and MOE_TL % RELAY_ROWS == 0


def _route_kernel(lg_ref, u_ref, ld_ref, w_ref, cnt_ref):
    tl = MOE_TL
    l = lg_ref[...]
    eio = lax.broadcasted_iota(jnp.int32, l.shape, 0)
    vals, hots = [], []
    for _ in range(TOP_K):
        m = jnp.max(l, axis=0, keepdims=True)
        idx = jnp.min(jnp.where(l == m, eio, N_EXPERTS), axis=0, keepdims=True)
        hot = eio == idx
        vals.append(m)
        hots.append(hot)
        l = jnp.where(hot, -jnp.inf, l)
    ex = [jnp.exp(v - vals[0]) for v in vals]
    den = ex[0] + ex[1] + ex[2] + ex[3]
    cnt = hots[0].astype(F32)
    for k in range(1, TOP_K):
        cnt = cnt + hots[k].astype(F32)
    prefix = jnp.dot(cnt.astype(BF16), u_ref[...], preferred_element_type=F32)
    n_e = jnp.sum(cnt, axis=1, keepdims=True)
    pad_e = jnp.floor((n_e + (MOE_GRP - 1)) * (1.0 / MOE_GRP)) * MOE_GRP
    scan = jnp.broadcast_to(pad_e, (N_EXPERTS, 128))
    ei = lax.broadcasted_iota(jnp.int32, scan.shape, 0)
    dist = 1
    while dist < N_EXPERTS:
        scan = scan + jnp.where(ei >= dist, pltpu.roll(scan, dist, axis=0), 0.0)
        dist *= 2
    base = prefix + (scan[:, 0:1] - pad_e)
    for k in range(TOP_K):
        dest = jnp.sum(jnp.where(hots[k], base, 0.0), axis=0, keepdims=True)
        ld_ref[0, :, k * tl:(k + 1) * tl] = dest.astype(jnp.int32)
        w_ref[0, :, k * tl:(k + 1) * tl] = ex[k] / den
    cnt_ref[0] = jnp.broadcast_to(n_e, (N_EXPERTS, 128)).astype(jnp.int32)


def _route(logits_t):
    ne, t = logits_t.shape
    tl = MOE_TL
    n_tiles = t // tl
    upper = jnp.asarray(np.triu(np.ones((tl, tl), np.float32), k=1), dtype=BF16)
    slot = pl.BlockSpec((1, 1, TOP_K * tl), lambda i: (i, 0, 0))
    return pl.pallas_call(
        _route_kernel,
        grid=(n_tiles,),
        in_specs=[pl.BlockSpec((ne, tl), lambda i: (0, i)),
                  pl.BlockSpec((tl, tl), lambda i: (0, 0))],
        out_specs=[slot, slot, pl.BlockSpec((1, ne, 128), lambda i: (i, 0, 0))],
        out_shape=[
            jax.ShapeDtypeStruct((n_tiles, 1, TOP_K * tl), jnp.int32),
            jax.ShapeDtypeStruct((n_tiles, 1, TOP_K * tl), F32),
            jax.ShapeDtypeStruct((n_tiles, ne, 128), jnp.int32),
        ],
        compiler_params=_params(("arbitrary",)),
        name="route",
    )(logits_t, upper)


def _gather_kernel(ld_ref, h3_ref, xs_ref, xs3):
    tl = MOE_TL
    last = pl.num_programs(0) - 1

    @pl.when((pl.program_id(0) == 0) | (pl.program_id(0) == last))
    def _():
        xs3[...] = jnp.zeros_like(xs3)

    def push(t, carry):
        tile = h3_ref[t]
        for k in range(TOP_K):
            xs3[ld_ref[0, 0, k * tl + t]] = tile
        return carry

    @pl.when(pl.program_id(0) < last)
    def _():
        lax.fori_loop(0, tl, push, 0, unroll=4)

    def relay(r, carry):
        r0 = pl.multiple_of(r * RELAY_ROWS, RELAY_ROWS)
        xs_ref[pl.ds(r0, RELAY_ROWS), :] = _tiles_to_rows(xs3[pl.ds(r0, RELAY_ROWS)]).astype(BF16)
        return carry

    lax.fori_loop(0, MOE_CAP // RELAY_ROWS, relay, 0)


def _gather(ld, h3):
    t = h3.shape[0]
    tl = MOE_TL
    n_tiles = t // tl
    last = n_tiles - 1
    return pl.pallas_call(
        _gather_kernel,
        grid=(n_tiles + 1,),
        in_specs=[
            pl.BlockSpec((1, 1, TOP_K * tl), lambda i: (jnp.minimum(i, last), 0, 0),
                         memory_space=pltpu.SMEM),
            pl.BlockSpec((tl,) + ROW_TILE, lambda i: (jnp.minimum(i, last), 0, 0)),
        ],
        out_specs=pl.BlockSpec((MOE_CAP, D_MODEL), lambda i: (i, 0)),
        out_shape=jax.ShapeDtypeStruct(((n_tiles + 1) * MOE_CAP, D_MODEL), BF16),
        scratch_shapes=[pltpu.VMEM((MOE_CAP,) + ROW_TILE, F32)],
        compiler_params=_params(("arbitrary",)),
        name="moe_gather",
    )(ld, h3)


def _table_kernel(cnt_ref, be_ref, src_ref, toff):
    n_tiles = cnt_ref.shape[0]
    n_slots = src_ref.shape[0]
    groups_per_tile = MOE_CAP // MOE_GRP
    step = 8

    for i in range(n_tiles):
        toff[i] = 0

    def per_expert(e, pos):
        def per_tile(i, p):
            g = (cnt_ref[i, e] + (MOE_GRP - 1)) // MOE_GRP
            base = i * groups_per_tile + toff[i]

            def put(j, c):
                for u in range(step):
                    src_ref[p + j * step + u] = base + j * step + u
                return c

            lax.fori_loop(0, (g + (step - 1)) // step, put, 0)
            toff[i] = toff[i] + g
            return p + g

        end = lax.fori_loop(0, n_tiles, per_tile, pos)
        new_pos = (end + (MOE_GPB - 1)) // MOE_GPB * MOE_GPB
        for u in range(MOE_GPB):
            src_ref[end + u] = -1

        def put_e(b, c):
            be_ref[b] = e
            return c

        lax.fori_loop(pos // MOE_GPB, new_pos // MOE_GPB, put_e, 0)
        return new_pos

    used = lax.fori_loop(0, N_EXPERTS, per_expert, 0)

    def tail(b, c):
        be_ref[b] = N_EXPERTS - 1
        for u in range(MOE_GPB):
            src_ref[b * MOE_GPB + u] = -1
        return c

    lax.fori_loop(used // MOE_GPB, n_slots // MOE_GPB, tail, 0)


def _block_table(cnt, n_blocks):
    smem = pl.BlockSpec(memory_space=pltpu.SMEM)
    return pl.pallas_call(
        _table_kernel,
        in_specs=[smem],
        out_specs=[smem, smem],
        out_shape=[jax.ShapeDtypeStruct((n_blocks,), jnp.int32),
                   jax.ShapeDtypeStruct((n_blocks * MOE_GPB,), jnp.int32)],
        scratch_shapes=[pltpu.SMEM((cnt.shape[0],), jnp.int32)],
        name="moe_block_table",
    )(cnt)


def _ffn_kernel(be_ref, src_ref, xs_hbm, w1_ref, b1_ref, w2_ref, b2_ref, ys_hbm,
                xbuf, ybuf, w1s, w2s, sem_in, sem_out, *, spare_grp):
    i = pl.program_id(0)
    nb = pl.num_programs(0)
    slot = i % 2

    def live(blk):
        return src_ref[blk * MOE_GPB] >= 0

    def group_rows(grp):
        return pl.ds(pl.multiple_of(grp * MOE_GRP, MOE_GRP), MOE_GRP)

    def start_in(blk, sl):
        for g in range(MOE_GPB):
            grp = src_ref[blk * MOE_GPB + g]
            grp = jnp.where(grp >= 0, grp, spare_grp)
            pltpu.make_async_copy(xs_hbm.at[group_rows(grp), :],
                                  xbuf.at[sl, pl.ds(g * MOE_GRP, MOE_GRP), :], sem_in.at[sl]).start()

    def start_out(blk, sl):
        for g in range(MOE_GPB):
            grp = src_ref[blk * MOE_GPB + g]
            grp = jnp.where(grp >= 0, grp, spare_grp + 1 + sl * MOE_GPB + g)
            pltpu.make_async_copy(ybuf.at[sl, pl.ds(g * MOE_GRP, MOE_GRP), :],
                                  ys_hbm.at[group_rows(grp), :], sem_out.at[sl]).start()

    def wait_in(sl):
        pltpu.make_async_copy(xs_hbm.at[pl.ds(0, MOE_BLOCK), :], xbuf.at[sl], sem_in.at[sl]).wait()

    def wait_out(sl):
        pltpu.make_async_copy(ybuf.at[sl], ys_hbm.at[pl.ds(0, MOE_BLOCK), :], sem_out.at[sl]).wait()

    @pl.when((i == 0) & live(0))
    def _():
        start_in(0, 0)

    nxt = jnp.minimum(i + 1, nb - 1)

    @pl.when((i + 1 < nb) & live(nxt))
    def _():
        start_in(nxt, 1 - slot)

    @pl.when((i >= 2) & live(jnp.maximum(i - 2, 0)))
    def _():
        wait_out(slot)

    prev = be_ref[jnp.maximum(i - 1, 0)]

    @pl.when((i == 0) | (be_ref[i] != prev))
    def _():
        w1s[...] = w1_ref[0].astype(BF16)
        w2s[...] = w2_ref[0].astype(BF16)

    @pl.when(live(i))
    def _():
        wait_in(slot)
        u = jnp.dot(xbuf[slot], w1s[...], preferred_element_type=F32) + b1_ref[0]
        gl = jnp.minimum(u[:, :D_FF], SWIGLU_LIMIT)
        lin = jnp.clip(u[:, D_FF:], -SWIGLU_LIMIT, SWIGLU_LIMIT)
        act = gl * _sigmoid(SWIGLU_ALPHA * gl) * (lin + 1.0)
        y = jnp.dot(act.astype(BF16), w2s[...], preferred_element_type=F32) + b2_ref[0]
        ybuf[slot] = y.astype(BF16)
        start_out(i, slot)

    @pl.when(i == nb - 1)
    def _():
        @pl.when(live(i))
        def _():
            wait_out(slot)

        @pl.when((i >= 1) & live(jnp.maximum(i - 1, 0)))
        def _():
            wait_out(1 - slot)


def _ffn(block_e, src, xs, w1, b1, w2, b2):
    r, d = xs.shape
    nb = block_e.shape[0]
    spare_grp = (r - MOE_CAP) // MOE_GRP
    assert 1 + 2 * MOE_GPB <= MOE_CAP // MOE_GRP
    grid_spec = pltpu.PrefetchScalarGridSpec(
        num_scalar_prefetch=2,
        grid=(nb,),
        in_specs=[
            pl.BlockSpec(memory_space=pl.ANY),
            pl.BlockSpec((1, d, 2 * D_FF), lambda i, be, sr: (be[i], 0, 0)),
            pl.BlockSpec((1, 1, 2 * D_FF), lambda i, be, sr: (be[i], 0, 0)),
            pl.BlockSpec((1, D_FF, d), lambda i, be, sr: (be[i], 0, 0)),
            pl.BlockSpec((1, 1, d), lambda i, be, sr: (be[i], 0, 0)),
        ],
        out_specs=pl.BlockSpec(memory_space=pl.ANY),
        scratch_shapes=[
            pltpu.VMEM((2, MOE_BLOCK, d), BF16), pltpu.VMEM((2, MOE_BLOCK, d), BF16),
            pltpu.VMEM((d, 2 * D_FF), BF16), pltpu.VMEM((D_FF, d), BF16),
            pltpu.SemaphoreType.DMA((2,)), pltpu.SemaphoreType.DMA((2,)),
        ],
    )
    return pl.pallas_call(
        functools.partial(_ffn_kernel, spare_grp=spare_grp),
        grid_spec=grid_spec,
        out_shape=jax.ShapeDtypeStruct((r, d), BF16),
        input_output_aliases={2: 0},
        compiler_params=_params(("arbitrary",)),
        name="expert_ffn",
    )(block_e, src, xs, w1, b1, w2, b2)


def _combine_kernel(ld_ref, w_ref, ys_ref, x1_ref, ga_ref, g_ref, o_ref, y3, o3):
    tl = MOE_TL

    def relay(r, carry):
        r0 = pl.multiple_of(r * RELAY_ROWS, RELAY_ROWS)
        y3[pl.ds(r0, RELAY_ROWS)] = _rows_to_tiles(ys_ref[pl.ds(r0, RELAY_ROWS), :].astype(F32))
        return carry

    lax.fori_loop(0, MOE_CAP // RELAY_ROWS, relay, 0)

    def pull(t, carry):
        acc = w_ref[0, 0, t] * y3[ld_ref[0, 0, t]]
        for k in range(1, TOP_K):
            acc = acc + w_ref[0, 0, k * tl + t] * y3[ld_ref[0, 0, k * tl + t]]
        o3[t] = acc
        return carry

    lax.fori_loop(0, tl, pull, 0, unroll=4)

    def finish(r, carry):
        r0 = pl.multiple_of(r * RELAY_ROWS, RELAY_ROWS)
        x2 = x1_ref[pl.ds(r0, RELAY_ROWS), :] + ga_ref[0] * _tiles_to_rows(o3[pl.ds(r0, RELAY_ROWS)])
        ms = jnp.mean(x2 * x2, axis=-1, keepdims=True)
        o_ref[pl.ds(r0, RELAY_ROWS), :] = x2 * lax.rsqrt(ms + EPS) * g_ref[...]
        return carry

    lax.fori_loop(0, tl // RELAY_ROWS, finish, 0)


def _combine(ld, wts, ys, x1f, mod3, g_final, tiles_per_batch):
    t, d = x1f.shape
    tl = MOE_TL
    n_tiles = t // tl
    smem = functools.partial(pl.BlockSpec, (1, 1, TOP_K * tl), lambda i: (i, 0, 0),
                             memory_space=pltpu.SMEM)
    return pl.pallas_call(
        _combine_kernel,
        grid=(n_tiles,),
        in_specs=[
            smem(), smem(),
            pl.BlockSpec((MOE_CAP, d), lambda i: (i, 0), pipeline_mode=pl.Buffered(1)),
            pl.BlockSpec((tl, d), lambda i: (i, 0)),
            pl.BlockSpec((1, 1, d), lambda i: ((i // tiles_per_batch) * N_MOD + 5, 0, 0)),
            pl.BlockSpec((1, d), lambda i: (0, 0)),
        ],
        out_specs=pl.BlockSpec((tl, d), lambda i: (i, 0)),
        out_shape=jax.ShapeDtypeStruct((t, d), F32),
        scratch_shapes=[pltpu.VMEM((MOE_CAP,) + ROW_TILE, F32), pltpu.VMEM((tl,) + ROW_TILE, F32)],
        compiler_params=_params(("arbitrary",)),
        name="moe_combine",
    )(ld, wts, ys, x1f, mod3, g_final)


def kernel(x, c, w_ada, b_ada, g_mix, w_in, conv_dw, conv_dw_bias, conv_ln_g, conv_ln_b,
           w_conv_out, lb_param, hgrn_norm_g, w_hgrn_out, w_out, g_ffn, w_router, b_router,
           w1, b1, w2, b2, g_final):
    b, s, d = x.shape
    assert w_ada.shape[0] == 1, "single-layer block"
    assert s % MOE_TL == 0
    t = b * s
    n_tiles = t // MOE_TL

    c_pad = jnp.zeros((8, d), F32).at[:b].set(c.astype(F32))
    mod, lb = _ada(c_pad, w_ada[0], b_ada, lb_param)
    mod3 = mod[:b].reshape(b * N_MOD, 1, d)

    dw_rows = jnp.broadcast_to(conv_dw[0][:, None, :], (CONV_K, SUBLANES, d))
    p, uc = _inproj_conv(x, g_mix, mod3, w_in[0].astype(BF16), dw_rows, conv_dw_bias, conv_ln_g,
                         conv_ln_b)
    uh = _hgrn_branch(p, lb, hgrn_norm_g)

    x1, h3, logits_t = _merge(uc, uh, p, x, mod3, g_ffn, w_conv_out[0].astype(BF16),
                              w_hgrn_out[0].astype(BF16), w_out[0].astype(BF16),
                              w_router[0].T, b_router[0][:, None])

    ld, wts, cnt = _route(logits_t)
    xs = _gather(ld, h3)
    groups_max = t * TOP_K // MOE_GRP + n_tiles * N_EXPERTS
    n_blocks = groups_max // MOE_GPB + N_EXPERTS
    block_e, src = _block_table(cnt[:, :, 0], n_blocks)
    ys = _ffn(block_e, src, xs, w1[0], b1[0][:, None, :], w2[0], b2[0][:, None, :])
    out = _combine(ld, wts, ys, x1.reshape(t, d), mod3, g_final.reshape(1, d), s // MOE_TL)
    return out.reshape(b, s, d)
```

```python
import functools

import jax
import jax.numpy as jnp
import numpy as np
from jax import lax
from jax.experimental import pallas as pl
from jax.experimental.pallas import tpu as pltpu

F32 = jnp.float32
BF16 = jnp.bfloat16

D_MODEL = 1024
CONV_K = 31
HG_HEADS = 8
HG_DK = 128
N_EXPERTS = 32
TOP_K = 4
D_FF = 1024
SWIGLU_ALPHA = 1.702
SWIGLU_LIMIT = 7.0
MOE_BLOCK = 256
EPS = 1e-6
N_MOD = 6
COL_CA, COL_CB, COL_Q, COL_F, COL_I, COL_G, COL_GC, COL_GH = range(8)
N_COLS = 8

HG_CHUNK = 128
CONV_HALO = 32
VMEM_LIMIT = 56 * 1024 * 1024


def _sigmoid(x):
    return 1.0 / (1.0 + jnp.exp(-x))


def _params(sem, vmem=VMEM_LIMIT):
    return pltpu.CompilerParams(dimension_semantics=sem, vmem_limit_bytes=vmem)


def _ada_kernel(c_ref, w_ref, b_ref, lbp_ref, mod_ref, lb_ref):
    c = c_ref[...]
    c_act = c * _sigmoid(c)
    mod_ref[...] = jnp.dot(c_act, w_ref[...], preferred_element_type=F32,
                           precision=lax.Precision.HIGHEST) + b_ref[...]
    p = lbp_ref[...]
    e = jnp.exp(p - jnp.max(p, axis=0, keepdims=True))
    lb_ref[...] = e[0:1, :] / jnp.sum(e, axis=0, keepdims=True)


def _ada(c_pad, w_ada, b_ada, lb_param):
    nb, d = c_pad.shape
    n = w_ada.shape[1]
    tn = 1536
    return pl.pallas_call(
        _ada_kernel,
        grid=(n // tn,),
        in_specs=[
            pl.BlockSpec((nb, d), lambda j: (0, 0)),
            pl.BlockSpec((d, tn), lambda j: (0, j)),
            pl.BlockSpec((1, tn), lambda j: (0, j)),
            pl.BlockSpec(lb_param.shape, lambda j: (0, 0)),
        ],
        out_specs=[
            pl.BlockSpec((nb, tn), lambda j: (0, j)),
            pl.BlockSpec((1, d), lambda j: (0, 0)),
        ],
        out_shape=[
            jax.ShapeDtypeStruct((nb, n), F32),
            jax.ShapeDtypeStruct((1, d), F32),
        ],
        compiler_params=_params(("arbitrary",)),
        name="ada_mod",
    )(c_pad, w_ada, b_ada, lb_param)


CONV_TS = 256
CONV_RG = 16
CONV_PHASES = 8
SUBLANES = 8
CONV_SPAN = CONV_TS + CONV_HALO
CONV_OFF = CONV_HALO - (CONV_K - 1)
N_PCOLS = N_COLS - 2
P_Q, P_F, P_I, P_G, P_GC, P_GH = range(N_PCOLS)


def _inproj_conv_kernel(x_ref, g_ref, sc_ref, sh_ref, w_hbm, dwb_ref, bias_ref, lng_ref, lnb_ref,
                        p_ref, uc_ref, buf, sh, cv, hbuf, wbf, wsem):
    i = pl.program_id(1)
    ts = CONV_TS
    d = D_MODEL

    @pl.when(i == 0)
    def _():
        buf[0:CONV_HALO, :] = jnp.zeros((CONV_HALO, d), F32)

    @pl.when(i > 0)
    def _():
        buf[0:CONV_HALO, :] = buf[ts:CONV_SPAN, :]

    @pl.when((i == 0) & (pl.program_id(0) == 0))
    def _():
        cp = pltpu.make_async_copy(w_hbm, wbf, wsem)
        cp.start()
        cp.wait()

    x = x_ref[0]
    ms = jnp.mean(x * x, axis=-1, keepdims=True)
    h = x * lax.rsqrt(ms + EPS) * g_ref[...]
    h = (h * (1.0 + sc_ref[0]) + sh_ref[0]).astype(BF16)

    ab = jnp.dot(h, wbf[:, 0:2 * d], preferred_element_type=F32)
    buf[CONV_HALO:CONV_SPAN, :] = ab[:, :d] * _sigmoid(ab[:, d:])
    hbuf[...] = h

    cols = (N_PCOLS * d) // CONV_PHASES
    rows_per_phase = ts // CONV_PHASES

    @pl.when(i >= 0)
    def _():
        for s in range(1, SUBLANES):
            sh[s - 1] = buf[s:s + CONV_SPAN - SUBLANES, :]

    for ph in range(CONV_PHASES):
        @pl.when(i >= 0)
        def _(ph=ph):
            lo = ph * cols
            p_ref[0, :, lo:lo + cols] = jnp.dot(hbuf[...], wbf[:, 2 * d + lo:2 * d + lo + cols],
                                                preferred_element_type=F32)
            for base in range(ph * rows_per_phase, (ph + 1) * rows_per_phase, CONV_RG):
                acc = jnp.zeros((CONV_RG // SUBLANES, SUBLANES, d), F32)
                for j in range(CONV_K):
                    s = (CONV_OFF + j) % SUBLANES
                    row = base + CONV_OFF + j - s
                    src = buf if s == 0 else sh.at[s - 1]
                    rows = src[row:row + CONV_RG, :].reshape(CONV_RG // SUBLANES, SUBLANES, d)
                    acc = acc + dwb_ref[j] * rows
                cv[base:base + CONV_RG, :] = acc.reshape(CONV_RG, d) + bias_ref[...]

    u = cv[...]
    mu = jnp.mean(u, axis=-1, keepdims=True)
    uc = u - mu
    var = jnp.mean(uc * uc, axis=-1, keepdims=True)
    y = uc * lax.rsqrt(var + EPS) * lng_ref[...] + lnb_ref[...]
    uc_ref[0] = (y * _sigmoid(y)).astype(BF16)


def _inproj_conv(x, g_mix, mod3, w_in_bf, dw_rows, bias, ln_g, ln_b):
    b, s, d = x.shape
    n = w_in_bf.shape[1]
    ts = CONV_TS
    vec = pl.BlockSpec((1, d), lambda bi, i: (0, 0))
    return pl.pallas_call(
        _inproj_conv_kernel,
        grid=(b, s // ts),
        in_specs=[
            pl.BlockSpec((1, ts, d), lambda bi, i: (bi, i, 0)),
            vec,
            pl.BlockSpec((1, 1, d), lambda bi, i: (bi * N_MOD + 1, 0, 0)),
            pl.BlockSpec((1, 1, d), lambda bi, i: (bi * N_MOD + 0, 0, 0)),
            pl.BlockSpec(memory_space=pl.ANY),
            pl.BlockSpec(dw_rows.shape, lambda bi, i: (0, 0, 0)),
            vec, vec, vec,
        ],
        out_specs=[
            pl.BlockSpec((1, ts, N_PCOLS * d), lambda bi, i: (bi, i, 0)),
            pl.BlockSpec((1, ts, d), lambda bi, i: (bi, i, 0)),
        ],
        out_shape=[
            jax.ShapeDtypeStruct((b, s, N_PCOLS * d), F32),
            jax.ShapeDtypeStruct((b, s, d), BF16),
        ],
        scratch_shapes=[pltpu.VMEM((CONV_SPAN, d), F32),
                        pltpu.VMEM((SUBLANES - 1, CONV_SPAN - SUBLANES, d), F32),
                        pltpu.VMEM((ts, d), F32), pltpu.VMEM((ts, d), BF16),
                        pltpu.VMEM((d, n), BF16), pltpu.SemaphoreType.DMA(())],
        compiler_params=_params(("arbitrary", "arbitrary")),
        name="inproj_conv",
    )(x, g_mix, mod3, mod3, w_in_bf, dw_rows, bias, ln_g, ln_b)


def _hgrn_levels():
    c = HG_CHUNK
    levels = []
    m = c // 2
    while m >= 1:
        levels.append(m)
        m //= 2
    return levels


def _level_exponent(g_inc, logf, m, row):
    c, d = g_inc.shape
    upper = (row & m) != 0
    if m == 1:
        return jnp.where(upper, logf, 0.0)
    if m >= SUBLANES:
        parts = [jnp.broadcast_to(g_inc[b * 2 * m + m - 1:b * 2 * m + m, :], (2 * m, d))
                 for b in range(c // (2 * m))]
        gref = parts[0] if len(parts) == 1 else jnp.concatenate(parts, axis=0)
    else:
        g3 = g_inc.reshape(c // SUBLANES, SUBLANES, d)

        def bcast(j):
            return jnp.broadcast_to(g3[:, j:j + 1, :], g3.shape).reshape(c, d)

        if 2 * m == SUBLANES:
            gref = bcast(m - 1)
        else:
            assert 4 * m == SUBLANES
            gref = jnp.where((row & (SUBLANES - 1)) < 2 * m, bcast(m - 1), bcast(3 * m - 1))
    return jnp.where(upper, 1.0, -1.0) * (g_inc - gref)


def _hgrn_pair_masks():
    c = HG_CHUNK
    t = np.arange(c)[:, None]
    s = np.arange(c)[None, :]
    masks = [((t // (2 * m)) == (s // (2 * m))) & ((t & m) != 0) & ((s & m) == 0)
             for m in _hgrn_levels()]
    masks.append(t == s)
    return np.stack(masks).astype(np.float32)


def _hgrn_kernel(q_ref, z_ref, v_ref, og_ref, lb_ref, ng_ref, tril_ref, pm_ref, o_ref, st):
    c = HG_CHUNK
    dk = HG_DK
    levels = _hgrn_levels()

    @pl.when(pl.program_id(1) == 0)
    def _():
        st[...] = jnp.zeros_like(st)

    z = z_ref[0]
    lb = lb_ref[...]
    sig = _sigmoid(z)
    f = lb + (1.0 - lb) * sig
    logf = jnp.log(f)
    kk = (1.0 - lb) * (1.0 - sig)
    q = q_ref[0] * (dk ** -0.5)
    v = v_ref[0]
    og = og_ref[0]

    hi = logf.astype(BF16)
    lo = (logf - hi.astype(F32)).astype(BF16)
    tril = tril_ref[...]
    g_inc = (jnp.dot(tril, hi, preferred_element_type=F32)
             + jnp.dot(tril, lo, preferred_element_type=F32))
    g_last = g_inc[c - 1:c, :]
    q_st = (q * jnp.exp(g_inc)).astype(BF16)
    k_st = (kk * jnp.exp(g_last - g_inc)).astype(BF16)
    dec_all = jnp.exp(g_last)
    v_bf = v.astype(BF16)
    q_bf = q.astype(BF16)
    k_bf = kk.astype(BF16)

    rr = lax.broadcasted_iota(jnp.int32, (c, 1), 0)

    qs, ks = [], []
    for m in levels:
        e = jnp.exp(_level_exponent(g_inc, logf, m, rr))
        qs.append((q * e).astype(BF16))
        ks.append((kk * e).astype(BF16))

    nt = (((1,), (1,)), ((), ()))
    tn = (((0,), (0,)), ((), ()))
    for h in range(HG_HEADS):
        sl = slice(h * dk, (h + 1) * dk)
        a = pm_ref[len(levels)] * lax.dot_general(q_bf[:, sl], k_bf[:, sl], nt,
                                                  preferred_element_type=F32)
        for li in range(len(levels)):
            a = a + pm_ref[li] * lax.dot_general(qs[li][:, sl], ks[li][:, sl], nt,
                                                 preferred_element_type=F32)
        s_t = st[h]
        o = jnp.dot(a.astype(BF16), v_bf[:, sl], preferred_element_type=F32)
        o = o + lax.dot_general(q_st[:, sl], s_t.astype(BF16), nt, preferred_element_type=F32)
        st[h] = s_t * dec_all[:, sl] + lax.dot_general(v_bf[:, sl], k_st[:, sl], tn,
                                                       preferred_element_type=F32)
        ms = jnp.mean(o * o, axis=-1, keepdims=True)
        o = o * lax.rsqrt(ms + EPS) * ng_ref[...]
        g = og[:, sl]
        o_ref[0, :, sl] = (o * (g * _sigmoid(g))).astype(BF16)


def _hgrn_branch(p, lb, norm_g):
    b, s, _ = p.shape
    d = D_MODEL
    c = HG_CHUNK
    tril = jnp.asarray(np.tril(np.ones((c, c), np.float32)), dtype=BF16)
    pair_masks = jnp.asarray(_hgrn_pair_masks())

    def col_spec(col):
        return pl.BlockSpec((1, c, d), lambda bi, i: (bi, i, col))

    return pl.pallas_call(
        _hgrn_kernel,
        grid=(b, s // c),
        in_specs=[
            col_spec(P_Q), col_spec(P_F), col_spec(P_I), col_spec(P_G),
            pl.BlockSpec((1, d), lambda bi, i: (0, 0)),
            pl.BlockSpec((1, HG_DK), lambda bi, i: (0, 0)),
            pl.BlockSpec((c, c), lambda bi, i: (0, 0)),
            pl.BlockSpec(pair_masks.shape, lambda bi, i: (0, 0, 0)),
        ],
        out_specs=pl.BlockSpec((1, c, d), lambda bi, i: (bi, i, 0)),
        out_shape=jax.ShapeDtypeStruct((b, s, d), BF16),
        scratch_shapes=[pltpu.VMEM((HG_HEADS, HG_DK, HG_DK), F32)],
        compiler_params=_params(("arbitrary", "arbitrary")),
        name="hgrn_branch",
    )(p, p, p, p, lb, norm_g, tril, pair_masks)


ROW_TILE = (8, 128)


def _rows_to_tiles(rows):
    st = jnp.stack([rows[:, j * 128:(j + 1) * 128] for j in range(ROW_TILE[0])], axis=0)
    return pltpu.einshape("jrl->rjl", st)


def _tiles_to_rows(tiles):
    y = pltpu.einshape("rjl->jrl", tiles)
    return jnp.concatenate([y[j] for j in range(ROW_TILE[0])], axis=-1)


_NT = (((1,), (1,)), ((), ()))
MERGE_SUB = 256


def _merge_kernel(uc_ref, uh_ref, gc_ref, gh_ref, x_ref, ga_ref, sc_ref, sh_ref, g_ref,
                  wc_ref, wh_ref, w_ref, wrt_ref, br_ref, x1_ref, h3_ref, lg_ref):
    wrt = wrt_ref[...]
    wrt_hi = wrt.astype(BF16)
    wrt_lo = (wrt - wrt_hi.astype(F32)).astype(BF16)
    for r in range(uc_ref.shape[1] // MERGE_SUB):
        rs = pl.ds(r * MERGE_SUB, MERGE_SUB)
        yc = jnp.dot(uc_ref[0, rs, :], wc_ref[...], preferred_element_type=F32)
        yh = jnp.dot(uh_ref[0, rs, :], wh_ref[...], preferred_element_type=F32)
        merged = (_sigmoid(gc_ref[0, rs, :]) * yc + _sigmoid(gh_ref[0, rs, :]) * yh).astype(BF16)
        x1 = x_ref[0, rs, :] + ga_ref[0] * jnp.dot(merged, w_ref[...],
                                                     preferred_element_type=F32)
        x1_ref[0, rs, :] = x1
        ms = jnp.mean(x1 * x1, axis=-1, keepdims=True)
        h2 = x1 * lax.rsqrt(ms + EPS) * g_ref[...]
        h2 = h2 * (1.0 + sc_ref[0]) + sh_ref[0]
        h3_ref[rs] = _rows_to_tiles(h2)
        h2_hi = h2.astype(BF16)
        h2_lo = (h2 - h2_hi.astype(F32)).astype(BF16)
        lg = (lax.dot_general(wrt_hi, h2_hi, _NT, preferred_element_type=F32)
              + lax.dot_general(wrt_hi, h2_lo, _NT, preferred_element_type=F32)
              + lax.dot_general(wrt_lo, h2_hi, _NT, preferred_element_type=F32))
        lg_ref[:, rs] = lg + br_ref[...]


def _merge(uc, uh, p, x, mod3, g_ffn, w_conv_bf, w_hgrn_bf, w_out_bf, w_router_t, b_router_col):
    b, s, d = x.shape
    tm = 512
    nt = s // tm
    ne = w_router_t.shape[0]

    def mod_spec(k):
        return pl.BlockSpec((1, 1, d), lambda bi, i: (bi * N_MOD + k, 0, 0))

    def col_spec(col):
        return pl.BlockSpec((1, tm, d), lambda bi, i: (bi, i, col))

    tile = pl.BlockSpec((1, tm, d), lambda bi, i: (bi, i, 0))
    weight = pl.BlockSpec((d, d), lambda bi, i: (0, 0))
    return pl.pallas_call(
        _merge_kernel,
        grid=(b, nt),
        in_specs=[
            tile, tile, col_spec(P_GC), col_spec(P_GH), tile,
            mod_spec(2), mod_spec(4), mod_spec(3),
            pl.BlockSpec((1, d), lambda bi, i: (0, 0)),
            weight, weight, weight,
            pl.BlockSpec((ne, d), lambda bi, i: (0, 0)),
            pl.BlockSpec((ne, 1), lambda bi, i: (0, 0)),
        ],
        out_specs=[
            tile,
            pl.BlockSpec((tm,) + ROW_TILE, lambda bi, i: (bi * nt + i, 0, 0)),
            pl.BlockSpec((ne, tm), lambda bi, i: (0, bi * nt + i)),
        ],
        out_shape=[
            jax.ShapeDtypeStruct((b, s, d), F32),
            jax.ShapeDtypeStruct((b * s,) + ROW_TILE, F32),
            jax.ShapeDtypeStruct((ne, b * s), F32),
        ],
        compiler_params=_params(("arbitrary", "arbitrary")),
        name="merge_router",
    )(uc, uh, p, p, x, mod3, mod3, mod3, g_ffn, w_conv_bf, w_hgrn_bf, w_out_bf, w_router_t,
      b_router_col)


MOE_TL = 1024
MOE_GRP = 16
MOE_CAP = MOE_TL * TOP_K + N_EXPERTS * MOE_GRP
MOE_GPB = MOE_BLOCK // MOE_GRP
RELAY_ROWS = 256
assert MOE_CAP % RELAY_ROWS == 0 and MOE_TL % RELAY_ROWS == 0


def _route_kernel(lg_ref, u_ref, ld_ref, w_ref, cnt_ref):
    tl = MOE_TL
    l = lg_ref[...]
    eio = lax.broadcasted_iota(jnp.int32, l.shape, 0)
    vals, hots = [], []
    for _ in range(TOP_K):
        m = jnp.max(l, axis=0, keepdims=True)
        idx = jnp.min(jnp.where(l == m, eio, N_EXPERTS), axis=0, keepdims=True)
        hot = eio == idx
        vals.append(m)
        hots.append(hot)
        l = jnp.where(hot, -jnp.inf, l)
    ex = [jnp.exp(v - vals[0]) for v in vals]
    den = ex[0] + ex[1] + ex[2] + ex[3]
    cnt = hots[0].astype(F32)
    for k in range(1, TOP_K):
        cnt = cnt + hots[k].astype(F32)
    prefix = jnp.dot(cnt.astype(BF16), u_ref[...], preferred_element_type=F32)
    n_e = jnp.sum(cnt, axis=1, keepdims=True)
    pad_e = jnp.floor((n_e + (MOE_GRP - 1)) * (1.0 / MOE_GRP)) * MOE_GRP
    scan = jnp.broadcast_to(pad_e, (N_EXPERTS, 128))
    ei = lax.broadcasted_iota(jnp.int32, scan.shape, 0)
    dist = 1
    while dist < N_EXPERTS:
        scan = scan + jnp.where(ei >= dist, pltpu.roll(scan, dist, axis=0), 0.0)
        dist *= 2
    base = prefix + (scan[:, 0:1] - pad_e)
    for k in range(TOP_K):
        dest = jnp.sum(jnp.where(hots[k], base, 0.0), axis=0, keepdims=True)
        ld_ref[0, :, k * tl:(k + 1) * tl] = dest.astype(jnp.int32)
        w_ref[0, :, k * tl:(k + 1) * tl] = ex[k] / den
    cnt_ref[0] = jnp.broadcast_to(n_e, (N_EXPERTS, 128)).astype(jnp.int32)


def _route(logits_t):
    ne, t = logits_t.shape
    tl = MOE_TL
    n_tiles = t // tl
    upper = jnp.asarray(np.triu(np.ones((tl, tl), np.float32), k=1), dtype=BF16)
    slot = pl.BlockSpec((1, 1, TOP_K * tl), lambda i: (i, 0, 0))
    return pl.pallas_call(
        _route_kernel,
        grid=(n_tiles,),
        in_specs=[pl.BlockSpec((ne, tl), lambda i: (0, i)),
                  pl.BlockSpec((tl, tl), lambda i: (0, 0))],
        out_specs=[slot, slot, pl.BlockSpec((1, ne, 128), lambda i: (i, 0, 0))],
        out_shape=[
            jax.ShapeDtypeStruct((n_tiles, 1, TOP_K * tl), jnp.int32),
            jax.ShapeDtypeStruct((n_tiles, 1, TOP_K * tl), F32),
            jax.ShapeDtypeStruct((n_tiles, ne, 128), jnp.int32),
        ],
        compiler_params=_params(("arbitrary",)),
        name="route",
    )(logits_t, upper)


def _gather_kernel(ld_ref, h3_ref, xs_ref, xs3):
    tl = MOE_TL
    last = pl.num_programs(0) - 1

    @pl.when((pl.program_id(0) == 0) | (pl.program_id(0) == last))
    def _():
        xs3[...] = jnp.zeros_like(xs3)

    def push(t, carry):
        tile = h3_ref[t]
        for k in range(TOP_K):
            xs3[ld_ref[0, 0, k * tl + t]] = tile
        return carry

    @pl.when(pl.program_id(0) < last)
    def _():
        lax.fori_loop(0, tl, push, 0, unroll=4)

    def relay(r, carry):
        r0 = pl.multiple_of(r * RELAY_ROWS, RELAY_ROWS)
        xs_ref[pl.ds(r0, RELAY_ROWS), :] = _tiles_to_rows(xs3[pl.ds(r0, RELAY_ROWS)]).astype(BF16)
        return carry

    lax.fori_loop(0, MOE_CAP // RELAY_ROWS, relay, 0)


def _gather(ld, h3):
    t = h3.shape[0]
    tl = MOE_TL
    n_tiles = t // tl
    last = n_tiles - 1
    return pl.pallas_call(
        _gather_kernel,
        grid=(n_tiles + 1,),
        in_specs=[
            pl.BlockSpec((1, 1, TOP_K * tl), lambda i: (jnp.minimum(i, last), 0, 0),
                         memory_space=pltpu.SMEM),
            pl.BlockSpec((tl,) + ROW_TILE, lambda i: (jnp.minimum(i, last), 0, 0)),
        ],
        out_specs=pl.BlockSpec((MOE_CAP, D_MODEL), lambda i: (i, 0)),
        out_shape=jax.ShapeDtypeStruct(((n_tiles + 1) * MOE_CAP, D_MODEL), BF16),
        scratch_shapes=[pltpu.VMEM((MOE_CAP,) + ROW_TILE, F32)],
        compiler_params=_params(("arbitrary",)),
        name="moe_gather",
    )(ld, h3)


def _table_kernel(cnt_ref, be_ref, src_ref, toff):
    n_tiles = cnt_ref.shape[0]
    n_slots = src_ref.shape[0]
    groups_per_tile = MOE_CAP // MOE_GRP
    step = 8

    for i in range(n_tiles):
        toff[i] = 0

    def per_expert(e, pos):
        def per_tile(i, p):
            g = (cnt_ref[i, e] + (MOE_GRP - 1)) // MOE_GRP
            base = i * groups_per_tile + toff[i]

            def put(j, c):
                for u in range(step):
                    src_ref[p + j * step + u] = base + j * step + u
                return c

            lax.fori_loop(0, (g + (step - 1)) // step, put, 0)
            toff[i] = toff[i] + g
            return p + g

        end = lax.fori_loop(0, n_tiles, per_tile, pos)
        new_pos = (end + (MOE_GPB - 1)) // MOE_GPB * MOE_GPB
        for u in range(MOE_GPB):
            src_ref[end + u] = -1

        def put_e(b, c):
            be_ref[b] = e
            return c

        lax.fori_loop(pos // MOE_GPB, new_pos // MOE_GPB, put_e, 0)
        return new_pos

    used = lax.fori_loop(0, N_EXPERTS, per_expert, 0)

    def tail(b, c):
        be_ref[b] = N_EXPERTS - 1
        for u in range(MOE_GPB):
            src_ref[b * MOE_GPB + u] = -1
        return c

    lax.fori_loop(used // MOE_GPB, n_slots // MOE_GPB, tail, 0)


def _block_table(cnt, n_blocks):
    smem = pl.BlockSpec(memory_space=pltpu.SMEM)
    return pl.pallas_call(
        _table_kernel,
        in_specs=[smem],
        out_specs=[smem, smem],
        out_shape=[jax.ShapeDtypeStruct((n_blocks,), jnp.int32),
                   jax.ShapeDtypeStruct((n_blocks * MOE_GPB,), jnp.int32)],
        scratch_shapes=[pltpu.SMEM((cnt.shape[0],), jnp.int32)],
        name="moe_block_table",
    )(cnt)


def _ffn_kernel(be_ref, src_ref, xs_hbm, w1_ref, b1_ref, w2_ref, b2_ref, ys_hbm,
                xbuf, ybuf, w1s, w2s, sem_in, sem_out, *, spare_grp):
    i = pl.program_id(0)
    nb = pl.num_programs(0)
    slot = i % 2

    def live(blk):
        return src_ref[blk * MOE_GPB] >= 0

    def group_rows(grp):
        return pl.ds(pl.multiple_of(grp * MOE_GRP, MOE_GRP), MOE_GRP)

    def start_in(blk, sl):
        for g in range(MOE_GPB):
            grp = src_ref[blk * MOE_GPB + g]
            grp = jnp.where(grp >= 0, grp, spare_grp)
            pltpu.make_async_copy(xs_hbm.at[group_rows(grp), :],
                                  xbuf.at[sl, pl.ds(g * MOE_GRP, MOE_GRP), :], sem_in.at[sl]).start()

    def start_out(blk, sl):
        for g in range(MOE_GPB):
            grp = src_ref[blk * MOE_GPB + g]
            grp = jnp.where(grp >= 0, grp, spare_grp + 1 + sl * MOE_GPB + g)
            pltpu.make_async_copy(ybuf.at[sl, pl.ds(g * MOE_GRP, MOE_GRP), :],
                                  ys_hbm.at[group_rows(grp), :], sem_out.at[sl]).start()

    def wait_in(sl):
        pltpu.make_async_copy(xs_hbm.at[pl.ds(0, MOE_BLOCK), :], xbuf.at[sl], sem_in.at[sl]).wait()

    def wait_out(sl):
        pltpu.make_async_copy(ybuf.at[sl], ys_hbm.at[pl.ds(0, MOE_BLOCK), :], sem_out.at[sl]).wait()

    @pl.when((i == 0) & live(0))
    def _():
        start_in(0, 0)

    nxt = jnp.minimum(i + 1, nb - 1)

    @pl.when((i + 1 < nb) & live(nxt))
    def _():
        start_in(nxt, 1 - slot)

    @pl.when((i >= 2) & live(jnp.maximum(i - 2, 0)))
    def _():
        wait_out(slot)

    prev = be_ref[jnp.maximum(i - 1, 0)]

    @pl.when((i == 0) | (be_ref[i] != prev))
    def _():
        w1s[...] = w1_ref[0].astype(BF16)
        w2s[...] = w2_ref[0].astype(BF16)

    @pl.when(live(i))
    def _():
        wait_in(slot)
        u = jnp.dot(xbuf[slot], w1s[...], preferred_element_type=F32) + b1_ref[0]
        gl = jnp.minimum(u[:, :D_FF], SWIGLU_LIMIT)
        lin = jnp.clip(u[:, D_FF:], -SWIGLU_LIMIT, SWIGLU_LIMIT)
        act = gl * _sigmoid(SWIGLU_ALPHA * gl) * (lin + 1.0)
        y = jnp.dot(act.astype(BF16), w2s[...], preferred_element_type=F32) + b2_ref[0]
        ybuf[slot] = y.astype(BF16)
        start_out(i, slot)

    @pl.when(i == nb - 1)
    def _():
        @pl.when(live(i))
        def _():
            wait_out(slot)

        @pl.when((i >= 1) & live(jnp.maximum(i - 1, 0)))
        def _():
            wait_out(1 - slot)


def _ffn(block_e, src, xs, w1, b1, w2, b2):
    r, d = xs.shape
    nb = block_e.shape[0]
    spare_grp = (r - MOE_CAP) // MOE_GRP
    assert 1 + 2 * MOE_GPB <= MOE_CAP // MOE_GRP
    grid_spec = pltpu.PrefetchScalarGridSpec(
        num_scalar_prefetch=2,
        grid=(nb,),
        in_specs=[
            pl.BlockSpec(memory_space=pl.ANY),
            pl.BlockSpec((1, d, 2 * D_FF), lambda i, be, sr: (be[i], 0, 0)),
            pl.BlockSpec((1, 1, 2 * D_FF), lambda i, be, sr: (be[i], 0, 0)),
            pl.BlockSpec((1, D_FF, d), lambda i, be, sr: (be[i], 0, 0)),
            pl.BlockSpec((1, 1, d), lambda i, be, sr: (be[i], 0, 0)),
        ],
        out_specs=pl.BlockSpec(memory_space=pl.ANY),
        scratch_shapes=[
            pltpu.VMEM((2, MOE_BLOCK, d), BF16), pltpu.VMEM((2, MOE_BLOCK, d), BF16),
            pltpu.VMEM((d, 2 * D_FF), BF16), pltpu.VMEM((D_FF, d), BF16),
            pltpu.SemaphoreType.DMA((2,)), pltpu.SemaphoreType.DMA((2,)),
        ],
    )
    return pl.pallas_call(
        functools.partial(_ffn_kernel, spare_grp=spare_grp),
        grid_spec=grid_spec,
        out_shape=jax.ShapeDtypeStruct((r, d), BF16),
        input_output_aliases={2: 0},
        compiler_params=_params(("arbitrary",)),
        name="expert_ffn",
    )(block_e, src, xs, w1, b1, w2, b2)


def _combine_kernel(ld_ref, w_ref, ys_ref, x1_ref, ga_ref, g_ref, o_ref, y3, o3):
    tl = MOE_TL

    def relay(r, carry):
        r0 = pl.multiple_of(r * RELAY_ROWS, RELAY_ROWS)
        y3[pl.ds(r0, RELAY_ROWS)] = _rows_to_tiles(ys_ref[pl.ds(r0, RELAY_ROWS), :].astype(F32))
        return carry

    lax.fori_loop(0, MOE_CAP // RELAY_ROWS, relay, 0)

    def pull(t, carry):
        acc = w_ref[0, 0, t] * y3[ld_ref[0, 0, t]]
        for k in range(1, TOP_K):
            acc = acc + w_ref[0, 0, k * tl + t] * y3[ld_ref[0, 0, k * tl + t]]
        o3[t] = acc
        return carry

    lax.fori_loop(0, tl, pull, 0, unroll=4)

    def finish(r, carry):
        r0 = pl.multiple_of(r * RELAY_ROWS, RELAY_ROWS)
        x2 = x1_ref[pl.ds(r0, RELAY_ROWS), :] + ga_ref[0] * _tiles_to_rows(o3[pl.ds(r0, RELAY_ROWS)])
        ms = jnp.mean(x2 * x2, axis=-1, keepdims=True)
        o_ref[pl.ds(r0, RELAY_ROWS), :] = x2 * lax.rsqrt(ms + EPS) * g_ref[...]
        return carry

    lax.fori_loop(0, tl // RELAY_ROWS, finish, 0)


def _combine(ld, wts, ys, x1f, mod3, g_final, tiles_per_batch):
    t, d = x1f.shape
    tl = MOE_TL
    n_tiles = t // tl
    smem = functools.partial(pl.BlockSpec, (1, 1, TOP_K * tl), lambda i: (i, 0, 0),
                             memory_space=pltpu.SMEM)
    return pl.pallas_call(
        _combine_kernel,
        grid=(n_tiles,),
        in_specs=[
            smem(), smem(),
            pl.BlockSpec((MOE_CAP, d), lambda i: (i, 0), pipeline_mode=pl.Buffered(1)),
            pl.BlockSpec((tl, d), lambda i: (i, 0)),
            pl.BlockSpec((1, 1, d), lambda i: ((i // tiles_per_batch) * N_MOD + 5, 0, 0)),
            pl.BlockSpec((1, d), lambda i: (0, 0)),
        ],
        out_specs=pl.BlockSpec((tl, d), lambda i: (i, 0)),
        out_shape=jax.ShapeDtypeStruct((t, d), F32),
        scratch_shapes=[pltpu.VMEM((MOE_CAP,) + ROW_TILE, F32), pltpu.VMEM((tl,) + ROW_TILE, F32)],
        compiler_params=_params(("arbitrary",)),
        name="moe_combine",
    )(ld, wts, ys, x1f, mod3, g_final)


def kernel(x, c, w_ada, b_ada, g_mix, w_in, conv_dw, conv_dw_bias, conv_ln_g, conv_ln_b,
           w_conv_out, lb_param, hgrn_norm_g, w_hgrn_out, w_out, g_ffn, w_router, b_router,
           w1, b1, w2, b2, g_final):
    b, s, d = x.shape
    assert w_ada.shape[0] == 1, "single-layer block"
    assert s % MOE_TL == 0
    t = b * s
    n_tiles = t // MOE_TL

    c_pad = jnp.zeros((8, d), F32).at[:b].set(c.astype(F32))
    mod, lb = _ada(c_pad, w_ada[0], b_ada, lb_param)
    mod3 = mod[:b].reshape(b * N_MOD, 1, d)

    dw_rows = jnp.broadcast_to(conv_dw[0][:, None, :], (CONV_K, SUBLANES, d))
    p, uc = _inproj_conv(x, g_mix, mod3, w_in[0].astype(BF16), dw_rows, conv_dw_bias, conv_ln_g,
                         conv_ln_b)
    uh = _hgrn_branch(p, lb, hgrn_norm_g)

    x1, h3, logits_t = _merge(uc, uh, p, x, mod3, g_ffn, w_conv_out[0].astype(BF16),
                              w_hgrn_out[0].astype(BF16), w_out[0].astype(BF16),
                              w_router[0].T, b_router[0][:, None])

    ld, wts, cnt = _route(logits_t)
    xs = _gather(ld, h3)
    groups_max = t * TOP_K // MOE_GRP + n_tiles * N_EXPERTS
    n_blocks = groups_max // MOE_GPB + N_EXPERTS
    block_e, src = _block_table(cnt[:, :, 0], n_blocks)
    ys = _ffn(block_e, src, xs, w1[0], b1[0][:, None, :], w2[0], b2[0][:, None, :])
    out = _combine(ld, wts, ys, x1.reshape(t, d), mod3, g_final.reshape(1, d), s // MOE_TL)
    return out.reshape(b, s, d)
```

```python
import functools

import jax
import jax.numpy as jnp
import numpy as np
from jax import lax
from jax.experimental import pallas as pl
from jax.experimental.pallas import tpu as pltpu

F32 = jnp.float32
BF16 = jnp.bfloat16

D_MODEL = 1024
CONV_K = 31
HG_HEADS = 8
HG_DK = 128
N_EXPERTS = 32
TOP_K = 4
D_FF = 1024
SWIGLU_ALPHA = 1.702
SWIGLU_LIMIT = 7.0
MOE_BLOCK = 512
EPS = 1e-6
N_MOD = 6
COL_CA, COL_CB, COL_Q, COL_F, COL_I, COL_G, COL_GC, COL_GH = range(8)
N_COLS = 8

HG_CHUNK = 128
CONV_HALO = 32
VMEM_LIMIT = 56 * 1024 * 1024


def _sigmoid(x):
    return 1.0 / (1.0 + jnp.exp(-x))


def _params(sem, vmem=VMEM_LIMIT):
    return pltpu.CompilerParams(dimension_semantics=sem, vmem_limit_bytes=vmem)


def _ada_kernel(c_ref, w_ref, b_ref, lbp_ref, mod_ref, lb_ref):
    c = c_ref[...]
    c_act = c * _sigmoid(c)
    mod_ref[...] = jnp.dot(c_act, w_ref[...], preferred_element_type=F32,
                           precision=lax.Precision.HIGHEST) + b_ref[...]
    p = lbp_ref[...]
    e = jnp.exp(p - jnp.max(p, axis=0, keepdims=True))
    lb_ref[...] = e[0:1, :] / jnp.sum(e, axis=0, keepdims=True)


def _ada(c_pad, w_ada, b_ada, lb_param):
    nb, d = c_pad.shape
    n = w_ada.shape[1]
    tn = 1536
    return pl.pallas_call(
        _ada_kernel,
        grid=(n // tn,),
        in_specs=[
            pl.BlockSpec((nb, d), lambda j: (0, 0)),
            pl.BlockSpec((d, tn), lambda j: (0, j)),
            pl.BlockSpec((1, tn), lambda j: (0, j)),
            pl.BlockSpec(lb_param.shape, lambda j: (0, 0)),
        ],
        out_specs=[
            pl.BlockSpec((nb, tn), lambda j: (0, j)),
            pl.BlockSpec((1, d), lambda j: (0, 0)),
        ],
        out_shape=[
            jax.ShapeDtypeStruct((nb, n), F32),
            jax.ShapeDtypeStruct((1, d), F32),
        ],
        compiler_params=_params(("arbitrary",)),
        name="ada_mod",
    )(c_pad, w_ada, b_ada, lb_param)


CONV_TS = 256
CONV_RG = 32
SUBLANES = 8
CONV_SPAN = CONV_TS + CONV_HALO
CONV_OFF = CONV_HALO - (CONV_K - 1)
N_PCOLS = N_COLS - 2
P_Q, P_F, P_I, P_G, P_GC, P_GH = range(N_PCOLS)


def _inproj_conv_kernel(x_ref, g_ref, sc_ref, sh_ref, w_ref, dwb_ref, bias_ref, lng_ref, lnb_ref,
                        p_ref, uc_ref, buf, sh, cv):
    i = pl.program_id(1)
    ts = CONV_TS
    d = D_MODEL

    @pl.when(i == 0)
    def _():
        buf[0:CONV_HALO, :] = jnp.zeros((CONV_HALO, d), F32)

    @pl.when(i > 0)
    def _():
        buf[0:CONV_HALO, :] = buf[ts:CONV_SPAN, :]

    x = x_ref[0]
    ms = jnp.mean(x * x, axis=-1, keepdims=True)
    h = x * lax.rsqrt(ms + EPS) * g_ref[...]
    h = (h * (1.0 + sc_ref[0]) + sh_ref[0]).astype(BF16)

    ab = jnp.dot(h, w_ref[:, 0:2 * d], preferred_element_type=F32)
    buf[CONV_HALO:CONV_SPAN, :] = ab[:, :d] * _sigmoid(ab[:, d:])
    p_ref[0] = jnp.dot(h, w_ref[:, 2 * d:], preferred_element_type=F32)

    for s in range(1, SUBLANES):
        sh[s - 1] = buf[s:s + CONV_SPAN - SUBLANES, :]
    for r in range(ts // CONV_RG):
        base = r * CONV_RG
        acc = jnp.zeros((CONV_RG // SUBLANES, SUBLANES, d), F32)
        for j in range(CONV_K):
            s = (CONV_OFF + j) % SUBLANES
            row = base + CONV_OFF + j - s
            src = buf if s == 0 else sh.at[s - 1]
            rows = src[row:row + CONV_RG, :].reshape(CONV_RG // SUBLANES, SUBLANES, d)
            acc = acc + dwb_ref[j] * rows
        cv[base:base + CONV_RG, :] = acc.reshape(CONV_RG, d) + bias_ref[...]
    u = cv[...]
    mu = jnp.mean(u, axis=-1, keepdims=True)
    uc = u - mu
    var = jnp.mean(uc * uc, axis=-1, keepdims=True)
    y = uc * lax.rsqrt(var + EPS) * lng_ref[...] + lnb_ref[...]
    uc_ref[0] = (y * _sigmoid(y)).astype(BF16)


def _inproj_conv(x, g_mix, mod3, w_in_bf, dw_rows, bias, ln_g, ln_b):
    b, s, d = x.shape
    n = w_in_bf.shape[1]
    ts = CONV_TS
    vec = pl.BlockSpec((1, d), lambda bi, i: (0, 0))
    return pl.pallas_call(
        _inproj_conv_kernel,
        grid=(b, s // ts),
        in_specs=[
            pl.BlockSpec((1, ts, d), lambda bi, i: (bi, i, 0)),
            vec,
            pl.BlockSpec((1, 1, d), lambda bi, i: (bi * N_MOD + 1, 0, 0)),
            pl.BlockSpec((1, 1, d), lambda bi, i: (bi * N_MOD + 0, 0, 0)),
            pl.BlockSpec((d, n), lambda bi, i: (0, 0), pipeline_mode=pl.Buffered(1)),
            pl.BlockSpec(dw_rows.shape, lambda bi, i: (0, 0, 0)),
            vec, vec, vec,
        ],
        out_specs=[
            pl.BlockSpec((1, ts, N_PCOLS * d), lambda bi, i: (bi, i, 0)),
            pl.BlockSpec((1, ts, d), lambda bi, i: (bi, i, 0)),
        ],
        out_shape=[
            jax.ShapeDtypeStruct((b, s, N_PCOLS * d), F32),
            jax.ShapeDtypeStruct((b, s, d), BF16),
        ],
        scratch_shapes=[pltpu.VMEM((CONV_SPAN, d), F32),
                        pltpu.VMEM((SUBLANES - 1, CONV_SPAN - SUBLANES, d), F32),
                        pltpu.VMEM((ts, d), F32)],
        compiler_params=_params(("arbitrary", "arbitrary")),
        name="inproj_conv",
    )(x, g_mix, mod3, mod3, w_in_bf, dw_rows, bias, ln_g, ln_b)


def _hgrn_levels():
    c = HG_CHUNK
    levels = []
    m = c // 2
    while m >= 1:
        levels.append(m)
        m //= 2
    return levels


def _level_exponent(g_inc, logf, m, row):
    c, d = g_inc.shape
    upper = (row & m) != 0
    if m == 1:
        return upper, jnp.where(upper, logf, 0.0)
    if m >= SUBLANES:
        parts = [jnp.broadcast_to(g_inc[b * 2 * m + m - 1:b * 2 * m + m, :], (2 * m, d))
                 for b in range(c // (2 * m))]
        gref = parts[0] if len(parts) == 1 else jnp.concatenate(parts, axis=0)
    else:
        g3 = g_inc.reshape(c // SUBLANES, SUBLANES, d)

        def bcast(j):
            return jnp.broadcast_to(g3[:, j:j + 1, :], g3.shape).reshape(c, d)

        if 2 * m == SUBLANES:
            gref = bcast(m - 1)
        else:
            assert 4 * m == SUBLANES
            gref = jnp.where((row & (SUBLANES - 1)) < 2 * m, bcast(m - 1), bcast(3 * m - 1))
    return upper, jnp.where(upper, g_inc - gref, gref - g_inc)


def _hgrn_kernel(q_ref, z_ref, v_ref, og_ref, lb_ref, ng_ref, tril_ref, o_ref, st):
    c = HG_CHUNK
    dk = HG_DK
    levels = _hgrn_levels()

    @pl.when(pl.program_id(1) == 0)
    def _():
        st[...] = jnp.zeros_like(st)

    tril = tril_ref[...]
    row = lax.broadcasted_iota(jnp.int32, (c, c), 0)
    col = lax.broadcasted_iota(jnp.int32, (c, c), 1)
    rr = lax.broadcasted_iota(jnp.int32, (c, 1), 0)
    masks = [(row >> int(np.log2(2 * m))) == (col >> int(np.log2(2 * m))) for m in levels]
    diag = row == col
    nt = (((1,), (1,)), ((), ()))
    tn = (((0,), (0,)), ((), ()))

    for h in range(HG_HEADS):
        sl = pl.ds(h * dk, dk)
        z = z_ref[0, :, sl]
        lb = lb_ref[:, sl]
        sig = _sigmoid(z)
        logf = jnp.log(lb + (1.0 - lb) * sig)
        kk = (1.0 - lb) * (1.0 - sig)
        q = q_ref[0, :, sl] * (dk ** -0.5)
        v_bf = v_ref[0, :, sl].astype(BF16)

        hi = logf.astype(BF16)
        lo = (logf - hi.astype(F32)).astype(BF16)
        g_inc = (jnp.dot(tril, hi, preferred_element_type=F32)
                 + jnp.dot(tril, lo, preferred_element_type=F32))
        g_last = g_inc[c - 1:c, :]
        q_st = (q * jnp.exp(g_inc)).astype(BF16)
        k_st = (kk * jnp.exp(g_last - g_inc)).astype(BF16)

        a = jnp.where(diag, lax.dot_general(q.astype(BF16), kk.astype(BF16), nt,
                                            preferred_element_type=F32), 0.0)
        for li, m in enumerate(levels):
            upper, ex = _level_exponent(g_inc, logf, m, rr)
            e = jnp.exp(ex)
            qs = jnp.where(upper, q * e, 0.0).astype(BF16)
            ks = jnp.where(upper, 0.0, kk * e).astype(BF16)
            a = a + jnp.where(masks[li],
                              lax.dot_general(qs, ks, nt, preferred_element_type=F32), 0.0)
        s_t = st[h]
        o = jnp.dot(a.astype(BF16), v_bf, preferred_element_type=F32)
        o = o + lax.dot_general(q_st, s_t.astype(BF16), nt, preferred_element_type=F32)
        st[h] = s_t * jnp.exp(g_last) + lax.dot_general(v_bf, k_st, tn,
                                                        preferred_element_type=F32)
        ms = jnp.mean(o * o, axis=-1, keepdims=True)
        o = o * lax.rsqrt(ms + EPS) * ng_ref[...]
        g = og_ref[0, :, sl]
        o_ref[0, :, sl] = (o * (g * _sigmoid(g))).astype(BF16)


def _hgrn_branch(p, lb, norm_g):
    b, s, _ = p.shape
    d = D_MODEL
    c = HG_CHUNK
    tril = jnp.asarray(np.tril(np.ones((c, c), np.float32)), dtype=BF16)

    def col_spec(col):
        return pl.BlockSpec((1, c, d), lambda bi, i: (bi, i, col))

    return pl.pallas_call(
        _hgrn_kernel,
        grid=(b, s // c),
        in_specs=[
            col_spec(P_Q), col_spec(P_F), col_spec(P_I), col_spec(P_G),
            pl.BlockSpec((1, d), lambda bi, i: (0, 0)),
            pl.BlockSpec((1, HG_DK), lambda bi, i: (0, 0)),
            pl.BlockSpec((c, c), lambda bi, i: (0, 0)),
        ],
        out_specs=pl.BlockSpec((1, c, d), lambda bi, i: (bi, i, 0)),
        out_shape=jax.ShapeDtypeStruct((b, s, d), BF16),
        scratch_shapes=[pltpu.VMEM((HG_HEADS, HG_DK, HG_DK), F32)],
        compiler_params=_params(("arbitrary", "arbitrary")),
        name="hgrn_branch",
    )(p, p, p, p, lb, norm_g, tril)


ROW_TILE = (8, 128)


def _rows_to_tiles(rows):
    st = jnp.stack([rows[:, j * 128:(j + 1) * 128] for j in range(ROW_TILE[0])], axis=0)
    return pltpu.einshape("jrl->rjl", st)


def _tiles_to_rows(tiles):
    y = pltpu.einshape("rjl->jrl", tiles)
    return jnp.concatenate([y[j] for j in range(ROW_TILE[0])], axis=-1)


_NT = (((1,), (1,)), ((), ()))
MERGE_SUB = 256


def _merge_kernel(uc_ref, uh_ref, gc_ref, gh_ref, x_ref, ga_ref, sc_ref, sh_ref, g_ref,
                  wc_ref, wh_ref, w_ref, wrt_ref, br_ref, x1_ref, h3_ref, lg_ref):
    wrt = wrt_ref[...]
    wrt_hi = wrt.astype(BF16)
    wrt_lo = (wrt - wrt_hi.astype(F32)).astype(BF16)
    for r in range(uc_ref.shape[1] // MERGE_SUB):
        rs = pl.ds(r * MERGE_SUB, MERGE_SUB)
        yc = jnp.dot(uc_ref[0, rs, :], wc_ref[...], preferred_element_type=F32)
        yh = jnp.dot(uh_ref[0, rs, :], wh_ref[...], preferred_element_type=F32)
        merged = (_sigmoid(gc_ref[0, rs, :]) * yc + _sigmoid(gh_ref[0, rs, :]) * yh).astype(BF16)
        x1 = x_ref[0, rs, :] + ga_ref[0] * jnp.dot(merged, w_ref[...],
                                                     preferred_element_type=F32)
        x1_ref[0, rs, :] = x1
        ms = jnp.mean(x1 * x1, axis=-1, keepdims=True)
        h2 = x1 * lax.rsqrt(ms + EPS) * g_ref[...]
        h2 = h2 * (1.0 + sc_ref[0]) + sh_ref[0]
        h3_ref[rs] = _rows_to_tiles(h2)
        h2_hi = h2.astype(BF16)
        h2_lo = (h2 - h2_hi.astype(F32)).astype(BF16)
        lg = (lax.dot_general(wrt_hi, h2_hi, _NT, preferred_element_type=F32)
              + lax.dot_general(wrt_hi, h2_lo, _NT, preferred_element_type=F32)
              + lax.dot_general(wrt_lo, h2_hi, _NT, preferred_element_type=F32))
        lg_ref[:, rs] = lg + br_ref[...]


def _merge(uc, uh, p, x, mod3, g_ffn, w_conv_bf, w_hgrn_bf, w_out_bf, w_router_t, b_router_col):
    b, s, d = x.shape
    tm = 512
    nt = s // tm
    ne = w_router_t.shape[0]

    def mod_spec(k):
        return pl.BlockSpec((1, 1, d), lambda bi, i: (bi * N_MOD + k, 0, 0))

    def col_spec(col):
        return pl.BlockSpec((1, tm, d), lambda bi, i: (bi, i, col))

    tile = pl.BlockSpec((1, tm, d), lambda bi, i: (bi, i, 0))
    weight = pl.BlockSpec((d, d), lambda bi, i: (0, 0))
    return pl.pallas_call(
        _merge_kernel,
        grid=(b, nt),
        in_specs=[
            tile, tile, col_spec(P_GC), col_spec(P_GH), tile,
            mod_spec(2), mod_spec(4), mod_spec(3),
            pl.BlockSpec((1, d), lambda bi, i: (0, 0)),
            weight, weight, weight,
            pl.BlockSpec((ne, d), lambda bi, i: (0, 0)),
            pl.BlockSpec((ne, 1), lambda bi, i: (0, 0)),
        ],
        out_specs=[
            tile,
            pl.BlockSpec((tm,) + ROW_TILE, lambda bi, i: (bi * nt + i, 0, 0)),
            pl.BlockSpec((ne, tm), lambda bi, i: (0, bi * nt + i)),
        ],
        out_shape=[
            jax.ShapeDtypeStruct((b, s, d), F32),
            jax.ShapeDtypeStruct((b * s,) + ROW_TILE, F32),
            jax.ShapeDtypeStruct((ne, b * s), F32),
        ],
        compiler_params=_params(("arbitrary", "arbitrary")),
        name="merge_router",
    )(uc, uh, p, p, x, mod3, mod3, mod3, g_ffn, w_conv_bf, w_hgrn_bf, w_out_bf, w_router_t,
      b_router_col)


MOE_TL = 1024
MOE_GRP = 16
MOE_CAP = MOE_TL * TOP_K + N_EXPERTS * MOE_GRP
MOE_GPB = MOE_BLOCK // MOE_GRP
RELAY_ROWS = 256
assert MOE_CAP % RELAY_ROWS == 0 and MOE_TL % RELAY_ROWS == 0


def _route_kernel(lg_ref, u_ref, ld_ref, w_ref, cnt_ref):
    tl = MOE_TL
    l = lg_ref[...]
    eio = lax.broadcasted_iota(jnp.int32, l.shape, 0)
    vals, hots = [], []
    for _ in range(TOP_K):
        m = jnp.max(l, axis=0, keepdims=True)
        idx = jnp.min(jnp.where(l == m, eio, N_EXPERTS), axis=0, keepdims=True)
        hot = eio == idx
        vals.append(m)
        hots.append(hot)
        l = jnp.where(hot, -jnp.inf, l)
    ex = [jnp.exp(v - vals[0]) for v in vals]
    den = ex[0] + ex[1] + ex[2] + ex[3]
    cnt = hots[0].astype(F32)
    for k in range(1, TOP_K):
        cnt = cnt + hots[k].astype(F32)
    prefix = jnp.dot(cnt.astype(BF16), u_ref[...], preferred_element_type=F32)
    n_e = jnp.sum(cnt, axis=1, keepdims=True)
    pad_e = jnp.floor((n_e + (MOE_GRP - 1)) * (1.0 / MOE_GRP)) * MOE_GRP
    scan = jnp.broadcast_to(pad_e, (N_EXPERTS, 128))
    ei = lax.broadcasted_iota(jnp.int32, scan.shape, 0)
    dist = 1
    while dist < N_EXPERTS:
        scan = scan + jnp.where(ei >= dist, pltpu.roll(scan, dist, axis=0), 0.0)
        dist *= 2
    base = prefix + (scan[:, 0:1] - pad_e)
    for k in range(TOP_K):
        dest = jnp.sum(jnp.where(hots[k], base, 0.0), axis=0, keepdims=True)
        ld_ref[0, :, k * tl:(k + 1) * tl] = dest.astype(jnp.int32)
        w_ref[0, :, k * tl:(k + 1) * tl] = ex[k] / den
    cnt_ref[0] = jnp.broadcast_to(n_e, (N_EXPERTS, 128)).astype(jnp.int32)


def _route(logits_t):
    ne, t = logits_t.shape
    tl = MOE_TL
    n_tiles = t // tl
    upper = jnp.asarray(np.triu(np.ones((tl, tl), np.float32), k=1), dtype=BF16)
    slot = pl.BlockSpec((1, 1, TOP_K * tl), lambda i: (i, 0, 0))
    return pl.pallas_call(
        _route_kernel,
        grid=(n_tiles,),
        in_specs=[pl.BlockSpec((ne, tl), lambda i: (0, i)),
                  pl.BlockSpec((tl, tl), lambda i: (0, 0))],
        out_specs=[slot, slot, pl.BlockSpec((1, ne, 128), lambda i: (i, 0, 0))],
        out_shape=[
            jax.ShapeDtypeStruct((n_tiles, 1, TOP_K * tl), jnp.int32),
            jax.ShapeDtypeStruct((n_tiles, 1, TOP_K * tl), F32),
            jax.ShapeDtypeStruct((n_tiles, ne, 128), jnp.int32),
        ],
        compiler_params=_params(("arbitrary",)),
        name="route",
    )(logits_t, upper)


def _gather_kernel(ld_ref, h3_ref, xs_ref, xs3):
    tl = MOE_TL
    last = pl.num_programs(0) - 1

    @pl.when((pl.program_id(0) == 0) | (pl.program_id(0) == last))
    def _():
        xs3[...] = jnp.zeros_like(xs3)

    def push(t, carry):
        tile = h3_ref[t]
        for k in range(TOP_K):
            xs3[ld_ref[0, 0, k * tl + t]] = tile
        return carry

    @pl.when(pl.program_id(0) < last)
    def _():
        lax.fori_loop(0, tl, push, 0, unroll=4)

    def relay(r, carry):
        r0 = pl.multiple_of(r * RELAY_ROWS, RELAY_ROWS)
        xs_ref[pl.ds(r0, RELAY_ROWS), :] = _tiles_to_rows(xs3[pl.ds(r0, RELAY_ROWS)]).astype(BF16)
        return carry

    lax.fori_loop(0, MOE_CAP // RELAY_ROWS, relay, 0)


def _gather(ld, h3):
    t = h3.shape[0]
    tl = MOE_TL
    n_tiles = t // tl
    last = n_tiles - 1
    return pl.pallas_call(
        _gather_kernel,
        grid=(n_tiles + 1,),
        in_specs=[
            pl.BlockSpec((1, 1, TOP_K * tl), lambda i: (jnp.minimum(i, last), 0, 0),
                         memory_space=pltpu.SMEM),
            pl.BlockSpec((tl,) + ROW_TILE, lambda i: (jnp.minimum(i, last), 0, 0)),
        ],
        out_specs=pl.BlockSpec((MOE_CAP, D_MODEL), lambda i: (i, 0)),
        out_shape=jax.ShapeDtypeStruct(((n_tiles + 1) * MOE_CAP, D_MODEL), BF16),
        scratch_shapes=[pltpu.VMEM((MOE_CAP,) + ROW_TILE, F32)],
        compiler_params=_params(("arbitrary",)),
        name="moe_gather",
    )(ld, h3)


def _table_kernel(cnt_ref, be_ref, src_ref, toff):
    n_tiles = cnt_ref.shape[0]
    n_slots = src_ref.shape[0]
    groups_per_tile = MOE_CAP // MOE_GRP
    step = 8

    for i in range(n_tiles):
        toff[i] = 0

    def per_expert(e, pos):
        def per_tile(i, p):
            g = (cnt_ref[i, e] + (MOE_GRP - 1)) // MOE_GRP
            base = i * groups_per_tile + toff[i]

            def put(j, c):
                for u in range(step):
                    src_ref[p + j * step + u] = base + j * step + u
                return c

            lax.fori_loop(0, (g + (step - 1)) // step, put, 0)
            toff[i] = toff[i] + g
            return p + g

        end = lax.fori_loop(0, n_tiles, per_tile, pos)
        new_pos = (end + (MOE_GPB - 1)) // MOE_GPB * MOE_GPB
        for u in range(MOE_GPB):
            src_ref[end + u] = -1

        def put_e(b, c):
            be_ref[b] = e
            return c

        lax.fori_loop(pos // MOE_GPB, new_pos // MOE_GPB, put_e, 0)
        return new_pos

    used = lax.fori_loop(0, N_EXPERTS, per_expert, 0)

    def tail(b, c):
        be_ref[b] = N_EXPERTS - 1
        for u in range(MOE_GPB):
            src_ref[b * MOE_GPB + u] = -1
        return c

    lax.fori_loop(used // MOE_GPB, n_slots // MOE_GPB, tail, 0)


def _block_table(cnt, n_blocks):
    smem = pl.BlockSpec(memory_space=pltpu.SMEM)
    return pl.pallas_call(
        _table_kernel,
        in_specs=[smem],
        out_specs=[smem, smem],
        out_shape=[jax.ShapeDtypeStruct((n_blocks,), jnp.int32),
                   jax.ShapeDtypeStruct((n_blocks * MOE_GPB,), jnp.int32)],
        scratch_shapes=[pltpu.SMEM((cnt.shape[0],), jnp.int32)],
        name="moe_block_table",
    )(cnt)


def _ffn_kernel(be_ref, src_ref, xs_hbm, w1_ref, b1_ref, w2_ref, b2_ref, ys_hbm,
                xbuf, ybuf, w1s, w2s, sem_in, sem_out, *, spare_grp):
    i = pl.program_id(0)
    nb = pl.num_programs(0)
    slot = i % 2

    def live(blk):
        return src_ref[blk * MOE_GPB] >= 0

    def group_rows(grp):
        return pl.ds(pl.multiple_of(grp * MOE_GRP, MOE_GRP), MOE_GRP)

    def start_in(blk, sl):
        for g in range(MOE_GPB):
            grp = src_ref[blk * MOE_GPB + g]
            grp = jnp.where(grp >= 0, grp, spare_grp)
            pltpu.make_async_copy(xs_hbm.at[group_rows(grp), :],
                                  xbuf.at[sl, pl.ds(g * MOE_GRP, MOE_GRP), :], sem_in.at[sl]).start()

    def start_out(blk, sl):
        for g in range(MOE_GPB):
            grp = src_ref[blk * MOE_GPB + g]
            grp = jnp.where(grp >= 0, grp, spare_grp + 1 + sl * MOE_GPB + g)
            pltpu.make_async_copy(ybuf.at[sl, pl.ds(g * MOE_GRP, MOE_GRP), :],
                                  ys_hbm.at[group_rows(grp), :], sem_out.at[sl]).start()

    def wait_in(sl):
        pltpu.make_async_copy(xs_hbm.at[pl.ds(0, MOE_BLOCK), :], xbuf.at[sl], sem_in.at[sl]).wait()

    def wait_out(sl):
        pltpu.make_async_copy(ybuf.at[sl], ys_hbm.at[pl.ds(0, MOE_BLOCK), :], sem_out.at[sl]).wait()

    @pl.when((i == 0) & live(0))
    def _():
        start_in(0, 0)

    nxt = jnp.minimum(i + 1, nb - 1)

    @pl.when((i + 1 < nb) & live(nxt))
    def _():
        start_in(nxt, 1 - slot)

    @pl.when((i >= 2) & live(jnp.maximum(i - 2, 0)))
    def _():
        wait_out(slot)

    prev = be_ref[jnp.maximum(i - 1, 0)]

    @pl.when((i == 0) | (be_ref[i] != prev))
    def _():
        w1s[...] = w1_ref[0].astype(BF16)
        w2s[...] = w2_ref[0].astype(BF16)

    @pl.when(live(i))
    def _():
        wait_in(slot)
        u = jnp.dot(xbuf[slot], w1s[...], preferred_element_type=F32) + b1_ref[0]
        gl = jnp.minimum(u[:, :D_FF], SWIGLU_LIMIT)
        lin = jnp.clip(u[:, D_FF:], -SWIGLU_LIMIT, SWIGLU_LIMIT)
        act = gl * _sigmoid(SWIGLU_ALPHA * gl) * (lin + 1.0)
        y = jnp.dot(act.astype(BF16), w2s[...], preferred_element_type=F32) + b2_ref[0]
        ybuf[slot] = y.astype(BF16)
        start_out(i, slot)

    @pl.when(i == nb - 1)
    def _():
        @pl.when(live(i))
        def _():
            wait_out(slot)

        @pl.when((i >= 1) & live(jnp.maximum(i - 1, 0)))
        def _():
            wait_out(1 - slot)


def _ffn(block_e, src, xs, w1, b1, w2, b2):
    r, d = xs.shape
    nb = block_e.shape[0]
    spare_grp = (r - MOE_CAP) // MOE_GRP
    assert 1 + 2 * MOE_GPB <= MOE_CAP // MOE_GRP
    grid_spec = pltpu.PrefetchScalarGridSpec(
        num_scalar_prefetch=2,
        grid=(nb,),
        in_specs=[
            pl.BlockSpec(memory_space=pl.ANY),
            pl.BlockSpec((1, d, 2 * D_FF), lambda i, be, sr: (be[i], 0, 0)),
            pl.BlockSpec((1, 1, 2 * D_FF), lambda i, be, sr: (be[i], 0, 0)),
            pl.BlockSpec((1, D_FF, d), lambda i, be, sr: (be[i], 0, 0)),
            pl.BlockSpec((1, 1, d), lambda i, be, sr: (be[i], 0, 0)),
        ],
        out_specs=pl.BlockSpec(memory_space=pl.ANY),
        scratch_shapes=[
            pltpu.VMEM((2, MOE_BLOCK, d), BF16), pltpu.VMEM((2, MOE_BLOCK, d), BF16),
            pltpu.VMEM((d, 2 * D_FF), BF16), pltpu.VMEM((D_FF, d), BF16),
            pltpu.SemaphoreType.DMA((2,)), pltpu.SemaphoreType.DMA((2,)),
        ],
    )
    return pl.pallas_call(
        functools.partial(_ffn_kernel, spare_grp=spare_grp),
        grid_spec=grid_spec,
        out_shape=jax.ShapeDtypeStruct((r, d), BF16),
        input_output_aliases={2: 0},
        compiler_params=_params(("arbitrary",)),
        name="expert_ffn",
    )(block_e, src, xs, w1, b1, w2, b2)


def _combine_kernel(ld_ref, w_ref, ys_ref, x1_ref, ga_ref, g_ref, o_ref, y3, o3):
    tl = MOE_TL

    def relay(r, carry):
        r0 = pl.multiple_of(r * RELAY_ROWS, RELAY_ROWS)
        y3[pl.ds(r0, RELAY_ROWS)] = _rows_to_tiles(ys_ref[pl.ds(r0, RELAY_ROWS), :].astype(F32))
        return carry

    lax.fori_loop(0, MOE_CAP // RELAY_ROWS, relay, 0)

    def pull(t, carry):
        acc = w_ref[0, 0, t] * y3[ld_ref[0, 0, t]]
        for k in range(1, TOP_K):
            acc = acc + w_ref[0, 0, k * tl + t] * y3[ld_ref[0, 0, k * tl + t]]
        o3[t] = acc
        return carry

    lax.fori_loop(0, tl, pull, 0, unroll=4)

    def finish(r, carry):
        r0 = pl.multiple_of(r * RELAY_ROWS, RELAY_ROWS)
        x2 = x1_ref[pl.ds(r0, RELAY_ROWS), :] + ga_ref[0] * _tiles_to_rows(o3[pl.ds(r0, RELAY_ROWS)])
        ms = jnp.mean(x2 * x2, axis=-1, keepdims=True)
        o_ref[pl.ds(r0, RELAY_ROWS), :] = x2 * lax.rsqrt(ms + EPS) * g_ref[...]
        return carry

    lax.fori_loop(0, tl // RELAY_ROWS, finish, 0)


def _combine(ld, wts, ys, x1f, mod3, g_final, tiles_per_batch):
    t, d = x1f.shape
    tl = MOE_TL
    n_tiles = t // tl
    smem = functools.partial(pl.BlockSpec, (1, 1, TOP_K * tl), lambda i: (i, 0, 0),
                             memory_space=pltpu.SMEM)
    return pl.pallas_call(
        _combine_kernel,
        grid=(n_tiles,),
        in_specs=[
            smem(), smem(),
            pl.BlockSpec((MOE_CAP, d), lambda i: (i, 0), pipeline_mode=pl.Buffered(1)),
            pl.BlockSpec((tl, d), lambda i: (i, 0)),
            pl.BlockSpec((1, 1, d), lambda i: ((i // tiles_per_batch) * N_MOD + 5, 0, 0)),
            pl.BlockSpec((1, d), lambda i: (0, 0)),
        ],
        out_specs=pl.BlockSpec((tl, d), lambda i: (i, 0)),
        out_shape=jax.ShapeDtypeStruct((t, d), F32),
        scratch_shapes=[pltpu.VMEM((MOE_CAP,) + ROW_TILE, F32), pltpu.VMEM((tl,) + ROW_TILE, F32)],
        compiler_params=_params(("arbitrary",)),
        name="moe_combine",
    )(ld, wts, ys, x1f, mod3, g_final)


def kernel(x, c, w_ada, b_ada, g_mix, w_in, conv_dw, conv_dw_bias, conv_ln_g, conv_ln_b,
           w_conv_out, lb_param, hgrn_norm_g, w_hgrn_out, w_out, g_ffn, w_router, b_router,
           w1, b1, w2, b2, g_final):
    b, s, d = x.shape
    assert w_ada.shape[0] == 1, "single-layer block"
    assert s % MOE_TL == 0
    t = b * s
    n_tiles = t // MOE_TL

    c_pad = jnp.zeros((8, d), F32).at[:b].set(c.astype(F32))
    mod, lb = _ada(c_pad, w_ada[0], b_ada, lb_param)
    mod3 = mod[:b].reshape(b * N_MOD, 1, d)

    dw_rows = jnp.broadcast_to(conv_dw[0][:, None, :], (CONV_K, SUBLANES, d))
    p, uc = _inproj_conv(x, g_mix, mod3, w_in[0].astype(BF16), dw_rows, conv_dw_bias, conv_ln_g,
                         conv_ln_b)
    uh = _hgrn_branch(p, lb, hgrn_norm_g)

    x1, h3, logits_t = _merge(uc, uh, p, x, mod3, g_ffn, w_conv_out[0].astype(BF16),
                              w_hgrn_out[0].astype(BF16), w_out[0].astype(BF16),
                              w_router[0].T, b_router[0][:, None])

    ld, wts, cnt = _route(logits_t)
    xs = _gather(ld, h3)
    groups_max = t * TOP_K // MOE_GRP + n_tiles * N_EXPERTS
    n_blocks = groups_max // MOE_GPB + N_EXPERTS
    block_e, src = _block_table(cnt[:, :, 0], n_blocks)
    ys = _ffn(block_e, src, xs, w1[0], b1[0][:, None, :], w2[0], b2[0][:, None, :])
    out = _combine(ld, wts, ys, x1.reshape(t, d), mod3, g_final.reshape(1, d), s // MOE_TL)
    return out.reshape(b, s, d)
```

```python
import functools

import jax
import jax.numpy as jnp
import numpy as np
from jax import lax
from jax.experimental import pallas as pl
from jax.experimental.pallas import tpu as pltpu

F32 = jnp.float32
BF16 = jnp.bfloat16

D_MODEL = 1024
CONV_K = 31
HG_HEADS = 8
HG_DK = 128
N_EXPERTS = 32
TOP_K = 4
D_FF = 1024
SWIGLU_ALPHA = 1.702
SWIGLU_LIMIT = 7.0
MOE_BLOCK = 512
EPS = 1e-6
N_MOD = 6
COL_CA, COL_CB, COL_Q, COL_F, COL_I, COL_G, COL_GC, COL_GH = range(8)
N_COLS = 8

HG_CHUNK = 128
CONV_HALO = 32
VMEM_LIMIT = 56 * 1024 * 1024
COMBINE_VMEM_LIMIT = 62 * 1024 * 1024


def _sigmoid(x):
    return 1.0 / (1.0 + jnp.exp(-x))


def _params(sem, vmem=VMEM_LIMIT):
    return pltpu.CompilerParams(dimension_semantics=sem, vmem_limit_bytes=vmem)


def _ada_kernel(c_ref, w_ref, b_ref, lbp_ref, mod_ref, lb_ref):
    c = c_ref[...]
    c_act = c * _sigmoid(c)
    mod_ref[...] = jnp.dot(c_act, w_ref[...], preferred_element_type=F32,
                           precision=lax.Precision.HIGHEST) + b_ref[...]
    p = lbp_ref[...]
    e = jnp.exp(p - jnp.max(p, axis=0, keepdims=True))
    lb_ref[...] = e[0:1, :] / jnp.sum(e, axis=0, keepdims=True)


def _ada(c_pad, w_ada, b_ada, lb_param):
    nb, d = c_pad.shape
    n = w_ada.shape[1]
    tn = 1536
    return pl.pallas_call(
        _ada_kernel,
        grid=(n // tn,),
        in_specs=[
            pl.BlockSpec((nb, d), lambda j: (0, 0)),
            pl.BlockSpec((d, tn), lambda j: (0, j)),
            pl.BlockSpec((1, tn), lambda j: (0, j)),
            pl.BlockSpec(lb_param.shape, lambda j: (0, 0)),
        ],
        out_specs=[
            pl.BlockSpec((nb, tn), lambda j: (0, j)),
            pl.BlockSpec((1, d), lambda j: (0, 0)),
        ],
        out_shape=[
            jax.ShapeDtypeStruct((nb, n), F32),
            jax.ShapeDtypeStruct((1, d), F32),
        ],
        compiler_params=_params(("arbitrary",)),
        name="ada_mod",
    )(c_pad, w_ada, b_ada, lb_param)


CONV_TS = 256
CONV_RG = 32
SUBLANES = 8
CONV_SPAN = CONV_TS + CONV_HALO
CONV_OFF = CONV_HALO - (CONV_K - 1)
N_PCOLS = N_COLS - 2
P_Q, P_F, P_I, P_G, P_GC, P_GH = range(N_PCOLS)


def _inproj_conv_kernel(x_ref, g_ref, sc_ref, sh_ref, w_ref, dwb_ref, bias_ref, lng_ref, lnb_ref,
                        p_ref, uc_ref, buf, sh, cv):
    i = pl.program_id(1)
    ts = CONV_TS
    d = D_MODEL

    @pl.when(i == 0)
    def _():
        buf[0:CONV_HALO, :] = jnp.zeros((CONV_HALO, d), F32)

    @pl.when(i > 0)
    def _():
        buf[0:CONV_HALO, :] = buf[ts:CONV_SPAN, :]

    x = x_ref[0]
    ms = jnp.mean(x * x, axis=-1, keepdims=True)
    h = x * lax.rsqrt(ms + EPS) * g_ref[...]
    h = (h * (1.0 + sc_ref[0]) + sh_ref[0]).astype(BF16)

    ab = jnp.dot(h, w_ref[:, 0:2 * d], preferred_element_type=F32)
    buf[CONV_HALO:CONV_SPAN, :] = ab[:, :d] * _sigmoid(ab[:, d:])
    p_ref[0] = jnp.dot(h, w_ref[:, 2 * d:], preferred_element_type=F32)

    for s in range(1, SUBLANES):
        sh[s - 1] = buf[s:s + CONV_SPAN - SUBLANES, :]
    for r in range(ts // CONV_RG):
        base = r * CONV_RG
        acc = jnp.zeros((CONV_RG // SUBLANES, SUBLANES, d), F32)
        for j in range(CONV_K):
            s = (CONV_OFF + j) % SUBLANES
            row = base + CONV_OFF + j - s
            src = buf if s == 0 else sh.at[s - 1]
            rows = src[row:row + CONV_RG, :].reshape(CONV_RG // SUBLANES, SUBLANES, d)
            acc = acc + dwb_ref[j] * rows
        cv[base:base + CONV_RG, :] = acc.reshape(CONV_RG, d) + bias_ref[...]
    u = cv[...]
    mu = jnp.mean(u, axis=-1, keepdims=True)
    uc = u - mu
    var = jnp.mean(uc * uc, axis=-1, keepdims=True)
    y = uc * lax.rsqrt(var + EPS) * lng_ref[...] + lnb_ref[...]
    uc_ref[0] = (y * _sigmoid(y)).astype(BF16)


def _inproj_conv(x, g_mix, mod3, w_in_bf, dw_rows, bias, ln_g, ln_b):
    b, s, d = x.shape
    n = w_in_bf.shape[1]
    ts = CONV_TS
    vec = pl.BlockSpec((1, d), lambda bi, i: (0, 0))
    return pl.pallas_call(
        _inproj_conv_kernel,
        grid=(b, s // ts),
        in_specs=[
            pl.BlockSpec((1, ts, d), lambda bi, i: (bi, i, 0)),
            vec,
            pl.BlockSpec((1, 1, d), lambda bi, i: (bi * N_MOD + 1, 0, 0)),
            pl.BlockSpec((1, 1, d), lambda bi, i: (bi * N_MOD + 0, 0, 0)),
            pl.BlockSpec((d, n), lambda bi, i: (0, 0), pipeline_mode=pl.Buffered(1)),
            pl.BlockSpec(dw_rows.shape, lambda bi, i: (0, 0, 0)),
            vec, vec, vec,
        ],
        out_specs=[
            pl.BlockSpec((1, ts, N_PCOLS * d), lambda bi, i: (bi, i, 0)),
            pl.BlockSpec((1, ts, d), lambda bi, i: (bi, i, 0)),
        ],
        out_shape=[
            jax.ShapeDtypeStruct((b, s, N_PCOLS * d), F32),
            jax.ShapeDtypeStruct((b, s, d), BF16),
        ],
        scratch_shapes=[pltpu.VMEM((CONV_SPAN, d), F32),
                        pltpu.VMEM((SUBLANES - 1, CONV_SPAN - SUBLANES, d), F32),
                        pltpu.VMEM((ts, d), F32)],
        compiler_params=_params(("arbitrary", "arbitrary")),
        name="inproj_conv",
    )(x, g_mix, mod3, mod3, w_in_bf, dw_rows, bias, ln_g, ln_b)


def _hgrn_levels():
    c = HG_CHUNK
    levels = []
    m = c // 2
    while m >= 1:
        levels.append(m)
        m //= 2
    return levels


def _level_exponent(g_inc, logf, m, row):
    c, d = g_inc.shape
    upper = (row & m) != 0
    if m == 1:
        return upper, jnp.where(upper, logf, 0.0)
    if m >= SUBLANES:
        parts = [jnp.broadcast_to(g_inc[b * 2 * m + m - 1:b * 2 * m + m, :], (2 * m, d))
                 for b in range(c // (2 * m))]
        gref = parts[0] if len(parts) == 1 else jnp.concatenate(parts, axis=0)
    else:
        g3 = g_inc.reshape(c // SUBLANES, SUBLANES, d)

        def bcast(j):
            return jnp.broadcast_to(g3[:, j:j + 1, :], g3.shape).reshape(c, d)

        if 2 * m == SUBLANES:
            gref = bcast(m - 1)
        else:
            assert 4 * m == SUBLANES
            gref = jnp.where((row & (SUBLANES - 1)) < 2 * m, bcast(m - 1), bcast(3 * m - 1))
    return upper, jnp.where(upper, g_inc - gref, gref - g_inc)


def _hgrn_kernel(q_ref, z_ref, v_ref, og_ref, lb_ref, ng_ref, tril_ref, o_ref, st):
    c = HG_CHUNK
    dk = HG_DK
    levels = _hgrn_levels()

    @pl.when(pl.program_id(1) == 0)
    def _():
        st[...] = jnp.zeros_like(st)

    z = z_ref[0]
    lb = lb_ref[...]
    sig = _sigmoid(z)
    f = lb + (1.0 - lb) * sig
    logf = jnp.log(f)
    kk = (1.0 - lb) * (1.0 - sig)
    q = q_ref[0] * (dk ** -0.5)
    v = v_ref[0]
    og = og_ref[0]

    hi = logf.astype(BF16)
    lo = (logf - hi.astype(F32)).astype(BF16)
    tril = tril_ref[...]
    g_inc = (jnp.dot(tril, hi, preferred_element_type=F32)
             + jnp.dot(tril, lo, preferred_element_type=F32))
    g_last = g_inc[c - 1:c, :]
    q_st = (q * jnp.exp(g_inc)).astype(BF16)
    k_st = (kk * jnp.exp(g_last - g_inc)).astype(BF16)
    dec_all = jnp.exp(g_last)
    v_bf = v.astype(BF16)
    q_bf = q.astype(BF16)
    k_bf = kk.astype(BF16)

    row = lax.broadcasted_iota(jnp.int32, (c, c), 0)
    col = lax.broadcasted_iota(jnp.int32, (c, c), 1)
    rr = lax.broadcasted_iota(jnp.int32, (c, 1), 0)

    qs, ks, masks = [], [], []
    for m in levels:
        upper, ex = _level_exponent(g_inc, logf, m, rr)
        e = jnp.exp(ex)
        qs.append(jnp.where(upper, q * e, 0.0).astype(BF16))
        ks.append(jnp.where(upper, 0.0, kk * e).astype(BF16))
        sh = int(np.log2(2 * m))
        masks.append((row >> sh) == (col >> sh))
    diag = row == col

    nt = (((1,), (1,)), ((), ()))
    tn = (((0,), (0,)), ((), ()))
    for h in range(HG_HEADS):
        sl = slice(h * dk, (h + 1) * dk)
        a = jnp.where(diag, lax.dot_general(q_bf[:, sl], k_bf[:, sl], nt,
                                            preferred_element_type=F32), 0.0)
        for li in range(len(levels)):
            a = a + jnp.where(masks[li],
                              lax.dot_general(qs[li][:, sl], ks[li][:, sl], nt,
                                              preferred_element_type=F32), 0.0)
        s_t = st[h]
        o = jnp.dot(a.astype(BF16), v_bf[:, sl], preferred_element_type=F32)
        o = o + lax.dot_general(q_st[:, sl], s_t.astype(BF16), nt, preferred_element_type=F32)
        st[h] = s_t * dec_all[:, sl] + lax.dot_general(v_bf[:, sl], k_st[:, sl], tn,
                                                       preferred_element_type=F32)
        ms = jnp.mean(o * o, axis=-1, keepdims=True)
        o = o * lax.rsqrt(ms + EPS) * ng_ref[...]
        g = og[:, sl]
        o_ref[0, :, sl] = (o * (g * _sigmoid(g))).astype(BF16)


def _hgrn_branch(p, lb, norm_g):
    b, s, _ = p.shape
    d = D_MODEL
    c = HG_CHUNK
    tril = jnp.asarray(np.tril(np.ones((c, c), np.float32)), dtype=BF16)

    def col_spec(col):
        return pl.BlockSpec((1, c, d), lambda bi, i: (bi, i, col))

    return pl.pallas_call(
        _hgrn_kernel,
        grid=(b, s // c),
        in_specs=[
            col_spec(P_Q), col_spec(P_F), col_spec(P_I), col_spec(P_G),
            pl.BlockSpec((1, d), lambda bi, i: (0, 0)),
            pl.BlockSpec((1, HG_DK), lambda bi, i: (0, 0)),
            pl.BlockSpec((c, c), lambda bi, i: (0, 0)),
        ],
        out_specs=pl.BlockSpec((1, c, d), lambda bi, i: (bi, i, 0)),
        out_shape=jax.ShapeDtypeStruct((b, s, d), BF16),
        scratch_shapes=[pltpu.VMEM((HG_HEADS, HG_DK, HG_DK), F32)],
        compiler_params=_params(("arbitrary", "arbitrary")),
        name="hgrn_branch",
    )(p, p, p, p, lb, norm_g, tril)


ROW_TILE = (8, 128)


def _rows_to_tiles(rows):
    st = jnp.stack([rows[:, j * 128:(j + 1) * 128] for j in range(ROW_TILE[0])], axis=0)
    return pltpu.einshape("jrl->rjl", st)


def _tiles_to_rows(tiles):
    y = pltpu.einshape("rjl->jrl", tiles)
    return jnp.concatenate([y[j] for j in range(ROW_TILE[0])], axis=-1)


_NT = (((1,), (1,)), ((), ()))
MERGE_SUB = 256


def _merge_kernel(uc_ref, uh_ref, gc_ref, gh_ref, x_ref, ga_ref, sc_ref, sh_ref, g_ref,
                  wc_ref, wh_ref, w_ref, wrt_ref, br_ref, x1_ref, h3_ref, lg_ref):
    wrt = wrt_ref[...]
    wrt_hi = wrt.astype(BF16)
    wrt_lo = (wrt - wrt_hi.astype(F32)).astype(BF16)
    for r in range(uc_ref.shape[1] // MERGE_SUB):
        rs = pl.ds(r * MERGE_SUB, MERGE_SUB)
        yc = jnp.dot(uc_ref[0, rs, :], wc_ref[...], preferred_element_type=F32)
        yh = jnp.dot(uh_ref[0, rs, :], wh_ref[...], preferred_element_type=F32)
        merged = (_sigmoid(gc_ref[0, rs, :]) * yc + _sigmoid(gh_ref[0, rs, :]) * yh).astype(BF16)
        x1 = x_ref[0, rs, :] + ga_ref[0] * jnp.dot(merged, w_ref[...],
                                                     preferred_element_type=F32)
        x1_ref[0, rs, :] = x1
        ms = jnp.mean(x1 * x1, axis=-1, keepdims=True)
        h2 = x1 * lax.rsqrt(ms + EPS) * g_ref[...]
        h2 = h2 * (1.0 + sc_ref[0]) + sh_ref[0]
        h3_ref[rs] = _rows_to_tiles(h2)
        h2_hi = h2.astype(BF16)
        h2_lo = (h2 - h2_hi.astype(F32)).astype(BF16)
        lg = (lax.dot_general(wrt_hi, h2_hi, _NT, preferred_element_type=F32)
              + lax.dot_general(wrt_hi, h2_lo, _NT, preferred_element_type=F32)
              + lax.dot_general(wrt_lo, h2_hi, _NT, preferred_element_type=F32))
        lg_ref[:, rs] = lg + br_ref[...]


def _merge(uc, uh, p, x, mod3, g_ffn, w_conv_bf, w_hgrn_bf, w_out_bf, w_router_t, b_router_col):
    b, s, d = x.shape
    tm = 512
    nt = s // tm
    ne = w_router_t.shape[0]

    def mod_spec(k):
        return pl.BlockSpec((1, 1, d), lambda bi, i: (bi * N_MOD + k, 0, 0))

    def col_spec(col):
        return pl.BlockSpec((1, tm, d), lambda bi, i: (bi, i, col))

    tile = pl.BlockSpec((1, tm, d), lambda bi, i: (bi, i, 0))
    weight = pl.BlockSpec((d, d), lambda bi, i: (0, 0))
    return pl.pallas_call(
        _merge_kernel,
        grid=(b, nt),
        in_specs=[
            tile, tile, col_spec(P_GC), col_spec(P_GH), tile,
            mod_spec(2), mod_spec(4), mod_spec(3),
            pl.BlockSpec((1, d), lambda bi, i: (0, 0)),
            weight, weight, weight,
            pl.BlockSpec((ne, d), lambda bi, i: (0, 0)),
            pl.BlockSpec((ne, 1), lambda bi, i: (0, 0)),
        ],
        out_specs=[
            tile,
            pl.BlockSpec((tm,) + ROW_TILE, lambda bi, i: (bi * nt + i, 0, 0)),
            pl.BlockSpec((ne, tm), lambda bi, i: (0, bi * nt + i)),
        ],
        out_shape=[
            jax.ShapeDtypeStruct((b, s, d), F32),
            jax.ShapeDtypeStruct((b * s,) + ROW_TILE, F32),
            jax.ShapeDtypeStruct((ne, b * s), F32),
        ],
        compiler_params=_params(("arbitrary", "arbitrary")),
        name="merge_router",
    )(uc, uh, p, p, x, mod3, mod3, mod3, g_ffn, w_conv_bf, w_hgrn_bf, w_out_bf, w_router_t,
      b_router_col)


MOE_TL = 1024
MOE_GRP = 16
MOE_CAP = MOE_TL * TOP_K + N_EXPERTS * MOE_GRP
MOE_GPB = MOE_BLOCK // MOE_GRP
RELAY_ROWS = 256
assert MOE_CAP % RELAY_ROWS == 0 and MOE_TL % RELAY_ROWS == 0


def _route_kernel(lg_ref, u_ref, ld_ref, w_ref, cnt_ref):
    tl = MOE_TL
    l = lg_ref[...]
    eio = lax.broadcasted_iota(jnp.int32, l.shape, 0)
    vals, hots = [], []
    for _ in range(TOP_K):
        m = jnp.max(l, axis=0, keepdims=True)
        idx = jnp.min(jnp.where(l == m, eio, N_EXPERTS), axis=0, keepdims=True)
        hot = eio == idx
        vals.append(m)
        hots.append(hot)
        l = jnp.where(hot, -jnp.inf, l)
    ex = [jnp.exp(v - vals[0]) for v in vals]
    den = ex[0] + ex[1] + ex[2] + ex[3]
    cnt = hots[0].astype(F32)
    for k in range(1, TOP_K):
        cnt = cnt + hots[k].astype(F32)
    prefix = jnp.dot(cnt.astype(BF16), u_ref[...], preferred_element_type=F32)
    n_e = jnp.sum(cnt, axis=1, keepdims=True)
    pad_e = jnp.floor((n_e + (MOE_GRP - 1)) * (1.0 / MOE_GRP)) * MOE_GRP
    scan = jnp.broadcast_to(pad_e, (N_EXPERTS, 128))
    ei = lax.broadcasted_iota(jnp.int32, scan.shape, 0)
    dist = 1
    while dist < N_EXPERTS:
        scan = scan + jnp.where(ei >= dist, pltpu.roll(scan, dist, axis=0), 0.0)
        dist *= 2
    base = prefix + (scan[:, 0:1] - pad_e)
    for k in range(TOP_K):
        dest = jnp.sum(jnp.where(hots[k], base, 0.0), axis=0, keepdims=True)
        ld_ref[0, :, k * tl:(k + 1) * tl] = dest.astype(jnp.int32)
        w_ref[0, :, k * tl:(k + 1) * tl] = ex[k] / den
    cnt_ref[0] = jnp.broadcast_to(n_e, (N_EXPERTS, 128)).astype(jnp.int32)


def _route(logits_t):
    ne, t = logits_t.shape
    tl = MOE_TL
    n_tiles = t // tl
    upper = jnp.asarray(np.triu(np.ones((tl, tl), np.float32), k=1), dtype=BF16)
    slot = pl.BlockSpec((1, 1, TOP_K * tl), lambda i: (i, 0, 0))
    return pl.pallas_call(
        _route_kernel,
        grid=(n_tiles,),
        in_specs=[pl.BlockSpec((ne, tl), lambda i: (0, i)),
                  pl.BlockSpec((tl, tl), lambda i: (0, 0))],
        out_specs=[slot, slot, pl.BlockSpec((1, ne, 128), lambda i: (i, 0, 0))],
        out_shape=[
            jax.ShapeDtypeStruct((n_tiles, 1, TOP_K * tl), jnp.int32),
            jax.ShapeDtypeStruct((n_tiles, 1, TOP_K * tl), F32),
            jax.ShapeDtypeStruct((n_tiles, ne, 128), jnp.int32),
        ],
        compiler_params=_params(("arbitrary",)),
        name="route",
    )(logits_t, upper)


def _gather_kernel(ld_ref, h3_ref, xs_ref, xs3):
    tl = MOE_TL
    last = pl.num_programs(0) - 1

    @pl.when((pl.program_id(0) == 0) | (pl.program_id(0) == last))
    def _():
        xs3[...] = jnp.zeros_like(xs3)

    def push(t, carry):
        tile = h3_ref[t]
        for k in range(TOP_K):
            xs3[ld_ref[0, 0, k * tl + t]] = tile
        return carry

    @pl.when(pl.program_id(0) < last)
    def _():
        lax.fori_loop(0, tl, push, 0, unroll=4)

    def relay(r, carry):
        r0 = pl.multiple_of(r * RELAY_ROWS, RELAY_ROWS)
        xs_ref[pl.ds(r0, RELAY_ROWS), :] = _tiles_to_rows(xs3[pl.ds(r0, RELAY_ROWS)]).astype(BF16)
        return carry

    lax.fori_loop(0, MOE_CAP // RELAY_ROWS, relay, 0)


def _gather(ld, h3):
    t = h3.shape[0]
    tl = MOE_TL
    n_tiles = t // tl
    last = n_tiles - 1
    return pl.pallas_call(
        _gather_kernel,
        grid=(n_tiles + 1,),
        in_specs=[
            pl.BlockSpec((1, 1, TOP_K * tl), lambda i: (jnp.minimum(i, last), 0, 0),
                         memory_space=pltpu.SMEM),
            pl.BlockSpec((tl,) + ROW_TILE, lambda i: (jnp.minimum(i, last), 0, 0)),
        ],
        out_specs=pl.BlockSpec((MOE_CAP, D_MODEL), lambda i: (i, 0)),
        out_shape=jax.ShapeDtypeStruct(((n_tiles + 1) * MOE_CAP, D_MODEL), BF16),
        scratch_shapes=[pltpu.VMEM((MOE_CAP,) + ROW_TILE, F32)],
        compiler_params=_params(("arbitrary",)),
        name="moe_gather",
    )(ld, h3)


def _table_kernel(cnt_ref, be_ref, src_ref, toff):
    n_tiles = cnt_ref.shape[0]
    n_slots = src_ref.shape[0]
    groups_per_tile = MOE_CAP // MOE_GRP
    step = 8

    for i in range(n_tiles):
        toff[i] = 0

    def per_expert(e, pos):
        def per_tile(i, p):
            g = (cnt_ref[i, e] + (MOE_GRP - 1)) // MOE_GRP
            base = i * groups_per_tile + toff[i]

            def put(j, c):
                for u in range(step):
                    src_ref[p + j * step + u] = base + j * step + u
                return c

            lax.fori_loop(0, (g + (step - 1)) // step, put, 0)
            toff[i] = toff[i] + g
            return p + g

        end = lax.fori_loop(0, n_tiles, per_tile, pos)
        new_pos = (end + (MOE_GPB - 1)) // MOE_GPB * MOE_GPB
        for u in range(MOE_GPB):
            src_ref[end + u] = -1

        def put_e(b, c):
            be_ref[b] = e
            return c

        lax.fori_loop(pos // MOE_GPB, new_pos // MOE_GPB, put_e, 0)
        return new_pos

    used = lax.fori_loop(0, N_EXPERTS, per_expert, 0)

    def tail(b, c):
        be_ref[b] = N_EXPERTS - 1
        for u in range(MOE_GPB):
            src_ref[b * MOE_GPB + u] = -1
        return c

    lax.fori_loop(used // MOE_GPB, n_slots // MOE_GPB, tail, 0)


def _block_table(cnt, n_blocks):
    smem = pl.BlockSpec(memory_space=pltpu.SMEM)
    return pl.pallas_call(
        _table_kernel,
        in_specs=[smem],
        out_specs=[smem, smem],
        out_shape=[jax.ShapeDtypeStruct((n_blocks,), jnp.int32),
                   jax.ShapeDtypeStruct((n_blocks * MOE_GPB,), jnp.int32)],
        scratch_shapes=[pltpu.SMEM((cnt.shape[0],), jnp.int32)],
        name="moe_block_table",
    )(cnt)


def _ffn_kernel(be_ref, src_ref, xs_hbm, w1_ref, b1_ref, w2_ref, b2_ref, ys_hbm,
                xbuf, ybuf, w1s, w2s, sem_in, sem_out, *, spare_grp):
    i = pl.program_id(0)
    nb = pl.num_programs(0)
    slot = i % 2

    def live(blk):
        return src_ref[blk * MOE_GPB] >= 0

    def group_rows(grp):
        return pl.ds(pl.multiple_of(grp * MOE_GRP, MOE_GRP), MOE_GRP)

    def start_in(blk, sl):
        for g in range(MOE_GPB):
            grp = src_ref[blk * MOE_GPB + g]
            grp = jnp.where(grp >= 0, grp, spare_grp)
            pltpu.make_async_copy(xs_hbm.at[group_rows(grp), :],
                                  xbuf.at[sl, pl.ds(g * MOE_GRP, MOE_GRP), :], sem_in.at[sl]).start()

    def start_out(blk, sl):
        for g in range(MOE_GPB):
            grp = src_ref[blk * MOE_GPB + g]
            grp = jnp.where(grp >= 0, grp, spare_grp + 1 + sl * MOE_GPB + g)
            pltpu.make_async_copy(ybuf.at[sl, pl.ds(g * MOE_GRP, MOE_GRP), :],
                                  ys_hbm.at[group_rows(grp), :], sem_out.at[sl]).start()

    def wait_in(sl):
        pltpu.make_async_copy(xs_hbm.at[pl.ds(0, MOE_BLOCK), :], xbuf.at[sl], sem_in.at[sl]).wait()

    def wait_out(sl):
        pltpu.make_async_copy(ybuf.at[sl], ys_hbm.at[pl.ds(0, MOE_BLOCK), :], sem_out.at[sl]).wait()

    @pl.when((i == 0) & live(0))
    def _():
        start_in(0, 0)

    nxt = jnp.minimum(i + 1, nb - 1)

    @pl.when((i + 1 < nb) & live(nxt))
    def _():
        start_in(nxt, 1 - slot)

    @pl.when((i >= 2) & live(jnp.maximum(i - 2, 0)))
    def _():
        wait_out(slot)

    prev = be_ref[jnp.maximum(i - 1, 0)]

    @pl.when((i == 0) | (be_ref[i] != prev))
    def _():
        w1s[...] = w1_ref[0].astype(BF16)
        w2s[...] = w2_ref[0].astype(BF16)

    @pl.when(live(i))
    def _():
        wait_in(slot)
        u = jnp.dot(xbuf[slot], w1s[...], preferred_element_type=F32) + b1_ref[0]
        gl = jnp.minimum(u[:, :D_FF], SWIGLU_LIMIT)
        lin = jnp.clip(u[:, D_FF:], -SWIGLU_LIMIT, SWIGLU_LIMIT)
        act = gl * _sigmoid(SWIGLU_ALPHA * gl) * (lin + 1.0)
        y = jnp.dot(act.astype(BF16), w2s[...], preferred_element_type=F32) + b2_ref[0]
        ybuf[slot] = y.astype(BF16)
        start_out(i, slot)

    @pl.when(i == nb - 1)
    def _():
        @pl.when(live(i))
        def _():
            wait_out(slot)

        @pl.when((i >= 1) & live(jnp.maximum(i - 1, 0)))
        def _():
            wait_out(1 - slot)


def _ffn(block_e, src, xs, w1, b1, w2, b2):
    r, d = xs.shape
    nb = block_e.shape[0]
    spare_grp = (r - MOE_CAP) // MOE_GRP
    assert 1 + 2 * MOE_GPB <= MOE_CAP // MOE_GRP
    grid_spec = pltpu.PrefetchScalarGridSpec(
        num_scalar_prefetch=2,
        grid=(nb,),
        in_specs=[
            pl.BlockSpec(memory_space=pl.ANY),
            pl.BlockSpec((1, d, 2 * D_FF), lambda i, be, sr: (be[i], 0, 0)),
            pl.BlockSpec((1, 1, 2 * D_FF), lambda i, be, sr: (be[i], 0, 0)),
            pl.BlockSpec((1, D_FF, d), lambda i, be, sr: (be[i], 0, 0)),
            pl.BlockSpec((1, 1, d), lambda i, be, sr: (be[i], 0, 0)),
        ],
        out_specs=pl.BlockSpec(memory_space=pl.ANY),
        scratch_shapes=[
            pltpu.VMEM((2, MOE_BLOCK, d), BF16), pltpu.VMEM((2, MOE_BLOCK, d), BF16),
            pltpu.VMEM((d, 2 * D_FF), BF16), pltpu.VMEM((D_FF, d), BF16),
            pltpu.SemaphoreType.DMA((2,)), pltpu.SemaphoreType.DMA((2,)),
        ],
    )
    return pl.pallas_call(
        functools.partial(_ffn_kernel, spare_grp=spare_grp),
        grid_spec=grid_spec,
        out_shape=jax.ShapeDtypeStruct((r, d), BF16),
        input_output_aliases={2: 0},
        compiler_params=_params(("arbitrary",)),
        name="expert_ffn",
    )(block_e, src, xs, w1, b1, w2, b2)


def _combine_kernel(ld_ref, w_ref, ys_ref, x1_ref, ga_ref, g_ref, o_ref, y3, o3, wb):
    tl = MOE_TL
    lanes = ROW_TILE[1]

    w4 = jnp.concatenate([w_ref[0, :, k * tl:(k + 1) * tl] for k in range(TOP_K)], axis=0)
    wt = jnp.concatenate([w4, jnp.zeros((lanes - TOP_K, tl), F32)], axis=0).T
    for k in range(TOP_K):
        wb[k] = jnp.broadcast_to(wt[:, k:k + 1], (tl, lanes))

    def relay(r, carry):
        r0 = pl.multiple_of(r * RELAY_ROWS, RELAY_ROWS)
        y3[pl.ds(r0, RELAY_ROWS)] = _rows_to_tiles(ys_ref[pl.ds(r0, RELAY_ROWS), :].astype(F32))
        return carry

    lax.fori_loop(0, MOE_CAP // RELAY_ROWS, relay, 0)

    def pull(t, carry):
        acc = None
        for k in range(TOP_K):
            wv = jnp.broadcast_to(wb[k, pl.ds(t, 1), :], ROW_TILE)
            term = wv * y3[ld_ref[0, 0, k * tl + t]]
            acc = term if acc is None else acc + term
        o3[t] = acc
        return carry

    lax.fori_loop(0, tl, pull, 0, unroll=8)

    def finish(r, carry):
        r0 = pl.multiple_of(r * RELAY_ROWS, RELAY_ROWS)
        x2 = x1_ref[pl.ds(r0, RELAY_ROWS), :] + ga_ref[0] * _tiles_to_rows(o3[pl.ds(r0, RELAY_ROWS)])
        ms = jnp.mean(x2 * x2, axis=-1, keepdims=True)
        o_ref[pl.ds(r0, RELAY_ROWS), :] = x2 * lax.rsqrt(ms + EPS) * g_ref[...]
        return carry

    lax.fori_loop(0, tl // RELAY_ROWS, finish, 0)


def _combine(ld, wts, ys, x1f, mod3, g_final, tiles_per_batch):
    t, d = x1f.shape
    tl = MOE_TL
    n_tiles = t // tl
    smem = functools.partial(pl.BlockSpec, (1, 1, TOP_K * tl), lambda i: (i, 0, 0),
                             memory_space=pltpu.SMEM)
    return pl.pallas_call(
        _combine_kernel,
        grid=(n_tiles,),
        in_specs=[
            smem(),
            pl.BlockSpec((1, 1, TOP_K * tl), lambda i: (i, 0, 0)),
            pl.BlockSpec((MOE_CAP, d), lambda i: (i, 0)),
            pl.BlockSpec((tl, d), lambda i: (i, 0)),
            pl.BlockSpec((1, 1, d), lambda i: ((i // tiles_per_batch) * N_MOD + 5, 0, 0)),
            pl.BlockSpec((1, d), lambda i: (0, 0)),
        ],
        out_specs=pl.BlockSpec((tl, d), lambda i: (i, 0)),
        out_shape=jax.ShapeDtypeStruct((t, d), F32),
        scratch_shapes=[pltpu.VMEM((MOE_CAP,) + ROW_TILE, F32), pltpu.VMEM((tl,) + ROW_TILE, F32),
                        pltpu.VMEM((TOP_K, tl, ROW_TILE[1]), F32)],
        compiler_params=_params(("arbitrary",), COMBINE_VMEM_LIMIT),
        name="moe_combine",
    )(ld, wts, ys, x1f, mod3, g_final)


def kernel(x, c, w_ada, b_ada, g_mix, w_in, conv_dw, conv_dw_bias, conv_ln_g, conv_ln_b,
           w_conv_out, lb_param, hgrn_norm_g, w_hgrn_out, w_out, g_ffn, w_router, b_router,
           w1, b1, w2, b2, g_final):
    b, s, d = x.shape
    assert w_ada.shape[0] == 1, "single-layer block"
    assert s % MOE_TL == 0
    t = b * s
    n_tiles = t // MOE_TL

    c_pad = jnp.zeros((8, d), F32).at[:b].set(c.astype(F32))
    mod, lb = _ada(c_pad, w_ada[0], b_ada, lb_param)
    mod3 = mod[:b].reshape(b * N_MOD, 1, d)

    dw_rows = jnp.broadcast_to(conv_dw[0][:, None, :], (CONV_K, SUBLANES, d))
    p, uc = _inproj_conv(x, g_mix, mod3, w_in[0].astype(BF16), dw_rows, conv_dw_bias, conv_ln_g,
                         conv_ln_b)
    uh = _hgrn_branch(p, lb, hgrn_norm_g)

    x1, h3, logits_t = _merge(uc, uh, p, x, mod3, g_ffn, w_conv_out[0].astype(BF16),
                              w_hgrn_out[0].astype(BF16), w_out[0].astype(BF16),
                              w_router[0].T, b_router[0][:, None])

    ld, wts, cnt = _route(logits_t)
    xs = _gather(ld, h3)
    groups_max = t * TOP_K // MOE_GRP + n_tiles * N_EXPERTS
    n_blocks = groups_max // MOE_GPB + N_EXPERTS
    block_e, src = _block_table(cnt[:, :, 0], n_blocks)
    ys = _ffn(block_e, src, xs, w1[0], b1[0][:, None, :], w2[0], b2[0][:, None, :])
    out = _combine(ld, wts, ys, x1.reshape(t, d), mod3, g_final.reshape(1, d), s // MOE_TL)
    return out.reshape(b, s, d)
```

```python
import functools

import jax
import jax.numpy as jnp
import numpy as np
from jax import lax
from jax.experimental import pallas as pl
from jax.experimental.pallas import tpu as pltpu

F32 = jnp.float32
BF16 = jnp.bfloat16

D_MODEL = 1024
CONV_K = 31
HG_HEADS = 8
HG_DK = 128
N_EXPERTS = 32
TOP_K = 4
D_FF = 1024
SWIGLU_ALPHA = 1.702
SWIGLU_LIMIT = 7.0
MOE_BLOCK = 512
EPS = 1e-6
N_MOD = 6
COL_CA, COL_CB, COL_Q, COL_F, COL_I, COL_G, COL_GC, COL_GH = range(8)
N_COLS = 8

HG_CHUNK = 128
CONV_HALO = 32
VMEM_LIMIT = 56 * 1024 * 1024
COMBINE_VMEM_LIMIT = 62 * 1024 * 1024


def _sigmoid(x):
    return 1.0 / (1.0 + jnp.exp(-x))


def _params(sem, vmem=VMEM_LIMIT):
    return pltpu.CompilerParams(dimension_semantics=sem, vmem_limit_bytes=vmem)


def _ada_kernel(c_ref, w_ref, b_ref, lbp_ref, mod_ref, lb_ref):
    c = c_ref[...]
    c_act = c * _sigmoid(c)
    mod_ref[...] = jnp.dot(c_act, w_ref[...], preferred_element_type=F32,
                           precision=lax.Precision.HIGHEST) + b_ref[...]
    p = lbp_ref[...]
    e = jnp.exp(p - jnp.max(p, axis=0, keepdims=True))
    lb_ref[...] = e[0:1, :] / jnp.sum(e, axis=0, keepdims=True)


def _ada(c_pad, w_ada, b_ada, lb_param):
    nb, d = c_pad.shape
    n = w_ada.shape[1]
    tn = 1536
    return pl.pallas_call(
        _ada_kernel,
        grid=(n // tn,),
        in_specs=[
            pl.BlockSpec((nb, d), lambda j: (0, 0)),
            pl.BlockSpec((d, tn), lambda j: (0, j)),
            pl.BlockSpec((1, tn), lambda j: (0, j)),
            pl.BlockSpec(lb_param.shape, lambda j: (0, 0)),
        ],
        out_specs=[
            pl.BlockSpec((nb, tn), lambda j: (0, j)),
            pl.BlockSpec((1, d), lambda j: (0, 0)),
        ],
        out_shape=[
            jax.ShapeDtypeStruct((nb, n), F32),
            jax.ShapeDtypeStruct((1, d), F32),
        ],
        compiler_params=_params(("arbitrary",)),
        name="ada_mod",
    )(c_pad, w_ada, b_ada, lb_param)


CONV_TS = 256
CONV_RG = 32
SUBLANES = 8
CONV_SPAN = CONV_TS + CONV_HALO
CONV_OFF = CONV_HALO - (CONV_K - 1)
P_Q, P_F, P_I, P_G = range(4)
N_HCOLS = 4
PG_C, PG_H = range(2)


def _inproj_conv_kernel(x_ref, g_ref, sc_ref, sh_ref, w_hbm, dwb_ref, bias_ref, lng_ref, lnb_ref,
                        p_ref, pg_ref, uc_ref, buf, sh, cv, w_ref, wsem):
    i = pl.program_id(1)
    ts = CONV_TS
    d = D_MODEL

    @pl.when((i == 0) & (pl.program_id(0) == 0))
    def _():
        cp = pltpu.make_async_copy(w_hbm, w_ref, wsem)
        cp.start()
        cp.wait()

    @pl.when(i == 0)
    def _():
        buf[0:CONV_HALO, :] = jnp.zeros((CONV_HALO, d), F32)

    @pl.when(i > 0)
    def _():
        buf[0:CONV_HALO, :] = buf[ts:CONV_SPAN, :]

    x = x_ref[0]
    ms = jnp.mean(x * x, axis=-1, keepdims=True)
    h = x * lax.rsqrt(ms + EPS) * g_ref[...]
    h = (h * (1.0 + sc_ref[0]) + sh_ref[0]).astype(BF16)

    ab = jnp.dot(h, w_ref[:, 0:2 * d], preferred_element_type=F32)
    buf[CONV_HALO:CONV_SPAN, :] = ab[:, :d] * _sigmoid(ab[:, d:])
    hcols = (2 + N_HCOLS) * d
    p_ref[0] = jnp.dot(h, w_ref[:, 2 * d:hcols], preferred_element_type=F32)
    pg_ref[0] = jnp.dot(h, w_ref[:, hcols:], preferred_element_type=F32).astype(BF16)

    for s in range(1, SUBLANES):
        sh[s - 1] = buf[s:s + CONV_SPAN - SUBLANES, :]
    for r in range(ts // CONV_RG):
        base = r * CONV_RG
        acc = jnp.zeros((CONV_RG // SUBLANES, SUBLANES, d), F32)
        for j in range(CONV_K):
            s = (CONV_OFF + j) % SUBLANES
            row = base + CONV_OFF + j - s
            src = buf if s == 0 else sh.at[s - 1]
            rows = src[row:row + CONV_RG, :].reshape(CONV_RG // SUBLANES, SUBLANES, d)
            acc = acc + dwb_ref[j] * rows
        cv[base:base + CONV_RG, :] = acc.reshape(CONV_RG, d) + bias_ref[...]
    u = cv[...]
    mu = jnp.mean(u, axis=-1, keepdims=True)
    uc = u - mu
    var = jnp.mean(uc * uc, axis=-1, keepdims=True)
    y = uc * lax.rsqrt(var + EPS) * lng_ref[...] + lnb_ref[...]
    uc_ref[0] = (y * _sigmoid(y)).astype(BF16)


def _inproj_conv(x, g_mix, mod3, w_in_bf, dw_rows, bias, ln_g, ln_b):
    b, s, d = x.shape
    n = w_in_bf.shape[1]
    ts = CONV_TS
    vec = pl.BlockSpec((1, d), lambda bi, i: (0, 0))
    return pl.pallas_call(
        _inproj_conv_kernel,
        grid=(b, s // ts),
        in_specs=[
            pl.BlockSpec((1, ts, d), lambda bi, i: (bi, i, 0)),
            vec,
            pl.BlockSpec((1, 1, d), lambda bi, i: (bi * N_MOD + 1, 0, 0)),
            pl.BlockSpec((1, 1, d), lambda bi, i: (bi * N_MOD + 0, 0, 0)),
            pl.BlockSpec(memory_space=pl.ANY),
            pl.BlockSpec(dw_rows.shape, lambda bi, i: (0, 0, 0)),
            vec, vec, vec,
        ],
        out_specs=[
            pl.BlockSpec((1, ts, N_HCOLS * d), lambda bi, i: (bi, i, 0)),
            pl.BlockSpec((1, ts, 2 * d), lambda bi, i: (bi, i, 0)),
            pl.BlockSpec((1, ts, d), lambda bi, i: (bi, i, 0)),
        ],
        out_shape=[
            jax.ShapeDtypeStruct((b, s, N_HCOLS * d), F32),
            jax.ShapeDtypeStruct((b, s, 2 * d), BF16),
            jax.ShapeDtypeStruct((b, s, d), BF16),
        ],
        scratch_shapes=[pltpu.VMEM((CONV_SPAN, d), F32),
                        pltpu.VMEM((SUBLANES - 1, CONV_SPAN - SUBLANES, d), F32),
                        pltpu.VMEM((ts, d), F32),
                        pltpu.VMEM((d, n), BF16), pltpu.SemaphoreType.DMA(())],
        compiler_params=_params(("arbitrary", "arbitrary")),
        name="inproj_conv",
    )(x, g_mix, mod3, mod3, w_in_bf, dw_rows, bias, ln_g, ln_b)


def _hgrn_levels():
    c = HG_CHUNK
    levels = []
    m = c // 2
    while m >= 1:
        levels.append(m)
        m //= 2
    return levels


def _level_exponent(g_inc, logf, m, row):
    c, d = g_inc.shape
    upper = (row & m) != 0
    if m == 1:
        return upper, jnp.where(upper, logf, 0.0)
    if m >= SUBLANES:
        parts = [jnp.broadcast_to(g_inc[b * 2 * m + m - 1:b * 2 * m + m, :], (2 * m, d))
                 for b in range(c // (2 * m))]
        gref = parts[0] if len(parts) == 1 else jnp.concatenate(parts, axis=0)
    else:
        g3 = g_inc.reshape(c // SUBLANES, SUBLANES, d)

        def bcast(j):
            return jnp.broadcast_to(g3[:, j:j + 1, :], g3.shape).reshape(c, d)

        if 2 * m == SUBLANES:
            gref = bcast(m - 1)
        else:
            assert 4 * m == SUBLANES
            gref = jnp.where((row & (SUBLANES - 1)) < 2 * m, bcast(m - 1), bcast(3 * m - 1))
    return upper, jnp.where(upper, g_inc - gref, gref - g_inc)


def _hgrn_kernel(q_ref, z_ref, v_ref, og_ref, lb_ref, ng_ref, tril_ref, o_ref, st):
    c = HG_CHUNK
    dk = HG_DK
    levels = _hgrn_levels()

    @pl.when(pl.program_id(1) == 0)
    def _():
        st[...] = jnp.zeros_like(st)

    z = z_ref[0]
    lb = lb_ref[...]
    sig = _sigmoid(z)
    f = lb + (1.0 - lb) * sig
    logf = jnp.log(f)
    kk = (1.0 - lb) * (1.0 - sig)
    q = q_ref[0] * (dk ** -0.5)
    v = v_ref[0]
    og = og_ref[0]

    hi = logf.astype(BF16)
    lo = (logf - hi.astype(F32)).astype(BF16)
    tril = tril_ref[...]
    g_inc = (jnp.dot(tril, hi, preferred_element_type=F32)
             + jnp.dot(tril, lo, preferred_element_type=F32))
    g_last = g_inc[c - 1:c, :]
    q_st = (q * jnp.exp(g_inc)).astype(BF16)
    k_st = (kk * jnp.exp(g_last - g_inc)).astype(BF16)
    dec_all = jnp.exp(g_last)
    v_bf = v.astype(BF16)
    q_bf = q.astype(BF16)
    k_bf = kk.astype(BF16)

    row = lax.broadcasted_iota(jnp.int32, (c, c), 0)
    col = lax.broadcasted_iota(jnp.int32, (c, c), 1)
    rr = lax.broadcasted_iota(jnp.int32, (c, 1), 0)

    qs, ks, masks = [], [], []
    for m in levels:
        upper, ex = _level_exponent(g_inc, logf, m, rr)
        e = jnp.exp(ex)
        qs.append(jnp.where(upper, q * e, 0.0).astype(BF16))
        ks.append(jnp.where(upper, 0.0, kk * e).astype(BF16))
        sh = int(np.log2(2 * m))
        masks.append((row >> sh) == (col >> sh))
    diag = row == col

    nt = (((1,), (1,)), ((), ()))
    tn = (((0,), (0,)), ((), ()))
    for h in range(HG_HEADS):
        sl = slice(h * dk, (h + 1) * dk)
        a = jnp.where(diag, lax.dot_general(q_bf[:, sl], k_bf[:, sl], nt,
                                            preferred_element_type=F32), 0.0)
        for li in range(len(levels)):
            a = a + jnp.where(masks[li],
                              lax.dot_general(qs[li][:, sl], ks[li][:, sl], nt,
                                              preferred_element_type=F32), 0.0)
        s_t = st[h]
        o = jnp.dot(a.astype(BF16), v_bf[:, sl], preferred_element_type=F32)
        o = o + lax.dot_general(q_st[:, sl], s_t.astype(BF16), nt, preferred_element_type=F32)
        st[h] = s_t * dec_all[:, sl] + lax.dot_general(v_bf[:, sl], k_st[:, sl], tn,
                                                       preferred_element_type=F32)
        ms = jnp.mean(o * o, axis=-1, keepdims=True)
        o = o * lax.rsqrt(ms + EPS) * ng_ref[...]
        g = og[:, sl]
        o_ref[0, :, sl] = (o * (g * _sigmoid(g))).astype(BF16)


def _hgrn_branch(p, lb, norm_g):
    b, s, _ = p.shape
    d = D_MODEL
    c = HG_CHUNK
    tril = jnp.asarray(np.tril(np.ones((c, c), np.float32)), dtype=BF16)

    def col_spec(col):
        return pl.BlockSpec((1, c, d), lambda bi, i: (bi, i, col))

    return pl.pallas_call(
        _hgrn_kernel,
        grid=(b, s // c),
        in_specs=[
            col_spec(P_Q), col_spec(P_F), col_spec(P_I), col_spec(P_G),
            pl.BlockSpec((1, d), lambda bi, i: (0, 0)),
            pl.BlockSpec((1, HG_DK), lambda bi, i: (0, 0)),
            pl.BlockSpec((c, c), lambda bi, i: (0, 0)),
        ],
        out_specs=pl.BlockSpec((1, c, d), lambda bi, i: (bi, i, 0)),
        out_shape=jax.ShapeDtypeStruct((b, s, d), BF16),
        scratch_shapes=[pltpu.VMEM((HG_HEADS, HG_DK, HG_DK), F32)],
        compiler_params=_params(("arbitrary", "arbitrary")),
        name="hgrn_branch",
    )(p, p, p, p, lb, norm_g, tril)


ROW_TILE = (8, 128)


def _rows_to_tiles(rows):
    st = jnp.stack([rows[:, j * 128:(j + 1) * 128] for j in range(ROW_TILE[0])], axis=0)
    return pltpu.einshape("jrl->rjl", st)


def _tiles_to_rows(tiles):
    y = pltpu.einshape("rjl->jrl", tiles)
    return jnp.concatenate([y[j] for j in range(ROW_TILE[0])], axis=-1)


_NT = (((1,), (1,)), ((), ()))
MERGE_SUB = 256


def _merge_kernel(uc_ref, uh_ref, gc_ref, gh_ref, x_ref, ga_ref, sc_ref, sh_ref, g_ref,
                  wc_ref, wh_ref, w_ref, wrt_ref, br_ref, x1_ref, h3_ref, lg_ref):
    wrt = wrt_ref[...]
    wrt_hi = wrt.astype(BF16)
    wrt_lo = (wrt - wrt_hi.astype(F32)).astype(BF16)
    for r in range(uc_ref.shape[1] // MERGE_SUB):
        rs = pl.ds(r * MERGE_SUB, MERGE_SUB)
        yc = jnp.dot(uc_ref[0, rs, :], wc_ref[...], preferred_element_type=F32)
        yh = jnp.dot(uh_ref[0, rs, :], wh_ref[...], preferred_element_type=F32)
        gate_c = _sigmoid(gc_ref[0, rs, :].astype(F32))
        gate_h = _sigmoid(gh_ref[0, rs, :].astype(F32))
        merged = (gate_c * yc + gate_h * yh).astype(BF16)
        x1 = x_ref[0, rs, :] + ga_ref[0] * jnp.dot(merged, w_ref[...],
                                                     preferred_element_type=F32)
        x1_ref[0, rs, :] = x1
        ms = jnp.mean(x1 * x1, axis=-1, keepdims=True)
        h2 = x1 * lax.rsqrt(ms + EPS) * g_ref[...]
        h2 = h2 * (1.0 + sc_ref[0]) + sh_ref[0]
        h3_ref[rs] = _rows_to_tiles(h2)
        h2_hi = h2.astype(BF16)
        h2_lo = (h2 - h2_hi.astype(F32)).astype(BF16)
        lg = (lax.dot_general(wrt_hi, h2_hi, _NT, preferred_element_type=F32)
              + lax.dot_general(wrt_hi, h2_lo, _NT, preferred_element_type=F32)
              + lax.dot_general(wrt_lo, h2_hi, _NT, preferred_element_type=F32))
        lg_ref[:, rs] = lg + br_ref[...]


def _merge(uc, uh, pg, x, mod3, g_ffn, w_conv_bf, w_hgrn_bf, w_out_bf, w_router_t, b_router_col):
    b, s, d = x.shape
    tm = 512
    nt = s // tm
    ne = w_router_t.shape[0]

    def mod_spec(k):
        return pl.BlockSpec((1, 1, d), lambda bi, i: (bi * N_MOD + k, 0, 0))

    def col_spec(col):
        return pl.BlockSpec((1, tm, d), lambda bi, i: (bi, i, col))

    tile = pl.BlockSpec((1, tm, d), lambda bi, i: (bi, i, 0))
    weight = pl.BlockSpec((d, d), lambda bi, i: (0, 0))
    return pl.pallas_call(
        _merge_kernel,
        grid=(b, nt),
        in_specs=[
            tile, tile, col_spec(PG_C), col_spec(PG_H), tile,
            mod_spec(2), mod_spec(4), mod_spec(3),
            pl.BlockSpec((1, d), lambda bi, i: (0, 0)),
            weight, weight, weight,
            pl.BlockSpec((ne, d), lambda bi, i: (0, 0)),
            pl.BlockSpec((ne, 1), lambda bi, i: (0, 0)),
        ],
        out_specs=[
            tile,
            pl.BlockSpec((tm,) + ROW_TILE, lambda bi, i: (bi * nt + i, 0, 0)),
            pl.BlockSpec((ne, tm), lambda bi, i: (0, bi * nt + i)),
        ],
        out_shape=[
            jax.ShapeDtypeStruct((b, s, d), F32),
            jax.ShapeDtypeStruct((b * s,) + ROW_TILE, F32),
            jax.ShapeDtypeStruct((ne, b * s), F32),
        ],
        compiler_params=_params(("arbitrary", "arbitrary")),
        name="merge_router",
    )(uc, uh, pg, pg, x, mod3, mod3, mod3, g_ffn, w_conv_bf, w_hgrn_bf, w_out_bf, w_router_t,
      b_router_col)


MOE_TL = 1024
MOE_GRP = 16
MOE_CAP = MOE_TL * TOP_K + N_EXPERTS * MOE_GRP
MOE_GPB = MOE_BLOCK // MOE_GRP
RELAY_ROWS = 256
assert MOE_CAP % RELAY_ROWS == 0 and MOE_TL % RELAY_ROWS == 0


def _route_kernel(lg_ref, u_ref, ld_ref, w_ref, cnt_ref):
    tl = MOE_TL
    l = lg_ref[...]
    eio = lax.broadcasted_iota(jnp.int32, l.shape, 0)
    vals, hots = [], []
    for _ in range(TOP_K):
        m = jnp.max(l, axis=0, keepdims=True)
        idx = jnp.min(jnp.where(l == m, eio, N_EXPERTS), axis=0, keepdims=True)
        hot = eio == idx
        vals.append(m)
        hots.append(hot)
        l = jnp.where(hot, -jnp.inf, l)
    ex = [jnp.exp(v - vals[0]) for v in vals]
    den = ex[0] + ex[1] + ex[2] + ex[3]
    cnt = hots[0].astype(F32)
    for k in range(1, TOP_K):
        cnt = cnt + hots[k].astype(F32)
    prefix = jnp.dot(cnt.astype(BF16), u_ref[...], preferred_element_type=F32)
    n_e = jnp.sum(cnt, axis=1, keepdims=True)
    pad_e = jnp.floor((n_e + (MOE_GRP - 1)) * (1.0 / MOE_GRP)) * MOE_GRP
    scan = jnp.broadcast_to(pad_e, (N_EXPERTS, 128))
    ei = lax.broadcasted_iota(jnp.int32, scan.shape, 0)
    dist = 1
    while dist < N_EXPERTS:
        scan = scan + jnp.where(ei >= dist, pltpu.roll(scan, dist, axis=0), 0.0)
        dist *= 2
    base = prefix + (scan[:, 0:1] - pad_e)
    for k in range(TOP_K):
        dest = jnp.sum(jnp.where(hots[k], base, 0.0), axis=0, keepdims=True)
        ld_ref[0, :, k * tl:(k + 1) * tl] = dest.astype(jnp.int32)
        w_ref[0, :, k * tl:(k + 1) * tl] = ex[k] / den
    cnt_ref[0] = jnp.broadcast_to(n_e, (N_EXPERTS, 128)).astype(jnp.int32)


def _route(logits_t):
    ne, t = logits_t.shape
    tl = MOE_TL
    n_tiles = t // tl
    upper = jnp.asarray(np.triu(np.ones((tl, tl), np.float32), k=1), dtype=BF16)
    slot = pl.BlockSpec((1, 1, TOP_K * tl), lambda i: (i, 0, 0))
    return pl.pallas_call(
        _route_kernel,
        grid=(n_tiles,),
        in_specs=[pl.BlockSpec((ne, tl), lambda i: (0, i)),
                  pl.BlockSpec((tl, tl), lambda i: (0, 0))],
        out_specs=[slot, slot, pl.BlockSpec((1, ne, 128), lambda i: (i, 0, 0))],
        out_shape=[
            jax.ShapeDtypeStruct((n_tiles, 1, TOP_K * tl), jnp.int32),
            jax.ShapeDtypeStruct((n_tiles, 1, TOP_K * tl), F32),
            jax.ShapeDtypeStruct((n_tiles, ne, 128), jnp.int32),
        ],
        compiler_params=_params(("arbitrary",)),
        name="route",
    )(logits_t, upper)


def _gather_kernel(ld_ref, h3_ref, xs_ref, xs3):
    tl = MOE_TL
    last = pl.num_programs(0) - 1

    @pl.when((pl.program_id(0) == 0) | (pl.program_id(0) == last))
    def _():
        xs3[...] = jnp.zeros_like(xs3)

    def push(t, carry):
        tile = h3_ref[t]
        for k in range(TOP_K):
            xs3[ld_ref[0, 0, k * tl + t]] = tile
        return carry

    @pl.when(pl.program_id(0) < last)
    def _():
        lax.fori_loop(0, tl, push, 0, unroll=4)

    def relay(r, carry):
        r0 = pl.multiple_of(r * RELAY_ROWS, RELAY_ROWS)
        xs_ref[pl.ds(r0, RELAY_ROWS), :] = _tiles_to_rows(xs3[pl.ds(r0, RELAY_ROWS)]).astype(BF16)
        return carry

    lax.fori_loop(0, MOE_CAP // RELAY_ROWS, relay, 0)


def _gather(ld, h3):
    t = h3.shape[0]
    tl = MOE_TL
    n_tiles = t // tl
    last = n_tiles - 1
    return pl.pallas_call(
        _gather_kernel,
        grid=(n_tiles + 1,),
        in_specs=[
            pl.BlockSpec((1, 1, TOP_K * tl), lambda i: (jnp.minimum(i, last), 0, 0),
                         memory_space=pltpu.SMEM),
            pl.BlockSpec((tl,) + ROW_TILE, lambda i: (jnp.minimum(i, last), 0, 0)),
        ],
        out_specs=pl.BlockSpec((MOE_CAP, D_MODEL), lambda i: (i, 0)),
        out_shape=jax.ShapeDtypeStruct(((n_tiles + 1) * MOE_CAP, D_MODEL), BF16),
        scratch_shapes=[pltpu.VMEM((MOE_CAP,) + ROW_TILE, F32)],
        compiler_params=_params(("arbitrary",)),
        name="moe_gather",
    )(ld, h3)


def _table_kernel(cnt_ref, be_ref, src_ref, toff):
    n_tiles = cnt_ref.shape[0]
    n_slots = src_ref.shape[0]
    groups_per_tile = MOE_CAP // MOE_GRP
    step = 8

    for i in range(n_tiles):
        toff[i] = 0

    def per_expert(e, pos):
        def per_tile(i, p):
            g = (cnt_ref[i, e] + (MOE_GRP - 1)) // MOE_GRP
            base = i * groups_per_tile + toff[i]

            def put(j, c):
                for u in range(step):
                    src_ref[p + j * step + u] = base + j * step + u
                return c

            lax.fori_loop(0, (g + (step - 1)) // step, put, 0)
            toff[i] = toff[i] + g
            return p + g

        end = lax.fori_loop(0, n_tiles, per_tile, pos)
        new_pos = (end + (MOE_GPB - 1)) // MOE_GPB * MOE_GPB
        for u in range(MOE_GPB):
            src_ref[end + u] = -1

        def put_e(b, c):
            be_ref[b] = e
            return c

        lax.fori_loop(pos // MOE_GPB, new_pos // MOE_GPB, put_e, 0)
        return new_pos

    used = lax.fori_loop(0, N_EXPERTS, per_expert, 0)

    def tail(b, c):
        be_ref[b] = N_EXPERTS - 1
        for u in range(MOE_GPB):
            src_ref[b * MOE_GPB + u] = -1
        return c

    lax.fori_loop(used // MOE_GPB, n_slots // MOE_GPB, tail, 0)


def _block_table(cnt, n_blocks):
    smem = pl.BlockSpec(memory_space=pltpu.SMEM)
    return pl.pallas_call(
        _table_kernel,
        in_specs=[smem],
        out_specs=[smem, smem],
        out_shape=[jax.ShapeDtypeStruct((n_blocks,), jnp.int32),
                   jax.ShapeDtypeStruct((n_blocks * MOE_GPB,), jnp.int32)],
        scratch_shapes=[pltpu.SMEM((cnt.shape[0],), jnp.int32)],
        name="moe_block_table",
    )(cnt)


def _ffn_kernel(be_ref, src_ref, xs_hbm, w1_ref, b1_ref, w2_ref, b2_ref, ys_hbm,
                xbuf, ybuf, w1s, w2s, sem_in, sem_out, *, spare_grp):
    i = pl.program_id(0)
    nb = pl.num_programs(0)
    slot = i % 2

    def live(blk):
        return src_ref[blk * MOE_GPB] >= 0

    def group_rows(grp):
        return pl.ds(pl.multiple_of(grp * MOE_GRP, MOE_GRP), MOE_GRP)

    def start_in(blk, sl):
        for g in range(MOE_GPB):
            grp = src_ref[blk * MOE_GPB + g]
            grp = jnp.where(grp >= 0, grp, spare_grp)
            pltpu.make_async_copy(xs_hbm.at[group_rows(grp), :],
                                  xbuf.at[sl, pl.ds(g * MOE_GRP, MOE_GRP), :], sem_in.at[sl]).start()

    def start_out(blk, sl):
        for g in range(MOE_GPB):
            grp = src_ref[blk * MOE_GPB + g]
            grp = jnp.where(grp >= 0, grp, spare_grp + 1 + sl * MOE_GPB + g)
            pltpu.make_async_copy(ybuf.at[sl, pl.ds(g * MOE_GRP, MOE_GRP), :],
                                  ys_hbm.at[group_rows(grp), :], sem_out.at[sl]).start()

    def wait_in(sl):
        pltpu.make_async_copy(xs_hbm.at[pl.ds(0, MOE_BLOCK), :], xbuf.at[sl], sem_in.at[sl]).wait()

    def wait_out(sl):
        pltpu.make_async_copy(ybuf.at[sl], ys_hbm.at[pl.ds(0, MOE_BLOCK), :], sem_out.at[sl]).wait()

    @pl.when((i == 0) & live(0))
    def _():
        start_in(0, 0)

    nxt = jnp.minimum(i + 1, nb - 1)

    @pl.when((i + 1 < nb) & live(nxt))
    def _():
        start_in(nxt, 1 - slot)

    @pl.when((i >= 2) & live(jnp.maximum(i - 2, 0)))
    def _():
        wait_out(slot)

    prev = be_ref[jnp.maximum(i - 1, 0)]

    @pl.when((i == 0) | (be_ref[i] != prev))
    def _():
        w1s[...] = w1_ref[0].astype(BF16)
        w2s[...] = w2_ref[0].astype(BF16)

    @pl.when(live(i))
    def _():
        wait_in(slot)
        u = jnp.dot(xbuf[slot], w1s[...], preferred_element_type=F32) + b1_ref[0]
        gl = jnp.minimum(u[:, :D_FF], SWIGLU_LIMIT)
        lin = jnp.clip(u[:, D_FF:], -SWIGLU_LIMIT, SWIGLU_LIMIT)
        act = gl * _sigmoid(SWIGLU_ALPHA * gl) * (lin + 1.0)
        y = jnp.dot(act.astype(BF16), w2s[...], preferred_element_type=F32) + b2_ref[0]
        ybuf[slot] = y.astype(BF16)
        start_out(i, slot)

    @pl.when(i == nb - 1)
    def _():
        @pl.when(live(i))
        def _():
            wait_out(slot)

        @pl.when((i >= 1) & live(jnp.maximum(i - 1, 0)))
        def _():
            wait_out(1 - slot)


def _ffn(block_e, src, xs, w1, b1, w2, b2):
    r, d = xs.shape
    nb = block_e.shape[0]
    spare_grp = (r - MOE_CAP) // MOE_GRP
    assert 1 + 2 * MOE_GPB <= MOE_CAP // MOE_GRP
    grid_spec = pltpu.PrefetchScalarGridSpec(
        num_scalar_prefetch=2,
        grid=(nb,),
        in_specs=[
            pl.BlockSpec(memory_space=pl.ANY),
            pl.BlockSpec((1, d, 2 * D_FF), lambda i, be, sr: (be[i], 0, 0)),
            pl.BlockSpec((1, 1, 2 * D_FF), lambda i, be, sr: (be[i], 0, 0)),
            pl.BlockSpec((1, D_FF, d), lambda i, be, sr: (be[i], 0, 0)),
            pl.BlockSpec((1, 1, d), lambda i, be, sr: (be[i], 0, 0)),
        ],
        out_specs=pl.BlockSpec(memory_space=pl.ANY),
        scratch_shapes=[
            pltpu.VMEM((2, MOE_BLOCK, d), BF16), pltpu.VMEM((2, MOE_BLOCK, d), BF16),
            pltpu.VMEM((d, 2 * D_FF), BF16), pltpu.VMEM((D_FF, d), BF16),
            pltpu.SemaphoreType.DMA((2,)), pltpu.SemaphoreType.DMA((2,)),
        ],
    )
    return pl.pallas_call(
        functools.partial(_ffn_kernel, spare_grp=spare_grp),
        grid_spec=grid_spec,
        out_shape=jax.ShapeDtypeStruct((r, d), BF16),
        input_output_aliases={2: 0},
        compiler_params=_params(("arbitrary",)),
        name="expert_ffn",
    )(block_e, src, xs, w1, b1, w2, b2)


def _combine_kernel(ld_ref, w_ref, ys_ref, x1_ref, ga_ref, g_ref, o_ref, y3, o3, wb):
    tl = MOE_TL
    lanes = ROW_TILE[1]

    w4 = jnp.concatenate([w_ref[0, :, k * tl:(k + 1) * tl] for k in range(TOP_K)], axis=0)
    wt = jnp.concatenate([w4, jnp.zeros((lanes - TOP_K, tl), F32)], axis=0).T
    for k in range(TOP_K):
        wb[k] = jnp.broadcast_to(wt[:, k:k + 1], (tl, lanes))

    def relay(r, carry):
        r0 = pl.multiple_of(r * RELAY_ROWS, RELAY_ROWS)
        y3[pl.ds(r0, RELAY_ROWS)] = _rows_to_tiles(ys_ref[pl.ds(r0, RELAY_ROWS), :].astype(F32))
        return carry

    lax.fori_loop(0, MOE_CAP // RELAY_ROWS, relay, 0)

    def pull(t, carry):
        acc = None
        for k in range(TOP_K):
            wv = jnp.broadcast_to(wb[k, pl.ds(t, 1), :], ROW_TILE)
            term = wv * y3[ld_ref[0, 0, k * tl + t]]
            acc = term if acc is None else acc + term
        o3[t] = acc
        return carry

    lax.fori_loop(0, tl, pull, 0, unroll=8)

    def finish(r, carry):
        r0 = pl.multiple_of(r * RELAY_ROWS, RELAY_ROWS)
        x2 = x1_ref[pl.ds(r0, RELAY_ROWS), :] + ga_ref[0] * _tiles_to_rows(o3[pl.ds(r0, RELAY_ROWS)])
        ms = jnp.mean(x2 * x2, axis=-1, keepdims=True)
        o_ref[pl.ds(r0, RELAY_ROWS), :] = x2 * lax.rsqrt(ms + EPS) * g_ref[...]
        return carry

    lax.fori_loop(0, tl // RELAY_ROWS, finish, 0)


def _combine(ld, wts, ys, x1f, mod3, g_final, tiles_per_batch):
    t, d = x1f.shape
    tl = MOE_TL
    n_tiles = t // tl
    smem = functools.partial(pl.BlockSpec, (1, 1, TOP_K * tl), lambda i: (i, 0, 0),
                             memory_space=pltpu.SMEM)
    return pl.pallas_call(
        _combine_kernel,
        grid=(n_tiles,),
        in_specs=[
            smem(),
            pl.BlockSpec((1, 1, TOP_K * tl), lambda i: (i, 0, 0)),
            pl.BlockSpec((MOE_CAP, d), lambda i: (i, 0)),
            pl.BlockSpec((tl, d), lambda i: (i, 0)),
            pl.BlockSpec((1, 1, d), lambda i: ((i // tiles_per_batch) * N_MOD + 5, 0, 0)),
            pl.BlockSpec((1, d), lambda i: (0, 0)),
        ],
        out_specs=pl.BlockSpec((tl, d), lambda i: (i, 0)),
        out_shape=jax.ShapeDtypeStruct((t, d), F32),
        scratch_shapes=[pltpu.VMEM((MOE_CAP,) + ROW_TILE, F32), pltpu.VMEM((tl,) + ROW_TILE, F32),
                        pltpu.VMEM((TOP_K, tl, ROW_TILE[1]), F32)],
        compiler_params=_params(("arbitrary",), COMBINE_VMEM_LIMIT),
        name="moe_combine",
    )(ld, wts, ys, x1f, mod3, g_final)


def kernel(x, c, w_ada, b_ada, g_mix, w_in, conv_dw, conv_dw_bias, conv_ln_g, conv_ln_b,
           w_conv_out, lb_param, hgrn_norm_g, w_hgrn_out, w_out, g_ffn, w_router, b_router,
           w1, b1, w2, b2, g_final):
    b, s, d = x.shape
    assert w_ada.shape[0] == 1, "single-layer block"
    assert s % MOE_TL == 0
    t = b * s
    n_tiles = t // MOE_TL

    c_pad = jnp.zeros((8, d), F32).at[:b].set(c.astype(F32))
    mod, lb = _ada(c_pad, w_ada[0], b_ada, lb_param)
    mod3 = mod[:b].reshape(b * N_MOD, 1, d)

    dw_rows = jnp.broadcast_to(conv_dw[0][:, None, :], (CONV_K, SUBLANES, d))
    p, pg, uc = _inproj_conv(x, g_mix, mod3, w_in[0].astype(BF16), dw_rows, conv_dw_bias,
                             conv_ln_g, conv_ln_b)
    uh = _hgrn_branch(p, lb, hgrn_norm_g)

    x1, h3, logits_t = _merge(uc, uh, pg, x, mod3, g_ffn, w_conv_out[0].astype(BF16),
                              w_hgrn_out[0].astype(BF16), w_out[0].astype(BF16),
                              w_router[0].T, b_router[0][:, None])

    ld, wts, cnt = _route(logits_t)
    xs = _gather(ld, h3)
    groups_max = t * TOP_K // MOE_GRP + n_tiles * N_EXPERTS
    n_blocks = groups_max // MOE_GPB + N_EXPERTS
    block_e, src = _block_table(cnt[:, :, 0], n_blocks)
    ys = _ffn(block_e, src, xs, w1[0], b1[0][:, None, :], w2[0], b2[0][:, None, :])
    out = _combine(ld, wts, ys, x1.reshape(t, d), mod3, g_final.reshape(1, d), s // MOE_TL)
    return out.reshape(b, s, d)
```

```python
import functools

import jax
import jax.numpy as jnp
import numpy as np
from jax import lax
from jax.experimental import pallas as pl
from jax.experimental.pallas import tpu as pltpu

F32 = jnp.float32
BF16 = jnp.bfloat16

D_MODEL = 1024
CONV_K = 31
HG_HEADS = 8
HG_DK = 128
N_EXPERTS = 32
TOP_K = 4
D_FF = 1024
SWIGLU_ALPHA = 1.702
SWIGLU_LIMIT = 7.0
MOE_BLOCK = 512
EPS = 1e-6
N_MOD = 6
COL_CA, COL_CB, COL_Q, COL_F, COL_I, COL_G, COL_GC, COL_GH = range(8)
N_COLS = 8

HG_CHUNK = 128
HG_NB = 4
CONV_HALO = 32
VMEM_LIMIT = 56 * 1024 * 1024
COMBINE_VMEM_LIMIT = 62 * 1024 * 1024


def _sigmoid(x):
    return 1.0 / (1.0 + jnp.exp(-x))


def _params(sem, vmem=VMEM_LIMIT):
    return pltpu.CompilerParams(dimension_semantics=sem, vmem_limit_bytes=vmem)


def _ada_kernel(c_ref, w_ref, b_ref, lbp_ref, mod_ref, lb_ref):
    c = c_ref[...]
    c_act = c * _sigmoid(c)
    mod_ref[...] = jnp.dot(c_act, w_ref[...], preferred_element_type=F32,
                           precision=lax.Precision.HIGHEST) + b_ref[...]
    p = lbp_ref[...]
    e = jnp.exp(p - jnp.max(p, axis=0, keepdims=True))
    lb_ref[...] = e[0:1, :] / jnp.sum(e, axis=0, keepdims=True)


def _ada(c_pad, w_ada, b_ada, lb_param):
    nb, d = c_pad.shape
    n = w_ada.shape[1]
    tn = 1536
    return pl.pallas_call(
        _ada_kernel,
        grid=(n // tn,),
        in_specs=[
            pl.BlockSpec((nb, d), lambda j: (0, 0)),
            pl.BlockSpec((d, tn), lambda j: (0, j)),
            pl.BlockSpec((1, tn), lambda j: (0, j)),
            pl.BlockSpec(lb_param.shape, lambda j: (0, 0)),
        ],
        out_specs=[
            pl.BlockSpec((nb, tn), lambda j: (0, j)),
            pl.BlockSpec((1, d), lambda j: (0, 0)),
        ],
        out_shape=[
            jax.ShapeDtypeStruct((nb, n), F32),
            jax.ShapeDtypeStruct((1, d), F32),
        ],
        compiler_params=_params(("arbitrary",)),
        name="ada_mod",
    )(c_pad, w_ada, b_ada, lb_param)


CONV_TS = 256
CONV_RG = 32
SUBLANES = 8
CONV_SPAN = CONV_TS + CONV_HALO
CONV_OFF = CONV_HALO - (CONV_K - 1)
P_Q, P_F, P_I, P_G = range(4)
N_HCOLS = 4
PG_C, PG_H = range(2)


def _inproj_conv_kernel(x_ref, g_ref, sc_ref, sh_ref, w_hbm, dwb_ref, bias_ref, lng_ref, lnb_ref,
                        p_ref, pg_ref, uc_ref, buf, sh, cv, w_ref, wsem):
    i = pl.program_id(1)
    ts = CONV_TS
    d = D_MODEL

    @pl.when((i == 0) & (pl.program_id(0) == 0))
    def _():
        cp = pltpu.make_async_copy(w_hbm, w_ref, wsem)
        cp.start()
        cp.wait()

    @pl.when(i == 0)
    def _():
        buf[0:CONV_HALO, :] = jnp.zeros((CONV_HALO, d), F32)

    @pl.when(i > 0)
    def _():
        buf[0:CONV_HALO, :] = buf[ts:CONV_SPAN, :]

    x = x_ref[0]
    ms = jnp.mean(x * x, axis=-1, keepdims=True)
    h = x * lax.rsqrt(ms + EPS) * g_ref[...]
    h = (h * (1.0 + sc_ref[0]) + sh_ref[0]).astype(BF16)

    ab = jnp.dot(h, w_ref[:, 0:2 * d], preferred_element_type=F32)
    buf[CONV_HALO:CONV_SPAN, :] = ab[:, :d] * _sigmoid(ab[:, d:])
    hcols = (2 + N_HCOLS) * d
    p_ref[0] = jnp.dot(h, w_ref[:, 2 * d:hcols], preferred_element_type=F32)
    pg_ref[0] = jnp.dot(h, w_ref[:, hcols:], preferred_element_type=F32).astype(BF16)

    for s in range(1, SUBLANES):
        sh[s - 1] = buf[s:s + CONV_SPAN - SUBLANES, :]
    for r in range(ts // CONV_RG):
        base = r * CONV_RG
        acc = jnp.zeros((CONV_RG // SUBLANES, SUBLANES, d), F32)
        for j in range(CONV_K):
            s = (CONV_OFF + j) % SUBLANES
            row = base + CONV_OFF + j - s
            src = buf if s == 0 else sh.at[s - 1]
            rows = src[row:row + CONV_RG, :].reshape(CONV_RG // SUBLANES, SUBLANES, d)
            acc = acc + dwb_ref[j] * rows
        cv[base:base + CONV_RG, :] = acc.reshape(CONV_RG, d) + bias_ref[...]
    u = cv[...]
    mu = jnp.mean(u, axis=-1, keepdims=True)
    uc = u - mu
    var = jnp.mean(uc * uc, axis=-1, keepdims=True)
    y = uc * lax.rsqrt(var + EPS) * lng_ref[...] + lnb_ref[...]
    uc_ref[0] = (y * _sigmoid(y)).astype(BF16)


def _inproj_conv(x, g_mix, mod3, w_in_bf, dw_rows, bias, ln_g, ln_b):
    b, s, d = x.shape
    n = w_in_bf.shape[1]
    ts = CONV_TS
    vec = pl.BlockSpec((1, d), lambda bi, i: (0, 0))
    return pl.pallas_call(
        _inproj_conv_kernel,
        grid=(b, s // ts),
        in_specs=[
            pl.BlockSpec((1, ts, d), lambda bi, i: (bi, i, 0)),
            vec,
            pl.BlockSpec((1, 1, d), lambda bi, i: (bi * N_MOD + 1, 0, 0)),
            pl.BlockSpec((1, 1, d), lambda bi, i: (bi * N_MOD + 0, 0, 0)),
            pl.BlockSpec(memory_space=pl.ANY),
            pl.BlockSpec(dw_rows.shape, lambda bi, i: (0, 0, 0)),
            vec, vec, vec,
        ],
        out_specs=[
            pl.BlockSpec((1, ts, N_HCOLS * d), lambda bi, i: (bi, i, 0)),
            pl.BlockSpec((1, ts, 2 * d), lambda bi, i: (bi, i, 0)),
            pl.BlockSpec((1, ts, d), lambda bi, i: (bi, i, 0)),
        ],
        out_shape=[
            jax.ShapeDtypeStruct((b, s, N_HCOLS * d), F32),
            jax.ShapeDtypeStruct((b, s, 2 * d), BF16),
            jax.ShapeDtypeStruct((b, s, d), BF16),
        ],
        scratch_shapes=[pltpu.VMEM((CONV_SPAN, d), F32),
                        pltpu.VMEM((SUBLANES - 1, CONV_SPAN - SUBLANES, d), F32),
                        pltpu.VMEM((ts, d), F32),
                        pltpu.VMEM((d, n), BF16), pltpu.SemaphoreType.DMA(())],
        compiler_params=_params(("arbitrary", "arbitrary")),
        name="inproj_conv",
    )(x, g_mix, mod3, mod3, w_in_bf, dw_rows, bias, ln_g, ln_b)


def _hgrn_levels():
    c = HG_CHUNK
    levels = []
    m = c // 2
    while m >= 1:
        levels.append(m)
        m //= 2
    return levels


def _level_exponent(g_inc, logf, m, row):
    c, d = g_inc.shape
    upper = (row & m) != 0
    if m == 1:
        return upper, jnp.where(upper, logf, 0.0)
    if m >= SUBLANES:
        parts = [jnp.broadcast_to(g_inc[b * 2 * m + m - 1:b * 2 * m + m, :], (2 * m, d))
                 for b in range(c // (2 * m))]
        gref = parts[0] if len(parts) == 1 else jnp.concatenate(parts, axis=0)
    else:
        g3 = g_inc.reshape(c // SUBLANES, SUBLANES, d)

        def bcast(j):
            return jnp.broadcast_to(g3[:, j:j + 1, :], g3.shape).reshape(c, d)

        if 2 * m == SUBLANES:
            gref = bcast(m - 1)
        else:
            assert 4 * m == SUBLANES
            gref = jnp.where((row & (SUBLANES - 1)) < 2 * m, bcast(m - 1), bcast(3 * m - 1))
    return upper, jnp.where(upper, g_inc - gref, gref - g_inc)


def _hgrn_kernel(q_ref, z_ref, v_ref, og_ref, lb_ref, ng_ref, tril_ref, o_ref, st):
    @pl.when(pl.program_id(1) == 0)
    def _():
        st[...] = jnp.zeros_like(st)

    for bb in range(HG_NB):
        _hgrn_chunk(bb, q_ref, z_ref, v_ref, og_ref, lb_ref, ng_ref, tril_ref, o_ref, st)


def _hgrn_chunk(bb, q_ref, z_ref, v_ref, og_ref, lb_ref, ng_ref, tril_ref, o_ref, st):
    c = HG_CHUNK
    dk = HG_DK
    levels = _hgrn_levels()

    z = z_ref[bb]
    lb = lb_ref[...]
    sig = _sigmoid(z)
    f = lb + (1.0 - lb) * sig
    logf = jnp.log(f)
    kk = (1.0 - lb) * (1.0 - sig)
    q = q_ref[bb] * (dk ** -0.5)
    v = v_ref[bb]
    og = og_ref[bb]

    hi = logf.astype(BF16)
    lo = (logf - hi.astype(F32)).astype(BF16)
    tril = tril_ref[...]
    g_inc = (jnp.dot(tril, hi, preferred_element_type=F32)
             + jnp.dot(tril, lo, preferred_element_type=F32))
    g_last = g_inc[c - 1:c, :]
    q_st = (q * jnp.exp(g_inc)).astype(BF16)
    k_st = (kk * jnp.exp(g_last - g_inc)).astype(BF16)
    dec_all = jnp.exp(g_last)
    v_bf = v.astype(BF16)
    q_bf = q.astype(BF16)
    k_bf = kk.astype(BF16)

    row = lax.broadcasted_iota(jnp.int32, (c, c), 0)
    col = lax.broadcasted_iota(jnp.int32, (c, c), 1)
    rr = lax.broadcasted_iota(jnp.int32, (c, 1), 0)

    qs, ks, masks = [], [], []
    for m in levels:
        upper, ex = _level_exponent(g_inc, logf, m, rr)
        e = jnp.exp(ex)
        qs.append(jnp.where(upper, q * e, 0.0).astype(BF16))
        ks.append(jnp.where(upper, 0.0, kk * e).astype(BF16))
        sh = int(np.log2(2 * m))
        masks.append((row >> sh) == (col >> sh))
    diag = row == col

    nt = (((1,), (1,)), ((), ()))
    tn = (((0,), (0,)), ((), ()))
    for h in range(HG_HEADS):
        sl = slice(h * dk, (h + 1) * dk)
        a = jnp.where(diag, lax.dot_general(q_bf[:, sl], k_bf[:, sl], nt,
                                            preferred_element_type=F32), 0.0)
        for li in range(len(levels)):
            a = a + jnp.where(masks[li],
                              lax.dot_general(qs[li][:, sl], ks[li][:, sl], nt,
                                              preferred_element_type=F32), 0.0)
        s_t = st[bb, h]
        o = jnp.dot(a.astype(BF16), v_bf[:, sl], preferred_element_type=F32)
        o = o + lax.dot_general(q_st[:, sl], s_t.astype(BF16), nt, preferred_element_type=F32)
        st[bb, h] = s_t * dec_all[:, sl] + lax.dot_general(v_bf[:, sl], k_st[:, sl], tn,
                                                           preferred_element_type=F32)
        ms = jnp.mean(o * o, axis=-1, keepdims=True)
        o = o * lax.rsqrt(ms + EPS) * ng_ref[...]
        g = og[:, sl]
        o_ref[bb, :, sl] = (o * (g * _sigmoid(g))).astype(BF16)


def _hgrn_branch(p, lb, norm_g):
    b, s, _ = p.shape
    d = D_MODEL
    c = HG_CHUNK
    tril = jnp.asarray(np.tril(np.ones((c, c), np.float32)), dtype=BF16)

    assert b % HG_NB == 0

    def col_spec(col):
        return pl.BlockSpec((HG_NB, c, d), lambda bi, i: (bi, i, col))

    return pl.pallas_call(
        _hgrn_kernel,
        grid=(b // HG_NB, s // c),
        in_specs=[
            col_spec(P_Q), col_spec(P_F), col_spec(P_I), col_spec(P_G),
            pl.BlockSpec((1, d), lambda bi, i: (0, 0)),
            pl.BlockSpec((1, HG_DK), lambda bi, i: (0, 0)),
            pl.BlockSpec((c, c), lambda bi, i: (0, 0)),
        ],
        out_specs=pl.BlockSpec((HG_NB, c, d), lambda bi, i: (bi, i, 0)),
        out_shape=jax.ShapeDtypeStruct((b, s, d), BF16),
        scratch_shapes=[pltpu.VMEM((HG_NB, HG_HEADS, HG_DK, HG_DK), F32)],
        compiler_params=_params(("arbitrary", "arbitrary")),
        name="hgrn_branch",
    )(p, p, p, p, lb, norm_g, tril)


ROW_TILE = (8, 128)


def _rows_to_tiles(rows):
    st = jnp.stack([rows[:, j * 128:(j + 1) * 128] for j in range(ROW_TILE[0])], axis=0)
    return pltpu.einshape("jrl->rjl", st)


def _tiles_to_rows(tiles):
    y = pltpu.einshape("rjl->jrl", tiles)
    return jnp.concatenate([y[j] for j in range(ROW_TILE[0])], axis=-1)


_NT = (((1,), (1,)), ((), ()))
MERGE_SUB = 256


def _merge_kernel(uc_ref, uh_ref, gc_ref, gh_ref, x_ref, ga_ref, sc_ref, sh_ref, g_ref,
                  wc_ref, wh_ref, w_ref, wrt_ref, br_ref, x1_ref, h3_ref, lg_ref):
    wrt = wrt_ref[...]
    wrt_hi = wrt.astype(BF16)
    wrt_lo = (wrt - wrt_hi.astype(F32)).astype(BF16)
    for r in range(uc_ref.shape[1] // MERGE_SUB):
        rs = pl.ds(r * MERGE_SUB, MERGE_SUB)
        yc = jnp.dot(uc_ref[0, rs, :], wc_ref[...], preferred_element_type=F32)
        yh = jnp.dot(uh_ref[0, rs, :], wh_ref[...], preferred_element_type=F32)
        gate_c = _sigmoid(gc_ref[0, rs, :].astype(F32))
        gate_h = _sigmoid(gh_ref[0, rs, :].astype(F32))
        merged = (gate_c * yc + gate_h * yh).astype(BF16)
        x1 = x_ref[0, rs, :] + ga_ref[0] * jnp.dot(merged, w_ref[...],
                                                     preferred_element_type=F32)
        x1_ref[0, rs, :] = x1
        ms = jnp.mean(x1 * x1, axis=-1, keepdims=True)
        h2 = x1 * lax.rsqrt(ms + EPS) * g_ref[...]
        h2 = h2 * (1.0 + sc_ref[0]) + sh_ref[0]
        h3_ref[rs] = _rows_to_tiles(h2)
        h2_hi = h2.astype(BF16)
        h2_lo = (h2 - h2_hi.astype(F32)).astype(BF16)
        lg = (lax.dot_general(wrt_hi, h2_hi, _NT, preferred_element_type=F32)
              + lax.dot_general(wrt_hi, h2_lo, _NT, preferred_element_type=F32)
              + lax.dot_general(wrt_lo, h2_hi, _NT, preferred_element_type=F32))
        lg_ref[:, rs] = lg + br_ref[...]


def _merge(uc, uh, pg, x, mod3, g_ffn, w_conv_bf, w_hgrn_bf, w_out_bf, w_router_t, b_router_col):
    b, s, d = x.shape
    tm = 512
    nt = s // tm
    ne = w_router_t.shape[0]

    def mod_spec(k):
        return pl.BlockSpec((1, 1, d), lambda bi, i: (bi * N_MOD + k, 0, 0))

    def col_spec(col):
        return pl.BlockSpec((1, tm, d), lambda bi, i: (bi, i, col))

    tile = pl.BlockSpec((1, tm, d), lambda bi, i: (bi, i, 0))
    weight = pl.BlockSpec((d, d), lambda bi, i: (0, 0))
    return pl.pallas_call(
        _merge_kernel,
        grid=(b, nt),
        in_specs=[
            tile, tile, col_spec(PG_C), col_spec(PG_H), tile,
            mod_spec(2), mod_spec(4), mod_spec(3),
            pl.BlockSpec((1, d), lambda bi, i: (0, 0)),
            weight, weight, weight,
            pl.BlockSpec((ne, d), lambda bi, i: (0, 0)),
            pl.BlockSpec((ne, 1), lambda bi, i: (0, 0)),
        ],
        out_specs=[
            tile,
            pl.BlockSpec((tm,) + ROW_TILE, lambda bi, i: (bi * nt + i, 0, 0)),
            pl.BlockSpec((ne, tm), lambda bi, i: (0, bi * nt + i)),
        ],
        out_shape=[
            jax.ShapeDtypeStruct((b, s, d), F32),
            jax.ShapeDtypeStruct((b * s,) + ROW_TILE, F32),
            jax.ShapeDtypeStruct((ne, b * s), F32),
        ],
        compiler_params=_params(("arbitrary", "arbitrary")),
        name="merge_router",
    )(uc, uh, pg, pg, x, mod3, mod3, mod3, g_ffn, w_conv_bf, w_hgrn_bf, w_out_bf, w_router_t,
      b_router_col)


MOE_TL = 1024
MOE_GRP = 16
MOE_CAP = MOE_TL * TOP_K + N_EXPERTS * MOE_GRP
MOE_GPB = MOE_BLOCK // MOE_GRP
RELAY_ROWS = 256
assert MOE_CAP % RELAY_ROWS == 0 and MOE_TL % RELAY_ROWS == 0


def _route_kernel(lg_ref, u_ref, ld_ref, w_ref, cnt_ref):
    tl = MOE_TL
    l = lg_ref[...]
    eio = lax.broadcasted_iota(jnp.int32, l.shape, 0)
    vals, hots = [], []
    for _ in range(TOP_K):
        m = jnp.max(l, axis=0, keepdims=True)
        idx = jnp.min(jnp.where(l == m, eio, N_EXPERTS), axis=0, keepdims=True)
        hot = eio == idx
        vals.append(m)
        hots.append(hot)
        l = jnp.where(hot, -jnp.inf, l)
    ex = [jnp.exp(v - vals[0]) for v in vals]
    den = ex[0] + ex[1] + ex[2] + ex[3]
    cnt = hots[0].astype(F32)
    for k in range(1, TOP_K):
        cnt = cnt + hots[k].astype(F32)
    prefix = jnp.dot(cnt.astype(BF16), u_ref[...], preferred_element_type=F32)
    n_e = jnp.sum(cnt, axis=1, keepdims=True)
    pad_e = jnp.floor((n_e + (MOE_GRP - 1)) * (1.0 / MOE_GRP)) * MOE_GRP
    scan = jnp.broadcast_to(pad_e, (N_EXPERTS, 128))
    ei = lax.broadcasted_iota(jnp.int32, scan.shape, 0)
    dist = 1
    while dist < N_EXPERTS:
        scan = scan + jnp.where(ei >= dist, pltpu.roll(scan, dist, axis=0), 0.0)
        dist *= 2
    base = prefix + (scan[:, 0:1] - pad_e)
    for k in range(TOP_K):
        dest = jnp.sum(jnp.where(hots[k], base, 0.0), axis=0, keepdims=True)
        ld_ref[0, :, k * tl:(k + 1) * tl] = dest.astype(jnp.int32)
        w_ref[0, :, k * tl:(k + 1) * tl] = ex[k] / den
    cnt_ref[0] = jnp.broadcast_to(n_e, (N_EXPERTS, 128)).astype(jnp.int32)


def _route(logits_t):
    ne, t = logits_t.shape
    tl = MOE_TL
    n_tiles = t // tl
    upper = jnp.asarray(np.triu(np.ones((tl, tl), np.float32), k=1), dtype=BF16)
    slot = pl.BlockSpec((1, 1, TOP_K * tl), lambda i: (i, 0, 0))
    return pl.pallas_call(
        _route_kernel,
        grid=(n_tiles,),
        in_specs=[pl.BlockSpec((ne, tl), lambda i: (0, i)),
                  pl.BlockSpec((tl, tl), lambda i: (0, 0))],
        out_specs=[slot, slot, pl.BlockSpec((1, ne, 128), lambda i: (i, 0, 0))],
        out_shape=[
            jax.ShapeDtypeStruct((n_tiles, 1, TOP_K * tl), jnp.int32),
            jax.ShapeDtypeStruct((n_tiles, 1, TOP_K * tl), F32),
            jax.ShapeDtypeStruct((n_tiles, ne, 128), jnp.int32),
        ],
        compiler_params=_params(("arbitrary",)),
        name="route",
    )(logits_t, upper)


def _gather_kernel(ld_ref, h3_ref, xs_ref, xs3):
    tl = MOE_TL
    last = pl.num_programs(0) - 1

    @pl.when((pl.program_id(0) == 0) | (pl.program_id(0) == last))
    def _():
        xs3[...] = jnp.zeros_like(xs3)

    def push(t, carry):
        tile = h3_ref[t]
        for k in range(TOP_K):
            xs3[ld_ref[0, 0, k * tl + t]] = tile
        return carry

    @pl.when(pl.program_id(0) < last)
    def _():
        lax.fori_loop(0, tl, push, 0, unroll=4)

    def relay(r, carry):
        r0 = pl.multiple_of(r * RELAY_ROWS, RELAY_ROWS)
        xs_ref[pl.ds(r0, RELAY_ROWS), :] = _tiles_to_rows(xs3[pl.ds(r0, RELAY_ROWS)]).astype(BF16)
        return carry

    lax.fori_loop(0, MOE_CAP // RELAY_ROWS, relay, 0)


def _gather(ld, h3):
    t = h3.shape[0]
    tl = MOE_TL
    n_tiles = t // tl
    last = n_tiles - 1
    return pl.pallas_call(
        _gather_kernel,
        grid=(n_tiles + 1,),
        in_specs=[
            pl.BlockSpec((1, 1, TOP_K * tl), lambda i: (jnp.minimum(i, last), 0, 0),
                         memory_space=pltpu.SMEM),
            pl.BlockSpec((tl,) + ROW_TILE, lambda i: (jnp.minimum(i, last), 0, 0)),
        ],
        out_specs=pl.BlockSpec((MOE_CAP, D_MODEL), lambda i: (i, 0)),
        out_shape=jax.ShapeDtypeStruct(((n_tiles + 1) * MOE_CAP, D_MODEL), BF16),
        scratch_shapes=[pltpu.VMEM((MOE_CAP,) + ROW_TILE, F32)],
        compiler_params=_params(("arbitrary",)),
        name="moe_gather",
    )(ld, h3)


def _table_kernel(cnt_ref, be_ref, src_ref, toff):
    n_tiles = cnt_ref.shape[0]
    n_slots = src_ref.shape[0]
    groups_per_tile = MOE_CAP // MOE_GRP
    step = 8

    for i in range(n_tiles):
        toff[i] = 0

    def per_expert(e, pos):
        def per_tile(i, p):
            g = (cnt_ref[i, e] + (MOE_GRP - 1)) // MOE_GRP
            base = i * groups_per_tile + toff[i]

            def put(j, c):
                for u in range(step):
                    src_ref[p + j * step + u] = base + j * step + u
                return c

            lax.fori_loop(0, (g + (step - 1)) // step, put, 0)
            toff[i] = toff[i] + g
            return p + g

        end = lax.fori_loop(0, n_tiles, per_tile, pos)
        new_pos = (end + (MOE_GPB - 1)) // MOE_GPB * MOE_GPB
        for u in range(MOE_GPB):
            src_ref[end + u] = -1

        def put_e(b, c):
            be_ref[b] = e
            return c

        lax.fori_loop(pos // MOE_GPB, new_pos // MOE_GPB, put_e, 0)
        return new_pos

    used = lax.fori_loop(0, N_EXPERTS, per_expert, 0)

    def tail(b, c):
        be_ref[b] = N_EXPERTS - 1
        for u in range(MOE_GPB):
            src_ref[b * MOE_GPB + u] = -1
        return c

    lax.fori_loop(used // MOE_GPB, n_slots // MOE_GPB, tail, 0)


def _block_table(cnt, n_blocks):
    smem = pl.BlockSpec(memory_space=pltpu.SMEM)
    return pl.pallas_call(
        _table_kernel,
        in_specs=[smem],
        out_specs=[smem, smem],
        out_shape=[jax.ShapeDtypeStruct((n_blocks,), jnp.int32),
                   jax.ShapeDtypeStruct((n_blocks * MOE_GPB,), jnp.int32)],
        scratch_shapes=[pltpu.SMEM((cnt.shape[0],), jnp.int32)],
        name="moe_block_table",
    )(cnt)


def _ffn_kernel(be_ref, src_ref, xs_hbm, w1_ref, b1_ref, w2_ref, b2_ref, ys_hbm,
                xbuf, ybuf, w1s, w2s, sem_in, sem_out, *, spare_grp):
    i = pl.program_id(0)
    nb = pl.num_programs(0)
    slot = i % 2

    def live(blk):
        return src_ref[blk * MOE_GPB] >= 0

    def group_rows(grp):
        return pl.ds(pl.multiple_of(grp * MOE_GRP, MOE_GRP), MOE_GRP)

    def start_in(blk, sl):
        for g in range(MOE_GPB):
            grp = src_ref[blk * MOE_GPB + g]
            grp = jnp.where(grp >= 0, grp, spare_grp)
            pltpu.make_async_copy(xs_hbm.at[group_rows(grp), :],
                                  xbuf.at[sl, pl.ds(g * MOE_GRP, MOE_GRP), :], sem_in.at[sl]).start()

    def start_out(blk, sl):
        for g in range(MOE_GPB):
            grp = src_ref[blk * MOE_GPB + g]
            grp = jnp.where(grp >= 0, grp, spare_grp + 1 + sl * MOE_GPB + g)
            pltpu.make_async_copy(ybuf.at[sl, pl.ds(g * MOE_GRP, MOE_GRP), :],
                                  ys_hbm.at[group_rows(grp), :], sem_out.at[sl]).start()

    def wait_in(sl):
        pltpu.make_async_copy(xs_hbm.at[pl.ds(0, MOE_BLOCK), :], xbuf.at[sl], sem_in.at[sl]).wait()

    def wait_out(sl):
        pltpu.make_async_copy(ybuf.at[sl], ys_hbm.at[pl.ds(0, MOE_BLOCK), :], sem_out.at[sl]).wait()

    @pl.when((i == 0) & live(0))
    def _():
        start_in(0, 0)

    nxt = jnp.minimum(i + 1, nb - 1)

    @pl.when((i + 1 < nb) & live(nxt))
    def _():
        start_in(nxt, 1 - slot)

    @pl.when((i >= 2) & live(jnp.maximum(i - 2, 0)))
    def _():
        wait_out(slot)

    prev = be_ref[jnp.maximum(i - 1, 0)]

    @pl.when((i == 0) | (be_ref[i] != prev))
    def _():
        w1s[...] = w1_ref[0].astype(BF16)
        w2s[...] = w2_ref[0].astype(BF16)

    @pl.when(live(i))
    def _():
        wait_in(slot)
        u = jnp.dot(xbuf[slot], w1s[...], preferred_element_type=F32) + b1_ref[0]
        gl = jnp.minimum(u[:, :D_FF], SWIGLU_LIMIT)
        lin = jnp.clip(u[:, D_FF:], -SWIGLU_LIMIT, SWIGLU_LIMIT)
        act = gl * _sigmoid(SWIGLU_ALPHA * gl) * (lin + 1.0)
        y = jnp.dot(act.astype(BF16), w2s[...], preferred_element_type=F32) + b2_ref[0]
        ybuf[slot] = y.astype(BF16)
        start_out(i, slot)

    @pl.when(i == nb - 1)
    def _():
        @pl.when(live(i))
        def _():
            wait_out(slot)

        @pl.when((i >= 1) & live(jnp.maximum(i - 1, 0)))
        def _():
            wait_out(1 - slot)


def _ffn(block_e, src, xs, w1, b1, w2, b2):
    r, d = xs.shape
    nb = block_e.shape[0]
    spare_grp = (r - MOE_CAP) // MOE_GRP
    assert 1 + 2 * MOE_GPB <= MOE_CAP // MOE_GRP
    grid_spec = pltpu.PrefetchScalarGridSpec(
        num_scalar_prefetch=2,
        grid=(nb,),
        in_specs=[
            pl.BlockSpec(memory_space=pl.ANY),
            pl.BlockSpec((1, d, 2 * D_FF), lambda i, be, sr: (be[i], 0, 0)),
            pl.BlockSpec((1, 1, 2 * D_FF), lambda i, be, sr: (be[i], 0, 0)),
            pl.BlockSpec((1, D_FF, d), lambda i, be, sr: (be[i], 0, 0)),
            pl.BlockSpec((1, 1, d), lambda i, be, sr: (be[i], 0, 0)),
        ],
        out_specs=pl.BlockSpec(memory_space=pl.ANY),
        scratch_shapes=[
            pltpu.VMEM((2, MOE_BLOCK, d), BF16), pltpu.VMEM((2, MOE_BLOCK, d), BF16),
            pltpu.VMEM((d, 2 * D_FF), BF16), pltpu.VMEM((D_FF, d), BF16),
            pltpu.SemaphoreType.DMA((2,)), pltpu.SemaphoreType.DMA((2,)),
        ],
    )
    return pl.pallas_call(
        functools.partial(_ffn_kernel, spare_grp=spare_grp),
        grid_spec=grid_spec,
        out_shape=jax.ShapeDtypeStruct((r, d), BF16),
        input_output_aliases={2: 0},
        compiler_params=_params(("arbitrary",)),
        name="expert_ffn",
    )(block_e, src, xs, w1, b1, w2, b2)


def _combine_kernel(ld_ref, w_ref, ys_ref, x1_ref, ga_ref, g_ref, o_ref, y3, o3, wb):
    tl = MOE_TL
    lanes = ROW_TILE[1]

    w4 = jnp.concatenate([w_ref[0, :, k * tl:(k + 1) * tl] for k in range(TOP_K)], axis=0)
    wt = jnp.concatenate([w4, jnp.zeros((lanes - TOP_K, tl), F32)], axis=0).T
    for k in range(TOP_K):
        wb[k] = jnp.broadcast_to(wt[:, k:k + 1], (tl, lanes))

    def relay(r, carry):
        r0 = pl.multiple_of(r * RELAY_ROWS, RELAY_ROWS)
        y3[pl.ds(r0, RELAY_ROWS)] = _rows_to_tiles(ys_ref[pl.ds(r0, RELAY_ROWS), :].astype(F32))
        return carry

    lax.fori_loop(0, MOE_CAP // RELAY_ROWS, relay, 0)

    def pull(t, carry):
        acc = None
        for k in range(TOP_K):
            wv = jnp.broadcast_to(wb[k, pl.ds(t, 1), :], ROW_TILE)
            term = wv * y3[ld_ref[0, 0, k * tl + t]]
            acc = term if acc is None else acc + term
        o3[t] = acc
        return carry

    lax.fori_loop(0, tl, pull, 0, unroll=8)

    def finish(r, carry):
        r0 = pl.multiple_of(r * RELAY_ROWS, RELAY_ROWS)
        x2 = x1_ref[pl.ds(r0, RELAY_ROWS), :] + ga_ref[0] * _tiles_to_rows(o3[pl.ds(r0, RELAY_ROWS)])
        ms = jnp.mean(x2 * x2, axis=-1, keepdims=True)
        o_ref[pl.ds(r0, RELAY_ROWS), :] = x2 * lax.rsqrt(ms + EPS) * g_ref[...]
        return carry

    lax.fori_loop(0, tl // RELAY_ROWS, finish, 0)


def _combine(ld, wts, ys, x1f, mod3, g_final, tiles_per_batch):
    t, d = x1f.shape
    tl = MOE_TL
    n_tiles = t // tl
    smem = functools.partial(pl.BlockSpec, (1, 1, TOP_K * tl), lambda i: (i, 0, 0),
                             memory_space=pltpu.SMEM)
    return pl.pallas_call(
        _combine_kernel,
        grid=(n_tiles,),
        in_specs=[
            smem(),
            pl.BlockSpec((1, 1, TOP_K * tl), lambda i: (i, 0, 0)),
            pl.BlockSpec((MOE_CAP, d), lambda i: (i, 0)),
            pl.BlockSpec((tl, d), lambda i: (i, 0)),
            pl.BlockSpec((1, 1, d), lambda i: ((i // tiles_per_batch) * N_MOD + 5, 0, 0)),
            pl.BlockSpec((1, d), lambda i: (0, 0)),
        ],
        out_specs=pl.BlockSpec((tl, d), lambda i: (i, 0)),
        out_shape=jax.ShapeDtypeStruct((t, d), F32),
        scratch_shapes=[pltpu.VMEM((MOE_CAP,) + ROW_TILE, F32), pltpu.VMEM((tl,) + ROW_TILE, F32),
                        pltpu.VMEM((TOP_K, tl, ROW_TILE[1]), F32)],
        compiler_params=_params(("arbitrary",), COMBINE_VMEM_LIMIT),
        name="moe_combine",
    )(ld, wts, ys, x1f, mod3, g_final)


def kernel(x, c, w_ada, b_ada, g_mix, w_in, conv_dw, conv_dw_bias, conv_ln_g, conv_ln_b,
           w_conv_out, lb_param, hgrn_norm_g, w_hgrn_out, w_out, g_ffn, w_router, b_router,
           w1, b1, w2, b2, g_final):
    b, s, d = x.shape
    assert w_ada.shape[0] == 1, "single-layer block"
    assert s % MOE_TL == 0
    t = b * s
    n_tiles = t // MOE_TL

    c_pad = jnp.zeros((8, d), F32).at[:b].set(c.astype(F32))
    mod, lb = _ada(c_pad, w_ada[0], b_ada, lb_param)
    mod3 = mod[:b].reshape(b * N_MOD, 1, d)

    dw_rows = jnp.broadcast_to(conv_dw[0][:, None, :], (CONV_K, SUBLANES, d))
    p, pg, uc = _inproj_conv(x, g_mix, mod3, w_in[0].astype(BF16), dw_rows, conv_dw_bias,
                             conv_ln_g, conv_ln_b)
    uh = _hgrn_branch(p, lb, hgrn_norm_g)

    x1, h3, logits_t = _merge(uc, uh, pg, x, mod3, g_ffn, w_conv_out[0].astype(BF16),
                              w_hgrn_out[0].astype(BF16), w_out[0].astype(BF16),
                              w_router[0].T, b_router[0][:, None])

    ld, wts, cnt = _route(logits_t)
    xs = _gather(ld, h3)
    groups_max = t * TOP_K // MOE_GRP + n_tiles * N_EXPERTS
    n_blocks = groups_max // MOE_GPB + N_EXPERTS
    block_e, src = _block_table(cnt[:, :, 0], n_blocks)
    ys = _ffn(block_e, src, xs, w1[0], b1[0][:, None, :], w2[0], b2[0][:, None, :])
    out = _combine(ld, wts, ys, x1.reshape(t, d), mod3, g_final.reshape(1, d), s // MOE_TL)
    return out.reshape(b, s, d)
```

```python
import functools

import jax
import jax.numpy as jnp
import numpy as np
from jax import lax
from jax.experimental import pallas as pl
from jax.experimental.pallas import tpu as pltpu

F32 = jnp.float32
BF16 = jnp.bfloat16

D_MODEL = 1024
CONV_K = 31
HG_HEADS = 8
HG_DK = 128
N_EXPERTS = 32
TOP_K = 4
D_FF = 1024
SWIGLU_ALPHA = 1.702
SWIGLU_LIMIT = 7.0
MOE_BLOCK = 512
EPS = 1e-6
N_MOD = 6
HG_CHUNK = 128
HG_NB = 4
CONV_HALO = 32
VMEM_LIMIT = 56 * 1024 * 1024
COMBINE_VMEM_LIMIT = 62 * 1024 * 1024


def _sigmoid(x):
    return 1.0 / (1.0 + jnp.exp(-x))


def _params(sem, vmem=VMEM_LIMIT):
    return pltpu.CompilerParams(dimension_semantics=sem, vmem_limit_bytes=vmem)


def _ada_kernel(c_ref, w_ref, b_ref, lbp_ref, mod_ref, lb_ref):
    c = c_ref[...]
    c_act = c * _sigmoid(c)
    mod_ref[...] = jnp.dot(c_act, w_ref[...], preferred_element_type=F32,
                           precision=lax.Precision.HIGHEST) + b_ref[...]
    p = lbp_ref[...]
    e = jnp.exp(p - jnp.max(p, axis=0, keepdims=True))
    lb_ref[...] = e[0:1, :] / jnp.sum(e, axis=0, keepdims=True)


def _ada(c_pad, w_ada, b_ada, lb_param):
    nb, d = c_pad.shape
    n = w_ada.shape[1]
    tn = 1536
    return pl.pallas_call(
        _ada_kernel,
        grid=(n // tn,),
        in_specs=[
            pl.BlockSpec((nb, d), lambda j: (0, 0)),
            pl.BlockSpec((d, tn), lambda j: (0, j)),
            pl.BlockSpec((1, tn), lambda j: (0, j)),
            pl.BlockSpec(lb_param.shape, lambda j: (0, 0)),
        ],
        out_specs=[
            pl.BlockSpec((nb, tn), lambda j: (0, j)),
            pl.BlockSpec((1, d), lambda j: (0, 0)),
        ],
        out_shape=[
            jax.ShapeDtypeStruct((nb, n), F32),
            jax.ShapeDtypeStruct((1, d), F32),
        ],
        compiler_params=_params(("arbitrary",)),
        name="ada_mod",
    )(c_pad, w_ada, b_ada, lb_param)


CONV_TS = 256
CONV_RG = 32
SUBLANES = 8
CONV_SPAN = CONV_TS + CONV_HALO
CONV_OFF = CONV_HALO - (CONV_K - 1)
P_Q, P_F, P_I, P_G = range(4)
N_HCOLS = 4
PG_C, PG_H = range(2)


def _inproj_conv_kernel(x_ref, g_ref, sc_ref, sh_ref, w_hbm, dwb_ref, bias_ref, lng_ref, lnb_ref,
                        p_ref, pg_ref, uc_ref, buf, sh, cv, w_ref, wsem):
    i = pl.program_id(1)
    ts = CONV_TS
    d = D_MODEL

    @pl.when((i == 0) & (pl.program_id(0) == 0))
    def _():
        cp = pltpu.make_async_copy(w_hbm, w_ref, wsem)
        cp.start()
        cp.wait()

    @pl.when(i == 0)
    def _():
        buf[0:CONV_HALO, :] = jnp.zeros((CONV_HALO, d), F32)

    @pl.when(i > 0)
    def _():
        buf[0:CONV_HALO, :] = buf[ts:CONV_SPAN, :]

    x = x_ref[0]
    ms = jnp.mean(x * x, axis=-1, keepdims=True)
    h = x * lax.rsqrt(ms + EPS) * g_ref[...]
    h = (h * (1.0 + sc_ref[0]) + sh_ref[0]).astype(BF16)

    ab = jnp.dot(h, w_ref[:, 0:2 * d], preferred_element_type=F32)
    buf[CONV_HALO:CONV_SPAN, :] = ab[:, :d] * _sigmoid(ab[:, d:])
    hcols = (2 + N_HCOLS) * d
    p_ref[0] = jnp.dot(h, w_ref[:, 2 * d:hcols], preferred_element_type=F32)
    pg_ref[0] = jnp.dot(h, w_ref[:, hcols:], preferred_element_type=F32).astype(BF16)

    for s in range(1, SUBLANES):
        sh[s - 1] = buf[s:s + CONV_SPAN - SUBLANES, :]
    for r in range(ts // CONV_RG):
        base = r * CONV_RG
        acc = jnp.zeros((CONV_RG // SUBLANES, SUBLANES, d), F32)
        for j in range(CONV_K):
            s = (CONV_OFF + j) % SUBLANES
            row = base + CONV_OFF + j - s
            src = buf if s == 0 else sh.at[s - 1]
            rows = src[row:row + CONV_RG, :].reshape(CONV_RG // SUBLANES, SUBLANES, d)
            acc = acc + dwb_ref[j] * rows
        cv[base:base + CONV_RG, :] = acc.reshape(CONV_RG, d) + bias_ref[...]
    u = cv[...]
    mu = jnp.mean(u, axis=-1, keepdims=True)
    uc = u - mu
    var = jnp.mean(uc * uc, axis=-1, keepdims=True)
    y = uc * lax.rsqrt(var + EPS) * lng_ref[...] + lnb_ref[...]
    uc_ref[0] = (y * _sigmoid(y)).astype(BF16)


def _inproj_conv(x, g_mix, mod3, w_in_bf, dw_rows, bias, ln_g, ln_b):
    b, s, d = x.shape
    n = w_in_bf.shape[1]
    ts = CONV_TS
    vec = pl.BlockSpec((1, d), lambda bi, i: (0, 0))
    return pl.pallas_call(
        _inproj_conv_kernel,
        grid=(b, s // ts),
        in_specs=[
            pl.BlockSpec((1, ts, d), lambda bi, i: (bi, i, 0)),
            vec,
            pl.BlockSpec((1, 1, d), lambda bi, i: (bi * N_MOD + 1, 0, 0)),
            pl.BlockSpec((1, 1, d), lambda bi, i: (bi * N_MOD + 0, 0, 0)),
            pl.BlockSpec(memory_space=pl.ANY),
            pl.BlockSpec(dw_rows.shape, lambda bi, i: (0, 0, 0)),
            vec, vec, vec,
        ],
        out_specs=[
            pl.BlockSpec((1, ts, N_HCOLS * d), lambda bi, i: (bi, i, 0)),
            pl.BlockSpec((1, ts, 2 * d), lambda bi, i: (bi, i, 0)),
            pl.BlockSpec((1, ts, d), lambda bi, i: (bi, i, 0)),
        ],
        out_shape=[
            jax.ShapeDtypeStruct((b, s, N_HCOLS * d), F32),
            jax.ShapeDtypeStruct((b, s, 2 * d), BF16),
            jax.ShapeDtypeStruct((b, s, d), BF16),
        ],
        scratch_shapes=[pltpu.VMEM((CONV_SPAN, d), F32),
                        pltpu.VMEM((SUBLANES - 1, CONV_SPAN - SUBLANES, d), F32),
                        pltpu.VMEM((ts, d), F32),
                        pltpu.VMEM((d, n), BF16), pltpu.SemaphoreType.DMA(())],
        compiler_params=_params(("arbitrary", "arbitrary")),
        name="inproj_conv",
    )(x, g_mix, mod3, mod3, w_in_bf, dw_rows, bias, ln_g, ln_b)


def _hgrn_levels():
    c = HG_CHUNK
    levels = []
    m = c // 2
    while m >= 1:
        levels.append(m)
        m //= 2
    return levels


def _level_exponent(g_inc, logf, m, row):
    c, d = g_inc.shape
    upper = (row & m) != 0
    if m == 1:
        return upper, jnp.where(upper, logf, 0.0)
    if m >= SUBLANES:
        parts = [jnp.broadcast_to(g_inc[b * 2 * m + m - 1:b * 2 * m + m, :], (2 * m, d))
                 for b in range(c // (2 * m))]
        gref = parts[0] if len(parts) == 1 else jnp.concatenate(parts, axis=0)
    else:
        g3 = g_inc.reshape(c // SUBLANES, SUBLANES, d)

        def bcast(j):
            return jnp.broadcast_to(g3[:, j:j + 1, :], g3.shape).reshape(c, d)

        if 2 * m == SUBLANES:
            gref = bcast(m - 1)
        else:
            assert 4 * m == SUBLANES
            gref = jnp.where((row & (SUBLANES - 1)) < 2 * m, bcast(m - 1), bcast(3 * m - 1))
    return upper, jnp.where(upper, g_inc - gref, gref - g_inc)


def _hgrn_kernel(q_ref, z_ref, v_ref, og_ref, lb_ref, ng_ref, tril_ref, o_ref, st):
    @pl.when(pl.program_id(1) == 0)
    def _():
        st[...] = jnp.zeros_like(st)

    for bb in range(HG_NB):
        _hgrn_chunk(bb, q_ref, z_ref, v_ref, og_ref, lb_ref, ng_ref, tril_ref, o_ref, st)


def _hgrn_chunk(bb, q_ref, z_ref, v_ref, og_ref, lb_ref, ng_ref, tril_ref, o_ref, st):
    c = HG_CHUNK
    dk = HG_DK
    levels = _hgrn_levels()

    z = z_ref[bb]
    lb = lb_ref[...]
    sig = _sigmoid(z)
    f = lb + (1.0 - lb) * sig
    logf = jnp.log(f)
    kk = (1.0 - lb) * (1.0 - sig)
    q = q_ref[bb] * (dk ** -0.5)
    v = v_ref[bb]
    og = og_ref[bb]

    hi = logf.astype(BF16)
    lo = (logf - hi.astype(F32)).astype(BF16)
    tril = tril_ref[...]
    g_inc = (jnp.dot(tril, hi, preferred_element_type=F32)
             + jnp.dot(tril, lo, preferred_element_type=F32))
    g_last = g_inc[c - 1:c, :]
    q_st = (q * jnp.exp(g_inc)).astype(BF16)
    k_st = (kk * jnp.exp(g_last - g_inc)).astype(BF16)
    dec_all = jnp.exp(g_last)
    v_bf = v.astype(BF16)
    q_bf = q.astype(BF16)
    k_bf = kk.astype(BF16)

    row = lax.broadcasted_iota(jnp.int32, (c, c), 0)
    col = lax.broadcasted_iota(jnp.int32, (c, c), 1)
    rr = lax.broadcasted_iota(jnp.int32, (c, 1), 0)

    qs, ks, masks = [], [], []
    for m in levels:
        upper, ex = _level_exponent(g_inc, logf, m, rr)
        e = jnp.exp(ex)
        qs.append(jnp.where(upper, q * e, 0.0).astype(BF16))
        ks.append(jnp.where(upper, 0.0, kk * e).astype(BF16))
        sh = int(np.log2(2 * m))
        masks.append((row >> sh) == (col >> sh))
    diag = row == col

    nt = (((1,), (1,)), ((), ()))
    tn = (((0,), (0,)), ((), ()))
    for h in range(HG_HEADS):
        sl = slice(h * dk, (h + 1) * dk)
        a = jnp.where(diag, lax.dot_general(q_bf[:, sl], k_bf[:, sl], nt,
                                            preferred_element_type=F32), 0.0)
        for li in range(len(levels)):
            a = a + jnp.where(masks[li],
                              lax.dot_general(qs[li][:, sl], ks[li][:, sl], nt,
                                              preferred_element_type=F32), 0.0)
        s_t = st[bb, h]
        o = jnp.dot(a.astype(BF16), v_bf[:, sl], preferred_element_type=F32)
        o = o + lax.dot_general(q_st[:, sl], s_t.astype(BF16), nt, preferred_element_type=F32)
        st[bb, h] = s_t * dec_all[:, sl] + lax.dot_general(v_bf[:, sl], k_st[:, sl], tn,
                                                           preferred_element_type=F32)
        ms = jnp.mean(o * o, axis=-1, keepdims=True)
        o = o * lax.rsqrt(ms + EPS) * ng_ref[...]
        g = og[:, sl]
        o_ref[bb, :, sl] = (o * (g * _sigmoid(g))).astype(BF16)


def _hgrn_branch(p, lb, norm_g):
    b, s, _ = p.shape
    d = D_MODEL
    c = HG_CHUNK
    tril = jnp.asarray(np.tril(np.ones((c, c), np.float32)), dtype=BF16)

    assert b % HG_NB == 0

    def col_spec(col):
        return pl.BlockSpec((HG_NB, c, d), lambda bi, i: (bi, i, col))

    return pl.pallas_call(
        _hgrn_kernel,
        grid=(b // HG_NB, s // c),
        in_specs=[
            col_spec(P_Q), col_spec(P_F), col_spec(P_I), col_spec(P_G),
            pl.BlockSpec((1, d), lambda bi, i: (0, 0)),
            pl.BlockSpec((1, HG_DK), lambda bi, i: (0, 0)),
            pl.BlockSpec((c, c), lambda bi, i: (0, 0)),
        ],
        out_specs=pl.BlockSpec((HG_NB, c, d), lambda bi, i: (bi, i, 0)),
        out_shape=jax.ShapeDtypeStruct((b, s, d), BF16),
        scratch_shapes=[pltpu.VMEM((HG_NB, HG_HEADS, HG_DK, HG_DK), F32)],
        compiler_params=_params(("arbitrary", "arbitrary")),
        name="hgrn_branch",
    )(p, p, p, p, lb, norm_g, tril)


ROW_TILE = (8, 128)


def _rows_to_tiles(rows):
    st = jnp.stack([rows[:, j * 128:(j + 1) * 128] for j in range(ROW_TILE[0])], axis=0)
    return pltpu.einshape("jrl->rjl", st)


def _tiles_to_rows(tiles):
    y = pltpu.einshape("rjl->jrl", tiles)
    return jnp.concatenate([y[j] for j in range(ROW_TILE[0])], axis=-1)


_NT = (((1,), (1,)), ((), ()))


def _merge_kernel(uc_ref, uh_ref, gc_ref, gh_ref, x_ref, ga_ref, sc_ref, sh_ref, g_ref,
                  wc_ref, wh_ref, w_ref, wrt_ref, br_ref, x1_ref, h3_ref, lg_ref):
    yc = jnp.dot(uc_ref[0], wc_ref[...], preferred_element_type=F32)
    yh = jnp.dot(uh_ref[0], wh_ref[...], preferred_element_type=F32)
    gate_c = _sigmoid(gc_ref[0].astype(F32))
    gate_h = _sigmoid(gh_ref[0].astype(F32))
    merged = (gate_c * yc + gate_h * yh).astype(BF16)
    x1 = x_ref[0] + ga_ref[0] * jnp.dot(merged, w_ref[...], preferred_element_type=F32)
    x1_ref[0] = x1
    ms = jnp.mean(x1 * x1, axis=-1, keepdims=True)
    h2 = x1 * lax.rsqrt(ms + EPS) * g_ref[...]
    h2 = h2 * (1.0 + sc_ref[0]) + sh_ref[0]
    h3_ref[...] = _rows_to_tiles(h2)
    wrt = wrt_ref[...]
    wrt_hi = wrt.astype(BF16)
    wrt_lo = (wrt - wrt_hi.astype(F32)).astype(BF16)
    h2_hi = h2.astype(BF16)
    h2_lo = (h2 - h2_hi.astype(F32)).astype(BF16)
    lg = (lax.dot_general(wrt_hi, h2_hi, _NT, preferred_element_type=F32)
          + lax.dot_general(wrt_hi, h2_lo, _NT, preferred_element_type=F32)
          + lax.dot_general(wrt_lo, h2_hi, _NT, preferred_element_type=F32))
    lg_ref[...] = lg + br_ref[...]


def _merge(uc, uh, pg, x, mod3, g_ffn, w_conv_bf, w_hgrn_bf, w_out_bf, w_router_t, b_router_col):
    b, s, d = x.shape
    tm = 512
    nt = s // tm
    ne = w_router_t.shape[0]

    def mod_spec(k):
        return pl.BlockSpec((1, 1, d), lambda bi, i: (bi * N_MOD + k, 0, 0))

    def col_spec(col):
        return pl.BlockSpec((1, tm, d), lambda bi, i: (bi, i, col))

    tile = pl.BlockSpec((1, tm, d), lambda bi, i: (bi, i, 0))
    weight = pl.BlockSpec((d, d), lambda bi, i: (0, 0))
    return pl.pallas_call(
        _merge_kernel,
        grid=(b, nt),
        in_specs=[
            tile, tile, col_spec(PG_C), col_spec(PG_H), tile,
            mod_spec(2), mod_spec(4), mod_spec(3),
            pl.BlockSpec((1, d), lambda bi, i: (0, 0)),
            weight, weight, weight,
            pl.BlockSpec((ne, d), lambda bi, i: (0, 0)),
            pl.BlockSpec((ne, 1), lambda bi, i: (0, 0)),
        ],
        out_specs=[
            tile,
            pl.BlockSpec((tm,) + ROW_TILE, lambda bi, i: (bi * nt + i, 0, 0)),
            pl.BlockSpec((ne, tm), lambda bi, i: (0, bi * nt + i)),
        ],
        out_shape=[
            jax.ShapeDtypeStruct((b, s, d), F32),
            jax.ShapeDtypeStruct((b * s,) + ROW_TILE, F32),
            jax.ShapeDtypeStruct((ne, b * s), F32),
        ],
        compiler_params=_params(("arbitrary", "arbitrary")),
        name="merge_router",
    )(uc, uh, pg, pg, x, mod3, mod3, mod3, g_ffn, w_conv_bf, w_hgrn_bf, w_out_bf, w_router_t,
      b_router_col)


MOE_TL = 1024
MOE_GRP = 16
MOE_CAP = MOE_TL * TOP_K + N_EXPERTS * MOE_GRP
MOE_GPB = MOE_BLOCK // MOE_GRP
RELAY_ROWS = 256
assert MOE_CAP % RELAY_ROWS == 0 and MOE_TL % RELAY_ROWS == 0


def _route_kernel(lg_ref, u_ref, ld_ref, w_ref, cnt_ref):
    tl = MOE_TL
    l = lg_ref[...]
    eio = lax.broadcasted_iota(jnp.int32, l.shape, 0)
    vals, hots = [], []
    for _ in range(TOP_K):
        m = jnp.max(l, axis=0, keepdims=True)
        idx = jnp.min(jnp.where(l == m, eio, N_EXPERTS), axis=0, keepdims=True)
        hot = eio == idx
        vals.append(m)
        hots.append(hot)
        l = jnp.where(hot, -jnp.inf, l)
    ex = [jnp.exp(v - vals[0]) for v in vals]
    den = ex[0] + ex[1] + ex[2] + ex[3]
    cnt = hots[0].astype(F32)
    for k in range(1, TOP_K):
        cnt = cnt + hots[k].astype(F32)
    prefix = jnp.dot(cnt.astype(BF16), u_ref[...], preferred_element_type=F32)
    n_e = jnp.sum(cnt, axis=1, keepdims=True)
    pad_e = jnp.floor((n_e + (MOE_GRP - 1)) * (1.0 / MOE_GRP)) * MOE_GRP
    scan = jnp.broadcast_to(pad_e, (N_EXPERTS, 128))
    ei = lax.broadcasted_iota(jnp.int32, scan.shape, 0)
    dist = 1
    while dist < N_EXPERTS:
        scan = scan + jnp.where(ei >= dist, pltpu.roll(scan, dist, axis=0), 0.0)
        dist *= 2
    base = prefix + (scan[:, 0:1] - pad_e)
    for k in range(TOP_K):
        dest = jnp.sum(jnp.where(hots[k], base, 0.0), axis=0, keepdims=True)
        ld_ref[0, :, k * tl:(k + 1) * tl] = dest.astype(jnp.int32)
        w_ref[0, :, k * tl:(k + 1) * tl] = ex[k] / den
    cnt_ref[0] = jnp.broadcast_to(n_e, (N_EXPERTS, 128)).astype(jnp.int32)


def _route(logits_t):
    ne, t = logits_t.shape
    tl = MOE_TL
    n_tiles = t // tl
    upper = jnp.asarray(np.triu(np.ones((tl, tl), np.float32), k=1), dtype=BF16)
    slot = pl.BlockSpec((1, 1, TOP_K * tl), lambda i: (i, 0, 0))
    return pl.pallas_call(
        _route_kernel,
        grid=(n_tiles,),
        in_specs=[pl.BlockSpec((ne, tl), lambda i: (0, i)),
                  pl.BlockSpec((tl, tl), lambda i: (0, 0))],
        out_specs=[slot, slot, pl.BlockSpec((1, ne, 128), lambda i: (i, 0, 0))],
        out_shape=[
            jax.ShapeDtypeStruct((n_tiles, 1, TOP_K * tl), jnp.int32),
            jax.ShapeDtypeStruct((n_tiles, 1, TOP_K * tl), F32),
            jax.ShapeDtypeStruct((n_tiles, ne, 128), jnp.int32),
        ],
        compiler_params=_params(("arbitrary",)),
        name="route",
    )(logits_t, upper)


def _gather_kernel(ld_ref, h3_ref, xs_ref, xs3):
    tl = MOE_TL
    last = pl.num_programs(0) - 1

    @pl.when((pl.program_id(0) == 0) | (pl.program_id(0) == last))
    def _():
        xs3[...] = jnp.zeros_like(xs3)

    def push(t, carry):
        tile = h3_ref[t]
        for k in range(TOP_K):
            xs3[ld_ref[0, 0, k * tl + t]] = tile
        return carry

    @pl.when(pl.program_id(0) < last)
    def _():
        lax.fori_loop(0, tl, push, 0, unroll=4)

    def relay(r, carry):
        r0 = pl.multiple_of(r * RELAY_ROWS, RELAY_ROWS)
        xs_ref[pl.ds(r0, RELAY_ROWS), :] = _tiles_to_rows(xs3[pl.ds(r0, RELAY_ROWS)]).astype(BF16)
        return carry

    lax.fori_loop(0, MOE_CAP // RELAY_ROWS, relay, 0)


def _gather(ld, h3):
    t = h3.shape[0]
    tl = MOE_TL
    n_tiles = t // tl
    last = n_tiles - 1
    return pl.pallas_call(
        _gather_kernel,
        grid=(n_tiles + 1,),
        in_specs=[
            pl.BlockSpec((1, 1, TOP_K * tl), lambda i: (jnp.minimum(i, last), 0, 0),
                         memory_space=pltpu.SMEM),
            pl.BlockSpec((tl,) + ROW_TILE, lambda i: (jnp.minimum(i, last), 0, 0)),
        ],
        out_specs=pl.BlockSpec((MOE_CAP, D_MODEL), lambda i: (i, 0)),
        out_shape=jax.ShapeDtypeStruct(((n_tiles + 1) * MOE_CAP, D_MODEL), BF16),
        scratch_shapes=[pltpu.VMEM((MOE_CAP,) + ROW_TILE, F32)],
        compiler_params=_params(("arbitrary",)),
        name="moe_gather",
    )(ld, h3)


def _table_kernel(cnt_ref, be_ref, src_ref, toff):
    n_tiles = cnt_ref.shape[0]
    n_slots = src_ref.shape[0]
    groups_per_tile = MOE_CAP // MOE_GRP
    step = 8

    for i in range(n_tiles):
        toff[i] = 0

    def per_expert(e, pos):
        def per_tile(i, p):
            g = (cnt_ref[i, e] + (MOE_GRP - 1)) // MOE_GRP
            base = i * groups_per_tile + toff[i]

            def put(j, c):
                for u in range(step):
                    src_ref[p + j * step + u] = base + j * step + u
                return c

            lax.fori_loop(0, (g + (step - 1)) // step, put, 0)
            toff[i] = toff[i] + g
            return p + g

        end = lax.fori_loop(0, n_tiles, per_tile, pos)
        new_pos = (end + (MOE_GPB - 1)) // MOE_GPB * MOE_GPB
        for u in range(MOE_GPB):
            src_ref[end + u] = -1

        def put_e(b, c):
            be_ref[b] = e
            return c

        lax.fori_loop(pos // MOE_GPB, new_pos // MOE_GPB, put_e, 0)
        return new_pos

    used = lax.fori_loop(0, N_EXPERTS, per_expert, 0)

    def tail(b, c):
        be_ref[b] = N_EXPERTS - 1
        for u in range(MOE_GPB):
            src_ref[b * MOE_GPB + u] = -1
        return c

    lax.fori_loop(used // MOE_GPB, n_slots // MOE_GPB, tail, 0)


def _block_table(cnt, n_blocks):
    smem = pl.BlockSpec(memory_space=pltpu.SMEM)
    return pl.pallas_call(
        _table_kernel,
        in_specs=[smem],
        out_specs=[smem, smem],
        out_shape=[jax.ShapeDtypeStruct((n_blocks,), jnp.int32),
                   jax.ShapeDtypeStruct((n_blocks * MOE_GPB,), jnp.int32)],
        scratch_shapes=[pltpu.SMEM((cnt.shape[0],), jnp.int32)],
        name="moe_block_table",
    )(cnt)


def _ffn_kernel(be_ref, src_ref, xs_hbm, w1_ref, b1_ref, w2_ref, b2_ref, ys_hbm,
                xbuf, ybuf, w1s, w2s, sem_in, sem_out, *, spare_grp):
    i = pl.program_id(0)
    nb = pl.num_programs(0)
    slot = i % 2

    def live(blk):
        return src_ref[blk * MOE_GPB] >= 0

    def group_rows(grp):
        return pl.ds(pl.multiple_of(grp * MOE_GRP, MOE_GRP), MOE_GRP)

    def start_in(blk, sl):
        for g in range(MOE_GPB):
            grp = src_ref[blk * MOE_GPB + g]
            grp = jnp.where(grp >= 0, grp, spare_grp)
            pltpu.make_async_copy(xs_hbm.at[group_rows(grp), :],
                                  xbuf.at[sl, pl.ds(g * MOE_GRP, MOE_GRP), :], sem_in.at[sl]).start()

    def start_out(blk, sl):
        for g in range(MOE_GPB):
            grp = src_ref[blk * MOE_GPB + g]
            grp = jnp.where(grp >= 0, grp, spare_grp + 1 + sl * MOE_GPB + g)
            pltpu.make_async_copy(ybuf.at[sl, pl.ds(g * MOE_GRP, MOE_GRP), :],
                                  ys_hbm.at[group_rows(grp), :], sem_out.at[sl]).start()

    def wait_in(sl):
        pltpu.make_async_copy(xs_hbm.at[pl.ds(0, MOE_BLOCK), :], xbuf.at[sl], sem_in.at[sl]).wait()

    def wait_out(sl):
        pltpu.make_async_copy(ybuf.at[sl], ys_hbm.at[pl.ds(0, MOE_BLOCK), :], sem_out.at[sl]).wait()

    @pl.when((i == 0) & live(0))
    def _():
        start_in(0, 0)

    nxt = jnp.minimum(i + 1, nb - 1)

    @pl.when((i + 1 < nb) & live(nxt))
    def _():
        start_in(nxt, 1 - slot)

    @pl.when((i >= 2) & live(jnp.maximum(i - 2, 0)))
    def _():
        wait_out(slot)

    prev = be_ref[jnp.maximum(i - 1, 0)]

    @pl.when((i == 0) | (be_ref[i] != prev))
    def _():
        w1s[...] = w1_ref[0].astype(BF16)
        w2s[...] = w2_ref[0].astype(BF16)

    @pl.when(live(i))
    def _():
        wait_in(slot)
        u = jnp.dot(xbuf[slot], w1s[...], preferred_element_type=F32) + b1_ref[0]
        gl = jnp.minimum(u[:, :D_FF], SWIGLU_LIMIT)
        lin = jnp.clip(u[:, D_FF:], -SWIGLU_LIMIT, SWIGLU_LIMIT)
        act = gl * _sigmoid(SWIGLU_ALPHA * gl) * (lin + 1.0)
        y = jnp.dot(act.astype(BF16), w2s[...], preferred_element_type=F32) + b2_ref[0]
        ybuf[slot] = y.astype(BF16)
        start_out(i, slot)

    @pl.when(i == nb - 1)
    def _():
        @pl.when(live(i))
        def _():
            wait_out(slot)

        @pl.when((i >= 1) & live(jnp.maximum(i - 1, 0)))
        def _():
            wait_out(1 - slot)


def _ffn(block_e, src, xs, w1, b1, w2, b2):
    r, d = xs.shape
    nb = block_e.shape[0]
    spare_grp = (r - MOE_CAP) // MOE_GRP
    assert 1 + 2 * MOE_GPB <= MOE_CAP // MOE_GRP
    grid_spec = pltpu.PrefetchScalarGridSpec(
        num_scalar_prefetch=2,
        grid=(nb,),
        in_specs=[
            pl.BlockSpec(memory_space=pl.ANY),
            pl.BlockSpec((1, d, 2 * D_FF), lambda i, be, sr: (be[i], 0, 0)),
            pl.BlockSpec((1, 1, 2 * D_FF), lambda i, be, sr: (be[i], 0, 0)),
            pl.BlockSpec((1, D_FF, d), lambda i, be, sr: (be[i], 0, 0)),
            pl.BlockSpec((1, 1, d), lambda i, be, sr: (be[i], 0, 0)),
        ],
        out_specs=pl.BlockSpec(memory_space=pl.ANY),
        scratch_shapes=[
            pltpu.VMEM((2, MOE_BLOCK, d), BF16), pltpu.VMEM((2, MOE_BLOCK, d), BF16),
            pltpu.VMEM((d, 2 * D_FF), BF16), pltpu.VMEM((D_FF, d), BF16),
            pltpu.SemaphoreType.DMA((2,)), pltpu.SemaphoreType.DMA((2,)),
        ],
    )
    return pl.pallas_call(
        functools.partial(_ffn_kernel, spare_grp=spare_grp),
        grid_spec=grid_spec,
        out_shape=jax.ShapeDtypeStruct((r, d), BF16),
        input_output_aliases={2: 0},
        compiler_params=_params(("arbitrary",)),
        name="expert_ffn",
    )(block_e, src, xs, w1, b1, w2, b2)


def _combine_kernel(ld_ref, w_ref, ys_ref, x1_ref, ga_ref, g_ref, o_ref, y3, o3, wb):
    tl = MOE_TL
    lanes = ROW_TILE[1]

    w4 = jnp.concatenate([w_ref[0, :, k * tl:(k + 1) * tl] for k in range(TOP_K)], axis=0)
    wt = jnp.concatenate([w4, jnp.zeros((lanes - TOP_K, tl), F32)], axis=0).T
    for k in range(TOP_K):
        wb[k] = jnp.broadcast_to(wt[:, k:k + 1], (tl, lanes))

    def relay(r, carry):
        r0 = pl.multiple_of(r * RELAY_ROWS, RELAY_ROWS)
        y3[pl.ds(r0, RELAY_ROWS)] = _rows_to_tiles(ys_ref[pl.ds(r0, RELAY_ROWS), :].astype(F32))
        return carry

    lax.fori_loop(0, MOE_CAP // RELAY_ROWS, relay, 0)

    def pull(t, carry):
        acc = None
        for k in range(TOP_K):
            wv = jnp.broadcast_to(wb[k, pl.ds(t, 1), :], ROW_TILE)
            term = wv * y3[ld_ref[0, 0, k * tl + t]]
            acc = term if acc is None else acc + term
        o3[t] = acc
        return carry

    lax.fori_loop(0, tl, pull, 0, unroll=8)

    def finish(r, carry):
        r0 = pl.multiple_of(r * RELAY_ROWS, RELAY_ROWS)
        x2 = x1_ref[pl.ds(r0, RELAY_ROWS), :] + ga_ref[0] * _tiles_to_rows(o3[pl.ds(r0, RELAY_ROWS)])
        ms = jnp.mean(x2 * x2, axis=-1, keepdims=True)
        o_ref[pl.ds(r0, RELAY_ROWS), :] = x2 * lax.rsqrt(ms + EPS) * g_ref[...]
        return carry

    lax.fori_loop(0, tl // RELAY_ROWS, finish, 0)


def _combine(ld, wts, ys, x1f, mod3, g_final, tiles_per_batch):
    t, d = x1f.shape
    tl = MOE_TL
    n_tiles = t // tl
    smem = functools.partial(pl.BlockSpec, (1, 1, TOP_K * tl), lambda i: (i, 0, 0),
                             memory_space=pltpu.SMEM)
    return pl.pallas_call(
        _combine_kernel,
        grid=(n_tiles,),
        in_specs=[
            smem(),
            pl.BlockSpec((1, 1, TOP_K * tl), lambda i: (i, 0, 0)),
            pl.BlockSpec((MOE_CAP, d), lambda i: (i, 0)),
            pl.BlockSpec((tl, d), lambda i: (i, 0)),
            pl.BlockSpec((1, 1, d), lambda i: ((i // tiles_per_batch) * N_MOD + 5, 0, 0)),
            pl.BlockSpec((1, d), lambda i: (0, 0)),
        ],
        out_specs=pl.BlockSpec((tl, d), lambda i: (i, 0)),
        out_shape=jax.ShapeDtypeStruct((t, d), F32),
        scratch_shapes=[pltpu.VMEM((MOE_CAP,) + ROW_TILE, F32), pltpu.VMEM((tl,) + ROW_TILE, F32),
                        pltpu.VMEM((TOP_K, tl, ROW_TILE[1]), F32)],
        compiler_params=_params(("arbitrary",), COMBINE_VMEM_LIMIT),
        name="moe_combine",
    )(ld, wts, ys, x1f, mod3, g_final)


def kernel(x, c, w_ada, b_ada, g_mix, w_in, conv_dw, conv_dw_bias, conv_ln_g, conv_ln_b,
           w_conv_out, lb_param, hgrn_norm_g, w_hgrn_out, w_out, g_ffn, w_router, b_router,
           w1, b1, w2, b2, g_final):
    b, s, d = x.shape
    assert w_ada.shape[0] == 1, "single-layer block"
    assert s % MOE_TL == 0
    t = b * s
    n_tiles = t // MOE_TL

    c_pad = jnp.zeros((8, d), F32).at[:b].set(c.astype(F32))
    mod, lb = _ada(c_pad, w_ada[0], b_ada, lb_param)
    mod3 = mod[:b].reshape(b * N_MOD, 1, d)

    dw_rows = jnp.broadcast_to(conv_dw[0][:, None, :], (CONV_K, SUBLANES, d))
    p, pg, uc = _inproj_conv(x, g_mix, mod3, w_in[0].astype(BF16), dw_rows, conv_dw_bias,
                             conv_ln_g, conv_ln_b)
    uh = _hgrn_branch(p, lb, hgrn_norm_g)

    x1, h3, logits_t = _merge(uc, uh, pg, x, mod3, g_ffn, w_conv_out[0].astype(BF16),
                              w_hgrn_out[0].astype(BF16), w_out[0].astype(BF16),
                              w_router[0].T, b_router[0][:, None])

    ld, wts, cnt = _route(logits_t)
    xs = _gather(ld, h3)
    groups_max = t * TOP_K // MOE_GRP + n_tiles * N_EXPERTS
    n_blocks = groups_max // MOE_GPB + N_EXPERTS
    block_e, src = _block_table(cnt[:, :, 0], n_blocks)
    ys = _ffn(block_e, src, xs, w1[0], b1[0][:, None, :], w2[0], b2[0][:, None, :])
    out = _combine(ld, wts, ys, x1.reshape(t, d), mod3, g_final.reshape(1, d), s // MOE_TL)
    return out.reshape(b, s, d)
```

```python
import functools

import jax
import jax.numpy as jnp
import numpy as np
from jax import lax
from jax.experimental import pallas as pl
from jax.experimental.pallas import tpu as pltpu

F32 = jnp.float32
BF16 = jnp.bfloat16

D_MODEL = 1024
CONV_K = 31
HG_HEADS = 8
HG_DK = 128
N_EXPERTS = 32
TOP_K = 4
D_FF = 1024
SWIGLU_ALPHA = 1.702
SWIGLU_LIMIT = 7.0
MOE_BLOCK = 512
EPS = 1e-6
N_MOD = 6
HG_CHUNK = 128
HG_NB = 4
CONV_HALO = 32
VMEM_LIMIT = 56 * 1024 * 1024
COMBINE_VMEM_LIMIT = 62 * 1024 * 1024


def _sigmoid(x):
    return 1.0 / (1.0 + jnp.exp(-x))


def _params(sem, vmem=VMEM_LIMIT):
    return pltpu.CompilerParams(dimension_semantics=sem, vmem_limit_bytes=vmem)


def _ada_kernel(c_ref, w_ref, b_ref, lbp_ref, mod_ref, lb_ref):
    c = c_ref[...]
    c_act = c * _sigmoid(c)
    mod_ref[...] = jnp.dot(c_act, w_ref[...], preferred_element_type=F32,
                           precision=lax.Precision.HIGHEST) + b_ref[...]
    p = lbp_ref[...]
    e = jnp.exp(p - jnp.max(p, axis=0, keepdims=True))
    lb_ref[...] = e[0:1, :] / jnp.sum(e, axis=0, keepdims=True)


def _ada(c_pad, w_ada, b_ada, lb_param):
    nb, d = c_pad.shape
    n = w_ada.shape[1]
    tn = 1536
    return pl.pallas_call(
        _ada_kernel,
        grid=(n // tn,),
        in_specs=[
            pl.BlockSpec((nb, d), lambda j: (0, 0)),
            pl.BlockSpec((d, tn), lambda j: (0, j)),
            pl.BlockSpec((1, tn), lambda j: (0, j)),
            pl.BlockSpec(lb_param.shape, lambda j: (0, 0)),
        ],
        out_specs=[
            pl.BlockSpec((nb, tn), lambda j: (0, j)),
            pl.BlockSpec((1, d), lambda j: (0, 0)),
        ],
        out_shape=[
            jax.ShapeDtypeStruct((nb, n), F32),
            jax.ShapeDtypeStruct((1, d), F32),
        ],
        compiler_params=_params(("arbitrary",)),
        name="ada_mod",
    )(c_pad, w_ada, b_ada, lb_param)


CONV_TS = 256
SUBLANES = 8
PACK = 2
CONV_SPAN = CONV_TS + CONV_HALO
CONV_OFF = CONV_HALO - (CONV_K - 1)
CONV_WROWS = CONV_SPAN // PACK
CONV_RGW = 16
P_Q, P_F, P_I, P_G = range(4)
N_HCOLS = 4
PG_C, PG_H = range(2)


def _inproj_conv_kernel(x_ref, g_ref, sc_ref, sh_ref, w_hbm, dw_ref, bias_ref, lng_ref, lnb_ref,
                        p_ref, pg_ref, uc_ref, buf, pe, po, she, sho, cv, w_ref, wsem, dwp):
    i = pl.program_id(1)
    ts = CONV_TS
    d = D_MODEL

    @pl.when((i == 0) & (pl.program_id(0) == 0))
    def _():
        cp = pltpu.make_async_copy(w_hbm, w_ref, wsem)
        cp.start()
        dwp[...] = dw_ref[...].astype(BF16)
        buf[CONV_SPAN:CONV_SPAN + SUBLANES, :] = jnp.zeros((SUBLANES, d), F32)
        cp.wait()

    @pl.when(i == 0)
    def _():
        buf[0:CONV_HALO, :] = jnp.zeros((CONV_HALO, d), F32)

    @pl.when(i > 0)
    def _():
        buf[0:CONV_HALO, :] = buf[ts:CONV_SPAN, :]

    x = x_ref[0]
    ms = jnp.mean(x * x, axis=-1, keepdims=True)
    h = x * lax.rsqrt(ms + EPS) * g_ref[...]
    h = (h * (1.0 + sc_ref[0]) + sh_ref[0]).astype(BF16)

    ab = jnp.dot(h, w_ref[:, 0:2 * d], preferred_element_type=F32)
    buf[CONV_HALO:CONV_SPAN, :] = ab[:, :d] * _sigmoid(ab[:, d:])
    hcols = (2 + N_HCOLS) * d
    p_ref[0] = jnp.dot(h, w_ref[:, 2 * d:hcols], preferred_element_type=F32)
    pg_ref[0] = jnp.dot(h, w_ref[:, hcols:], preferred_element_type=F32).astype(BF16)

    pe[...] = pltpu.bitcast(buf[0:CONV_SPAN, :].astype(BF16), jnp.uint32)
    po[...] = pltpu.bitcast(buf[1:CONV_SPAN + 1, :].astype(BF16), jnp.uint32)
    for s in range(1, SUBLANES):
        she[s - 1] = pe[s:s + CONV_WROWS - SUBLANES, :]
        sho[s - 1] = po[s:s + CONV_WROWS - SUBLANES, :]
    rgw = CONV_RGW
    for base in range(0, ts // PACK, rgw):
        acc = None
        for j in range(CONV_K):
            o = CONV_OFF + j
            s = (o // PACK) % SUBLANES
            row = base + o // PACK - s
            if o % PACK == 0:
                src = pe if s == 0 else she.at[s - 1]
            else:
                src = po if s == 0 else sho.at[s - 1]
            words = pltpu.bitcast(src[row:row + rgw, :], BF16)
            term = words.reshape(rgw // SUBLANES, PACK * SUBLANES, d) * dwp[j]
            acc = term if acc is None else acc + term
        cv[PACK * base:PACK * (base + rgw), :] = (acc.reshape(PACK * rgw, d).astype(F32)
                                                  + bias_ref[...])
    u = cv[...]
    mu = jnp.mean(u, axis=-1, keepdims=True)
    uc = u - mu
    var = jnp.mean(uc * uc, axis=-1, keepdims=True)
    y = uc * lax.rsqrt(var + EPS) * lng_ref[...] + lnb_ref[...]
    uc_ref[0] = (y * _sigmoid(y)).astype(BF16)


def _inproj_conv(x, g_mix, mod3, w_in_bf, dw_rows, bias, ln_g, ln_b):
    b, s, d = x.shape
    n = w_in_bf.shape[1]
    ts = CONV_TS
    vec = pl.BlockSpec((1, d), lambda bi, i: (0, 0))
    return pl.pallas_call(
        _inproj_conv_kernel,
        grid=(b, s // ts),
        in_specs=[
            pl.BlockSpec((1, ts, d), lambda bi, i: (bi, i, 0)),
            vec,
            pl.BlockSpec((1, 1, d), lambda bi, i: (bi * N_MOD + 1, 0, 0)),
            pl.BlockSpec((1, 1, d), lambda bi, i: (bi * N_MOD + 0, 0, 0)),
            pl.BlockSpec(memory_space=pl.ANY),
            pl.BlockSpec(dw_rows.shape, lambda bi, i: (0, 0, 0)),
            vec, vec, vec,
        ],
        out_specs=[
            pl.BlockSpec((1, ts, N_HCOLS * d), lambda bi, i: (bi, i, 0)),
            pl.BlockSpec((1, ts, 2 * d), lambda bi, i: (bi, i, 0)),
            pl.BlockSpec((1, ts, d), lambda bi, i: (bi, i, 0)),
        ],
        out_shape=[
            jax.ShapeDtypeStruct((b, s, N_HCOLS * d), F32),
            jax.ShapeDtypeStruct((b, s, 2 * d), BF16),
            jax.ShapeDtypeStruct((b, s, d), BF16),
        ],
        scratch_shapes=[pltpu.VMEM((CONV_SPAN + SUBLANES, d), F32),
                        pltpu.VMEM((CONV_WROWS, d), jnp.uint32),
                        pltpu.VMEM((CONV_WROWS, d), jnp.uint32),
                        pltpu.VMEM((SUBLANES - 1, CONV_WROWS - SUBLANES, d), jnp.uint32),
                        pltpu.VMEM((SUBLANES - 1, CONV_WROWS - SUBLANES, d), jnp.uint32),
                        pltpu.VMEM((ts, d), F32),
                        pltpu.VMEM((d, n), BF16), pltpu.SemaphoreType.DMA(()),
                        pltpu.VMEM(dw_rows.shape, BF16)],
        compiler_params=_params(("arbitrary", "arbitrary")),
        name="inproj_conv",
    )(x, g_mix, mod3, mod3, w_in_bf, dw_rows, bias, ln_g, ln_b)


def _hgrn_levels():
    c = HG_CHUNK
    levels = []
    m = c // 2
    while m >= 1:
        levels.append(m)
        m //= 2
    return levels


def _level_exponent(g_inc, logf, m, row):
    c, d = g_inc.shape
    upper = (row & m) != 0
    if m == 1:
        return upper, jnp.where(upper, logf, 0.0)
    if m >= SUBLANES:
        parts = [jnp.broadcast_to(g_inc[b * 2 * m + m - 1:b * 2 * m + m, :], (2 * m, d))
                 for b in range(c // (2 * m))]
        gref = parts[0] if len(parts) == 1 else jnp.concatenate(parts, axis=0)
    else:
        g3 = g_inc.reshape(c // SUBLANES, SUBLANES, d)

        def bcast(j):
            return jnp.broadcast_to(g3[:, j:j + 1, :], g3.shape).reshape(c, d)

        if 2 * m == SUBLANES:
            gref = bcast(m - 1)
        else:
            assert 4 * m == SUBLANES
            gref = jnp.where((row & (SUBLANES - 1)) < 2 * m, bcast(m - 1), bcast(3 * m - 1))
    return upper, jnp.where(upper, g_inc - gref, gref - g_inc)


def _hgrn_kernel(q_ref, z_ref, v_ref, og_ref, lb_ref, ng_ref, tril_ref, o_ref, st):
    @pl.when(pl.program_id(1) == 0)
    def _():
        st[...] = jnp.zeros_like(st)

    for bb in range(HG_NB):
        _hgrn_chunk(bb, q_ref, z_ref, v_ref, og_ref, lb_ref, ng_ref, tril_ref, o_ref, st)


def _hgrn_chunk(bb, q_ref, z_ref, v_ref, og_ref, lb_ref, ng_ref, tril_ref, o_ref, st):
    c = HG_CHUNK
    dk = HG_DK
    levels = _hgrn_levels()

    z = z_ref[bb]
    lb = lb_ref[...]
    sig = _sigmoid(z)
    f = lb + (1.0 - lb) * sig
    logf = jnp.log(f)
    kk = (1.0 - lb) * (1.0 - sig)
    q = q_ref[bb] * (dk ** -0.5)
    v = v_ref[bb]
    og = og_ref[bb]

    hi = logf.astype(BF16)
    lo = (logf - hi.astype(F32)).astype(BF16)
    tril = tril_ref[...]
    g_inc = (jnp.dot(tril, hi, preferred_element_type=F32)
             + jnp.dot(tril, lo, preferred_element_type=F32))
    g_last = g_inc[c - 1:c, :]
    q_st = (q * jnp.exp(g_inc)).astype(BF16)
    k_st = (kk * jnp.exp(g_last - g_inc)).astype(BF16)
    dec_all = jnp.exp(g_last)
    v_bf = v.astype(BF16)
    q_bf = q.astype(BF16)
    k_bf = kk.astype(BF16)

    row = lax.broadcasted_iota(jnp.int32, (c, c), 0)
    col = lax.broadcasted_iota(jnp.int32, (c, c), 1)
    rr = lax.broadcasted_iota(jnp.int32, (c, 1), 0)

    qs, ks, masks = [], [], []
    for m in levels:
        upper, ex = _level_exponent(g_inc, logf, m, rr)
        e = jnp.exp(ex)
        qs.append(jnp.where(upper, q * e, 0.0).astype(BF16))
        ks.append(jnp.where(upper, 0.0, kk * e).astype(BF16))
        sh = int(np.log2(2 * m))
        masks.append((row >> sh) == (col >> sh))
    diag = row == col

    nt = (((1,), (1,)), ((), ()))
    tn = (((0,), (0,)), ((), ()))
    for h in range(HG_HEADS):
        sl = slice(h * dk, (h + 1) * dk)
        a = jnp.where(diag, lax.dot_general(q_bf[:, sl], k_bf[:, sl], nt,
                                            preferred_element_type=F32), 0.0)
        for li in range(len(levels)):
            a = a + jnp.where(masks[li],
                              lax.dot_general(qs[li][:, sl], ks[li][:, sl], nt,
                                              preferred_element_type=F32), 0.0)
        s_t = st[bb, h]
        o = jnp.dot(a.astype(BF16), v_bf[:, sl], preferred_element_type=F32)
        o = o + lax.dot_general(q_st[:, sl], s_t.astype(BF16), nt, preferred_element_type=F32)
        st[bb, h] = s_t * dec_all[:, sl] + lax.dot_general(v_bf[:, sl], k_st[:, sl], tn,
                                                           preferred_element_type=F32)
        ms = jnp.mean(o * o, axis=-1, keepdims=True)
        o = o * lax.rsqrt(ms + EPS) * ng_ref[...]
        g = og[:, sl]
        o_ref[bb, :, sl] = (o * (g * _sigmoid(g))).astype(BF16)


def _hgrn_branch(p, lb, norm_g):
    b, s, _ = p.shape
    d = D_MODEL
    c = HG_CHUNK
    tril = jnp.asarray(np.tril(np.ones((c, c), np.float32)), dtype=BF16)

    assert b % HG_NB == 0

    def col_spec(col):
        return pl.BlockSpec((HG_NB, c, d), lambda bi, i: (bi, i, col))

    return pl.pallas_call(
        _hgrn_kernel,
        grid=(b // HG_NB, s // c),
        in_specs=[
            col_spec(P_Q), col_spec(P_F), col_spec(P_I), col_spec(P_G),
            pl.BlockSpec((1, d), lambda bi, i: (0, 0)),
            pl.BlockSpec((1, HG_DK), lambda bi, i: (0, 0)),
            pl.BlockSpec((c, c), lambda bi, i: (0, 0)),
        ],
        out_specs=pl.BlockSpec((HG_NB, c, d), lambda bi, i: (bi, i, 0)),
        out_shape=jax.ShapeDtypeStruct((b, s, d), BF16),
        scratch_shapes=[pltpu.VMEM((HG_NB, HG_HEADS, HG_DK, HG_DK), F32)],
        compiler_params=_params(("arbitrary", "arbitrary")),
        name="hgrn_branch",
    )(p, p, p, p, lb, norm_g, tril)


ROW_TILE = (8, 128)


def _rows_to_tiles(rows):
    st = jnp.stack([rows[:, j * 128:(j + 1) * 128] for j in range(ROW_TILE[0])], axis=0)
    return pltpu.einshape("jrl->rjl", st)


def _tiles_to_rows(tiles):
    y = pltpu.einshape("rjl->jrl", tiles)
    return jnp.concatenate([y[j] for j in range(ROW_TILE[0])], axis=-1)


_NT = (((1,), (1,)), ((), ()))


def _merge_kernel(uc_ref, uh_ref, gc_ref, gh_ref, x_ref, ga_ref, sc_ref, sh_ref, g_ref,
                  wc_ref, wh_ref, w_ref, wrt_ref, br_ref, x1_ref, h3_ref, lg_ref):
    yc = jnp.dot(uc_ref[0], wc_ref[...], preferred_element_type=F32)
    yh = jnp.dot(uh_ref[0], wh_ref[...], preferred_element_type=F32)
    gate_c = _sigmoid(gc_ref[0].astype(F32))
    gate_h = _sigmoid(gh_ref[0].astype(F32))
    merged = (gate_c * yc + gate_h * yh).astype(BF16)
    x1 = x_ref[0] + ga_ref[0] * jnp.dot(merged, w_ref[...], preferred_element_type=F32)
    x1_ref[0] = x1
    ms = jnp.mean(x1 * x1, axis=-1, keepdims=True)
    h2 = x1 * lax.rsqrt(ms + EPS) * g_ref[...]
    h2 = h2 * (1.0 + sc_ref[0]) + sh_ref[0]
    h3_ref[...] = _rows_to_tiles(h2)
    wrt = wrt_ref[...]
    wrt_hi = wrt.astype(BF16)
    wrt_lo = (wrt - wrt_hi.astype(F32)).astype(BF16)
    h2_hi = h2.astype(BF16)
    h2_lo = (h2 - h2_hi.astype(F32)).astype(BF16)
    lg = (lax.dot_general(wrt_hi, h2_hi, _NT, preferred_element_type=F32)
          + lax.dot_general(wrt_hi, h2_lo, _NT, preferred_element_type=F32)
          + lax.dot_general(wrt_lo, h2_hi, _NT, preferred_element_type=F32))
    lg_ref[...] = lg + br_ref[...]


def _merge(uc, uh, pg, x, mod3, g_ffn, w_conv_bf, w_hgrn_bf, w_out_bf, w_router_t, b_router_col):
    b, s, d = x.shape
    tm = 512
    nt = s // tm
    ne = w_router_t.shape[0]

    def mod_spec(k):
        return pl.BlockSpec((1, 1, d), lambda bi, i: (bi * N_MOD + k, 0, 0))

    def col_spec(col):
        return pl.BlockSpec((1, tm, d), lambda bi, i: (bi, i, col))

    tile = pl.BlockSpec((1, tm, d), lambda bi, i: (bi, i, 0))
    weight = pl.BlockSpec((d, d), lambda bi, i: (0, 0))
    return pl.pallas_call(
        _merge_kernel,
        grid=(b, nt),
        in_specs=[
            tile, tile, col_spec(PG_C), col_spec(PG_H), tile,
            mod_spec(2), mod_spec(4), mod_spec(3),
            pl.BlockSpec((1, d), lambda bi, i: (0, 0)),
            weight, weight, weight,
            pl.BlockSpec((ne, d), lambda bi, i: (0, 0)),
            pl.BlockSpec((ne, 1), lambda bi, i: (0, 0)),
        ],
        out_specs=[
            tile,
            pl.BlockSpec((tm,) + ROW_TILE, lambda bi, i: (bi * nt + i, 0, 0)),
            pl.BlockSpec((ne, tm), lambda bi, i: (0, bi * nt + i)),
        ],
        out_shape=[
            jax.ShapeDtypeStruct((b, s, d), F32),
            jax.ShapeDtypeStruct((b * s,) + ROW_TILE, F32),
            jax.ShapeDtypeStruct((ne, b * s), F32),
        ],
        compiler_params=_params(("arbitrary", "arbitrary")),
        name="merge_router",
    )(uc, uh, pg, pg, x, mod3, mod3, mod3, g_ffn, w_conv_bf, w_hgrn_bf, w_out_bf, w_router_t,
      b_router_col)


MOE_TL = 1024
MOE_GRP = 16
MOE_CAP = MOE_TL * TOP_K + N_EXPERTS * MOE_GRP
MOE_GPB = MOE_BLOCK // MOE_GRP
RELAY_ROWS = 256
assert MOE_CAP % RELAY_ROWS == 0 and MOE_TL % RELAY_ROWS == 0


def _route_kernel(lg_ref, u_ref, ld_ref, w_ref, cnt_ref):
    tl = MOE_TL
    l = lg_ref[...]
    eio = lax.broadcasted_iota(jnp.int32, l.shape, 0)
    vals, hots = [], []
    for _ in range(TOP_K):
        m = jnp.max(l, axis=0, keepdims=True)
        idx = jnp.min(jnp.where(l == m, eio, N_EXPERTS), axis=0, keepdims=True)
        hot = eio == idx
        vals.append(m)
        hots.append(hot)
        l = jnp.where(hot, -jnp.inf, l)
    ex = [jnp.exp(v - vals[0]) for v in vals]
    den = ex[0] + ex[1] + ex[2] + ex[3]
    cnt = hots[0].astype(F32)
    for k in range(1, TOP_K):
        cnt = cnt + hots[k].astype(F32)
    prefix = jnp.dot(cnt.astype(BF16), u_ref[...], preferred_element_type=F32)
    n_e = jnp.sum(cnt, axis=1, keepdims=True)
    pad_e = jnp.floor((n_e + (MOE_GRP - 1)) * (1.0 / MOE_GRP)) * MOE_GRP
    scan = jnp.broadcast_to(pad_e, (N_EXPERTS, 128))
    ei = lax.broadcasted_iota(jnp.int32, scan.shape, 0)
    dist = 1
    while dist < N_EXPERTS:
        scan = scan + jnp.where(ei >= dist, pltpu.roll(scan, dist, axis=0), 0.0)
        dist *= 2
    base = prefix + (scan[:, 0:1] - pad_e)
    for k in range(TOP_K):
        dest = jnp.sum(jnp.where(hots[k], base, 0.0), axis=0, keepdims=True)
        ld_ref[0, :, k * tl:(k + 1) * tl] = dest.astype(jnp.int32)
        w_ref[0, :, k * tl:(k + 1) * tl] = ex[k] / den
    cnt_ref[0] = jnp.broadcast_to(n_e, (N_EXPERTS, 128)).astype(jnp.int32)


def _route(logits_t):
    ne, t = logits_t.shape
    tl = MOE_TL
    n_tiles = t // tl
    upper = jnp.asarray(np.triu(np.ones((tl, tl), np.float32), k=1), dtype=BF16)
    slot = pl.BlockSpec((1, 1, TOP_K * tl), lambda i: (i, 0, 0))
    return pl.pallas_call(
        _route_kernel,
        grid=(n_tiles,),
        in_specs=[pl.BlockSpec((ne, tl), lambda i: (0, i)),
                  pl.BlockSpec((tl, tl), lambda i: (0, 0))],
        out_specs=[slot, slot, pl.BlockSpec((1, ne, 128), lambda i: (i, 0, 0))],
        out_shape=[
            jax.ShapeDtypeStruct((n_tiles, 1, TOP_K * tl), jnp.int32),
            jax.ShapeDtypeStruct((n_tiles, 1, TOP_K * tl), F32),
            jax.ShapeDtypeStruct((n_tiles, ne, 128), jnp.int32),
        ],
        compiler_params=_params(("arbitrary",)),
        name="route",
    )(logits_t, upper)


def _gather_kernel(ld_ref, h3_ref, xs_ref, xs3):
    tl = MOE_TL
    last = pl.num_programs(0) - 1

    @pl.when((pl.program_id(0) == 0) | (pl.program_id(0) == last))
    def _():
        xs3[...] = jnp.zeros_like(xs3)

    def push(t, carry):
        tile = h3_ref[t]
        for k in range(TOP_K):
            xs3[ld_ref[0, 0, k * tl + t]] = tile
        return carry

    @pl.when(pl.program_id(0) < last)
    def _():
        lax.fori_loop(0, tl, push, 0, unroll=4)

    def relay(r, carry):
        r0 = pl.multiple_of(r * RELAY_ROWS, RELAY_ROWS)
        xs_ref[pl.ds(r0, RELAY_ROWS), :] = _tiles_to_rows(xs3[pl.ds(r0, RELAY_ROWS)]).astype(BF16)
        return carry

    lax.fori_loop(0, MOE_CAP // RELAY_ROWS, relay, 0)


def _gather(ld, h3):
    t = h3.shape[0]
    tl = MOE_TL
    n_tiles = t // tl
    last = n_tiles - 1
    return pl.pallas_call(
        _gather_kernel,
        grid=(n_tiles + 1,),
        in_specs=[
            pl.BlockSpec((1, 1, TOP_K * tl), lambda i: (jnp.minimum(i, last), 0, 0),
                         memory_space=pltpu.SMEM),
            pl.BlockSpec((tl,) + ROW_TILE, lambda i: (jnp.minimum(i, last), 0, 0)),
        ],
        out_specs=pl.BlockSpec((MOE_CAP, D_MODEL), lambda i: (i, 0)),
        out_shape=jax.ShapeDtypeStruct(((n_tiles + 1) * MOE_CAP, D_MODEL), BF16),
        scratch_shapes=[pltpu.VMEM((MOE_CAP,) + ROW_TILE, F32)],
        compiler_params=_params(("arbitrary",)),
        name="moe_gather",
    )(ld, h3)


def _table_kernel(cnt_ref, be_ref, src_ref, toff):
    n_tiles = cnt_ref.shape[0]
    n_slots = src_ref.shape[0]
    groups_per_tile = MOE_CAP // MOE_GRP
    step = 8

    for i in range(n_tiles):
        toff[i] = 0

    def per_expert(e, pos):
        def per_tile(i, p):
            g = (cnt_ref[i, e] + (MOE_GRP - 1)) // MOE_GRP
            base = i * groups_per_tile + toff[i]

            def put(j, c):
                for u in range(step):
                    src_ref[p + j * step + u] = base + j * step + u
                return c

            lax.fori_loop(0, (g + (step - 1)) // step, put, 0)
            toff[i] = toff[i] + g
            return p + g

        end = lax.fori_loop(0, n_tiles, per_tile, pos)
        new_pos = (end + (MOE_GPB - 1)) // MOE_GPB * MOE_GPB
        for u in range(MOE_GPB):
            src_ref[end + u] = -1

        def put_e(b, c):
            be_ref[b] = e
            return c

        lax.fori_loop(pos // MOE_GPB, new_pos // MOE_GPB, put_e, 0)
        return new_pos

    used = lax.fori_loop(0, N_EXPERTS, per_expert, 0)

    def tail(b, c):
        be_ref[b] = N_EXPERTS - 1
        for u in range(MOE_GPB):
            src_ref[b * MOE_GPB + u] = -1
        return c

    lax.fori_loop(used // MOE_GPB, n_slots // MOE_GPB, tail, 0)


def _block_table(cnt, n_blocks):
    smem = pl.BlockSpec(memory_space=pltpu.SMEM)
    return pl.pallas_call(
        _table_kernel,
        in_specs=[smem],
        out_specs=[smem, smem],
        out_shape=[jax.ShapeDtypeStruct((n_blocks,), jnp.int32),
                   jax.ShapeDtypeStruct((n_blocks * MOE_GPB,), jnp.int32)],
        scratch_shapes=[pltpu.SMEM((cnt.shape[0],), jnp.int32)],
        name="moe_block_table",
    )(cnt)


def _ffn_kernel(be_ref, src_ref, xs_hbm, w1_ref, b1_ref, w2_ref, b2_ref, ys_hbm,
                xbuf, ybuf, w1s, w2s, sem_in, sem_out, *, spare_grp):
    i = pl.program_id(0)
    nb = pl.num_programs(0)
    slot = i % 2

    def live(blk):
        return src_ref[blk * MOE_GPB] >= 0

    def group_rows(grp):
        return pl.ds(pl.multiple_of(grp * MOE_GRP, MOE_GRP), MOE_GRP)

    def start_in(blk, sl):
        for g in range(MOE_GPB):
            grp = src_ref[blk * MOE_GPB + g]
            grp = jnp.where(grp >= 0, grp, spare_grp)
            pltpu.make_async_copy(xs_hbm.at[group_rows(grp), :],
                                  xbuf.at[sl, pl.ds(g * MOE_GRP, MOE_GRP), :], sem_in.at[sl]).start()

    def start_out(blk, sl):
        for g in range(MOE_GPB):
            grp = src_ref[blk * MOE_GPB + g]
            grp = jnp.where(grp >= 0, grp, spare_grp + 1 + sl * MOE_GPB + g)
            pltpu.make_async_copy(ybuf.at[sl, pl.ds(g * MOE_GRP, MOE_GRP), :],
                                  ys_hbm.at[group_rows(grp), :], sem_out.at[sl]).start()

    def wait_in(sl):
        pltpu.make_async_copy(xs_hbm.at[pl.ds(0, MOE_BLOCK), :], xbuf.at[sl], sem_in.at[sl]).wait()

    def wait_out(sl):
        pltpu.make_async_copy(ybuf.at[sl], ys_hbm.at[pl.ds(0, MOE_BLOCK), :], sem_out.at[sl]).wait()

    @pl.when((i == 0) & live(0))
    def _():
        start_in(0, 0)

    nxt = jnp.minimum(i + 1, nb - 1)

    @pl.when((i + 1 < nb) & live(nxt))
    def _():
        start_in(nxt, 1 - slot)

    @pl.when((i >= 2) & live(jnp.maximum(i - 2, 0)))
    def _():
        wait_out(slot)

    prev = be_ref[jnp.maximum(i - 1, 0)]

    @pl.when((i == 0) | (be_ref[i] != prev))
    def _():
        w1s[...] = w1_ref[0].astype(BF16)
        w2s[...] = w2_ref[0].astype(BF16)

    @pl.when(live(i))
    def _():
        wait_in(slot)
        u = jnp.dot(xbuf[slot], w1s[...], preferred_element_type=F32) + b1_ref[0]
        gl = jnp.minimum(u[:, :D_FF], SWIGLU_LIMIT)
        lin = jnp.clip(u[:, D_FF:], -SWIGLU_LIMIT, SWIGLU_LIMIT)
        act = gl * _sigmoid(SWIGLU_ALPHA * gl) * (lin + 1.0)
        y = jnp.dot(act.astype(BF16), w2s[...], preferred_element_type=F32) + b2_ref[0]
        ybuf[slot] = y.astype(BF16)
        start_out(i, slot)

    @pl.when(i == nb - 1)
    def _():
        @pl.when(live(i))
        def _():
            wait_out(slot)

        @pl.when((i >= 1) & live(jnp.maximum(i - 1, 0)))
        def _():
            wait_out(1 - slot)


def _ffn(block_e, src, xs, w1, b1, w2, b2):
    r, d = xs.shape
    nb = block_e.shape[0]
    spare_grp = (r - MOE_CAP) // MOE_GRP
    assert 1 + 2 * MOE_GPB <= MOE_CAP // MOE_GRP
    grid_spec = pltpu.PrefetchScalarGridSpec(
        num_scalar_prefetch=2,
        grid=(nb,),
        in_specs=[
            pl.BlockSpec(memory_space=pl.ANY),
            pl.BlockSpec((1, d, 2 * D_FF), lambda i, be, sr: (be[i], 0, 0)),
            pl.BlockSpec((1, 1, 2 * D_FF), lambda i, be, sr: (be[i], 0, 0)),
            pl.BlockSpec((1, D_FF, d), lambda i, be, sr: (be[i], 0, 0)),
            pl.BlockSpec((1, 1, d), lambda i, be, sr: (be[i], 0, 0)),
        ],
        out_specs=pl.BlockSpec(memory_space=pl.ANY),
        scratch_shapes=[
            pltpu.VMEM((2, MOE_BLOCK, d), BF16), pltpu.VMEM((2, MOE_BLOCK, d), BF16),
            pltpu.VMEM((d, 2 * D_FF), BF16), pltpu.VMEM((D_FF, d), BF16),
            pltpu.SemaphoreType.DMA((2,)), pltpu.SemaphoreType.DMA((2,)),
        ],
    )
    return pl.pallas_call(
        functools.partial(_ffn_kernel, spare_grp=spare_grp),
        grid_spec=grid_spec,
        out_shape=jax.ShapeDtypeStruct((r, d), BF16),
        input_output_aliases={2: 0},
        compiler_params=_params(("arbitrary",)),
        name="expert_ffn",
    )(block_e, src, xs, w1, b1, w2, b2)


def _combine_kernel(ld_ref, w_ref, ys_ref, x1_ref, ga_ref, g_ref, o_ref, y3, o3, wb):
    tl = MOE_TL
    lanes = ROW_TILE[1]

    w4 = jnp.concatenate([w_ref[0, :, k * tl:(k + 1) * tl] for k in range(TOP_K)], axis=0)
    wt = jnp.concatenate([w4, jnp.zeros((lanes - TOP_K, tl), F32)], axis=0).T
    for k in range(TOP_K):
        wb[k] = jnp.broadcast_to(wt[:, k:k + 1], (tl, lanes))

    def relay(r, carry):
        r0 = pl.multiple_of(r * RELAY_ROWS, RELAY_ROWS)
        y3[pl.ds(r0, RELAY_ROWS)] = _rows_to_tiles(ys_ref[pl.ds(r0, RELAY_ROWS), :].astype(F32))
        return carry

    lax.fori_loop(0, MOE_CAP // RELAY_ROWS, relay, 0)

    def pull(t, carry):
        acc = None
        for k in range(TOP_K):
            wv = jnp.broadcast_to(wb[k, pl.ds(t, 1), :], ROW_TILE)
            term = wv * y3[ld_ref[0, 0, k * tl + t]]
            acc = term if acc is None else acc + term
        o3[t] = acc
        return carry

    lax.fori_loop(0, tl, pull, 0, unroll=8)

    def finish(r, carry):
        r0 = pl.multiple_of(r * RELAY_ROWS, RELAY_ROWS)
        x2 = x1_ref[pl.ds(r0, RELAY_ROWS), :] + ga_ref[0] * _tiles_to_rows(o3[pl.ds(r0, RELAY_ROWS)])
        ms = jnp.mean(x2 * x2, axis=-1, keepdims=True)
        o_ref[pl.ds(r0, RELAY_ROWS), :] = x2 * lax.rsqrt(ms + EPS) * g_ref[...]
        return carry

    lax.fori_loop(0, tl // RELAY_ROWS, finish, 0)


def _combine(ld, wts, ys, x1f, mod3, g_final, tiles_per_batch):
    t, d = x1f.shape
    tl = MOE_TL
    n_tiles = t // tl
    smem = functools.partial(pl.BlockSpec, (1, 1, TOP_K * tl), lambda i: (i, 0, 0),
                             memory_space=pltpu.SMEM)
    return pl.pallas_call(
        _combine_kernel,
        grid=(n_tiles,),
        in_specs=[
            smem(),
            pl.BlockSpec((1, 1, TOP_K * tl), lambda i: (i, 0, 0)),
            pl.BlockSpec((MOE_CAP, d), lambda i: (i, 0)),
            pl.BlockSpec((tl, d), lambda i: (i, 0)),
            pl.BlockSpec((1, 1, d), lambda i: ((i // tiles_per_batch) * N_MOD + 5, 0, 0)),
            pl.BlockSpec((1, d), lambda i: (0, 0)),
        ],
        out_specs=pl.BlockSpec((tl, d), lambda i: (i, 0)),
        out_shape=jax.ShapeDtypeStruct((t, d), F32),
        scratch_shapes=[pltpu.VMEM((MOE_CAP,) + ROW_TILE, F32), pltpu.VMEM((tl,) + ROW_TILE, F32),
                        pltpu.VMEM((TOP_K, tl, ROW_TILE[1]), F32)],
        compiler_params=_params(("arbitrary",), COMBINE_VMEM_LIMIT),
        name="moe_combine",
    )(ld, wts, ys, x1f, mod3, g_final)


def kernel(x, c, w_ada, b_ada, g_mix, w_in, conv_dw, conv_dw_bias, conv_ln_g, conv_ln_b,
           w_conv_out, lb_param, hgrn_norm_g, w_hgrn_out, w_out, g_ffn, w_router, b_router,
           w1, b1, w2, b2, g_final):
    b, s, d = x.shape
    assert w_ada.shape[0] == 1, "single-layer block"
    assert s % MOE_TL == 0
    t = b * s
    n_tiles = t // MOE_TL

    c_pad = jnp.zeros((8, d), F32).at[:b].set(c.astype(F32))
    mod, lb = _ada(c_pad, w_ada[0], b_ada, lb_param)
    mod3 = mod[:b].reshape(b * N_MOD, 1, d)

    dw_rows = jnp.broadcast_to(conv_dw[0][:, None, :], (CONV_K, PACK * SUBLANES, d))
    p, pg, uc = _inproj_conv(x, g_mix, mod3, w_in[0].astype(BF16), dw_rows, conv_dw_bias,
                             conv_ln_g, conv_ln_b)
    uh = _hgrn_branch(p, lb, hgrn_norm_g)

    x1, h3, logits_t = _merge(uc, uh, pg, x, mod3, g_ffn, w_conv_out[0].astype(BF16),
                              w_hgrn_out[0].astype(BF16), w_out[0].astype(BF16),
                              w_router[0].T, b_router[0][:, None])

    ld, wts, cnt = _route(logits_t)
    xs = _gather(ld, h3)
    groups_max = t * TOP_K // MOE_GRP + n_tiles * N_EXPERTS
    n_blocks = groups_max // MOE_GPB + N_EXPERTS
    block_e, src = _block_table(cnt[:, :, 0], n_blocks)
    ys = _ffn(block_e, src, xs, w1[0], b1[0][:, None, :], w2[0], b2[0][:, None, :])
    out = _combine(ld, wts, ys, x1.reshape(t, d), mod3, g_final.reshape(1, d), s // MOE_TL)
    return out.reshape(b, s, d)
```

```python
import functools

import jax
import jax.numpy as jnp
import numpy as np
from jax import lax
from jax.experimental import pallas as pl
from jax.experimental.pallas import tpu as pltpu

F32 = jnp.float32
BF16 = jnp.bfloat16

D_MODEL = 1024
CONV_K = 31
HG_HEADS = 8
HG_DK = 128
N_EXPERTS = 32
TOP_K = 4
D_FF = 1024
SWIGLU_ALPHA = 1.702
SWIGLU_LIMIT = 7.0
MOE_BLOCK = 512
EPS = 1e-6
N_MOD = 6
HG_CHUNK = 128
HG_NB = 4
CONV_HALO = 32
VMEM_LIMIT = 56 * 1024 * 1024
COMBINE_VMEM_LIMIT = 62 * 1024 * 1024


def _sigmoid(x):
    return 1.0 / (1.0 + jnp.exp(-x))


def _params(sem, vmem=VMEM_LIMIT):
    return pltpu.CompilerParams(dimension_semantics=sem, vmem_limit_bytes=vmem)


def _ada_kernel(c_ref, w_ref, b_ref, lbp_ref, mod_ref, lb_ref):
    c = c_ref[...]
    c_act = c * _sigmoid(c)
    mod_ref[...] = jnp.dot(c_act, w_ref[...], preferred_element_type=F32,
                           precision=lax.Precision.HIGHEST) + b_ref[...]
    p = lbp_ref[...]
    e = jnp.exp(p - jnp.max(p, axis=0, keepdims=True))
    lb_ref[...] = e[0:1, :] / jnp.sum(e, axis=0, keepdims=True)


def _ada(c_pad, w_ada, b_ada, lb_param):
    nb, d = c_pad.shape
    n = w_ada.shape[1]
    tn = 1536
    return pl.pallas_call(
        _ada_kernel,
        grid=(n // tn,),
        in_specs=[
            pl.BlockSpec((nb, d), lambda j: (0, 0)),
            pl.BlockSpec((d, tn), lambda j: (0, j)),
            pl.BlockSpec((1, tn), lambda j: (0, j)),
            pl.BlockSpec(lb_param.shape, lambda j: (0, 0)),
        ],
        out_specs=[
            pl.BlockSpec((nb, tn), lambda j: (0, j)),
            pl.BlockSpec((1, d), lambda j: (0, 0)),
        ],
        out_shape=[
            jax.ShapeDtypeStruct((nb, n), F32),
            jax.ShapeDtypeStruct((1, d), F32),
        ],
        compiler_params=_params(("arbitrary",)),
        name="ada_mod",
    )(c_pad, w_ada, b_ada, lb_param)


CONV_TS = 256
SUBLANES = 8
PACK = 2
CONV_SPAN = CONV_TS + CONV_HALO
CONV_OFF = CONV_HALO - (CONV_K - 1)
CONV_WROWS = CONV_SPAN // PACK
CONV_RGW = 16
P_Q, P_F, P_I, P_G = range(4)
N_HCOLS = 4
PG_C, PG_H = range(2)


def _inproj_conv_kernel(x_ref, g_ref, sc_ref, sh_ref, w_hbm, dw_ref, bias_ref, lng_ref, lnb_ref,
                        p_ref, pg_ref, uc_ref, buf, pe, po, she, sho, cv, w_ref, wsem, dwp):
    i = pl.program_id(1)
    ts = CONV_TS
    d = D_MODEL

    @pl.when((i == 0) & (pl.program_id(0) == 0))
    def _():
        cp = pltpu.make_async_copy(w_hbm, w_ref, wsem)
        cp.start()
        dwp[...] = dw_ref[...].astype(BF16)
        buf[CONV_SPAN:CONV_SPAN + SUBLANES, :] = jnp.zeros((SUBLANES, d), F32)
        cp.wait()

    @pl.when(i == 0)
    def _():
        buf[0:CONV_HALO, :] = jnp.zeros((CONV_HALO, d), F32)

    @pl.when(i > 0)
    def _():
        buf[0:CONV_HALO, :] = buf[ts:CONV_SPAN, :]

    x = x_ref[0]
    ms = jnp.mean(x * x, axis=-1, keepdims=True)
    h = x * lax.rsqrt(ms + EPS) * g_ref[...]
    h = (h * (1.0 + sc_ref[0]) + sh_ref[0]).astype(BF16)

    ab = jnp.dot(h, w_ref[:, 0:2 * d], preferred_element_type=F32)
    buf[CONV_HALO:CONV_SPAN, :] = ab[:, :d] * _sigmoid(ab[:, d:])
    hcols = (2 + N_HCOLS) * d
    p_ref[0] = jnp.dot(h, w_ref[:, 2 * d:hcols], preferred_element_type=F32)
    pg_ref[0] = jnp.dot(h, w_ref[:, hcols:], preferred_element_type=F32).astype(BF16)

    pe[...] = pltpu.bitcast(buf[0:CONV_SPAN, :].astype(BF16), jnp.uint32)
    po[...] = pltpu.bitcast(buf[1:CONV_SPAN + 1, :].astype(BF16), jnp.uint32)
    for s in range(1, SUBLANES):
        she[s - 1] = pe[s:s + CONV_WROWS - SUBLANES, :]
        sho[s - 1] = po[s:s + CONV_WROWS - SUBLANES, :]
    rgw = CONV_RGW
    for base in range(0, ts // PACK, rgw):
        acc = None
        for j in range(CONV_K):
            o = CONV_OFF + j
            s = (o // PACK) % SUBLANES
            row = base + o // PACK - s
            if o % PACK == 0:
                src = pe if s == 0 else she.at[s - 1]
            else:
                src = po if s == 0 else sho.at[s - 1]
            words = pltpu.bitcast(src[row:row + rgw, :], BF16)
            term = words.reshape(rgw // SUBLANES, PACK * SUBLANES, d) * dwp[j]
            acc = term if acc is None else acc + term
        cv[PACK * base:PACK * (base + rgw), :] = (acc.reshape(PACK * rgw, d).astype(F32)
                                                  + bias_ref[...])
    u = cv[...]
    mu = jnp.mean(u, axis=-1, keepdims=True)
    uc = u - mu
    var = jnp.mean(uc * uc, axis=-1, keepdims=True)
    y = uc * lax.rsqrt(var + EPS) * lng_ref[...] + lnb_ref[...]
    uc_ref[0] = (y * _sigmoid(y)).astype(BF16)


def _inproj_conv(x, g_mix, mod3, w_in_bf, dw_rows, bias, ln_g, ln_b):
    b, s, d = x.shape
    n = w_in_bf.shape[1]
    ts = CONV_TS
    vec = pl.BlockSpec((1, d), lambda bi, i: (0, 0))
    return pl.pallas_call(
        _inproj_conv_kernel,
        grid=(b, s // ts),
        in_specs=[
            pl.BlockSpec((1, ts, d), lambda bi, i: (bi, i, 0)),
            vec,
            pl.BlockSpec((1, 1, d), lambda bi, i: (bi * N_MOD + 1, 0, 0)),
            pl.BlockSpec((1, 1, d), lambda bi, i: (bi * N_MOD + 0, 0, 0)),
            pl.BlockSpec(memory_space=pl.ANY),
            pl.BlockSpec(dw_rows.shape, lambda bi, i: (0, 0, 0)),
            vec, vec, vec,
        ],
        out_specs=[
            pl.BlockSpec((1, ts, N_HCOLS * d), lambda bi, i: (bi, i, 0)),
            pl.BlockSpec((1, ts, 2 * d), lambda bi, i: (bi, i, 0)),
            pl.BlockSpec((1, ts, d), lambda bi, i: (bi, i, 0)),
        ],
        out_shape=[
            jax.ShapeDtypeStruct((b, s, N_HCOLS * d), F32),
            jax.ShapeDtypeStruct((b, s, 2 * d), BF16),
            jax.ShapeDtypeStruct((b, s, d), BF16),
        ],
        scratch_shapes=[pltpu.VMEM((CONV_SPAN + SUBLANES, d), F32),
                        pltpu.VMEM((CONV_WROWS, d), jnp.uint32),
                        pltpu.VMEM((CONV_WROWS, d), jnp.uint32),
                        pltpu.VMEM((SUBLANES - 1, CONV_WROWS - SUBLANES, d), jnp.uint32),
                        pltpu.VMEM((SUBLANES - 1, CONV_WROWS - SUBLANES, d), jnp.uint32),
                        pltpu.VMEM((ts, d), F32),
                        pltpu.VMEM((d, n), BF16), pltpu.SemaphoreType.DMA(()),
                        pltpu.VMEM(dw_rows.shape, BF16)],
        compiler_params=_params(("arbitrary", "arbitrary")),
        name="inproj_conv",
    )(x, g_mix, mod3, mod3, w_in_bf, dw_rows, bias, ln_g, ln_b)


def _hgrn_levels():
    c = HG_CHUNK
    levels = []
    m = c // 2
    while m >= 1:
        levels.append(m)
        m //= 2
    return levels


def _level_exponent(g_inc, logf, m, row):
    c, d = g_inc.shape
    upper = (row & m) != 0
    if m == 1:
        return upper, jnp.where(upper, logf, 0.0)
    if m >= SUBLANES:
        parts = [jnp.broadcast_to(g_inc[b * 2 * m + m - 1:b * 2 * m + m, :], (2 * m, d))
                 for b in range(c // (2 * m))]
        gref = parts[0] if len(parts) == 1 else jnp.concatenate(parts, axis=0)
    else:
        g3 = g_inc.reshape(c // SUBLANES, SUBLANES, d)

        def bcast(j):
            return jnp.broadcast_to(g3[:, j:j + 1, :], g3.shape).reshape(c, d)

        if 2 * m == SUBLANES:
            gref = bcast(m - 1)
        else:
            assert 4 * m == SUBLANES
            gref = jnp.where((row & (SUBLANES - 1)) < 2 * m, bcast(m - 1), bcast(3 * m - 1))
    return upper, jnp.where(upper, g_inc - gref, gref - g_inc)


def _hgrn_kernel(q_ref, z_ref, v_ref, og_ref, lb_ref, ng_ref, tril_ref, o_ref, st):
    @pl.when(pl.program_id(1) == 0)
    def _():
        st[...] = jnp.zeros_like(st)

    for bb in range(HG_NB):
        _hgrn_chunk(bb, q_ref, z_ref, v_ref, og_ref, lb_ref, ng_ref, tril_ref, o_ref, st)


def _hgrn_chunk(bb, q_ref, z_ref, v_ref, og_ref, lb_ref, ng_ref, tril_ref, o_ref, st):
    c = HG_CHUNK
    dk = HG_DK
    levels = _hgrn_levels()

    z = z_ref[bb]
    lb = lb_ref[...]
    sig = _sigmoid(z)
    f = lb + (1.0 - lb) * sig
    logf = jnp.log(f)
    kk = (1.0 - lb) * (1.0 - sig)
    q = q_ref[bb] * (dk ** -0.5)
    v = v_ref[bb]
    og = og_ref[bb]

    hi = logf.astype(BF16)
    lo = (logf - hi.astype(F32)).astype(BF16)
    tril = tril_ref[...]
    g_inc = (jnp.dot(tril, hi, preferred_element_type=F32)
             + jnp.dot(tril, lo, preferred_element_type=F32))
    g_last = g_inc[c - 1:c, :]
    q_st = (q * jnp.exp(g_inc)).astype(BF16)
    k_st = (kk * jnp.exp(g_last - g_inc)).astype(BF16)
    dec_all = jnp.exp(g_last)
    v_bf = v.astype(BF16)
    q_bf = q.astype(BF16)
    k_bf = kk.astype(BF16)

    row = lax.broadcasted_iota(jnp.int32, (c, c), 0)
    col = lax.broadcasted_iota(jnp.int32, (c, c), 1)
    rr = lax.broadcasted_iota(jnp.int32, (c, 1), 0)

    qs, ks, masks = [], [], []
    for m in levels:
        upper, ex = _level_exponent(g_inc, logf, m, rr)
        e = jnp.exp(ex)
        qs.append(jnp.where(upper, q * e, 0.0).astype(BF16))
        ks.append(jnp.where(upper, 0.0, kk * e).astype(BF16))
        sh = int(np.log2(2 * m))
        masks.append((row >> sh) == (col >> sh))
    diag = row == col

    nt = (((1,), (1,)), ((), ()))
    tn = (((0,), (0,)), ((), ()))
    for h in range(HG_HEADS):
        sl = slice(h * dk, (h + 1) * dk)
        a = jnp.where(diag, lax.dot_general(q_bf[:, sl], k_bf[:, sl], nt,
                                            preferred_element_type=F32), 0.0)
        for li in range(len(levels)):
            a = a + jnp.where(masks[li],
                              lax.dot_general(qs[li][:, sl], ks[li][:, sl], nt,
                                              preferred_element_type=F32), 0.0)
        s_t = st[bb, h]
        o = jnp.dot(a.astype(BF16), v_bf[:, sl], preferred_element_type=F32)
        o = o + lax.dot_general(q_st[:, sl], s_t.astype(BF16), nt, preferred_element_type=F32)
        st[bb, h] = s_t * dec_all[:, sl] + lax.dot_general(v_bf[:, sl], k_st[:, sl], tn,
                                                           preferred_element_type=F32)
        ms = jnp.mean(o * o, axis=-1, keepdims=True)
        o = o * lax.rsqrt(ms + EPS) * ng_ref[...]
        g = og[:, sl]
        o_ref[bb, :, sl] = (o * (g * _sigmoid(g))).astype(BF16)


def _hgrn_branch(p, lb, norm_g):
    b, s, _ = p.shape
    d = D_MODEL
    c = HG_CHUNK
    tril = jnp.asarray(np.tril(np.ones((c, c), np.float32)), dtype=BF16)

    assert b % HG_NB == 0

    def col_spec(col):
        return pl.BlockSpec((HG_NB, c, d), lambda bi, i: (bi, i, col))

    return pl.pallas_call(
        _hgrn_kernel,
        grid=(b // HG_NB, s // c),
        in_specs=[
            col_spec(P_Q), col_spec(P_F), col_spec(P_I), col_spec(P_G),
            pl.BlockSpec((1, d), lambda bi, i: (0, 0)),
            pl.BlockSpec((1, HG_DK), lambda bi, i: (0, 0)),
            pl.BlockSpec((c, c), lambda bi, i: (0, 0)),
        ],
        out_specs=pl.BlockSpec((HG_NB, c, d), lambda bi, i: (bi, i, 0)),
        out_shape=jax.ShapeDtypeStruct((b, s, d), BF16),
        scratch_shapes=[pltpu.VMEM((HG_NB, HG_HEADS, HG_DK, HG_DK), F32)],
        compiler_params=_params(("arbitrary", "arbitrary")),
        name="hgrn_branch",
    )(p, p, p, p, lb, norm_g, tril)


ROW_TILE = (8, 128)


def _rows_to_tiles(rows):
    st = jnp.stack([rows[:, j * 128:(j + 1) * 128] for j in range(ROW_TILE[0])], axis=0)
    return pltpu.einshape("jrl->rjl", st)


def _tile_rows(first_row):
    return pl.ds(pl.multiple_of(first_row, ROW_TILE[0]), ROW_TILE[0])


def _tiles_to_rows(tiles):
    y = pltpu.einshape("rjl->jrl", tiles)
    return jnp.concatenate([y[j] for j in range(ROW_TILE[0])], axis=-1)


_NT = (((1,), (1,)), ((), ()))


def _merge_kernel(uc_ref, uh_ref, gc_ref, gh_ref, x_ref, ga_ref, sc_ref, sh_ref, g_ref,
                  wc_ref, wh_ref, w_ref, wrt_ref, br_ref, x1_ref, h3_ref, lg_ref):
    yc = jnp.dot(uc_ref[0], wc_ref[...], preferred_element_type=F32)
    yh = jnp.dot(uh_ref[0], wh_ref[...], preferred_element_type=F32)
    gate_c = _sigmoid(gc_ref[0].astype(F32))
    gate_h = _sigmoid(gh_ref[0].astype(F32))
    merged = (gate_c * yc + gate_h * yh).astype(BF16)
    x1 = x_ref[0] + ga_ref[0] * jnp.dot(merged, w_ref[...], preferred_element_type=F32)
    x1_ref[0] = x1
    ms = jnp.mean(x1 * x1, axis=-1, keepdims=True)
    h2 = x1 * lax.rsqrt(ms + EPS) * g_ref[...]
    h2 = h2 * (1.0 + sc_ref[0]) + sh_ref[0]
    h3_ref[...] = _rows_to_tiles(h2)
    wrt = wrt_ref[...]
    wrt_hi = wrt.astype(BF16)
    wrt_lo = (wrt - wrt_hi.astype(F32)).astype(BF16)
    h2_hi = h2.astype(BF16)
    h2_lo = (h2 - h2_hi.astype(F32)).astype(BF16)
    lg = (lax.dot_general(wrt_hi, h2_hi, _NT, preferred_element_type=F32)
          + lax.dot_general(wrt_hi, h2_lo, _NT, preferred_element_type=F32)
          + lax.dot_general(wrt_lo, h2_hi, _NT, preferred_element_type=F32))
    lg_ref[...] = lg + br_ref[...]


def _merge(uc, uh, pg, x, mod3, g_ffn, w_conv_bf, w_hgrn_bf, w_out_bf, w_router_t, b_router_col):
    b, s, d = x.shape
    tm = 512
    nt = s // tm
    ne = w_router_t.shape[0]

    def mod_spec(k):
        return pl.BlockSpec((1, 1, d), lambda bi, i: (bi * N_MOD + k, 0, 0))

    def col_spec(col):
        return pl.BlockSpec((1, tm, d), lambda bi, i: (bi, i, col))

    tile = pl.BlockSpec((1, tm, d), lambda bi, i: (bi, i, 0))
    weight = pl.BlockSpec((d, d), lambda bi, i: (0, 0))
    return pl.pallas_call(
        _merge_kernel,
        grid=(b, nt),
        in_specs=[
            tile, tile, col_spec(PG_C), col_spec(PG_H), tile,
            mod_spec(2), mod_spec(4), mod_spec(3),
            pl.BlockSpec((1, d), lambda bi, i: (0, 0)),
            weight, weight, weight,
            pl.BlockSpec((ne, d), lambda bi, i: (0, 0)),
            pl.BlockSpec((ne, 1), lambda bi, i: (0, 0)),
        ],
        out_specs=[
            tile,
            pl.BlockSpec((tm,) + ROW_TILE, lambda bi, i: (bi * nt + i, 0, 0)),
            pl.BlockSpec((ne, tm), lambda bi, i: (0, bi * nt + i)),
        ],
        out_shape=[
            jax.ShapeDtypeStruct((b, s, d), F32),
            jax.ShapeDtypeStruct((b * s,) + ROW_TILE, F32),
            jax.ShapeDtypeStruct((ne, b * s), F32),
        ],
        compiler_params=_params(("arbitrary", "arbitrary")),
        name="merge_router",
    )(uc, uh, pg, pg, x, mod3, mod3, mod3, g_ffn, w_conv_bf, w_hgrn_bf, w_out_bf, w_router_t,
      b_router_col)


MOE_TL = 1024
MOE_GRP = 16
MOE_CAP = MOE_TL * TOP_K + N_EXPERTS * MOE_GRP
MOE_GPB = MOE_BLOCK // MOE_GRP
RELAY_ROWS = 256
assert MOE_CAP % RELAY_ROWS == 0 and MOE_TL % RELAY_ROWS == 0


def _route_kernel(lg_ref, u_ref, ld_ref, w_ref, cnt_ref):
    tl = MOE_TL
    l = lg_ref[...]
    eio = lax.broadcasted_iota(jnp.int32, l.shape, 0)
    vals, hots = [], []
    for _ in range(TOP_K):
        m = jnp.max(l, axis=0, keepdims=True)
        idx = jnp.min(jnp.where(l == m, eio, N_EXPERTS), axis=0, keepdims=True)
        hot = eio == idx
        vals.append(m)
        hots.append(hot)
        l = jnp.where(hot, -jnp.inf, l)
    ex = [jnp.exp(v - vals[0]) for v in vals]
    den = ex[0] + ex[1] + ex[2] + ex[3]
    cnt = hots[0].astype(F32)
    for k in range(1, TOP_K):
        cnt = cnt + hots[k].astype(F32)
    prefix = jnp.dot(cnt.astype(BF16), u_ref[...], preferred_element_type=F32)
    n_e = jnp.sum(cnt, axis=1, keepdims=True)
    pad_e = jnp.floor((n_e + (MOE_GRP - 1)) * (1.0 / MOE_GRP)) * MOE_GRP
    scan = jnp.broadcast_to(pad_e, (N_EXPERTS, ROW_TILE[1]))
    ei = lax.broadcasted_iota(jnp.int32, scan.shape, 0)
    dist = 1
    while dist < N_EXPERTS:
        scan = scan + jnp.where(ei >= dist, pltpu.roll(scan, dist, axis=0), 0.0)
        dist *= 2
    base = prefix + (scan[:, 0:1] - pad_e)
    for k in range(TOP_K):
        dest = jnp.sum(jnp.where(hots[k], base, 0.0), axis=0, keepdims=True)
        ld_ref[0, :, k * tl:(k + 1) * tl] = (dest * float(ROW_TILE[0])).astype(jnp.int32)
        w_ref[0, :, k * tl:(k + 1) * tl] = ex[k] / den
    cnt_ref[0] = jnp.broadcast_to(n_e, (N_EXPERTS, ROW_TILE[1])).astype(jnp.int32)


def _route(logits_t):
    ne, t = logits_t.shape
    tl = MOE_TL
    n_tiles = t // tl
    upper = jnp.asarray(np.triu(np.ones((tl, tl), np.float32), k=1), dtype=BF16)
    slot = pl.BlockSpec((1, 1, TOP_K * tl), lambda i: (i, 0, 0))
    return pl.pallas_call(
        _route_kernel,
        grid=(n_tiles,),
        in_specs=[pl.BlockSpec((ne, tl), lambda i: (0, i)),
                  pl.BlockSpec((tl, tl), lambda i: (0, 0))],
        out_specs=[slot, slot, pl.BlockSpec((1, ne, ROW_TILE[1]), lambda i: (i, 0, 0))],
        out_shape=[
            jax.ShapeDtypeStruct((n_tiles, 1, TOP_K * tl), jnp.int32),
            jax.ShapeDtypeStruct((n_tiles, 1, TOP_K * tl), F32),
            jax.ShapeDtypeStruct((n_tiles, ne, ROW_TILE[1]), jnp.int32),
        ],
        compiler_params=_params(("arbitrary",)),
        name="route",
    )(logits_t, upper)


def _gather_kernel(ld_ref, h3_ref, xs_ref, xs3):
    tl = MOE_TL
    last = pl.num_programs(0) - 1

    @pl.when((pl.program_id(0) == 0) | (pl.program_id(0) == last))
    def _():
        xs3[...] = jnp.zeros_like(xs3)

    def push(t, carry):
        tile = h3_ref[t]
        for k in range(TOP_K):
            xs3[_tile_rows(ld_ref[0, 0, k * tl + t]), :] = tile
        return carry

    @pl.when(pl.program_id(0) < last)
    def _():
        lax.fori_loop(0, tl, push, 0, unroll=4)

    def relay(r, carry):
        r0 = pl.multiple_of(r * RELAY_ROWS, RELAY_ROWS)
        tiles = xs3[pl.ds(r0 * ROW_TILE[0], RELAY_ROWS * ROW_TILE[0]), :]
        tiles = tiles.reshape((RELAY_ROWS,) + ROW_TILE)
        xs_ref[pl.ds(r0, RELAY_ROWS), :] = _tiles_to_rows(tiles).astype(BF16)
        return carry

    lax.fori_loop(0, MOE_CAP // RELAY_ROWS, relay, 0)


def _gather(ld, h3):
    t = h3.shape[0]
    tl = MOE_TL
    n_tiles = t // tl
    last = n_tiles - 1
    return pl.pallas_call(
        _gather_kernel,
        grid=(n_tiles + 1,),
        in_specs=[
            pl.BlockSpec((1, 1, TOP_K * tl), lambda i: (jnp.minimum(i, last), 0, 0),
                         memory_space=pltpu.SMEM),
            pl.BlockSpec((tl,) + ROW_TILE, lambda i: (jnp.minimum(i, last), 0, 0)),
        ],
        out_specs=pl.BlockSpec((MOE_CAP, D_MODEL), lambda i: (i, 0)),
        out_shape=jax.ShapeDtypeStruct(((n_tiles + 1) * MOE_CAP, D_MODEL), BF16),
        scratch_shapes=[pltpu.VMEM((MOE_CAP * ROW_TILE[0], ROW_TILE[1]), F32)],
        compiler_params=_params(("arbitrary",)),
        name="moe_gather",
    )(ld, h3)


def _table_kernel(cnt_ref, be_ref, src_ref, toff):
    n_tiles = cnt_ref.shape[0]
    n_slots = src_ref.shape[0]
    groups_per_tile = MOE_CAP // MOE_GRP
    step = 8

    for i in range(n_tiles):
        toff[i] = 0

    def per_expert(e, pos):
        def per_tile(i, p):
            g = (cnt_ref[i, e] + (MOE_GRP - 1)) // MOE_GRP
            base = i * groups_per_tile + toff[i]

            def put(j, c):
                for u in range(step):
                    src_ref[p + j * step + u] = base + j * step + u
                return c

            lax.fori_loop(0, (g + (step - 1)) // step, put, 0)
            toff[i] = toff[i] + g
            return p + g

        end = lax.fori_loop(0, n_tiles, per_tile, pos)
        new_pos = (end + (MOE_GPB - 1)) // MOE_GPB * MOE_GPB
        for u in range(MOE_GPB):
            src_ref[end + u] = -1

        def put_e(b, c):
            be_ref[b] = e
            return c

        lax.fori_loop(pos // MOE_GPB, new_pos // MOE_GPB, put_e, 0)
        return new_pos

    used = lax.fori_loop(0, N_EXPERTS, per_expert, 0)

    def tail(b, c):
        be_ref[b] = N_EXPERTS - 1
        for u in range(MOE_GPB):
            src_ref[b * MOE_GPB + u] = -1
        return c

    lax.fori_loop(used // MOE_GPB, n_slots // MOE_GPB, tail, 0)


def _block_table(cnt, n_blocks):
    smem = pl.BlockSpec(memory_space=pltpu.SMEM)
    return pl.pallas_call(
        _table_kernel,
        in_specs=[smem],
        out_specs=[smem, smem],
        out_shape=[jax.ShapeDtypeStruct((n_blocks,), jnp.int32),
                   jax.ShapeDtypeStruct((n_blocks * MOE_GPB,), jnp.int32)],
        scratch_shapes=[pltpu.SMEM((cnt.shape[0],), jnp.int32)],
        name="moe_block_table",
    )(cnt)


def _ffn_kernel(be_ref, src_ref, xs_hbm, w1_ref, b1_ref, w2_ref, b2_ref, ys_hbm,
                xbuf, ybuf, w1s, w2s, sem_in, sem_out, *, spare_grp):
    i = pl.program_id(0)
    nb = pl.num_programs(0)
    slot = i % 2

    def live(blk):
        return src_ref[blk * MOE_GPB] >= 0

    def group_rows(grp):
        return pl.ds(pl.multiple_of(grp * MOE_GRP, MOE_GRP), MOE_GRP)

    def start_in(blk, sl):
        for g in range(MOE_GPB):
            grp = src_ref[blk * MOE_GPB + g]
            grp = jnp.where(grp >= 0, grp, spare_grp)
            pltpu.make_async_copy(xs_hbm.at[group_rows(grp), :],
                                  xbuf.at[sl, pl.ds(g * MOE_GRP, MOE_GRP), :], sem_in.at[sl]).start()

    def start_out(blk, sl):
        for g in range(MOE_GPB):
            grp = src_ref[blk * MOE_GPB + g]
            grp = jnp.where(grp >= 0, grp, spare_grp + 1 + sl * MOE_GPB + g)
            pltpu.make_async_copy(ybuf.at[sl, pl.ds(g * MOE_GRP, MOE_GRP), :],
                                  ys_hbm.at[group_rows(grp), :], sem_out.at[sl]).start()

    def wait_in(sl):
        pltpu.make_async_copy(xs_hbm.at[pl.ds(0, MOE_BLOCK), :], xbuf.at[sl], sem_in.at[sl]).wait()

    def wait_out(sl):
        pltpu.make_async_copy(ybuf.at[sl], ys_hbm.at[pl.ds(0, MOE_BLOCK), :], sem_out.at[sl]).wait()

    @pl.when((i == 0) & live(0))
    def _():
        start_in(0, 0)

    nxt = jnp.minimum(i + 1, nb - 1)

    @pl.when((i + 1 < nb) & live(nxt))
    def _():
        start_in(nxt, 1 - slot)

    @pl.when((i >= 2) & live(jnp.maximum(i - 2, 0)))
    def _():
        wait_out(slot)

    prev = be_ref[jnp.maximum(i - 1, 0)]

    @pl.when((i == 0) | (be_ref[i] != prev))
    def _():
        w1s[...] = w1_ref[0].astype(BF16)
        w2s[...] = w2_ref[0].astype(BF16)

    @pl.when(live(i))
    def _():
        wait_in(slot)
        u = jnp.dot(xbuf[slot], w1s[...], preferred_element_type=F32) + b1_ref[0]
        gl = jnp.minimum(u[:, :D_FF], SWIGLU_LIMIT)
        lin = jnp.clip(u[:, D_FF:], -SWIGLU_LIMIT, SWIGLU_LIMIT)
        act = gl * _sigmoid(SWIGLU_ALPHA * gl) * (lin + 1.0)
        y = jnp.dot(act.astype(BF16), w2s[...], preferred_element_type=F32) + b2_ref[0]
        ybuf[slot] = y.astype(BF16)
        start_out(i, slot)

    @pl.when(i == nb - 1)
    def _():
        @pl.when(live(i))
        def _():
            wait_out(slot)

        @pl.when((i >= 1) & live(jnp.maximum(i - 1, 0)))
        def _():
            wait_out(1 - slot)


def _ffn(block_e, src, xs, w1, b1, w2, b2):
    r, d = xs.shape
    nb = block_e.shape[0]
    spare_grp = (r - MOE_CAP) // MOE_GRP
    assert 1 + 2 * MOE_GPB <= MOE_CAP // MOE_GRP
    grid_spec = pltpu.PrefetchScalarGridSpec(
        num_scalar_prefetch=2,
        grid=(nb,),
        in_specs=[
            pl.BlockSpec(memory_space=pl.ANY),
            pl.BlockSpec((1, d, 2 * D_FF), lambda i, be, sr: (be[i], 0, 0)),
            pl.BlockSpec((1, 1, 2 * D_FF), lambda i, be, sr: (be[i], 0, 0)),
            pl.BlockSpec((1, D_FF, d), lambda i, be, sr: (be[i], 0, 0)),
            pl.BlockSpec((1, 1, d), lambda i, be, sr: (be[i], 0, 0)),
        ],
        out_specs=pl.BlockSpec(memory_space=pl.ANY),
        scratch_shapes=[
            pltpu.VMEM((2, MOE_BLOCK, d), BF16), pltpu.VMEM((2, MOE_BLOCK, d), BF16),
            pltpu.VMEM((d, 2 * D_FF), BF16), pltpu.VMEM((D_FF, d), BF16),
            pltpu.SemaphoreType.DMA((2,)), pltpu.SemaphoreType.DMA((2,)),
        ],
    )
    return pl.pallas_call(
        functools.partial(_ffn_kernel, spare_grp=spare_grp),
        grid_spec=grid_spec,
        out_shape=jax.ShapeDtypeStruct((r, d), BF16),
        input_output_aliases={2: 0},
        compiler_params=_params(("arbitrary",)),
        name="expert_ffn",
    )(block_e, src, xs, w1, b1, w2, b2)


def _combine_kernel(ld_ref, w_ref, ys_ref, x1_ref, ga_ref, g_ref, o_ref, y3, o3, wb):
    tl = MOE_TL
    lanes = ROW_TILE[1]

    w4 = jnp.concatenate([w_ref[0, :, k * tl:(k + 1) * tl] for k in range(TOP_K)], axis=0)
    wt = jnp.concatenate([w4, jnp.zeros((lanes - TOP_K, tl), F32)], axis=0).T
    for k in range(TOP_K):
        wb[k] = jnp.broadcast_to(wt[:, k:k + 1], (tl, lanes))

    def relay(r, carry):
        r0 = pl.multiple_of(r * RELAY_ROWS, RELAY_ROWS)
        tiles = _rows_to_tiles(ys_ref[pl.ds(r0, RELAY_ROWS), :].astype(F32))
        y3[pl.ds(r0 * ROW_TILE[0], RELAY_ROWS * ROW_TILE[0]), :] = tiles.reshape(
            RELAY_ROWS * ROW_TILE[0], ROW_TILE[1])
        return carry

    lax.fori_loop(0, MOE_CAP // RELAY_ROWS, relay, 0)

    def pull(t, carry):
        acc = None
        for k in range(TOP_K):
            wv = jnp.broadcast_to(wb[k, pl.ds(t, 1), :], ROW_TILE)
            term = wv * y3[_tile_rows(ld_ref[0, 0, k * tl + t]), :]
            acc = term if acc is None else acc + term
        o3[t] = acc
        return carry

    lax.fori_loop(0, tl, pull, 0, unroll=8)

    def finish(r, carry):
        r0 = pl.multiple_of(r * RELAY_ROWS, RELAY_ROWS)
        x2 = x1_ref[pl.ds(r0, RELAY_ROWS), :] + ga_ref[0] * _tiles_to_rows(o3[pl.ds(r0, RELAY_ROWS)])
        ms = jnp.mean(x2 * x2, axis=-1, keepdims=True)
        o_ref[pl.ds(r0, RELAY_ROWS), :] = x2 * lax.rsqrt(ms + EPS) * g_ref[...]
        return carry

    lax.fori_loop(0, tl // RELAY_ROWS, finish, 0)


def _combine(ld, wts, ys, x1f, mod3, g_final, tiles_per_batch):
    t, d = x1f.shape
    tl = MOE_TL
    n_tiles = t // tl
    smem = functools.partial(pl.BlockSpec, (1, 1, TOP_K * tl), lambda i: (i, 0, 0),
                             memory_space=pltpu.SMEM)
    return pl.pallas_call(
        _combine_kernel,
        grid=(n_tiles,),
        in_specs=[
            smem(),
            pl.BlockSpec((1, 1, TOP_K * tl), lambda i: (i, 0, 0)),
            pl.BlockSpec((MOE_CAP, d), lambda i: (i, 0)),
            pl.BlockSpec((tl, d), lambda i: (i, 0)),
            pl.BlockSpec((1, 1, d), lambda i: ((i // tiles_per_batch) * N_MOD + 5, 0, 0)),
            pl.BlockSpec((1, d), lambda i: (0, 0)),
        ],
        out_specs=pl.BlockSpec((tl, d), lambda i: (i, 0)),
        out_shape=jax.ShapeDtypeStruct((t, d), F32),
        scratch_shapes=[pltpu.VMEM((MOE_CAP * ROW_TILE[0], ROW_TILE[1]), F32),
                        pltpu.VMEM((tl,) + ROW_TILE, F32),
                        pltpu.VMEM((TOP_K, tl, ROW_TILE[1]), F32)],
        compiler_params=_params(("arbitrary",), COMBINE_VMEM_LIMIT),
        name="moe_combine",
    )(ld, wts, ys, x1f, mod3, g_final)


def kernel(x, c, w_ada, b_ada, g_mix, w_in, conv_dw, conv_dw_bias, conv_ln_g, conv_ln_b,
           w_conv_out, lb_param, hgrn_norm_g, w_hgrn_out, w_out, g_ffn, w_router, b_router,
           w1, b1, w2, b2, g_final):
    b, s, d = x.shape
    assert w_ada.shape[0] == 1, "single-layer block"
    assert s % MOE_TL == 0
    t = b * s
    n_tiles = t // MOE_TL

    c_pad = jnp.zeros((8, d), F32).at[:b].set(c.astype(F32))
    mod, lb = _ada(c_pad, w_ada[0], b_ada, lb_param)
    mod3 = mod[:b].reshape(b * N_MOD, 1, d)

    dw_rows = jnp.broadcast_to(conv_dw[0][:, None, :], (CONV_K, PACK * SUBLANES, d))
    p, pg, uc = _inproj_conv(x, g_mix, mod3, w_in[0].astype(BF16), dw_rows, conv_dw_bias,
                             conv_ln_g, conv_ln_b)
    uh = _hgrn_branch(p, lb, hgrn_norm_g)

    x1, h3, logits_t = _merge(uc, uh, pg, x, mod3, g_ffn, w_conv_out[0].astype(BF16),
                              w_hgrn_out[0].astype(BF16), w_out[0].astype(BF16),
                              w_router[0].T, b_router[0][:, None])

    ld, wts, cnt = _route(logits_t)
    xs = _gather(ld, h3)
    groups_max = t * TOP_K // MOE_GRP + n_tiles * N_EXPERTS
    n_blocks = groups_max // MOE_GPB + N_EXPERTS
    block_e, src = _block_table(cnt[:, :, 0], n_blocks)
    ys = _ffn(block_e, src, xs, w1[0], b1[0][:, None, :], w2[0], b2[0][:, None, :])
    out = _combine(ld, wts, ys, x1.reshape(t, d), mod3, g_final.reshape(1, d), s // MOE_TL)
    return out.reshape(b, s, d)
```

```python
import functools

import jax
import jax.numpy as jnp
import numpy as np
from jax import lax
from jax.experimental import pallas as pl
from jax.experimental.pallas import tpu as pltpu

F32 = jnp.float32
BF16 = jnp.bfloat16

D_MODEL = 1024
CONV_K = 31
HG_HEADS = 8
HG_DK = 128
N_EXPERTS = 32
TOP_K = 4
D_FF = 1024
SWIGLU_ALPHA = 1.702
SWIGLU_LIMIT = 7.0
MOE_BLOCK = 512
EPS = 1e-6
N_MOD = 6
HG_CHUNK = 128
HG_NB = 4
CONV_HALO = 32
VMEM_LIMIT = 56 * 1024 * 1024
COMBINE_VMEM_LIMIT = 62 * 1024 * 1024


def _sigmoid(x):
    return 1.0 / (1.0 + jnp.exp(-x))


def _params(sem, vmem=VMEM_LIMIT):
    return pltpu.CompilerParams(dimension_semantics=sem, vmem_limit_bytes=vmem)


def _ada_kernel(c_ref, w_ref, b_ref, lbp_ref, mod_ref, lb_ref):
    c = c_ref[...]
    c_act = c * _sigmoid(c)
    mod_ref[...] = jnp.dot(c_act, w_ref[...], preferred_element_type=F32,
                           precision=lax.Precision.HIGHEST) + b_ref[...]
    p = lbp_ref[...]
    e = jnp.exp(p - jnp.max(p, axis=0, keepdims=True))
    lb_ref[...] = e[0:1, :] / jnp.sum(e, axis=0, keepdims=True)


def _ada(c_pad, w_ada, b_ada, lb_param):
    nb, d = c_pad.shape
    n = w_ada.shape[1]
    tn = 1536
    return pl.pallas_call(
        _ada_kernel,
        grid=(n // tn,),
        in_specs=[
            pl.BlockSpec((nb, d), lambda j: (0, 0)),
            pl.BlockSpec((d, tn), lambda j: (0, j)),
            pl.BlockSpec((1, tn), lambda j: (0, j)),
            pl.BlockSpec(lb_param.shape, lambda j: (0, 0)),
        ],
        out_specs=[
            pl.BlockSpec((nb, tn), lambda j: (0, j)),
            pl.BlockSpec((1, d), lambda j: (0, 0)),
        ],
        out_shape=[
            jax.ShapeDtypeStruct((nb, n), F32),
            jax.ShapeDtypeStruct((1, d), F32),
        ],
        compiler_params=_params(("arbitrary",)),
        name="ada_mod",
    )(c_pad, w_ada, b_ada, lb_param)


CONV_TS = 256
SUBLANES = 8
PACK = 2
CONV_SPAN = CONV_TS + CONV_HALO
CONV_OFF = CONV_HALO - (CONV_K - 1)
CONV_WROWS = CONV_SPAN // PACK
CONV_RGW = 16
P_Q, P_F, P_I, P_G = range(4)
N_HCOLS = 4
PG_C, PG_H = range(2)


def _inproj_conv_kernel(x_ref, g_ref, sc_ref, sh_ref, w_hbm, dw_ref, bias_ref, lng_ref, lnb_ref,
                        p_ref, pg_ref, uc_ref, buf, pe, po, she, sho, cv, w_ref, wsem, dwp):
    i = pl.program_id(1)
    ts = CONV_TS
    d = D_MODEL

    @pl.when((i == 0) & (pl.program_id(0) == 0))
    def _():
        cp = pltpu.make_async_copy(w_hbm, w_ref, wsem)
        cp.start()
        dwp[...] = dw_ref[...].astype(BF16)
        buf[CONV_SPAN:CONV_SPAN + SUBLANES, :] = jnp.zeros((SUBLANES, d), F32)
        cp.wait()

    @pl.when(i == 0)
    def _():
        buf[0:CONV_HALO, :] = jnp.zeros((CONV_HALO, d), F32)

    @pl.when(i > 0)
    def _():
        buf[0:CONV_HALO, :] = buf[ts:CONV_SPAN, :]

    x = x_ref[0]
    ms = jnp.mean(x * x, axis=-1, keepdims=True)
    h = x * lax.rsqrt(ms + EPS) * g_ref[...]
    h = (h * (1.0 + sc_ref[0]) + sh_ref[0]).astype(BF16)

    ab = jnp.dot(h, w_ref[:, 0:2 * d], preferred_element_type=F32)
    buf[CONV_HALO:CONV_SPAN, :] = ab[:, :d] * _sigmoid(ab[:, d:])
    hcols = (2 + N_HCOLS) * d
    p_ref[0] = jnp.dot(h, w_ref[:, 2 * d:hcols], preferred_element_type=F32)
    pg_ref[0] = jnp.dot(h, w_ref[:, hcols:], preferred_element_type=F32).astype(BF16)

    pe[...] = pltpu.bitcast(buf[0:CONV_SPAN, :].astype(BF16), jnp.uint32)
    po[...] = pltpu.bitcast(buf[1:CONV_SPAN + 1, :].astype(BF16), jnp.uint32)
    for s in range(1, SUBLANES):
        she[s - 1] = pe[s:s + CONV_WROWS - SUBLANES, :]
        sho[s - 1] = po[s:s + CONV_WROWS - SUBLANES, :]
    rgw = CONV_RGW
    for base in range(0, ts // PACK, rgw):
        acc = None
        for j in range(CONV_K):
            o = CONV_OFF + j
            s = (o // PACK) % SUBLANES
            row = base + o // PACK - s
            if o % PACK == 0:
                src = pe if s == 0 else she.at[s - 1]
            else:
                src = po if s == 0 else sho.at[s - 1]
            words = pltpu.bitcast(src[row:row + rgw, :], BF16)
            term = words.reshape(rgw // SUBLANES, PACK * SUBLANES, d) * dwp[j]
            acc = term if acc is None else acc + term
        cv[PACK * base:PACK * (base + rgw), :] = (acc.reshape(PACK * rgw, d).astype(F32)
                                                  + bias_ref[...])
    u = cv[...]
    mu = jnp.mean(u, axis=-1, keepdims=True)
    uc = u - mu
    var = jnp.mean(uc * uc, axis=-1, keepdims=True)
    y = uc * lax.rsqrt(var + EPS) * lng_ref[...] + lnb_ref[...]
    uc_ref[0] = (y * _sigmoid(y)).astype(BF16)


def _inproj_conv(x, g_mix, mod3, w_in_bf, dw_rows, bias, ln_g, ln_b):
    b, s, d = x.shape
    n = w_in_bf.shape[1]
    ts = CONV_TS
    vec = pl.BlockSpec((1, d), lambda bi, i: (0, 0))
    return pl.pallas_call(
        _inproj_conv_kernel,
        grid=(b, s // ts),
        in_specs=[
            pl.BlockSpec((1, ts, d), lambda bi, i: (bi, i, 0)),
            vec,
            pl.BlockSpec((1, 1, d), lambda bi, i: (bi * N_MOD + 1, 0, 0)),
            pl.BlockSpec((1, 1, d), lambda bi, i: (bi * N_MOD + 0, 0, 0)),
            pl.BlockSpec(memory_space=pl.ANY),
            pl.BlockSpec(dw_rows.shape, lambda bi, i: (0, 0, 0)),
            vec, vec, vec,
        ],
        out_specs=[
            pl.BlockSpec((1, ts, N_HCOLS * d), lambda bi, i: (bi, i, 0)),
            pl.BlockSpec((1, ts, 2 * d), lambda bi, i: (bi, i, 0)),
            pl.BlockSpec((1, ts, d), lambda bi, i: (bi, i, 0)),
        ],
        out_shape=[
            jax.ShapeDtypeStruct((b, s, N_HCOLS * d), F32),
            jax.ShapeDtypeStruct((b, s, 2 * d), BF16),
            jax.ShapeDtypeStruct((b, s, d), BF16),
        ],
        scratch_shapes=[pltpu.VMEM((CONV_SPAN + SUBLANES, d), F32),
                        pltpu.VMEM((CONV_WROWS, d), jnp.uint32),
                        pltpu.VMEM((CONV_WROWS, d), jnp.uint32),
                        pltpu.VMEM((SUBLANES - 1, CONV_WROWS - SUBLANES, d), jnp.uint32),
                        pltpu.VMEM((SUBLANES - 1, CONV_WROWS - SUBLANES, d), jnp.uint32),
                        pltpu.VMEM((ts, d), F32),
                        pltpu.VMEM((d, n), BF16), pltpu.SemaphoreType.DMA(()),
                        pltpu.VMEM(dw_rows.shape, BF16)],
        compiler_params=_params(("arbitrary", "arbitrary")),
        name="inproj_conv",
    )(x, g_mix, mod3, mod3, w_in_bf, dw_rows, bias, ln_g, ln_b)


def _hgrn_levels():
    c = HG_CHUNK
    levels = []
    m = c // 2
    while m >= 1:
        levels.append(m)
        m //= 2
    return levels


def _level_exponent(g_inc, logf, m, row):
    c, d = g_inc.shape
    upper = (row & m) != 0
    if m == 1:
        return upper, jnp.where(upper, logf, 0.0)
    if m >= SUBLANES:
        parts = [jnp.broadcast_to(g_inc[b * 2 * m + m - 1:b * 2 * m + m, :], (2 * m, d))
                 for b in range(c // (2 * m))]
        gref = parts[0] if len(parts) == 1 else jnp.concatenate(parts, axis=0)
    else:
        g3 = g_inc.reshape(c // SUBLANES, SUBLANES, d)

        def bcast(j):
            return jnp.broadcast_to(g3[:, j:j + 1, :], g3.shape).reshape(c, d)

        if 2 * m == SUBLANES:
            gref = bcast(m - 1)
        else:
            assert 4 * m == SUBLANES
            gref = jnp.where((row & (SUBLANES - 1)) < 2 * m, bcast(m - 1), bcast(3 * m - 1))
    return upper, jnp.where(upper, g_inc - gref, gref - g_inc)


def _hgrn_kernel(q_ref, z_ref, v_ref, og_ref, lb_ref, ng_ref, tril_ref, o_ref, st):
    @pl.when(pl.program_id(1) == 0)
    def _():
        st[...] = jnp.zeros_like(st)

    for bb in range(HG_NB):
        _hgrn_chunk(bb, q_ref, z_ref, v_ref, og_ref, lb_ref, ng_ref, tril_ref, o_ref, st)


def _hgrn_chunk(bb, q_ref, z_ref, v_ref, og_ref, lb_ref, ng_ref, tril_ref, o_ref, st):
    c = HG_CHUNK
    dk = HG_DK
    levels = _hgrn_levels()

    z = z_ref[bb]
    lb = lb_ref[...]
    sig = _sigmoid(z)
    f = lb + (1.0 - lb) * sig
    logf = jnp.log(f)
    kk = (1.0 - lb) * (1.0 - sig)
    q = q_ref[bb] * (dk ** -0.5)
    v = v_ref[bb]
    og = og_ref[bb]

    hi = logf.astype(BF16)
    lo = (logf - hi.astype(F32)).astype(BF16)
    tril = tril_ref[...]
    g_inc = (jnp.dot(tril, hi, preferred_element_type=F32)
             + jnp.dot(tril, lo, preferred_element_type=F32))
    g_last = g_inc[c - 1:c, :]
    q_st = (q * jnp.exp(g_inc)).astype(BF16)
    k_st = (kk * jnp.exp(g_last - g_inc)).astype(BF16)
    dec_all = jnp.exp(g_last)
    v_bf = v.astype(BF16)
    q_bf = q.astype(BF16)
    k_bf = kk.astype(BF16)

    row = lax.broadcasted_iota(jnp.int32, (c, c), 0)
    col = lax.broadcasted_iota(jnp.int32, (c, c), 1)
    rr = lax.broadcasted_iota(jnp.int32, (c, 1), 0)

    qs, ks, masks = [], [], []
    for m in levels:
        upper, ex = _level_exponent(g_inc, logf, m, rr)
        e = jnp.exp(ex)
        qs.append(jnp.where(upper, q * e, 0.0).astype(BF16))
        ks.append(jnp.where(upper, 0.0, kk * e).astype(BF16))
        sh = int(np.log2(2 * m))
        masks.append((row >> sh) == (col >> sh))
    diag = row == col

    nt = (((1,), (1,)), ((), ()))
    tn = (((0,), (0,)), ((), ()))
    for h in range(HG_HEADS):
        sl = slice(h * dk, (h + 1) * dk)
        a = jnp.where(diag, lax.dot_general(q_bf[:, sl], k_bf[:, sl], nt,
                                            preferred_element_type=F32), 0.0)
        for li in range(len(levels)):
            a = a + jnp.where(masks[li],
                              lax.dot_general(qs[li][:, sl], ks[li][:, sl], nt,
                                              preferred_element_type=F32), 0.0)
        s_t = st[bb, h]
        o = jnp.dot(a.astype(BF16), v_bf[:, sl], preferred_element_type=F32)
        o = o + lax.dot_general(q_st[:, sl], s_t.astype(BF16), nt, preferred_element_type=F32)
        st[bb, h] = s_t * dec_all[:, sl] + lax.dot_general(v_bf[:, sl], k_st[:, sl], tn,
                                                           preferred_element_type=F32)
        ms = jnp.mean(o * o, axis=-1, keepdims=True)
        o = o * lax.rsqrt(ms + EPS) * ng_ref[...]
        g = og[:, sl]
        o_ref[bb, :, sl] = (o * (g * _sigmoid(g))).astype(BF16)


def _hgrn_branch(p, lb, norm_g):
    b, s, _ = p.shape
    d = D_MODEL
    c = HG_CHUNK
    tril = jnp.asarray(np.tril(np.ones((c, c), np.float32)), dtype=BF16)

    assert b % HG_NB == 0

    def col_spec(col):
        return pl.BlockSpec((HG_NB, c, d), lambda bi, i: (bi, i, col))

    return pl.pallas_call(
        _hgrn_kernel,
        grid=(b // HG_NB, s // c),
        in_specs=[
            col_spec(P_Q), col_spec(P_F), col_spec(P_I), col_spec(P_G),
            pl.BlockSpec((1, d), lambda bi, i: (0, 0)),
            pl.BlockSpec((1, HG_DK), lambda bi, i: (0, 0)),
            pl.BlockSpec((c, c), lambda bi, i: (0, 0)),
        ],
        out_specs=pl.BlockSpec((HG_NB, c, d), lambda bi, i: (bi, i, 0)),
        out_shape=jax.ShapeDtypeStruct((b, s, d), BF16),
        scratch_shapes=[pltpu.VMEM((HG_NB, HG_HEADS, HG_DK, HG_DK), F32)],
        compiler_params=_params(("arbitrary", "arbitrary")),
        name="hgrn_branch",
    )(p, p, p, p, lb, norm_g, tril)


ROW_TILE = (8, 128)


def _rows_to_tiles(rows):
    st = jnp.stack([rows[:, j * 128:(j + 1) * 128] for j in range(ROW_TILE[0])], axis=0)
    return pltpu.einshape("jrl->rjl", st)


def _tile_rows(first_row):
    return pl.ds(pl.multiple_of(first_row, ROW_TILE[0]), ROW_TILE[0])


def _tiles_to_rows(tiles):
    y = pltpu.einshape("rjl->jrl", tiles)
    return jnp.concatenate([y[j] for j in range(ROW_TILE[0])], axis=-1)


_NT = (((1,), (1,)), ((), ()))


def _merge_kernel(uc_ref, uh_ref, gc_ref, gh_ref, x_ref, ga_ref, sc_ref, sh_ref, g_ref,
                  wc_ref, wh_ref, w_ref, wrt_ref, br_ref, x1_ref, h3_ref, lg_ref):
    yc = jnp.dot(uc_ref[0], wc_ref[...], preferred_element_type=F32)
    yh = jnp.dot(uh_ref[0], wh_ref[...], preferred_element_type=F32)
    gate_c = _sigmoid(gc_ref[0].astype(F32))
    gate_h = _sigmoid(gh_ref[0].astype(F32))
    merged = (gate_c * yc + gate_h * yh).astype(BF16)
    x1 = x_ref[0] + ga_ref[0] * jnp.dot(merged, w_ref[...], preferred_element_type=F32)
    x1_ref[0] = x1
    ms = jnp.mean(x1 * x1, axis=-1, keepdims=True)
    h2 = x1 * lax.rsqrt(ms + EPS) * g_ref[...]
    h2 = h2 * (1.0 + sc_ref[0]) + sh_ref[0]
    h3_ref[...] = _rows_to_tiles(h2)
    wrt = wrt_ref[...]
    wrt_hi = wrt.astype(BF16)
    wrt_lo = (wrt - wrt_hi.astype(F32)).astype(BF16)
    h2_hi = h2.astype(BF16)
    h2_lo = (h2 - h2_hi.astype(F32)).astype(BF16)
    lg = (lax.dot_general(wrt_hi, h2_hi, _NT, preferred_element_type=F32)
          + lax.dot_general(wrt_hi, h2_lo, _NT, preferred_element_type=F32)
          + lax.dot_general(wrt_lo, h2_hi, _NT, preferred_element_type=F32))
    lg_ref[...] = lg + br_ref[...]


def _merge(uc, uh, pg, x, mod3, g_ffn, w_conv_bf, w_hgrn_bf, w_out_bf, w_router_t, b_router_col):
    b, s, d = x.shape
    tm = 512
    nt = s // tm
    ne = w_router_t.shape[0]

    def mod_spec(k):
        return pl.BlockSpec((1, 1, d), lambda bi, i: (bi * N_MOD + k, 0, 0))

    def col_spec(col):
        return pl.BlockSpec((1, tm, d), lambda bi, i: (bi, i, col))

    tile = pl.BlockSpec((1, tm, d), lambda bi, i: (bi, i, 0))
    weight = pl.BlockSpec((d, d), lambda bi, i: (0, 0))
    return pl.pallas_call(
        _merge_kernel,
        grid=(b, nt),
        in_specs=[
            tile, tile, col_spec(PG_C), col_spec(PG_H), tile,
            mod_spec(2), mod_spec(4), mod_spec(3),
            pl.BlockSpec((1, d), lambda bi, i: (0, 0)),
            weight, weight, weight,
            pl.BlockSpec((ne, d), lambda bi, i: (0, 0)),
            pl.BlockSpec((ne, 1), lambda bi, i: (0, 0)),
        ],
        out_specs=[
            tile,
            pl.BlockSpec((tm,) + ROW_TILE, lambda bi, i: (bi * nt + i, 0, 0)),
            pl.BlockSpec((ne, tm), lambda bi, i: (0, bi * nt + i)),
        ],
        out_shape=[
            jax.ShapeDtypeStruct((b, s, d), F32),
            jax.ShapeDtypeStruct((b * s,) + ROW_TILE, F32),
            jax.ShapeDtypeStruct((ne, b * s), F32),
        ],
        compiler_params=_params(("arbitrary", "arbitrary")),
        name="merge_router",
    )(uc, uh, pg, pg, x, mod3, mod3, mod3, g_ffn, w_conv_bf, w_hgrn_bf, w_out_bf, w_router_t,
      b_router_col)


MOE_TL = 1024
MOE_GRP = 16
MOE_CAP = MOE_TL * TOP_K + N_EXPERTS * MOE_GRP
MOE_GPB = MOE_BLOCK // MOE_GRP
FFN_PARTS = 2
RELAY_ROWS = 256
assert MOE_CAP % RELAY_ROWS == 0 and MOE_TL % RELAY_ROWS == 0


def _route_kernel(lg_ref, u_ref, ld_ref, w_ref, cnt_ref):
    tl = MOE_TL
    l = lg_ref[...]
    eio = lax.broadcasted_iota(jnp.int32, l.shape, 0)
    vals, hots = [], []
    for _ in range(TOP_K):
        m = jnp.max(l, axis=0, keepdims=True)
        idx = jnp.min(jnp.where(l == m, eio, N_EXPERTS), axis=0, keepdims=True)
        hot = eio == idx
        vals.append(m)
        hots.append(hot)
        l = jnp.where(hot, -jnp.inf, l)
    ex = [jnp.exp(v - vals[0]) for v in vals]
    den = ex[0] + ex[1] + ex[2] + ex[3]
    cnt = hots[0].astype(F32)
    for k in range(1, TOP_K):
        cnt = cnt + hots[k].astype(F32)
    prefix = jnp.dot(cnt.astype(BF16), u_ref[...], preferred_element_type=F32)
    n_e = jnp.sum(cnt, axis=1, keepdims=True)
    pad_e = jnp.floor((n_e + (MOE_GRP - 1)) * (1.0 / MOE_GRP)) * MOE_GRP
    scan = jnp.broadcast_to(pad_e, (N_EXPERTS, ROW_TILE[1]))
    ei = lax.broadcasted_iota(jnp.int32, scan.shape, 0)
    dist = 1
    while dist < N_EXPERTS:
        scan = scan + jnp.where(ei >= dist, pltpu.roll(scan, dist, axis=0), 0.0)
        dist *= 2
    base = prefix + (scan[:, 0:1] - pad_e)
    for k in range(TOP_K):
        dest = jnp.sum(jnp.where(hots[k], base, 0.0), axis=0, keepdims=True)
        ld_ref[0, :, k * tl:(k + 1) * tl] = (dest * float(ROW_TILE[0])).astype(jnp.int32)
        w_ref[0, :, k * tl:(k + 1) * tl] = ex[k] / den
    cnt_ref[0] = jnp.broadcast_to(n_e, (N_EXPERTS, ROW_TILE[1])).astype(jnp.int32)


def _route(logits_t):
    ne, t = logits_t.shape
    tl = MOE_TL
    n_tiles = t // tl
    upper = jnp.asarray(np.triu(np.ones((tl, tl), np.float32), k=1), dtype=BF16)
    slot = pl.BlockSpec((1, 1, TOP_K * tl), lambda i: (i, 0, 0))
    return pl.pallas_call(
        _route_kernel,
        grid=(n_tiles,),
        in_specs=[pl.BlockSpec((ne, tl), lambda i: (0, i)),
                  pl.BlockSpec((tl, tl), lambda i: (0, 0))],
        out_specs=[slot, slot, pl.BlockSpec((1, ne, ROW_TILE[1]), lambda i: (i, 0, 0))],
        out_shape=[
            jax.ShapeDtypeStruct((n_tiles, 1, TOP_K * tl), jnp.int32),
            jax.ShapeDtypeStruct((n_tiles, 1, TOP_K * tl), F32),
            jax.ShapeDtypeStruct((n_tiles, ne, ROW_TILE[1]), jnp.int32),
        ],
        compiler_params=_params(("arbitrary",)),
        name="route",
    )(logits_t, upper)


def _gather_kernel(ld_ref, h3_ref, xs_ref, xs3):
    tl = MOE_TL
    last = pl.num_programs(0) - 1

    @pl.when((pl.program_id(0) == 0) | (pl.program_id(0) == last))
    def _():
        xs3[...] = jnp.zeros_like(xs3)

    def push(t, carry):
        tile = h3_ref[t]
        for k in range(TOP_K):
            xs3[_tile_rows(ld_ref[0, 0, k * tl + t]), :] = tile
        return carry

    @pl.when(pl.program_id(0) < last)
    def _():
        lax.fori_loop(0, tl, push, 0, unroll=4)

    def relay(r, carry):
        r0 = pl.multiple_of(r * RELAY_ROWS, RELAY_ROWS)
        tiles = xs3[pl.ds(r0 * ROW_TILE[0], RELAY_ROWS * ROW_TILE[0]), :]
        tiles = tiles.reshape((RELAY_ROWS,) + ROW_TILE)
        xs_ref[pl.ds(r0, RELAY_ROWS), :] = _tiles_to_rows(tiles).astype(BF16)
        return carry

    lax.fori_loop(0, MOE_CAP // RELAY_ROWS, relay, 0)


def _gather(ld, h3):
    t = h3.shape[0]
    tl = MOE_TL
    n_tiles = t // tl
    last = n_tiles - 1
    return pl.pallas_call(
        _gather_kernel,
        grid=(n_tiles + 1,),
        in_specs=[
            pl.BlockSpec((1, 1, TOP_K * tl), lambda i: (jnp.minimum(i, last), 0, 0),
                         memory_space=pltpu.SMEM),
            pl.BlockSpec((tl,) + ROW_TILE, lambda i: (jnp.minimum(i, last), 0, 0)),
        ],
        out_specs=pl.BlockSpec((MOE_CAP, D_MODEL), lambda i: (i, 0)),
        out_shape=jax.ShapeDtypeStruct(((n_tiles + 1) * MOE_CAP, D_MODEL), BF16),
        scratch_shapes=[pltpu.VMEM((MOE_CAP * ROW_TILE[0], ROW_TILE[1]), F32)],
        compiler_params=_params(("arbitrary",)),
        name="moe_gather",
    )(ld, h3)


def _table_kernel(cnt_ref, be_ref, src_ref, toff):
    n_tiles = cnt_ref.shape[0]
    n_slots = src_ref.shape[0]
    groups_per_tile = MOE_CAP // MOE_GRP
    step = 8

    for i in range(n_tiles):
        toff[i] = 0

    def per_expert(e, pos):
        def per_tile(i, p):
            g = (cnt_ref[i, e] + (MOE_GRP - 1)) // MOE_GRP
            base = i * groups_per_tile + toff[i]

            def put(j, c):
                for u in range(step):
                    src_ref[p + j * step + u] = base + j * step + u
                return c

            lax.fori_loop(0, (g + (step - 1)) // step, put, 0)
            toff[i] = toff[i] + g
            return p + g

        end = lax.fori_loop(0, n_tiles, per_tile, pos)
        new_pos = (end + (MOE_GPB - 1)) // MOE_GPB * MOE_GPB
        for u in range(MOE_GPB):
            src_ref[end + u] = -1

        def put_e(b, c):
            be_ref[b] = e
            return c

        lax.fori_loop(pos // MOE_GPB, new_pos // MOE_GPB, put_e, 0)
        return new_pos

    used = lax.fori_loop(0, N_EXPERTS, per_expert, 0)

    def tail(b, c):
        be_ref[b] = N_EXPERTS - 1
        for u in range(MOE_GPB):
            src_ref[b * MOE_GPB + u] = -1
        return c

    lax.fori_loop(used // MOE_GPB, n_slots // MOE_GPB, tail, 0)


def _block_table(cnt, n_blocks):
    smem = pl.BlockSpec(memory_space=pltpu.SMEM)
    return pl.pallas_call(
        _table_kernel,
        in_specs=[smem],
        out_specs=[smem, smem],
        out_shape=[jax.ShapeDtypeStruct((n_blocks,), jnp.int32),
                   jax.ShapeDtypeStruct((n_blocks * MOE_GPB,), jnp.int32)],
        scratch_shapes=[pltpu.SMEM((cnt.shape[0],), jnp.int32)],
        name="moe_block_table",
    )(cnt)


def _ffn_kernel(be_ref, src_ref, xs_hbm, w1_ref, b1_ref, w2_ref, b2_ref, ys_hbm,
                xbuf, ybuf, w1s, w2s, sem_in, sem_out, *, spare_grp):
    i = pl.program_id(0)
    nb = pl.num_programs(0)
    slot = i % 2

    def live(blk):
        return src_ref[blk * MOE_GPB] >= 0

    def group_rows(grp):
        return pl.ds(pl.multiple_of(grp * MOE_GRP, MOE_GRP), MOE_GRP)

    def start_in(blk, sl):
        for g in range(MOE_GPB):
            grp = src_ref[blk * MOE_GPB + g]
            grp = jnp.where(grp >= 0, grp, spare_grp)
            pltpu.make_async_copy(xs_hbm.at[group_rows(grp), :],
                                  xbuf.at[sl, pl.ds(g * MOE_GRP, MOE_GRP), :], sem_in.at[sl]).start()

    def start_out(blk, sl):
        for g in range(MOE_GPB):
            grp = src_ref[blk * MOE_GPB + g]
            grp = jnp.where(grp >= 0, grp, spare_grp + 1 + sl * MOE_GPB + g)
            pltpu.make_async_copy(ybuf.at[sl, pl.ds(g * MOE_GRP, MOE_GRP), :],
                                  ys_hbm.at[group_rows(grp), :], sem_out.at[sl]).start()

    def wait_in(sl):
        pltpu.make_async_copy(xs_hbm.at[pl.ds(0, MOE_BLOCK), :], xbuf.at[sl], sem_in.at[sl]).wait()

    def wait_out(sl):
        pltpu.make_async_copy(ybuf.at[sl], ys_hbm.at[pl.ds(0, MOE_BLOCK), :], sem_out.at[sl]).wait()

    @pl.when(i == 0)
    def _():
        ybuf[...] = jnp.zeros_like(ybuf)

    @pl.when((i == 0) & live(0))
    def _():
        start_in(0, 0)

    nxt = jnp.minimum(i + 1, nb - 1)

    @pl.when((i + 1 < nb) & live(nxt))
    def _():
        start_in(nxt, 1 - slot)

    @pl.when((i >= 2) & live(jnp.maximum(i - 2, 0)))
    def _():
        wait_out(slot)

    prev = be_ref[jnp.maximum(i - 1, 0)]

    @pl.when((i == 0) | (be_ref[i] != prev))
    def _():
        w1s[...] = w1_ref[0].astype(BF16)
        w2s[...] = w2_ref[0].astype(BF16)

    @pl.when(live(i))
    def _():
        wait_in(slot)
        part = MOE_BLOCK // FFN_PARTS
        for r in range(FFN_PARTS):
            @pl.when(src_ref[i * MOE_GPB + r * (MOE_GPB // FFN_PARTS)] >= 0)
            def _(r=r):
                rs = pl.ds(r * part, part)
                u = jnp.dot(xbuf[slot, rs, :], w1s[...], preferred_element_type=F32) + b1_ref[0]
                gl = jnp.minimum(u[:, :D_FF], SWIGLU_LIMIT)
                lin = jnp.clip(u[:, D_FF:], -SWIGLU_LIMIT, SWIGLU_LIMIT)
                act = gl * _sigmoid(SWIGLU_ALPHA * gl) * (lin + 1.0)
                y = jnp.dot(act.astype(BF16), w2s[...], preferred_element_type=F32) + b2_ref[0]
                ybuf[slot, rs, :] = y.astype(BF16)
        start_out(i, slot)

    @pl.when(i == nb - 1)
    def _():
        @pl.when(live(i))
        def _():
            wait_out(slot)

        @pl.when((i >= 1) & live(jnp.maximum(i - 1, 0)))
        def _():
            wait_out(1 - slot)


def _ffn(block_e, src, xs, w1, b1, w2, b2):
    r, d = xs.shape
    nb = block_e.shape[0]
    spare_grp = (r - MOE_CAP) // MOE_GRP
    assert 1 + 2 * MOE_GPB <= MOE_CAP // MOE_GRP
    grid_spec = pltpu.PrefetchScalarGridSpec(
        num_scalar_prefetch=2,
        grid=(nb,),
        in_specs=[
            pl.BlockSpec(memory_space=pl.ANY),
            pl.BlockSpec((1, d, 2 * D_FF), lambda i, be, sr: (be[i], 0, 0)),
            pl.BlockSpec((1, 1, 2 * D_FF), lambda i, be, sr: (be[i], 0, 0)),
            pl.BlockSpec((1, D_FF, d), lambda i, be, sr: (be[i], 0, 0)),
            pl.BlockSpec((1, 1, d), lambda i, be, sr: (be[i], 0, 0)),
        ],
        out_specs=pl.BlockSpec(memory_space=pl.ANY),
        scratch_shapes=[
            pltpu.VMEM((2, MOE_BLOCK, d), BF16), pltpu.VMEM((2, MOE_BLOCK, d), BF16),
            pltpu.VMEM((d, 2 * D_FF), BF16), pltpu.VMEM((D_FF, d), BF16),
            pltpu.SemaphoreType.DMA((2,)), pltpu.SemaphoreType.DMA((2,)),
        ],
    )
    return pl.pallas_call(
        functools.partial(_ffn_kernel, spare_grp=spare_grp),
        grid_spec=grid_spec,
        out_shape=jax.ShapeDtypeStruct((r, d), BF16),
        input_output_aliases={2: 0},
        compiler_params=_params(("arbitrary",)),
        name="expert_ffn",
    )(block_e, src, xs, w1, b1, w2, b2)


def _combine_kernel(ld_ref, w_ref, ys_ref, x1_ref, ga_ref, g_ref, o_ref, y3, o3, wb):
    tl = MOE_TL
    lanes = ROW_TILE[1]

    w4 = jnp.concatenate([w_ref[0, :, k * tl:(k + 1) * tl] for k in range(TOP_K)], axis=0)
    wt = jnp.concatenate([w4, jnp.zeros((lanes - TOP_K, tl), F32)], axis=0).T
    for k in range(TOP_K):
        wb[k] = jnp.broadcast_to(wt[:, k:k + 1], (tl, lanes))

    def relay(r, carry):
        r0 = pl.multiple_of(r * RELAY_ROWS, RELAY_ROWS)
        tiles = _rows_to_tiles(ys_ref[pl.ds(r0, RELAY_ROWS), :].astype(F32))
        y3[pl.ds(r0 * ROW_TILE[0], RELAY_ROWS * ROW_TILE[0]), :] = tiles.reshape(
            RELAY_ROWS * ROW_TILE[0], ROW_TILE[1])
        return carry

    lax.fori_loop(0, MOE_CAP // RELAY_ROWS, relay, 0)

    def pull(t, carry):
        acc = None
        for k in range(TOP_K):
            wv = jnp.broadcast_to(wb[k, pl.ds(t, 1), :], ROW_TILE)
            term = wv * y3[_tile_rows(ld_ref[0, 0, k * tl + t]), :]
            acc = term if acc is None else acc + term
        o3[t] = acc
        return carry

    lax.fori_loop(0, tl, pull, 0, unroll=8)

    def finish(r, carry):
        r0 = pl.multiple_of(r * RELAY_ROWS, RELAY_ROWS)
        x2 = x1_ref[pl.ds(r0, RELAY_ROWS), :] + ga_ref[0] * _tiles_to_rows(o3[pl.ds(r0, RELAY_ROWS)])
        ms = jnp.mean(x2 * x2, axis=-1, keepdims=True)
        o_ref[pl.ds(r0, RELAY_ROWS), :] = x2 * lax.rsqrt(ms + EPS) * g_ref[...]
        return carry

    lax.fori_loop(0, tl // RELAY_ROWS, finish, 0)


def _combine(ld, wts, ys, x1f, mod3, g_final, tiles_per_batch):
    t, d = x1f.shape
    tl = MOE_TL
    n_tiles = t // tl
    smem = functools.partial(pl.BlockSpec, (1, 1, TOP_K * tl), lambda i: (i, 0, 0),
                             memory_space=pltpu.SMEM)
    return pl.pallas_call(
        _combine_kernel,
        grid=(n_tiles,),
        in_specs=[
            smem(),
            pl.BlockSpec((1, 1, TOP_K * tl), lambda i: (i, 0, 0)),
            pl.BlockSpec((MOE_CAP, d), lambda i: (i, 0)),
            pl.BlockSpec((tl, d), lambda i: (i, 0)),
            pl.BlockSpec((1, 1, d), lambda i: ((i // tiles_per_batch) * N_MOD + 5, 0, 0)),
            pl.BlockSpec((1, d), lambda i: (0, 0)),
        ],
        out_specs=pl.BlockSpec((tl, d), lambda i: (i, 0)),
        out_shape=jax.ShapeDtypeStruct((t, d), F32),
        scratch_shapes=[pltpu.VMEM((MOE_CAP * ROW_TILE[0], ROW_TILE[1]), F32),
                        pltpu.VMEM((tl,) + ROW_TILE, F32),
                        pltpu.VMEM((TOP_K, tl, ROW_TILE[1]), F32)],
        compiler_params=_params(("arbitrary",), COMBINE_VMEM_LIMIT),
        name="moe_combine",
    )(ld, wts, ys, x1f, mod3, g_final)


def kernel(x, c, w_ada, b_ada, g_mix, w_in, conv_dw, conv_dw_bias, conv_ln_g, conv_ln_b,
           w_conv_out, lb_param, hgrn_norm_g, w_hgrn_out, w_out, g_ffn, w_router, b_router,
           w1, b1, w2, b2, g_final):
    b, s, d = x.shape
    assert w_ada.shape[0] == 1, "single-layer block"
    assert s % MOE_TL == 0
    t = b * s
    n_tiles = t // MOE_TL

    c_pad = jnp.zeros((8, d), F32).at[:b].set(c.astype(F32))
    mod, lb = _ada(c_pad, w_ada[0], b_ada, lb_param)
    mod3 = mod[:b].reshape(b * N_MOD, 1, d)

    dw_rows = jnp.broadcast_to(conv_dw[0][:, None, :], (CONV_K, PACK * SUBLANES, d))
    p, pg, uc = _inproj_conv(x, g_mix, mod3, w_in[0].astype(BF16), dw_rows, conv_dw_bias,
                             conv_ln_g, conv_ln_b)
    uh = _hgrn_branch(p, lb, hgrn_norm_g)

    x1, h3, logits_t = _merge(uc, uh, pg, x, mod3, g_ffn, w_conv_out[0].astype(BF16),
                              w_hgrn_out[0].astype(BF16), w_out[0].astype(BF16),
                              w_router[0].T, b_router[0][:, None])

    ld, wts, cnt = _route(logits_t)
    xs = _gather(ld, h3)
    groups_max = t * TOP_K // MOE_GRP + n_tiles * N_EXPERTS
    n_blocks = groups_max // MOE_GPB + N_EXPERTS
    block_e, src = _block_table(cnt[:, :, 0], n_blocks)
    ys = _ffn(block_e, src, xs, w1[0], b1[0][:, None, :], w2[0], b2[0][:, None, :])
    out = _combine(ld, wts, ys, x1.reshape(t, d), mod3, g_final.reshape(1, d), s // MOE_TL)
    return out.reshape(b, s, d)
```

```python
import functools

import jax
import jax.numpy as jnp
import numpy as np
from jax import lax
from jax.experimental import pallas as pl
from jax.experimental.pallas import tpu as pltpu

F32 = jnp.float32
BF16 = jnp.bfloat16

D_MODEL = 1024
CONV_K = 31
HG_HEADS = 8
HG_DK = 128
N_EXPERTS = 32
TOP_K = 4
D_FF = 1024
SWIGLU_ALPHA = 1.702
SWIGLU_LIMIT = 7.0
MOE_BLOCK = 512
EPS = 1e-6
N_MOD = 6
HG_CHUNK = 128
HG_NB = 4
CONV_HALO = 32
VMEM_LIMIT = 56 * 1024 * 1024
COMBINE_VMEM_LIMIT = 62 * 1024 * 1024


def _sigmoid(x):
    return 1.0 / (1.0 + jnp.exp(-x))


def _params(sem, vmem=VMEM_LIMIT):
    return pltpu.CompilerParams(dimension_semantics=sem, vmem_limit_bytes=vmem)


def _ada_kernel(c_ref, w_ref, b_ref, lbp_ref, mod_ref, lb_ref):
    c = c_ref[...]
    c_act = c * _sigmoid(c)
    mod_ref[...] = jnp.dot(c_act, w_ref[...], preferred_element_type=F32,
                           precision=lax.Precision.HIGHEST) + b_ref[...]
    p = lbp_ref[...]
    e = jnp.exp(p - jnp.max(p, axis=0, keepdims=True))
    lb_ref[...] = e[0:1, :] / jnp.sum(e, axis=0, keepdims=True)


def _ada(c_pad, w_ada, b_ada, lb_param):
    nb, d = c_pad.shape
    n = w_ada.shape[1]
    tn = 1536
    return pl.pallas_call(
        _ada_kernel,
        grid=(n // tn,),
        in_specs=[
            pl.BlockSpec((nb, d), lambda j: (0, 0)),
            pl.BlockSpec((d, tn), lambda j: (0, j)),
            pl.BlockSpec((1, tn), lambda j: (0, j)),
            pl.BlockSpec(lb_param.shape, lambda j: (0, 0)),
        ],
        out_specs=[
            pl.BlockSpec((nb, tn), lambda j: (0, j)),
            pl.BlockSpec((1, d), lambda j: (0, 0)),
        ],
        out_shape=[
            jax.ShapeDtypeStruct((nb, n), F32),
            jax.ShapeDtypeStruct((1, d), F32),
        ],
        compiler_params=_params(("arbitrary",)),
        name="ada_mod",
    )(c_pad, w_ada, b_ada, lb_param)


CONV_TS = 256
SUBLANES = 8
PACK = 2
CONV_SPAN = CONV_TS + CONV_HALO
CONV_OFF = CONV_HALO - (CONV_K - 1)
CONV_WROWS = CONV_SPAN // PACK
CONV_RGW = 16
P_Q, P_F, P_I, P_G = range(4)
N_HCOLS = 4
PG_C, PG_H = range(2)


def _inproj_conv_kernel(x_ref, g_ref, sc_ref, sh_ref, w_hbm, dw_ref, bias_ref, lng_ref, lnb_ref,
                        p_ref, pg_ref, uc_ref, buf, pe, po, she, sho, cv, w_ref, wsem, dwp):
    i = pl.program_id(1)
    ts = CONV_TS
    d = D_MODEL

    @pl.when((i == 0) & (pl.program_id(0) == 0))
    def _():
        cp = pltpu.make_async_copy(w_hbm, w_ref, wsem)
        cp.start()
        dwp[...] = dw_ref[...].astype(BF16)
        buf[CONV_SPAN:CONV_SPAN + SUBLANES, :] = jnp.zeros((SUBLANES, d), F32)
        cp.wait()

    @pl.when(i == 0)
    def _():
        buf[0:CONV_HALO, :] = jnp.zeros((CONV_HALO, d), F32)

    @pl.when(i > 0)
    def _():
        buf[0:CONV_HALO, :] = buf[ts:CONV_SPAN, :]

    x = x_ref[0]
    ms = jnp.mean(x * x, axis=-1, keepdims=True)
    h = x * lax.rsqrt(ms + EPS) * g_ref[...]
    h = (h * (1.0 + sc_ref[0]) + sh_ref[0]).astype(BF16)

    ab = jnp.dot(h, w_ref[:, 0:2 * d], preferred_element_type=F32)
    buf[CONV_HALO:CONV_SPAN, :] = ab[:, :d] * _sigmoid(ab[:, d:])
    hcols = (2 + N_HCOLS) * d
    p_ref[0] = jnp.dot(h, w_ref[:, 2 * d:hcols], preferred_element_type=F32)
    pg_ref[0] = jnp.dot(h, w_ref[:, hcols:], preferred_element_type=F32).astype(BF16)

    pe[...] = pltpu.bitcast(buf[0:CONV_SPAN, :].astype(BF16), jnp.uint32)
    po[...] = pltpu.bitcast(buf[1:CONV_SPAN + 1, :].astype(BF16), jnp.uint32)
    for s in range(1, SUBLANES):
        she[s - 1] = pe[s:s + CONV_WROWS - SUBLANES, :]
        sho[s - 1] = po[s:s + CONV_WROWS - SUBLANES, :]
    rgw = CONV_RGW
    for base in range(0, ts // PACK, rgw):
        acc = None
        for j in range(CONV_K):
            o = CONV_OFF + j
            s = (o // PACK) % SUBLANES
            row = base + o // PACK - s
            if o % PACK == 0:
                src = pe if s == 0 else she.at[s - 1]
            else:
                src = po if s == 0 else sho.at[s - 1]
            words = pltpu.bitcast(src[row:row + rgw, :], BF16)
            term = words.reshape(rgw // SUBLANES, PACK * SUBLANES, d) * dwp[j]
            acc = term if acc is None else acc + term
        cv[PACK * base:PACK * (base + rgw), :] = (acc.reshape(PACK * rgw, d).astype(F32)
                                                  + bias_ref[...])
    u = cv[...]
    mu = jnp.mean(u, axis=-1, keepdims=True)
    uc = u - mu
    var = jnp.mean(uc * uc, axis=-1, keepdims=True)
    y = uc * lax.rsqrt(var + EPS) * lng_ref[...] + lnb_ref[...]
    uc_ref[0] = (y * _sigmoid(y)).astype(BF16)


def _inproj_conv(x, g_mix, mod3, w_in_bf, dw_rows, bias, ln_g, ln_b):
    b, s, d = x.shape
    n = w_in_bf.shape[1]
    ts = CONV_TS
    vec = pl.BlockSpec((1, d), lambda bi, i: (0, 0))
    return pl.pallas_call(
        _inproj_conv_kernel,
        grid=(b, s // ts),
        in_specs=[
            pl.BlockSpec((1, ts, d), lambda bi, i: (bi, i, 0)),
            vec,
            pl.BlockSpec((1, 1, d), lambda bi, i: (bi * N_MOD + 1, 0, 0)),
            pl.BlockSpec((1, 1, d), lambda bi, i: (bi * N_MOD + 0, 0, 0)),
            pl.BlockSpec(memory_space=pl.ANY),
            pl.BlockSpec(dw_rows.shape, lambda bi, i: (0, 0, 0)),
            vec, vec, vec,
        ],
        out_specs=[
            pl.BlockSpec((1, ts, N_HCOLS * d), lambda bi, i: (bi, i, 0)),
            pl.BlockSpec((1, ts, 2 * d), lambda bi, i: (bi, i, 0)),
            pl.BlockSpec((1, ts, d), lambda bi, i: (bi, i, 0)),
        ],
        out_shape=[
            jax.ShapeDtypeStruct((b, s, N_HCOLS * d), F32),
            jax.ShapeDtypeStruct((b, s, 2 * d), BF16),
            jax.ShapeDtypeStruct((b, s, d), BF16),
        ],
        scratch_shapes=[pltpu.VMEM((CONV_SPAN + SUBLANES, d), F32),
                        pltpu.VMEM((CONV_WROWS, d), jnp.uint32),
                        pltpu.VMEM((CONV_WROWS, d), jnp.uint32),
                        pltpu.VMEM((SUBLANES - 1, CONV_WROWS - SUBLANES, d), jnp.uint32),
                        pltpu.VMEM((SUBLANES - 1, CONV_WROWS - SUBLANES, d), jnp.uint32),
                        pltpu.VMEM((ts, d), F32),
                        pltpu.VMEM((d, n), BF16), pltpu.SemaphoreType.DMA(()),
                        pltpu.VMEM(dw_rows.shape, BF16)],
        compiler_params=_params(("arbitrary", "arbitrary")),
        name="inproj_conv",
    )(x, g_mix, mod3, mod3, w_in_bf, dw_rows, bias, ln_g, ln_b)


def _hgrn_levels():
    c = HG_CHUNK
    levels = []
    m = c // 2
    while m >= 1:
        levels.append(m)
        m //= 2
    return levels


def _level_exponent(g_inc, logf, m, row):
    c, d = g_inc.shape
    upper = (row & m) != 0
    if m == 1:
        return upper, jnp.where(upper, logf, 0.0)
    if m >= SUBLANES:
        parts = [jnp.broadcast_to(g_inc[b * 2 * m + m - 1:b * 2 * m + m, :], (2 * m, d))
                 for b in range(c // (2 * m))]
        gref = parts[0] if len(parts) == 1 else jnp.concatenate(parts, axis=0)
    else:
        g3 = g_inc.reshape(c // SUBLANES, SUBLANES, d)

        def bcast(j):
            return jnp.broadcast_to(g3[:, j:j + 1, :], g3.shape).reshape(c, d)

        if 2 * m == SUBLANES:
            gref = bcast(m - 1)
        else:
            assert 4 * m == SUBLANES
            gref = jnp.where((row & (SUBLANES - 1)) < 2 * m, bcast(m - 1), bcast(3 * m - 1))
    return upper, jnp.where(upper, g_inc - gref, gref - g_inc)


def _hgrn_kernel(q_ref, z_ref, v_ref, og_ref, lb_ref, ng_ref, tril_ref, o_ref, st):
    @pl.when(pl.program_id(1) == 0)
    def _():
        st[...] = jnp.zeros_like(st)

    for bb in range(HG_NB):
        _hgrn_chunk(bb, q_ref, z_ref, v_ref, og_ref, lb_ref, ng_ref, tril_ref, o_ref, st)


def _hgrn_chunk(bb, q_ref, z_ref, v_ref, og_ref, lb_ref, ng_ref, tril_ref, o_ref, st):
    c = HG_CHUNK
    dk = HG_DK
    levels = _hgrn_levels()

    z = z_ref[bb]
    lb = lb_ref[...]
    sig = _sigmoid(z)
    f = lb + (1.0 - lb) * sig
    logf = jnp.log(f)
    kk = (1.0 - lb) * (1.0 - sig)
    q = q_ref[bb] * (dk ** -0.5)
    v = v_ref[bb]
    og = og_ref[bb]

    hi = logf.astype(BF16)
    lo = (logf - hi.astype(F32)).astype(BF16)
    tril = tril_ref[...]
    g_inc = (jnp.dot(tril, hi, preferred_element_type=F32)
             + jnp.dot(tril, lo, preferred_element_type=F32))
    g_last = g_inc[c - 1:c, :]
    q_st = (q * jnp.exp(g_inc)).astype(BF16)
    k_st = (kk * jnp.exp(g_last - g_inc)).astype(BF16)
    dec_all = jnp.exp(g_last)
    v_bf = v.astype(BF16)
    q_bf = q.astype(BF16)
    k_bf = kk.astype(BF16)

    row = lax.broadcasted_iota(jnp.int32, (c, c), 0)
    col = lax.broadcasted_iota(jnp.int32, (c, c), 1)
    rr = lax.broadcasted_iota(jnp.int32, (c, 1), 0)

    qs, ks, masks = [], [], []
    for m in levels:
        upper, ex = _level_exponent(g_inc, logf, m, rr)
        e = jnp.exp(ex)
        qs.append(jnp.where(upper, q * e, 0.0).astype(BF16))
        ks.append(jnp.where(upper, 0.0, kk * e).astype(BF16))
        sh = int(np.log2(2 * m))
        masks.append((row >> sh) == (col >> sh))
    diag = row == col

    nt = (((1,), (1,)), ((), ()))
    tn = (((0,), (0,)), ((), ()))
    for h in range(HG_HEADS):
        sl = slice(h * dk, (h + 1) * dk)
        a = jnp.where(diag, lax.dot_general(q_bf[:, sl], k_bf[:, sl], nt,
                                            preferred_element_type=F32), 0.0)
        for li in range(len(levels)):
            a = a + jnp.where(masks[li],
                              lax.dot_general(qs[li][:, sl], ks[li][:, sl], nt,
                                              preferred_element_type=F32), 0.0)
        s_t = st[bb, h]
        o = jnp.dot(a.astype(BF16), v_bf[:, sl], preferred_element_type=F32)
        o = o + lax.dot_general(q_st[:, sl], s_t.astype(BF16), nt, preferred_element_type=F32)
        st[bb, h] = s_t * dec_all[:, sl] + lax.dot_general(v_bf[:, sl], k_st[:, sl], tn,
                                                           preferred_element_type=F32)
        ms = jnp.mean(o * o, axis=-1, keepdims=True)
        o = o * lax.rsqrt(ms + EPS) * ng_ref[...]
        g = og[:, sl]
        o_ref[bb, :, sl] = (o * (g * _sigmoid(g))).astype(BF16)


def _hgrn_branch(p, lb, norm_g):
    b, s, _ = p.shape
    d = D_MODEL
    c = HG_CHUNK
    tril = jnp.asarray(np.tril(np.ones((c, c), np.float32)), dtype=BF16)

    assert b % HG_NB == 0

    def col_spec(col):
        return pl.BlockSpec((HG_NB, c, d), lambda bi, i: (bi, i, col))

    return pl.pallas_call(
        _hgrn_kernel,
        grid=(b // HG_NB, s // c),
        in_specs=[
            col_spec(P_Q), col_spec(P_F), col_spec(P_I), col_spec(P_G),
            pl.BlockSpec((1, d), lambda bi, i: (0, 0)),
            pl.BlockSpec((1, HG_DK), lambda bi, i: (0, 0)),
            pl.BlockSpec((c, c), lambda bi, i: (0, 0)),
        ],
        out_specs=pl.BlockSpec((HG_NB, c, d), lambda bi, i: (bi, i, 0)),
        out_shape=jax.ShapeDtypeStruct((b, s, d), BF16),
        scratch_shapes=[pltpu.VMEM((HG_NB, HG_HEADS, HG_DK, HG_DK), F32)],
        compiler_params=_params(("arbitrary", "arbitrary")),
        name="hgrn_branch",
    )(p, p, p, p, lb, norm_g, tril)


ROW_TILE = (8, 128)


def _rows_to_tiles(rows):
    st = jnp.stack([rows[:, j * 128:(j + 1) * 128] for j in range(ROW_TILE[0])], axis=0)
    return pltpu.einshape("jrl->rjl", st)


def _tile_rows(first_row):
    return pl.ds(pl.multiple_of(first_row, ROW_TILE[0]), ROW_TILE[0])


def _tiles_to_rows(tiles):
    y = pltpu.einshape("rjl->jrl", tiles)
    return jnp.concatenate([y[j] for j in range(ROW_TILE[0])], axis=-1)


_NT = (((1,), (1,)), ((), ()))


def _merge_kernel(uc_ref, uh_ref, gc_ref, gh_ref, x_ref, ga_ref, sc_ref, sh_ref, g_ref,
                  wc_ref, wh_ref, w_ref, wrt_ref, br_ref, x1_ref, h3_ref, lg_ref):
    yc = jnp.dot(uc_ref[0], wc_ref[...], preferred_element_type=F32)
    yh = jnp.dot(uh_ref[0], wh_ref[...], preferred_element_type=F32)
    gate_c = _sigmoid(gc_ref[0].astype(F32))
    gate_h = _sigmoid(gh_ref[0].astype(F32))
    merged = (gate_c * yc + gate_h * yh).astype(BF16)
    x1 = x_ref[0] + ga_ref[0] * jnp.dot(merged, w_ref[...], preferred_element_type=F32)
    x1_ref[0] = x1
    ms = jnp.mean(x1 * x1, axis=-1, keepdims=True)
    h2 = x1 * lax.rsqrt(ms + EPS) * g_ref[...]
    h2 = h2 * (1.0 + sc_ref[0]) + sh_ref[0]
    h3_ref[...] = _rows_to_tiles(h2)
    wrt = wrt_ref[...]
    wrt_hi = wrt.astype(BF16)
    wrt_lo = (wrt - wrt_hi.astype(F32)).astype(BF16)
    h2_hi = h2.astype(BF16)
    h2_lo = (h2 - h2_hi.astype(F32)).astype(BF16)
    lg = (lax.dot_general(wrt_hi, h2_hi, _NT, preferred_element_type=F32)
          + lax.dot_general(wrt_hi, h2_lo, _NT, preferred_element_type=F32)
          + lax.dot_general(wrt_lo, h2_hi, _NT, preferred_element_type=F32))
    lg_ref[...] = lg + br_ref[...]


def _merge(uc, uh, pg, x, mod3, g_ffn, w_conv_bf, w_hgrn_bf, w_out_bf, w_router_t, b_router_col):
    b, s, d = x.shape
    tm = 512
    nt = s // tm
    ne = w_router_t.shape[0]

    def mod_spec(k):
        return pl.BlockSpec((1, 1, d), lambda bi, i: (bi * N_MOD + k, 0, 0))

    def col_spec(col):
        return pl.BlockSpec((1, tm, d), lambda bi, i: (bi, i, col))

    tile = pl.BlockSpec((1, tm, d), lambda bi, i: (bi, i, 0))
    weight = pl.BlockSpec((d, d), lambda bi, i: (0, 0))
    return pl.pallas_call(
        _merge_kernel,
        grid=(b, nt),
        in_specs=[
            tile, tile, col_spec(PG_C), col_spec(PG_H), tile,
            mod_spec(2), mod_spec(4), mod_spec(3),
            pl.BlockSpec((1, d), lambda bi, i: (0, 0)),
            weight, weight, weight,
            pl.BlockSpec((ne, d), lambda bi, i: (0, 0)),
            pl.BlockSpec((ne, 1), lambda bi, i: (0, 0)),
        ],
        out_specs=[
            tile,
            pl.BlockSpec((tm,) + ROW_TILE, lambda bi, i: (bi * nt + i, 0, 0)),
            pl.BlockSpec((ne, tm), lambda bi, i: (0, bi * nt + i)),
        ],
        out_shape=[
            jax.ShapeDtypeStruct((b, s, d), F32),
            jax.ShapeDtypeStruct((b * s,) + ROW_TILE, F32),
            jax.ShapeDtypeStruct((ne, b * s), F32),
        ],
        compiler_params=_params(("arbitrary", "arbitrary")),
        name="merge_router",
    )(uc, uh, pg, pg, x, mod3, mod3, mod3, g_ffn, w_conv_bf, w_hgrn_bf, w_out_bf, w_router_t,
      b_router_col)


MOE_TL = 1024
MOE_GRP = 16
MOE_CAP = MOE_TL * TOP_K + N_EXPERTS * MOE_GRP
MOE_GPB = MOE_BLOCK // MOE_GRP
RELAY_ROWS = 256
assert MOE_CAP % RELAY_ROWS == 0 and MOE_TL % RELAY_ROWS == 0


def _route_kernel(lg_ref, u_ref, ld_ref, w_ref, cnt_ref):
    tl = MOE_TL
    l = lg_ref[...]
    eio = lax.broadcasted_iota(jnp.int32, l.shape, 0)
    vals, hots = [], []
    for _ in range(TOP_K):
        m = jnp.max(l, axis=0, keepdims=True)
        idx = jnp.min(jnp.where(l == m, eio, N_EXPERTS), axis=0, keepdims=True)
        hot = eio == idx
        vals.append(m)
        hots.append(hot)
        l = jnp.where(hot, -jnp.inf, l)
    ex = [jnp.exp(v - vals[0]) for v in vals]
    den = ex[0] + ex[1] + ex[2] + ex[3]
    cnt = hots[0].astype(F32)
    for k in range(1, TOP_K):
        cnt = cnt + hots[k].astype(F32)
    prefix = jnp.dot(cnt.astype(BF16), u_ref[...], preferred_element_type=F32)
    n_e = jnp.sum(cnt, axis=1, keepdims=True)
    pad_e = jnp.floor((n_e + (MOE_GRP - 1)) * (1.0 / MOE_GRP)) * MOE_GRP
    scan = jnp.broadcast_to(pad_e, (N_EXPERTS, ROW_TILE[1]))
    ei = lax.broadcasted_iota(jnp.int32, scan.shape, 0)
    dist = 1
    while dist < N_EXPERTS:
        scan = scan + jnp.where(ei >= dist, pltpu.roll(scan, dist, axis=0), 0.0)
        dist *= 2
    base = prefix + (scan[:, 0:1] - pad_e)
    for k in range(TOP_K):
        dest = jnp.sum(jnp.where(hots[k], base, 0.0), axis=0, keepdims=True)
        ld_ref[0, :, k * tl:(k + 1) * tl] = (dest * float(ROW_TILE[0])).astype(jnp.int32)
        w_ref[0, :, k * tl:(k + 1) * tl] = ex[k] / den
    cnt_ref[0] = jnp.broadcast_to(n_e, (N_EXPERTS, ROW_TILE[1])).astype(jnp.int32)


def _route(logits_t):
    ne, t = logits_t.shape
    tl = MOE_TL
    n_tiles = t // tl
    upper = jnp.asarray(np.triu(np.ones((tl, tl), np.float32), k=1), dtype=BF16)
    slot = pl.BlockSpec((1, 1, TOP_K * tl), lambda i: (i, 0, 0))
    return pl.pallas_call(
        _route_kernel,
        grid=(n_tiles,),
        in_specs=[pl.BlockSpec((ne, tl), lambda i: (0, i)),
                  pl.BlockSpec((tl, tl), lambda i: (0, 0))],
        out_specs=[slot, slot, pl.BlockSpec((1, ne, ROW_TILE[1]), lambda i: (i, 0, 0))],
        out_shape=[
            jax.ShapeDtypeStruct((n_tiles, 1, TOP_K * tl), jnp.int32),
            jax.ShapeDtypeStruct((n_tiles, 1, TOP_K * tl), F32),
            jax.ShapeDtypeStruct((n_tiles, ne, ROW_TILE[1]), jnp.int32),
        ],
        compiler_params=_params(("arbitrary",)),
        name="route",
    )(logits_t, upper)


def _gather_kernel(ld_ref, h3_ref, xs_ref, xs3):
    tl = MOE_TL
    last = pl.num_programs(0) - 1

    @pl.when((pl.program_id(0) == 0) | (pl.program_id(0) == last))
    def _():
        xs3[...] = jnp.zeros_like(xs3)

    def push(t, carry):
        tile = h3_ref[t]
        for k in range(TOP_K):
            xs3[_tile_rows(ld_ref[0, 0, k * tl + t]), :] = tile
        return carry

    @pl.when(pl.program_id(0) < last)
    def _():
        lax.fori_loop(0, tl, push, 0, unroll=16)

    def relay(r, carry):
        r0 = pl.multiple_of(r * RELAY_ROWS, RELAY_ROWS)
        tiles = xs3[pl.ds(r0 * ROW_TILE[0], RELAY_ROWS * ROW_TILE[0]), :]
        tiles = tiles.reshape((RELAY_ROWS,) + ROW_TILE)
        xs_ref[pl.ds(r0, RELAY_ROWS), :] = _tiles_to_rows(tiles).astype(BF16)
        return carry

    lax.fori_loop(0, MOE_CAP // RELAY_ROWS, relay, 0)


def _gather(ld, h3):
    t = h3.shape[0]
    tl = MOE_TL
    n_tiles = t // tl
    last = n_tiles - 1
    return pl.pallas_call(
        _gather_kernel,
        grid=(n_tiles + 1,),
        in_specs=[
            pl.BlockSpec((1, 1, TOP_K * tl), lambda i: (jnp.minimum(i, last), 0, 0),
                         memory_space=pltpu.SMEM),
            pl.BlockSpec((tl,) + ROW_TILE, lambda i: (jnp.minimum(i, last), 0, 0)),
        ],
        out_specs=pl.BlockSpec((MOE_CAP, D_MODEL), lambda i: (i, 0)),
        out_shape=jax.ShapeDtypeStruct(((n_tiles + 1) * MOE_CAP, D_MODEL), BF16),
        scratch_shapes=[pltpu.VMEM((MOE_CAP * ROW_TILE[0], ROW_TILE[1]), F32)],
        compiler_params=_params(("arbitrary",)),
        name="moe_gather",
    )(ld, h3)


def _table_kernel(cnt_ref, be_ref, src_ref, toff):
    n_tiles = cnt_ref.shape[0]
    n_slots = src_ref.shape[0]
    groups_per_tile = MOE_CAP // MOE_GRP
    step = 16

    for i in range(n_tiles):
        toff[i] = 0

    def per_expert(e, pos):
        def per_tile(i, p):
            g = (cnt_ref[i, e] + (MOE_GRP - 1)) // MOE_GRP
            base = i * groups_per_tile + toff[i]

            def put(j, c):
                for u in range(step):
                    src_ref[p + j * step + u] = base + j * step + u
                return c

            put(0, 0)
            lax.fori_loop(1, (g + (step - 1)) // step, put, 0)
            toff[i] = toff[i] + g
            return p + g

        end = lax.fori_loop(0, n_tiles, per_tile, pos)
        new_pos = (end + (MOE_GPB - 1)) // MOE_GPB * MOE_GPB
        for u in range(MOE_GPB):
            src_ref[end + u] = -1

        def put_e(b, c):
            be_ref[b] = e
            return c

        lax.fori_loop(pos // MOE_GPB, new_pos // MOE_GPB, put_e, 0)
        return new_pos

    used = lax.fori_loop(0, N_EXPERTS, per_expert, 0)

    def tail(b, c):
        be_ref[b] = N_EXPERTS - 1
        for u in range(MOE_GPB):
            src_ref[b * MOE_GPB + u] = -1
        return c

    lax.fori_loop(used // MOE_GPB, n_slots // MOE_GPB, tail, 0)


def _block_table(cnt, n_blocks):
    smem = pl.BlockSpec(memory_space=pltpu.SMEM)
    return pl.pallas_call(
        _table_kernel,
        in_specs=[smem],
        out_specs=[smem, smem],
        out_shape=[jax.ShapeDtypeStruct((n_blocks,), jnp.int32),
                   jax.ShapeDtypeStruct((n_blocks * MOE_GPB,), jnp.int32)],
        scratch_shapes=[pltpu.SMEM((cnt.shape[0],), jnp.int32)],
        name="moe_block_table",
    )(cnt)


def _ffn_kernel(be_ref, src_ref, xs_hbm, w1_ref, b1_ref, w2_ref, b2_ref, ys_hbm,
                xbuf, ybuf, w1s, w2s, sem_in, sem_out, *, spare_grp):
    i = pl.program_id(0)
    nb = pl.num_programs(0)
    slot = i % 2

    def live(blk):
        return src_ref[blk * MOE_GPB] >= 0

    def group_rows(grp):
        return pl.ds(pl.multiple_of(grp * MOE_GRP, MOE_GRP), MOE_GRP)

    def start_in(blk, sl):
        for g in range(MOE_GPB):
            grp = src_ref[blk * MOE_GPB + g]
            grp = jnp.where(grp >= 0, grp, spare_grp)
            pltpu.make_async_copy(xs_hbm.at[group_rows(grp), :],
                                  xbuf.at[sl, pl.ds(g * MOE_GRP, MOE_GRP), :], sem_in.at[sl]).start()

    def start_out(blk, sl):
        for g in range(MOE_GPB):
            grp = src_ref[blk * MOE_GPB + g]
            grp = jnp.where(grp >= 0, grp, spare_grp + 1 + sl * MOE_GPB + g)
            pltpu.make_async_copy(ybuf.at[sl, pl.ds(g * MOE_GRP, MOE_GRP), :],
                                  ys_hbm.at[group_rows(grp), :], sem_out.at[sl]).start()

    def wait_in(sl):
        pltpu.make_async_copy(xs_hbm.at[pl.ds(0, MOE_BLOCK), :], xbuf.at[sl], sem_in.at[sl]).wait()

    def wait_out(sl):
        pltpu.make_async_copy(ybuf.at[sl], ys_hbm.at[pl.ds(0, MOE_BLOCK), :], sem_out.at[sl]).wait()

    @pl.when((i == 0) & live(0))
    def _():
        start_in(0, 0)

    nxt = jnp.minimum(i + 1, nb - 1)

    @pl.when((i + 1 < nb) & live(nxt))
    def _():
        start_in(nxt, 1 - slot)

    @pl.when((i >= 2) & live(jnp.maximum(i - 2, 0)))
    def _():
        wait_out(slot)

    prev = be_ref[jnp.maximum(i - 1, 0)]

    @pl.when((i == 0) | (be_ref[i] != prev))
    def _():
        w1s[...] = w1_ref[0].astype(BF16)
        w2s[...] = w2_ref[0].astype(BF16)

    @pl.when(live(i))
    def _():
        wait_in(slot)
        u = jnp.dot(xbuf[slot], w1s[...], preferred_element_type=F32) + b1_ref[0]
        gl = jnp.minimum(u[:, :D_FF], SWIGLU_LIMIT)
        lin = jnp.clip(u[:, D_FF:], -SWIGLU_LIMIT, SWIGLU_LIMIT)
        act = gl * _sigmoid(SWIGLU_ALPHA * gl) * (lin + 1.0)
        y = jnp.dot(act.astype(BF16), w2s[...], preferred_element_type=F32) + b2_ref[0]
        ybuf[slot] = y.astype(BF16)
        start_out(i, slot)

    @pl.when(i == nb - 1)
    def _():
        @pl.when(live(i))
        def _():
            wait_out(slot)

        @pl.when((i >= 1) & live(jnp.maximum(i - 1, 0)))
        def _():
            wait_out(1 - slot)


def _ffn(block_e, src, xs, w1, b1, w2, b2):
    r, d = xs.shape
    nb = block_e.shape[0]
    spare_grp = (r - MOE_CAP) // MOE_GRP
    assert 1 + 2 * MOE_GPB <= MOE_CAP // MOE_GRP
    grid_spec = pltpu.PrefetchScalarGridSpec(
        num_scalar_prefetch=2,
        grid=(nb,),
        in_specs=[
            pl.BlockSpec(memory_space=pl.ANY),
            pl.BlockSpec((1, d, 2 * D_FF), lambda i, be, sr: (be[i], 0, 0)),
            pl.BlockSpec((1, 1, 2 * D_FF), lambda i, be, sr: (be[i], 0, 0)),
            pl.BlockSpec((1, D_FF, d), lambda i, be, sr: (be[i], 0, 0)),
            pl.BlockSpec((1, 1, d), lambda i, be, sr: (be[i], 0, 0)),
        ],
        out_specs=pl.BlockSpec(memory_space=pl.ANY),
        scratch_shapes=[
            pltpu.VMEM((2, MOE_BLOCK, d), BF16), pltpu.VMEM((2, MOE_BLOCK, d), BF16),
            pltpu.VMEM((d, 2 * D_FF), BF16), pltpu.VMEM((D_FF, d), BF16),
            pltpu.SemaphoreType.DMA((2,)), pltpu.SemaphoreType.DMA((2,)),
        ],
    )
    return pl.pallas_call(
        functools.partial(_ffn_kernel, spare_grp=spare_grp),
        grid_spec=grid_spec,
        out_shape=jax.ShapeDtypeStruct((r, d), BF16),
        input_output_aliases={2: 0},
        compiler_params=_params(("arbitrary",)),
        name="expert_ffn",
    )(block_e, src, xs, w1, b1, w2, b2)


def _combine_kernel(ld_ref, w_ref, ys_ref, x1_ref, ga_ref, g_ref, o_ref, y3, o3, wb):
    tl = MOE_TL
    lanes = ROW_TILE[1]

    w4 = jnp.concatenate([w_ref[0, :, k * tl:(k + 1) * tl] for k in range(TOP_K)], axis=0)
    wt = jnp.concatenate([w4, jnp.zeros((lanes - TOP_K, tl), F32)], axis=0).T
    for k in range(TOP_K):
        wb[k] = jnp.broadcast_to(wt[:, k:k + 1], (tl, lanes))

    def relay(r, carry):
        r0 = pl.multiple_of(r * RELAY_ROWS, RELAY_ROWS)
        tiles = _rows_to_tiles(ys_ref[pl.ds(r0, RELAY_ROWS), :].astype(F32))
        y3[pl.ds(r0 * ROW_TILE[0], RELAY_ROWS * ROW_TILE[0]), :] = tiles.reshape(
            RELAY_ROWS * ROW_TILE[0], ROW_TILE[1])
        return carry

    lax.fori_loop(0, MOE_CAP // RELAY_ROWS, relay, 0)

    def pull(t, carry):
        acc = None
        for k in range(TOP_K):
            wv = jnp.broadcast_to(wb[k, pl.ds(t, 1), :], ROW_TILE)
            term = wv * y3[_tile_rows(ld_ref[0, 0, k * tl + t]), :]
            acc = term if acc is None else acc + term
        o3[t] = acc
        return carry

    lax.fori_loop(0, tl, pull, 0, unroll=32)

    def finish(r, carry):
        r0 = pl.multiple_of(r * RELAY_ROWS, RELAY_ROWS)
        x2 = x1_ref[pl.ds(r0, RELAY_ROWS), :] + ga_ref[0] * _tiles_to_rows(o3[pl.ds(r0, RELAY_ROWS)])
        ms = jnp.mean(x2 * x2, axis=-1, keepdims=True)
        o_ref[pl.ds(r0, RELAY_ROWS), :] = x2 * lax.rsqrt(ms + EPS) * g_ref[...]
        return carry

    lax.fori_loop(0, tl // RELAY_ROWS, finish, 0)


def _combine(ld, wts, ys, x1f, mod3, g_final, tiles_per_batch):
    t, d = x1f.shape
    tl = MOE_TL
    n_tiles = t // tl
    smem = functools.partial(pl.BlockSpec, (1, 1, TOP_K * tl), lambda i: (i, 0, 0),
                             memory_space=pltpu.SMEM)
    return pl.pallas_call(
        _combine_kernel,
        grid=(n_tiles,),
        in_specs=[
            smem(),
            pl.BlockSpec((1, 1, TOP_K * tl), lambda i: (i, 0, 0)),
            pl.BlockSpec((MOE_CAP, d), lambda i: (i, 0)),
            pl.BlockSpec((tl, d), lambda i: (i, 0)),
            pl.BlockSpec((1, 1, d), lambda i: ((i // tiles_per_batch) * N_MOD + 5, 0, 0)),
            pl.BlockSpec((1, d), lambda i: (0, 0)),
        ],
        out_specs=pl.BlockSpec((tl, d), lambda i: (i, 0)),
        out_shape=jax.ShapeDtypeStruct((t, d), F32),
        scratch_shapes=[pltpu.VMEM((MOE_CAP * ROW_TILE[0], ROW_TILE[1]), F32),
                        pltpu.VMEM((tl,) + ROW_TILE, F32),
                        pltpu.VMEM((TOP_K, tl, ROW_TILE[1]), F32)],
        compiler_params=_params(("arbitrary",), COMBINE_VMEM_LIMIT),
        name="moe_combine",
    )(ld, wts, ys, x1f, mod3, g_final)


def kernel(x, c, w_ada, b_ada, g_mix, w_in, conv_dw, conv_dw_bias, conv_ln_g, conv_ln_b,
           w_conv_out, lb_param, hgrn_norm_g, w_hgrn_out, w_out, g_ffn, w_router, b_router,
           w1, b1, w2, b2, g_final):
    b, s, d = x.shape
    assert w_ada.shape[0] == 1, "single-layer block"
    assert s % MOE_TL == 0
    t = b * s
    n_tiles = t // MOE_TL

    c_pad = jnp.zeros((8, d), F32).at[:b].set(c.astype(F32))
    mod, lb = _ada(c_pad, w_ada[0], b_ada, lb_param)
    mod3 = mod[:b].reshape(b * N_MOD, 1, d)

    dw_rows = jnp.broadcast_to(conv_dw[0][:, None, :], (CONV_K, PACK * SUBLANES, d))
    p, pg, uc = _inproj_conv(x, g_mix, mod3, w_in[0].astype(BF16), dw_rows, conv_dw_bias,
                             conv_ln_g, conv_ln_b)
    uh = _hgrn_branch(p, lb, hgrn_norm_g)

    x1, h3, logits_t = _merge(uc, uh, pg, x, mod3, g_ffn, w_conv_out[0].astype(BF16),
                              w_hgrn_out[0].astype(BF16), w_out[0].astype(BF16),
                              w_router[0].T, b_router[0][:, None])

    ld, wts, cnt = _route(logits_t)
    xs = _gather(ld, h3)
    groups_max = t * TOP_K // MOE_GRP + n_tiles * N_EXPERTS
    n_blocks = groups_max // MOE_GPB + N_EXPERTS
    block_e, src = _block_table(cnt[:, :, 0], n_blocks)
    ys = _ffn(block_e, src, xs, w1[0], b1[0][:, None, :], w2[0], b2[0][:, None, :])
    out = _combine(ld, wts, ys, x1.reshape(t, d), mod3, g_final.reshape(1, d), s // MOE_TL)
    return out.reshape(b, s, d)
```

```python
import functools

import jax
import jax.numpy as jnp
import numpy as np
from jax import lax
from jax.experimental import pallas as pl
from jax.experimental.pallas import tpu as pltpu

F32 = jnp.float32
BF16 = jnp.bfloat16

D_MODEL = 1024
CONV_K = 31
HG_HEADS = 8
HG_DK = 128
N_EXPERTS = 32
TOP_K = 4
D_FF = 1024
SWIGLU_ALPHA = 1.702
SWIGLU_LIMIT = 7.0
MOE_BLOCK = 512
EPS = 1e-6
N_MOD = 6
HG_CHUNK = 128
HG_NB = 4
CONV_HALO = 32
VMEM_LIMIT = 56 * 1024 * 1024
COMBINE_VMEM_LIMIT = 62 * 1024 * 1024


def _sigmoid(x):
    return 1.0 / (1.0 + jnp.exp(-x))


def _params(sem, vmem=VMEM_LIMIT):
    return pltpu.CompilerParams(dimension_semantics=sem, vmem_limit_bytes=vmem)


def _ada_kernel(c_ref, w_ref, b_ref, lbp_ref, mod_ref, lb_ref):
    c = c_ref[...]
    c_act = c * _sigmoid(c)
    mod_ref[...] = jnp.dot(c_act, w_ref[...], preferred_element_type=F32,
                           precision=lax.Precision.HIGHEST) + b_ref[...]
    p = lbp_ref[...]
    e = jnp.exp(p - jnp.max(p, axis=0, keepdims=True))
    lb_ref[...] = e[0:1, :] / jnp.sum(e, axis=0, keepdims=True)


def _ada(c_pad, w_ada, b_ada, lb_param):
    nb, d = c_pad.shape
    n = w_ada.shape[1]
    tn = 1536
    return pl.pallas_call(
        _ada_kernel,
        grid=(n // tn,),
        in_specs=[
            pl.BlockSpec((nb, d), lambda j: (0, 0)),
            pl.BlockSpec((d, tn), lambda j: (0, j)),
            pl.BlockSpec((1, tn), lambda j: (0, j)),
            pl.BlockSpec(lb_param.shape, lambda j: (0, 0)),
        ],
        out_specs=[
            pl.BlockSpec((nb, tn), lambda j: (0, j)),
            pl.BlockSpec((1, d), lambda j: (0, 0)),
        ],
        out_shape=[
            jax.ShapeDtypeStruct((nb, n), F32),
            jax.ShapeDtypeStruct((1, d), F32),
        ],
        compiler_params=_params(("arbitrary",)),
        name="ada_mod",
    )(c_pad, w_ada, b_ada, lb_param)


CONV_TS = 256
SUBLANES = 8
PACK = 2
CONV_SPAN = CONV_TS + CONV_HALO
CONV_OFF = CONV_HALO - (CONV_K - 1)
CONV_WROWS = CONV_SPAN // PACK
CONV_RGW = 16
P_Q, P_F, P_I, P_G = range(4)
N_HCOLS = 4
PG_C, PG_H = range(2)


def _inproj_conv_kernel(x_ref, g_ref, sc_ref, sh_ref, w_hbm, dw_ref, bias_ref, lng_ref, lnb_ref,
                        p_ref, pg_ref, uc_ref, buf, pe, po, she, sho, cv, w_ref, wsem, dwp):
    i = pl.program_id(1)
    ts = CONV_TS
    d = D_MODEL

    @pl.when((i == 0) & (pl.program_id(0) == 0))
    def _():
        cp = pltpu.make_async_copy(w_hbm, w_ref, wsem)
        cp.start()
        dwp[...] = dw_ref[...].astype(BF16)
        buf[CONV_SPAN:CONV_SPAN + SUBLANES, :] = jnp.zeros((SUBLANES, d), F32)
        cp.wait()

    @pl.when(i == 0)
    def _():
        buf[0:CONV_HALO, :] = jnp.zeros((CONV_HALO, d), F32)

    @pl.when(i > 0)
    def _():
        buf[0:CONV_HALO, :] = buf[ts:CONV_SPAN, :]

    x = x_ref[0]
    ms = jnp.mean(x * x, axis=-1, keepdims=True)
    h = x * lax.rsqrt(ms + EPS) * g_ref[...]
    h = (h * (1.0 + sc_ref[0]) + sh_ref[0]).astype(BF16)

    ab = jnp.dot(h, w_ref[:, 0:2 * d], preferred_element_type=F32)
    buf[CONV_HALO:CONV_SPAN, :] = ab[:, :d] * _sigmoid(ab[:, d:])
    hcols = (2 + N_HCOLS) * d
    p_ref[0] = jnp.dot(h, w_ref[:, 2 * d:hcols], preferred_element_type=F32)
    pg_ref[0] = jnp.dot(h, w_ref[:, hcols:], preferred_element_type=F32).astype(BF16)

    pe[...] = pltpu.bitcast(buf[0:CONV_SPAN, :].astype(BF16), jnp.uint32)
    po[...] = pltpu.bitcast(buf[1:CONV_SPAN + 1, :].astype(BF16), jnp.uint32)
    for s in range(1, SUBLANES):
        she[s - 1] = pe[s:s + CONV_WROWS - SUBLANES, :]
        sho[s - 1] = po[s:s + CONV_WROWS - SUBLANES, :]
    rgw = CONV_RGW
    for base in range(0, ts // PACK, rgw):
        acc = None
        for j in range(CONV_K):
            o = CONV_OFF + j
            s = (o // PACK) % SUBLANES
            row = base + o // PACK - s
            if o % PACK == 0:
                src = pe if s == 0 else she.at[s - 1]
            else:
                src = po if s == 0 else sho.at[s - 1]
            words = pltpu.bitcast(src[row:row + rgw, :], BF16)
            term = words.reshape(rgw // SUBLANES, PACK * SUBLANES, d) * dwp[j]
            acc = term if acc is None else acc + term
        cv[PACK * base:PACK * (base + rgw), :] = (acc.reshape(PACK * rgw, d).astype(F32)
                                                  + bias_ref[...])
    u = cv[...]
    mu = jnp.mean(u, axis=-1, keepdims=True)
    uc = u - mu
    var = jnp.mean(uc * uc, axis=-1, keepdims=True)
    y = uc * lax.rsqrt(var + EPS) * lng_ref[...] + lnb_ref[...]
    uc_ref[0] = (y * _sigmoid(y)).astype(BF16)


def _inproj_conv(x, g_mix, mod3, w_in_bf, dw_rows, bias, ln_g, ln_b):
    b, s, d = x.shape
    n = w_in_bf.shape[1]
    ts = CONV_TS
    vec = pl.BlockSpec((1, d), lambda bi, i: (0, 0))
    return pl.pallas_call(
        _inproj_conv_kernel,
        grid=(b, s // ts),
        in_specs=[
            pl.BlockSpec((1, ts, d), lambda bi, i: (bi, i, 0)),
            vec,
            pl.BlockSpec((1, 1, d), lambda bi, i: (bi * N_MOD + 1, 0, 0)),
            pl.BlockSpec((1, 1, d), lambda bi, i: (bi * N_MOD + 0, 0, 0)),
            pl.BlockSpec(memory_space=pl.ANY),
            pl.BlockSpec(dw_rows.shape, lambda bi, i: (0, 0, 0)),
            vec, vec, vec,
        ],
        out_specs=[
            pl.BlockSpec((1, ts, N_HCOLS * d), lambda bi, i: (bi, i, 0)),
            pl.BlockSpec((1, ts, 2 * d), lambda bi, i: (bi, i, 0)),
            pl.BlockSpec((1, ts, d), lambda bi, i: (bi, i, 0)),
        ],
        out_shape=[
            jax.ShapeDtypeStruct((b, s, N_HCOLS * d), F32),
            jax.ShapeDtypeStruct((b, s, 2 * d), BF16),
            jax.ShapeDtypeStruct((b, s, d), BF16),
        ],
        scratch_shapes=[pltpu.VMEM((CONV_SPAN + SUBLANES, d), F32),
                        pltpu.VMEM((CONV_WROWS, d), jnp.uint32),
                        pltpu.VMEM((CONV_WROWS, d), jnp.uint32),
                        pltpu.VMEM((SUBLANES - 1, CONV_WROWS - SUBLANES, d), jnp.uint32),
                        pltpu.VMEM((SUBLANES - 1, CONV_WROWS - SUBLANES, d), jnp.uint32),
                        pltpu.VMEM((ts, d), F32),
                        pltpu.VMEM((d, n), BF16), pltpu.SemaphoreType.DMA(()),
                        pltpu.VMEM(dw_rows.shape, BF16)],
        compiler_params=_params(("arbitrary", "arbitrary")),
        name="inproj_conv",
    )(x, g_mix, mod3, mod3, w_in_bf, dw_rows, bias, ln_g, ln_b)


def _hgrn_levels():
    c = HG_CHUNK
    levels = []
    m = c // 2
    while m >= 1:
        levels.append(m)
        m //= 2
    return levels


def _level_exponent(g_inc, logf, m, row):
    c, d = g_inc.shape
    upper = (row & m) != 0
    if m == 1:
        return upper, jnp.where(upper, logf, 0.0)
    if m >= SUBLANES:
        parts = [jnp.broadcast_to(g_inc[b * 2 * m + m - 1:b * 2 * m + m, :], (2 * m, d))
                 for b in range(c // (2 * m))]
        gref = parts[0] if len(parts) == 1 else jnp.concatenate(parts, axis=0)
    else:
        g3 = g_inc.reshape(c // SUBLANES, SUBLANES, d)

        def bcast(j):
            return jnp.broadcast_to(g3[:, j:j + 1, :], g3.shape).reshape(c, d)

        if 2 * m == SUBLANES:
            gref = bcast(m - 1)
        else:
            assert 4 * m == SUBLANES
            gref = jnp.where((row & (SUBLANES - 1)) < 2 * m, bcast(m - 1), bcast(3 * m - 1))
    return upper, jnp.where(upper, g_inc - gref, gref - g_inc)


def _hgrn_kernel(q_ref, z_ref, v_ref, og_ref, lb_ref, ng_ref, tril_ref, o_ref, st):
    @pl.when(pl.program_id(1) == 0)
    def _():
        st[...] = jnp.zeros_like(st)

    for bb in range(HG_NB):
        _hgrn_chunk(bb, q_ref, z_ref, v_ref, og_ref, lb_ref, ng_ref, tril_ref, o_ref, st)


def _hgrn_chunk(bb, q_ref, z_ref, v_ref, og_ref, lb_ref, ng_ref, tril_ref, o_ref, st):
    c = HG_CHUNK
    dk = HG_DK
    levels = _hgrn_levels()

    z = z_ref[bb]
    lb = lb_ref[...]
    sig = _sigmoid(z)
    f = lb + (1.0 - lb) * sig
    logf = jnp.log(f)
    kk = (1.0 - lb) * (1.0 - sig)
    q = q_ref[bb] * (dk ** -0.5)
    v = v_ref[bb]
    og = og_ref[bb]

    hi = logf.astype(BF16)
    lo = (logf - hi.astype(F32)).astype(BF16)
    tril = tril_ref[...]
    g_inc = (jnp.dot(tril, hi, preferred_element_type=F32)
             + jnp.dot(tril, lo, preferred_element_type=F32))
    g_last = g_inc[c - 1:c, :]
    q_st = (q * jnp.exp(g_inc)).astype(BF16)
    k_st = (kk * jnp.exp(g_last - g_inc)).astype(BF16)
    dec_all = jnp.exp(g_last)
    v_bf = v.astype(BF16)
    q_bf = q.astype(BF16)
    k_bf = kk.astype(BF16)

    row = lax.broadcasted_iota(jnp.int32, (c, c), 0)
    col = lax.broadcasted_iota(jnp.int32, (c, c), 1)
    rr = lax.broadcasted_iota(jnp.int32, (c, 1), 0)

    qs, ks, masks = [], [], []
    for m in levels:
        upper, ex = _level_exponent(g_inc, logf, m, rr)
        e = jnp.exp(ex)
        qs.append(jnp.where(upper, q * e, 0.0).astype(BF16))
        ks.append(jnp.where(upper, 0.0, kk * e).astype(BF16))
        sh = int(np.log2(2 * m))
        masks.append((row >> sh) == (col >> sh))
    diag = row == col

    nt = (((1,), (1,)), ((), ()))
    tn = (((0,), (0,)), ((), ()))
    for h in range(HG_HEADS):
        sl = slice(h * dk, (h + 1) * dk)
        a = jnp.where(diag, lax.dot_general(q_bf[:, sl], k_bf[:, sl], nt,
                                            preferred_element_type=F32), 0.0)
        for li in range(len(levels)):
            a = a + jnp.where(masks[li],
                              lax.dot_general(qs[li][:, sl], ks[li][:, sl], nt,
                                              preferred_element_type=F32), 0.0)
        s_t = st[bb, h]
        o = jnp.dot(a.astype(BF16), v_bf[:, sl], preferred_element_type=F32)
        o = o + lax.dot_general(q_st[:, sl], s_t.astype(BF16), nt, preferred_element_type=F32)
        st[bb, h] = s_t * dec_all[:, sl] + lax.dot_general(v_bf[:, sl], k_st[:, sl], tn,
                                                           preferred_element_type=F32)
        ms = jnp.mean(o * o, axis=-1, keepdims=True)
        o = o * lax.rsqrt(ms + EPS) * ng_ref[...]
        g = og[:, sl]
        o_ref[bb, :, sl] = (o * (g * _sigmoid(g))).astype(BF16)


def _hgrn_branch(p, lb, norm_g):
    b, s, _ = p.shape
    d = D_MODEL
    c = HG_CHUNK
    tril = jnp.asarray(np.tril(np.ones((c, c), np.float32)), dtype=BF16)

    assert b % HG_NB == 0

    def col_spec(col):
        return pl.BlockSpec((HG_NB, c, d), lambda bi, i: (bi, i, col))

    return pl.pallas_call(
        _hgrn_kernel,
        grid=(b // HG_NB, s // c),
        in_specs=[
            col_spec(P_Q), col_spec(P_F), col_spec(P_I), col_spec(P_G),
            pl.BlockSpec((1, d), lambda bi, i: (0, 0)),
            pl.BlockSpec((1, HG_DK), lambda bi, i: (0, 0)),
            pl.BlockSpec((c, c), lambda bi, i: (0, 0)),
        ],
        out_specs=pl.BlockSpec((HG_NB, c, d), lambda bi, i: (bi, i, 0)),
        out_shape=jax.ShapeDtypeStruct((b, s, d), BF16),
        scratch_shapes=[pltpu.VMEM((HG_NB, HG_HEADS, HG_DK, HG_DK), F32)],
        compiler_params=_params(("arbitrary", "arbitrary")),
        name="hgrn_branch",
    )(p, p, p, p, lb, norm_g, tril)


ROW_TILE = (8, 128)


def _rows_to_tiles(rows):
    st = jnp.stack([rows[:, j * 128:(j + 1) * 128] for j in range(ROW_TILE[0])], axis=0)
    return pltpu.einshape("jrl->rjl", st)


def _tile_rows(first_row):
    return pl.ds(pl.multiple_of(first_row, ROW_TILE[0]), ROW_TILE[0])


def _tiles_to_rows(tiles):
    y = pltpu.einshape("rjl->jrl", tiles)
    return jnp.concatenate([y[j] for j in range(ROW_TILE[0])], axis=-1)


_NT = (((1,), (1,)), ((), ()))


def _merge_kernel(uc_ref, uh_ref, gc_ref, gh_ref, x_ref, ga_ref, sc_ref, sh_ref, g_ref,
                  wc_ref, wh_ref, w_ref, wrt_ref, br_ref, x1_ref, h3_ref, lg_ref):
    yc = jnp.dot(uc_ref[0], wc_ref[...], preferred_element_type=F32)
    yh = jnp.dot(uh_ref[0], wh_ref[...], preferred_element_type=F32)
    gate_c = _sigmoid(gc_ref[0].astype(F32))
    gate_h = _sigmoid(gh_ref[0].astype(F32))
    merged = (gate_c * yc + gate_h * yh).astype(BF16)
    x1 = x_ref[0] + ga_ref[0] * jnp.dot(merged, w_ref[...], preferred_element_type=F32)
    x1_ref[0] = x1
    ms = jnp.mean(x1 * x1, axis=-1, keepdims=True)
    h2 = x1 * lax.rsqrt(ms + EPS) * g_ref[...]
    h2 = h2 * (1.0 + sc_ref[0]) + sh_ref[0]
    h3_ref[...] = _rows_to_tiles(h2)
    wrt = wrt_ref[...]
    wrt_hi = wrt.astype(BF16)
    wrt_lo = (wrt - wrt_hi.astype(F32)).astype(BF16)
    h2_hi = h2.astype(BF16)
    h2_lo = (h2 - h2_hi.astype(F32)).astype(BF16)
    lg = (lax.dot_general(wrt_hi, h2_hi, _NT, preferred_element_type=F32)
          + lax.dot_general(wrt_hi, h2_lo, _NT, preferred_element_type=F32)
          + lax.dot_general(wrt_lo, h2_hi, _NT, preferred_element_type=F32))
    lg_ref[...] = lg + br_ref[...]


def _merge(uc, uh, pg, x, mod3, g_ffn, w_conv_bf, w_hgrn_bf, w_out_bf, w_router_t, b_router_col):
    b, s, d = x.shape
    tm = 512
    nt = s // tm
    ne = w_router_t.shape[0]

    def mod_spec(k):
        return pl.BlockSpec((1, 1, d), lambda bi, i: (bi * N_MOD + k, 0, 0))

    def col_spec(col):
        return pl.BlockSpec((1, tm, d), lambda bi, i: (bi, i, col))

    tile = pl.BlockSpec((1, tm, d), lambda bi, i: (bi, i, 0))
    weight = pl.BlockSpec((d, d), lambda bi, i: (0, 0))
    return pl.pallas_call(
        _merge_kernel,
        grid=(b, nt),
        in_specs=[
            tile, tile, col_spec(PG_C), col_spec(PG_H), tile,
            mod_spec(2), mod_spec(4), mod_spec(3),
            pl.BlockSpec((1, d), lambda bi, i: (0, 0)),
            weight, weight, weight,
            pl.BlockSpec((ne, d), lambda bi, i: (0, 0)),
            pl.BlockSpec((ne, 1), lambda bi, i: (0, 0)),
        ],
        out_specs=[
            tile,
            pl.BlockSpec((tm,) + ROW_TILE, lambda bi, i: (bi * nt + i, 0, 0)),
            pl.BlockSpec((ne, tm), lambda bi, i: (0, bi * nt + i)),
        ],
        out_shape=[
            jax.ShapeDtypeStruct((b, s, d), F32),
            jax.ShapeDtypeStruct((b * s,) + ROW_TILE, F32),
            jax.ShapeDtypeStruct((ne, b * s), F32),
        ],
        compiler_params=_params(("arbitrary", "arbitrary")),
        name="merge_router",
    )(uc, uh, pg, pg, x, mod3, mod3, mod3, g_ffn, w_conv_bf, w_hgrn_bf, w_out_bf, w_router_t,
      b_router_col)


MOE_TL = 1024
MOE_GRP = 16
MOE_CAP = MOE_TL * TOP_K + N_EXPERTS * MOE_GRP
MOE_GPB = MOE_BLOCK // MOE_GRP
RELAY_ROWS = 256
assert MOE_CAP % RELAY_ROWS == 0 and MOE_TL % RELAY_ROWS == 0


def _route_kernel(lg_ref, u_ref, ld_ref, w_ref, cnt_ref):
    tl = MOE_TL
    l = lg_ref[...]
    eio = lax.broadcasted_iota(jnp.int32, l.shape, 0)
    vals, hots = [], []
    for _ in range(TOP_K):
        m = jnp.max(l, axis=0, keepdims=True)
        idx = jnp.min(jnp.where(l == m, eio, N_EXPERTS), axis=0, keepdims=True)
        hot = eio == idx
        vals.append(m)
        hots.append(hot)
        l = jnp.where(hot, -jnp.inf, l)
    ex = [jnp.exp(v - vals[0]) for v in vals]
    den = ex[0] + ex[1] + ex[2] + ex[3]
    cnt = hots[0].astype(F32)
    for k in range(1, TOP_K):
        cnt = cnt + hots[k].astype(F32)
    prefix = jnp.dot(cnt.astype(BF16), u_ref[...], preferred_element_type=F32)
    n_e = jnp.sum(cnt, axis=1, keepdims=True)
    pad_e = jnp.floor((n_e + (MOE_GRP - 1)) * (1.0 / MOE_GRP)) * MOE_GRP
    scan = jnp.broadcast_to(pad_e, (N_EXPERTS, ROW_TILE[1]))
    ei = lax.broadcasted_iota(jnp.int32, scan.shape, 0)
    dist = 1
    while dist < N_EXPERTS:
        scan = scan + jnp.where(ei >= dist, pltpu.roll(scan, dist, axis=0), 0.0)
        dist *= 2
    base = prefix + (scan[:, 0:1] - pad_e)
    for k in range(TOP_K):
        dest = jnp.sum(jnp.where(hots[k], base, 0.0), axis=0, keepdims=True)
        ld_ref[0, :, k * tl:(k + 1) * tl] = (dest * float(ROW_TILE[0])).astype(jnp.int32)
        w_ref[0, :, k * tl:(k + 1) * tl] = ex[k] / den
    cnt_ref[0] = jnp.broadcast_to(n_e, (N_EXPERTS, ROW_TILE[1])).astype(jnp.int32)


def _route(logits_t):
    ne, t = logits_t.shape
    tl = MOE_TL
    n_tiles = t // tl
    upper = jnp.asarray(np.triu(np.ones((tl, tl), np.float32), k=1), dtype=BF16)
    slot = pl.BlockSpec((1, 1, TOP_K * tl), lambda i: (i, 0, 0))
    return pl.pallas_call(
        _route_kernel,
        grid=(n_tiles,),
        in_specs=[pl.BlockSpec((ne, tl), lambda i: (0, i)),
                  pl.BlockSpec((tl, tl), lambda i: (0, 0))],
        out_specs=[slot, slot, pl.BlockSpec((1, ne, ROW_TILE[1]), lambda i: (i, 0, 0))],
        out_shape=[
            jax.ShapeDtypeStruct((n_tiles, 1, TOP_K * tl), jnp.int32),
            jax.ShapeDtypeStruct((n_tiles, 1, TOP_K * tl), F32),
            jax.ShapeDtypeStruct((n_tiles, ne, ROW_TILE[1]), jnp.int32),
        ],
        compiler_params=_params(("arbitrary",)),
        name="route",
    )(logits_t, upper)


def _gather_kernel(ld_ref, h3_ref, xs_ref, xs3):
    tl = MOE_TL
    last = pl.num_programs(0) - 1

    @pl.when((pl.program_id(0) == 0) | (pl.program_id(0) == last))
    def _():
        xs3[...] = jnp.zeros_like(xs3)

    def push(t, carry):
        tile = h3_ref[t]
        for k in range(TOP_K):
            xs3[_tile_rows(ld_ref[0, 0, k * tl + t]), :] = tile
        return carry

    @pl.when(pl.program_id(0) < last)
    def _():
        lax.fori_loop(0, tl, push, 0, unroll=16)

    def relay(r, carry):
        r0 = pl.multiple_of(r * RELAY_ROWS, RELAY_ROWS)
        tiles = xs3[pl.ds(r0 * ROW_TILE[0], RELAY_ROWS * ROW_TILE[0]), :]
        tiles = tiles.reshape((RELAY_ROWS,) + ROW_TILE)
        xs_ref[pl.ds(r0, RELAY_ROWS), :] = _tiles_to_rows(tiles).astype(BF16)
        return carry

    lax.fori_loop(0, MOE_CAP // RELAY_ROWS, relay, 0)


def _gather(ld, h3):
    t = h3.shape[0]
    tl = MOE_TL
    n_tiles = t // tl
    last = n_tiles - 1
    return pl.pallas_call(
        _gather_kernel,
        grid=(n_tiles + 1,),
        in_specs=[
            pl.BlockSpec((1, 1, TOP_K * tl), lambda i: (jnp.minimum(i, last), 0, 0),
                         memory_space=pltpu.SMEM),
            pl.BlockSpec((tl,) + ROW_TILE, lambda i: (jnp.minimum(i, last), 0, 0)),
        ],
        out_specs=pl.BlockSpec((MOE_CAP, D_MODEL), lambda i: (i, 0)),
        out_shape=jax.ShapeDtypeStruct(((n_tiles + 1) * MOE_CAP, D_MODEL), BF16),
        scratch_shapes=[pltpu.VMEM((MOE_CAP * ROW_TILE[0], ROW_TILE[1]), F32)],
        compiler_params=_params(("arbitrary",)),
        name="moe_gather",
    )(ld, h3)


def _table_kernel(cnt_ref, be_ref, rd_ref, wr_ref, toff):
    n_tiles = cnt_ref.shape[0]
    n_slots = rd_ref.shape[0]
    groups_per_tile = MOE_CAP // MOE_GRP
    spare_grp = n_tiles * groups_per_tile
    step = 16

    for i in range(n_tiles):
        toff[i] = 0

    def put_padding(j):
        rd_ref[j] = spare_grp * MOE_GRP
        dump = spare_grp + 1 + ((j // MOE_GPB) % 2) * MOE_GPB + j % MOE_GPB
        wr_ref[j] = dump * MOE_GRP

    def per_expert(e, pos):
        def per_tile(i, p):
            g = (cnt_ref[i, e] + (MOE_GRP - 1)) // MOE_GRP
            base = i * groups_per_tile + toff[i]

            def put(j, c):
                for u in range(step):
                    row = (base + j * step + u) * MOE_GRP
                    rd_ref[p + j * step + u] = row
                    wr_ref[p + j * step + u] = row
                return c

            put(0, 0)
            lax.fori_loop(1, (g + (step - 1)) // step, put, 0)
            toff[i] = toff[i] + g
            return p + g

        end = lax.fori_loop(0, n_tiles, per_tile, pos)
        new_pos = (end + (MOE_GPB - 1)) // MOE_GPB * MOE_GPB
        for u in range(MOE_GPB):
            put_padding(end + u)

        def put_e(b, c):
            be_ref[b] = e
            return c

        lax.fori_loop(pos // MOE_GPB, new_pos // MOE_GPB, put_e, 0)
        return new_pos

    used = lax.fori_loop(0, N_EXPERTS, per_expert, 0)

    def tail(b, c):
        be_ref[b] = N_EXPERTS - 1
        for u in range(MOE_GPB):
            put_padding(b * MOE_GPB + u)
        return c

    lax.fori_loop(used // MOE_GPB, n_slots // MOE_GPB, tail, 0)


def _block_table(cnt, n_blocks):
    smem = pl.BlockSpec(memory_space=pltpu.SMEM)
    return pl.pallas_call(
        _table_kernel,
        in_specs=[smem],
        out_specs=[smem, smem, smem],
        out_shape=[jax.ShapeDtypeStruct((n_blocks,), jnp.int32),
                   jax.ShapeDtypeStruct((n_blocks * MOE_GPB,), jnp.int32),
                   jax.ShapeDtypeStruct((n_blocks * MOE_GPB,), jnp.int32)],
        scratch_shapes=[pltpu.SMEM((cnt.shape[0],), jnp.int32)],
        name="moe_block_table",
    )(cnt)


def _ffn_kernel(be_ref, rd_ref, wr_ref, xs_hbm, w1_ref, b1_ref, w2_ref, b2_ref, ys_hbm,
                xbuf, ybuf, w1s, w2s, sem_in, sem_out, *, zero_row):
    i = pl.program_id(0)
    nb = pl.num_programs(0)
    slot = i % 2

    def live(blk):
        return rd_ref[blk * MOE_GPB] != zero_row

    def group_rows(first_row):
        return pl.ds(pl.multiple_of(first_row, MOE_GRP), MOE_GRP)

    def start_in(blk, sl):
        for g in range(MOE_GPB):
            pltpu.make_async_copy(xs_hbm.at[group_rows(rd_ref[blk * MOE_GPB + g]), :],
                                  xbuf.at[sl, pl.ds(g * MOE_GRP, MOE_GRP), :], sem_in.at[sl]).start()

    def start_out(blk, sl):
        for g in range(MOE_GPB):
            pltpu.make_async_copy(ybuf.at[sl, pl.ds(g * MOE_GRP, MOE_GRP), :],
                                  ys_hbm.at[group_rows(wr_ref[blk * MOE_GPB + g]), :],
                                  sem_out.at[sl]).start()

    def wait_in(sl):
        pltpu.make_async_copy(xs_hbm.at[pl.ds(0, MOE_BLOCK), :], xbuf.at[sl], sem_in.at[sl]).wait()

    def wait_out(sl):
        pltpu.make_async_copy(ybuf.at[sl], ys_hbm.at[pl.ds(0, MOE_BLOCK), :], sem_out.at[sl]).wait()

    @pl.when((i == 0) & live(0))
    def _():
        start_in(0, 0)

    nxt = jnp.minimum(i + 1, nb - 1)

    @pl.when((i + 1 < nb) & live(nxt))
    def _():
        start_in(nxt, 1 - slot)

    @pl.when((i >= 2) & live(jnp.maximum(i - 2, 0)))
    def _():
        wait_out(slot)

    prev = be_ref[jnp.maximum(i - 1, 0)]

    @pl.when((i == 0) | (be_ref[i] != prev))
    def _():
        w1s[...] = w1_ref[0].astype(BF16)
        w2s[...] = w2_ref[0].astype(BF16)

    @pl.when(live(i))
    def _():
        wait_in(slot)
        u = jnp.dot(xbuf[slot], w1s[...], preferred_element_type=F32) + b1_ref[0]
        gl = jnp.minimum(u[:, :D_FF], SWIGLU_LIMIT)
        lin = jnp.clip(u[:, D_FF:], -SWIGLU_LIMIT, SWIGLU_LIMIT)
        act = gl * _sigmoid(SWIGLU_ALPHA * gl) * (lin + 1.0)
        y = jnp.dot(act.astype(BF16), w2s[...], preferred_element_type=F32) + b2_ref[0]
        ybuf[slot] = y.astype(BF16)
        start_out(i, slot)

    @pl.when(i == nb - 1)
    def _():
        @pl.when(live(i))
        def _():
            wait_out(slot)

        @pl.when((i >= 1) & live(jnp.maximum(i - 1, 0)))
        def _():
            wait_out(1 - slot)


def _ffn(block_e, rd_rows, wr_rows, xs, w1, b1, w2, b2):
    r, d = xs.shape
    nb = block_e.shape[0]
    zero_row = r - MOE_CAP
    assert 1 + 2 * MOE_GPB <= MOE_CAP // MOE_GRP
    grid_spec = pltpu.PrefetchScalarGridSpec(
        num_scalar_prefetch=3,
        grid=(nb,),
        in_specs=[
            pl.BlockSpec(memory_space=pl.ANY),
            pl.BlockSpec((1, d, 2 * D_FF), lambda i, be, rd, wr: (be[i], 0, 0)),
            pl.BlockSpec((1, 1, 2 * D_FF), lambda i, be, rd, wr: (be[i], 0, 0)),
            pl.BlockSpec((1, D_FF, d), lambda i, be, rd, wr: (be[i], 0, 0)),
            pl.BlockSpec((1, 1, d), lambda i, be, rd, wr: (be[i], 0, 0)),
        ],
        out_specs=pl.BlockSpec(memory_space=pl.ANY),
        scratch_shapes=[
            pltpu.VMEM((2, MOE_BLOCK, d), BF16), pltpu.VMEM((2, MOE_BLOCK, d), BF16),
            pltpu.VMEM((d, 2 * D_FF), BF16), pltpu.VMEM((D_FF, d), BF16),
            pltpu.SemaphoreType.DMA((2,)), pltpu.SemaphoreType.DMA((2,)),
        ],
    )
    return pl.pallas_call(
        functools.partial(_ffn_kernel, zero_row=zero_row),
        grid_spec=grid_spec,
        out_shape=jax.ShapeDtypeStruct((r, d), BF16),
        input_output_aliases={3: 0},
        compiler_params=_params(("arbitrary",)),
        name="expert_ffn",
    )(block_e, rd_rows, wr_rows, xs, w1, b1, w2, b2)


def _combine_kernel(ld_ref, w_ref, ys_ref, x1_ref, ga_ref, g_ref, o_ref, y3, o3, wb):
    tl = MOE_TL
    lanes = ROW_TILE[1]

    w4 = jnp.concatenate([w_ref[0, :, k * tl:(k + 1) * tl] for k in range(TOP_K)], axis=0)
    wt = jnp.concatenate([w4, jnp.zeros((lanes - TOP_K, tl), F32)], axis=0).T
    for k in range(TOP_K):
        wb[k] = jnp.broadcast_to(wt[:, k:k + 1], (tl, lanes))

    def relay(r, carry):
        r0 = pl.multiple_of(r * RELAY_ROWS, RELAY_ROWS)
        tiles = _rows_to_tiles(ys_ref[pl.ds(r0, RELAY_ROWS), :].astype(F32))
        y3[pl.ds(r0 * ROW_TILE[0], RELAY_ROWS * ROW_TILE[0]), :] = tiles.reshape(
            RELAY_ROWS * ROW_TILE[0], ROW_TILE[1])
        return carry

    lax.fori_loop(0, MOE_CAP // RELAY_ROWS, relay, 0)

    def pull(t, carry):
        acc = None
        for k in range(TOP_K):
            wv = jnp.broadcast_to(wb[k, pl.ds(t, 1), :], ROW_TILE)
            term = wv * y3[_tile_rows(ld_ref[0, 0, k * tl + t]), :]
            acc = term if acc is None else acc + term
        o3[t] = acc
        return carry

    lax.fori_loop(0, tl, pull, 0, unroll=32)

    def finish(r, carry):
        r0 = pl.multiple_of(r * RELAY_ROWS, RELAY_ROWS)
        x2 = x1_ref[pl.ds(r0, RELAY_ROWS), :] + ga_ref[0] * _tiles_to_rows(o3[pl.ds(r0, RELAY_ROWS)])
        ms = jnp.mean(x2 * x2, axis=-1, keepdims=True)
        o_ref[pl.ds(r0, RELAY_ROWS), :] = x2 * lax.rsqrt(ms + EPS) * g_ref[...]
        return carry

    lax.fori_loop(0, tl // RELAY_ROWS, finish, 0)


def _combine(ld, wts, ys, x1f, mod3, g_final, tiles_per_batch):
    t, d = x1f.shape
    tl = MOE_TL
    n_tiles = t // tl
    smem = functools.partial(pl.BlockSpec, (1, 1, TOP_K * tl), lambda i: (i, 0, 0),
                             memory_space=pltpu.SMEM)
    return pl.pallas_call(
        _combine_kernel,
        grid=(n_tiles,),
        in_specs=[
            smem(),
            pl.BlockSpec((1, 1, TOP_K * tl), lambda i: (i, 0, 0)),
            pl.BlockSpec((MOE_CAP, d), lambda i: (i, 0)),
            pl.BlockSpec((tl, d), lambda i: (i, 0)),
            pl.BlockSpec((1, 1, d), lambda i: ((i // tiles_per_batch) * N_MOD + 5, 0, 0)),
            pl.BlockSpec((1, d), lambda i: (0, 0)),
        ],
        out_specs=pl.BlockSpec((tl, d), lambda i: (i, 0)),
        out_shape=jax.ShapeDtypeStruct((t, d), F32),
        scratch_shapes=[pltpu.VMEM((MOE_CAP * ROW_TILE[0], ROW_TILE[1]), F32),
                        pltpu.VMEM((tl,) + ROW_TILE, F32),
                        pltpu.VMEM((TOP_K, tl, ROW_TILE[1]), F32)],
        compiler_params=_params(("arbitrary",), COMBINE_VMEM_LIMIT),
        name="moe_combine",
    )(ld, wts, ys, x1f, mod3, g_final)


def kernel(x, c, w_ada, b_ada, g_mix, w_in, conv_dw, conv_dw_bias, conv_ln_g, conv_ln_b,
           w_conv_out, lb_param, hgrn_norm_g, w_hgrn_out, w_out, g_ffn, w_router, b_router,
           w1, b1, w2, b2, g_final):
    b, s, d = x.shape
    assert w_ada.shape[0] == 1, "single-layer block"
    assert s % MOE_TL == 0
    t = b * s
    n_tiles = t // MOE_TL

    c_pad = jnp.zeros((8, d), F32).at[:b].set(c.astype(F32))
    mod, lb = _ada(c_pad, w_ada[0], b_ada, lb_param)
    mod3 = mod[:b].reshape(b * N_MOD, 1, d)

    dw_rows = jnp.broadcast_to(conv_dw[0][:, None, :], (CONV_K, PACK * SUBLANES, d))
    p, pg, uc = _inproj_conv(x, g_mix, mod3, w_in[0].astype(BF16), dw_rows, conv_dw_bias,
                             conv_ln_g, conv_ln_b)
    uh = _hgrn_branch(p, lb, hgrn_norm_g)

    x1, h3, logits_t = _merge(uc, uh, pg, x, mod3, g_ffn, w_conv_out[0].astype(BF16),
                              w_hgrn_out[0].astype(BF16), w_out[0].astype(BF16),
                              w_router[0].T, b_router[0][:, None])

    ld, wts, cnt = _route(logits_t)
    xs = _gather(ld, h3)
    groups_max = t * TOP_K // MOE_GRP + n_tiles * N_EXPERTS
    n_blocks = groups_max // MOE_GPB + N_EXPERTS
    block_e, rd_rows, wr_rows = _block_table(cnt[:, :, 0], n_blocks)
    ys = _ffn(block_e, rd_rows, wr_rows, xs, w1[0], b1[0][:, None, :], w2[0], b2[0][:, None, :])
    out = _combine(ld, wts, ys, x1.reshape(t, d), mod3, g_final.reshape(1, d), s // MOE_TL)
    return out.reshape(b, s, d)
```

```python
import functools

import jax
import jax.numpy as jnp
import numpy as np
from jax import lax
from jax.experimental import pallas as pl
from jax.experimental.pallas import tpu as pltpu

F32 = jnp.float32
BF16 = jnp.bfloat16

D_MODEL = 1024
CONV_K = 31
HG_HEADS = 8
HG_DK = 128
N_EXPERTS = 32
TOP_K = 4
D_FF = 1024
SWIGLU_ALPHA = 1.702
SWIGLU_LIMIT = 7.0
MOE_BLOCK = 512
EPS = 1e-6
N_MOD = 6
HG_CHUNK = 128
HG_NB = 4
HG_NC = 2
CONV_HALO = 32
VMEM_LIMIT = 56 * 1024 * 1024
COMBINE_VMEM_LIMIT = 62 * 1024 * 1024


def _sigmoid(x):
    return 1.0 / (1.0 + jnp.exp(-x))


def _params(sem, vmem=VMEM_LIMIT):
    return pltpu.CompilerParams(dimension_semantics=sem, vmem_limit_bytes=vmem)


def _ada_kernel(c_ref, w_ref, b_ref, lbp_ref, mod_ref, lb_ref):
    c = c_ref[...]
    c_act = c * _sigmoid(c)
    mod_ref[...] = jnp.dot(c_act, w_ref[...], preferred_element_type=F32,
                           precision=lax.Precision.HIGHEST) + b_ref[...]
    p = lbp_ref[...]
    e = jnp.exp(p - jnp.max(p, axis=0, keepdims=True))
    lb_ref[...] = e[0:1, :] / jnp.sum(e, axis=0, keepdims=True)


def _ada(c_pad, w_ada, b_ada, lb_param):
    nb, d = c_pad.shape
    n = w_ada.shape[1]
    tn = 1536
    return pl.pallas_call(
        _ada_kernel,
        grid=(n // tn,),
        in_specs=[
            pl.BlockSpec((nb, d), lambda j: (0, 0)),
            pl.BlockSpec((d, tn), lambda j: (0, j)),
            pl.BlockSpec((1, tn), lambda j: (0, j)),
            pl.BlockSpec(lb_param.shape, lambda j: (0, 0)),
        ],
        out_specs=[
            pl.BlockSpec((nb, tn), lambda j: (0, j)),
            pl.BlockSpec((1, d), lambda j: (0, 0)),
        ],
        out_shape=[
            jax.ShapeDtypeStruct((nb, n), F32),
            jax.ShapeDtypeStruct((1, d), F32),
        ],
        compiler_params=_params(("arbitrary",)),
        name="ada_mod",
    )(c_pad, w_ada, b_ada, lb_param)


CONV_TS = 256
SUBLANES = 8
PACK = 2
CONV_SPAN = CONV_TS + CONV_HALO
CONV_OFF = CONV_HALO - (CONV_K - 1)
CONV_WROWS = CONV_SPAN // PACK
CONV_RGW = 16
P_Q, P_F, P_I, P_G = range(4)
N_HCOLS = 4
PG_C, PG_H = range(2)


def _inproj_conv_kernel(x_ref, g_ref, sc_ref, sh_ref, w_hbm, dw_ref, bias_ref, lng_ref, lnb_ref,
                        p_ref, pg_ref, uc_ref, buf, pe, po, she, sho, cv, w_ref, wsem, dwp):
    i = pl.program_id(1)
    ts = CONV_TS
    d = D_MODEL

    @pl.when((i == 0) & (pl.program_id(0) == 0))
    def _():
        cp = pltpu.make_async_copy(w_hbm, w_ref, wsem)
        cp.start()
        dwp[...] = dw_ref[...].astype(BF16)
        buf[CONV_SPAN:CONV_SPAN + SUBLANES, :] = jnp.zeros((SUBLANES, d), F32)
        cp.wait()

    @pl.when(i == 0)
    def _():
        buf[0:CONV_HALO, :] = jnp.zeros((CONV_HALO, d), F32)

    @pl.when(i > 0)
    def _():
        buf[0:CONV_HALO, :] = buf[ts:CONV_SPAN, :]

    x = x_ref[0]
    ms = jnp.mean(x * x, axis=-1, keepdims=True)
    h = x * lax.rsqrt(ms + EPS) * g_ref[...]
    h = (h * (1.0 + sc_ref[0]) + sh_ref[0]).astype(BF16)

    ab = jnp.dot(h, w_ref[:, 0:2 * d], preferred_element_type=F32)
    buf[CONV_HALO:CONV_SPAN, :] = ab[:, :d] * _sigmoid(ab[:, d:])
    hcols = (2 + N_HCOLS) * d
    p_ref[0] = jnp.dot(h, w_ref[:, 2 * d:hcols], preferred_element_type=F32)
    pg_ref[0] = jnp.dot(h, w_ref[:, hcols:], preferred_element_type=F32).astype(BF16)

    pe[...] = pltpu.bitcast(buf[0:CONV_SPAN, :].astype(BF16), jnp.uint32)
    po[...] = pltpu.bitcast(buf[1:CONV_SPAN + 1, :].astype(BF16), jnp.uint32)
    for s in range(1, SUBLANES):
        she[s - 1] = pe[s:s + CONV_WROWS - SUBLANES, :]
        sho[s - 1] = po[s:s + CONV_WROWS - SUBLANES, :]
    rgw = CONV_RGW
    for base in range(0, ts // PACK, rgw):
        acc = None
        for j in range(CONV_K):
            o = CONV_OFF + j
            s = (o // PACK) % SUBLANES
            row = base + o // PACK - s
            if o % PACK == 0:
                src = pe if s == 0 else she.at[s - 1]
            else:
                src = po if s == 0 else sho.at[s - 1]
            words = pltpu.bitcast(src[row:row + rgw, :], BF16)
            term = words.reshape(rgw // SUBLANES, PACK * SUBLANES, d) * dwp[j]
            acc = term if acc is None else acc + term
        cv[PACK * base:PACK * (base + rgw), :] = (acc.reshape(PACK * rgw, d).astype(F32)
                                                  + bias_ref[...])
    u = cv[...]
    mu = jnp.mean(u, axis=-1, keepdims=True)
    uc = u - mu
    var = jnp.mean(uc * uc, axis=-1, keepdims=True)
    y = uc * lax.rsqrt(var + EPS) * lng_ref[...] + lnb_ref[...]
    uc_ref[0] = (y * _sigmoid(y)).astype(BF16)


def _inproj_conv(x, g_mix, mod3, w_in_bf, dw_rows, bias, ln_g, ln_b):
    b, s, d = x.shape
    n = w_in_bf.shape[1]
    ts = CONV_TS
    vec = pl.BlockSpec((1, d), lambda bi, i: (0, 0))
    return pl.pallas_call(
        _inproj_conv_kernel,
        grid=(b, s // ts),
        in_specs=[
            pl.BlockSpec((1, ts, d), lambda bi, i: (bi, i, 0)),
            vec,
            pl.BlockSpec((1, 1, d), lambda bi, i: (bi * N_MOD + 1, 0, 0)),
            pl.BlockSpec((1, 1, d), lambda bi, i: (bi * N_MOD + 0, 0, 0)),
            pl.BlockSpec(memory_space=pl.ANY),
            pl.BlockSpec(dw_rows.shape, lambda bi, i: (0, 0, 0)),
            vec, vec, vec,
        ],
        out_specs=[
            pl.BlockSpec((1, ts, N_HCOLS * d), lambda bi, i: (bi, i, 0)),
            pl.BlockSpec((1, ts, 2 * d), lambda bi, i: (bi, i, 0)),
            pl.BlockSpec((1, ts, d), lambda bi, i: (bi, i, 0)),
        ],
        out_shape=[
            jax.ShapeDtypeStruct((b, s, N_HCOLS * d), F32),
            jax.ShapeDtypeStruct((b, s, 2 * d), BF16),
            jax.ShapeDtypeStruct((b, s, d), BF16),
        ],
        scratch_shapes=[pltpu.VMEM((CONV_SPAN + SUBLANES, d), F32),
                        pltpu.VMEM((CONV_WROWS, d), jnp.uint32),
                        pltpu.VMEM((CONV_WROWS, d), jnp.uint32),
                        pltpu.VMEM((SUBLANES - 1, CONV_WROWS - SUBLANES, d), jnp.uint32),
                        pltpu.VMEM((SUBLANES - 1, CONV_WROWS - SUBLANES, d), jnp.uint32),
                        pltpu.VMEM((ts, d), F32),
                        pltpu.VMEM((d, n), BF16), pltpu.SemaphoreType.DMA(()),
                        pltpu.VMEM(dw_rows.shape, BF16)],
        compiler_params=_params(("arbitrary", "arbitrary")),
        name="inproj_conv",
    )(x, g_mix, mod3, mod3, w_in_bf, dw_rows, bias, ln_g, ln_b)


def _hgrn_levels():
    c = HG_CHUNK
    levels = []
    m = c // 2
    while m >= 1:
        levels.append(m)
        m //= 2
    return levels


def _level_exponent(g_inc, logf, m, row):
    c, d = g_inc.shape
    upper = (row & m) != 0
    if m == 1:
        return upper, jnp.where(upper, logf, 0.0)
    if m >= SUBLANES:
        parts = [jnp.broadcast_to(g_inc[b * 2 * m + m - 1:b * 2 * m + m, :], (2 * m, d))
                 for b in range(c // (2 * m))]
        gref = parts[0] if len(parts) == 1 else jnp.concatenate(parts, axis=0)
    else:
        g3 = g_inc.reshape(c // SUBLANES, SUBLANES, d)

        def bcast(j):
            return jnp.broadcast_to(g3[:, j:j + 1, :], g3.shape).reshape(c, d)

        if 2 * m == SUBLANES:
            gref = bcast(m - 1)
        else:
            assert 4 * m == SUBLANES
            gref = jnp.where((row & (SUBLANES - 1)) < 2 * m, bcast(m - 1), bcast(3 * m - 1))
    return upper, jnp.where(upper, g_inc - gref, gref - g_inc)


def _hgrn_kernel(q_ref, z_ref, v_ref, og_ref, lb_ref, ng_ref, tril_ref, o_ref, st):
    @pl.when(pl.program_id(1) == 0)
    def _():
        st[...] = jnp.zeros_like(st)

    for cc in range(HG_NC):
        for bb in range(HG_NB):
            _hgrn_chunk(bb, cc, q_ref, z_ref, v_ref, og_ref, lb_ref, ng_ref, tril_ref, o_ref, st)


def _hgrn_chunk(bb, cc, q_ref, z_ref, v_ref, og_ref, lb_ref, ng_ref, tril_ref, o_ref, st):
    c = HG_CHUNK
    dk = HG_DK
    levels = _hgrn_levels()
    rs = pl.ds(cc * c, c)

    z = z_ref[bb, rs, :]
    lb = lb_ref[...]
    sig = _sigmoid(z)
    f = lb + (1.0 - lb) * sig
    logf = jnp.log(f)
    kk = (1.0 - lb) * (1.0 - sig)
    q = q_ref[bb, rs, :] * (dk ** -0.5)
    v = v_ref[bb, rs, :]
    og = og_ref[bb, rs, :]

    hi = logf.astype(BF16)
    lo = (logf - hi.astype(F32)).astype(BF16)
    tril = tril_ref[...]
    g_inc = (jnp.dot(tril, hi, preferred_element_type=F32)
             + jnp.dot(tril, lo, preferred_element_type=F32))
    g_last = g_inc[c - 1:c, :]
    q_st = (q * jnp.exp(g_inc)).astype(BF16)
    k_st = (kk * jnp.exp(g_last - g_inc)).astype(BF16)
    dec_all = jnp.exp(g_last)
    v_bf = v.astype(BF16)
    q_bf = q.astype(BF16)
    k_bf = kk.astype(BF16)

    row = lax.broadcasted_iota(jnp.int32, (c, c), 0)
    col = lax.broadcasted_iota(jnp.int32, (c, c), 1)
    rr = lax.broadcasted_iota(jnp.int32, (c, 1), 0)

    qs, ks, masks = [], [], []
    for m in levels:
        upper, ex = _level_exponent(g_inc, logf, m, rr)
        e = jnp.exp(ex)
        qs.append(jnp.where(upper, q * e, 0.0).astype(BF16))
        ks.append(jnp.where(upper, 0.0, kk * e).astype(BF16))
        sh = int(np.log2(2 * m))
        masks.append((row >> sh) == (col >> sh))
    diag = row == col

    nt = (((1,), (1,)), ((), ()))
    tn = (((0,), (0,)), ((), ()))
    for h in range(HG_HEADS):
        sl = slice(h * dk, (h + 1) * dk)
        a = jnp.where(diag, lax.dot_general(q_bf[:, sl], k_bf[:, sl], nt,
                                            preferred_element_type=F32), 0.0)
        for li in range(len(levels)):
            a = a + jnp.where(masks[li],
                              lax.dot_general(qs[li][:, sl], ks[li][:, sl], nt,
                                              preferred_element_type=F32), 0.0)
        s_t = st[bb, h]
        o = jnp.dot(a.astype(BF16), v_bf[:, sl], preferred_element_type=F32)
        o = o + lax.dot_general(q_st[:, sl], s_t.astype(BF16), nt, preferred_element_type=F32)
        st[bb, h] = s_t * dec_all[:, sl] + lax.dot_general(v_bf[:, sl], k_st[:, sl], tn,
                                                           preferred_element_type=F32)
        ms = jnp.mean(o * o, axis=-1, keepdims=True)
        o = o * lax.rsqrt(ms + EPS) * ng_ref[...]
        g = og[:, sl]
        o_ref[bb, rs, sl] = (o * (g * _sigmoid(g))).astype(BF16)


def _hgrn_branch(p, lb, norm_g):
    b, s, _ = p.shape
    d = D_MODEL
    c = HG_CHUNK
    tril = jnp.asarray(np.tril(np.ones((c, c), np.float32)), dtype=BF16)

    assert b % HG_NB == 0 and s % (c * HG_NC) == 0
    rows = c * HG_NC

    def col_spec(col):
        return pl.BlockSpec((HG_NB, rows, d), lambda bi, i: (bi, i, col))

    return pl.pallas_call(
        _hgrn_kernel,
        grid=(b // HG_NB, s // rows),
        in_specs=[
            col_spec(P_Q), col_spec(P_F), col_spec(P_I), col_spec(P_G),
            pl.BlockSpec((1, d), lambda bi, i: (0, 0)),
            pl.BlockSpec((1, HG_DK), lambda bi, i: (0, 0)),
            pl.BlockSpec((c, c), lambda bi, i: (0, 0)),
        ],
        out_specs=pl.BlockSpec((HG_NB, rows, d), lambda bi, i: (bi, i, 0)),
        out_shape=jax.ShapeDtypeStruct((b, s, d), BF16),
        scratch_shapes=[pltpu.VMEM((HG_NB, HG_HEADS, HG_DK, HG_DK), F32)],
        compiler_params=_params(("arbitrary", "arbitrary")),
        name="hgrn_branch",
    )(p, p, p, p, lb, norm_g, tril)


ROW_TILE = (8, 128)


def _rows_to_tiles(rows):
    st = jnp.stack([rows[:, j * 128:(j + 1) * 128] for j in range(ROW_TILE[0])], axis=0)
    return pltpu.einshape("jrl->rjl", st)


def _tile_rows(first_row):
    return pl.ds(pl.multiple_of(first_row, ROW_TILE[0]), ROW_TILE[0])


def _tiles_to_rows(tiles):
    y = pltpu.einshape("rjl->jrl", tiles)
    return jnp.concatenate([y[j] for j in range(ROW_TILE[0])], axis=-1)


_NT = (((1,), (1,)), ((), ()))


def _merge_kernel(uc_ref, uh_ref, gc_ref, gh_ref, x_ref, ga_ref, sc_ref, sh_ref, g_ref,
                  wc_ref, wh_ref, w_ref, wrt_ref, br_ref, x1_ref, h3_ref, lg_ref):
    yc = jnp.dot(uc_ref[0], wc_ref[...], preferred_element_type=F32)
    yh = jnp.dot(uh_ref[0], wh_ref[...], preferred_element_type=F32)
    gate_c = _sigmoid(gc_ref[0].astype(F32))
    gate_h = _sigmoid(gh_ref[0].astype(F32))
    merged = (gate_c * yc + gate_h * yh).astype(BF16)
    x1 = x_ref[0] + ga_ref[0] * jnp.dot(merged, w_ref[...], preferred_element_type=F32)
    x1_ref[0] = x1
    ms = jnp.mean(x1 * x1, axis=-1, keepdims=True)
    h2 = x1 * lax.rsqrt(ms + EPS) * g_ref[...]
    h2 = h2 * (1.0 + sc_ref[0]) + sh_ref[0]
    h3_ref[...] = _rows_to_tiles(h2)
    wrt = wrt_ref[...]
    wrt_hi = wrt.astype(BF16)
    wrt_lo = (wrt - wrt_hi.astype(F32)).astype(BF16)
    h2_hi = h2.astype(BF16)
    h2_lo = (h2 - h2_hi.astype(F32)).astype(BF16)
    lg = (lax.dot_general(wrt_hi, h2_hi, _NT, preferred_element_type=F32)
          + lax.dot_general(wrt_hi, h2_lo, _NT, preferred_element_type=F32)
          + lax.dot_general(wrt_lo, h2_hi, _NT, preferred_element_type=F32))
    lg_ref[...] = lg + br_ref[...]


def _merge(uc, uh, pg, x, mod3, g_ffn, w_conv_bf, w_hgrn_bf, w_out_bf, w_router_t, b_router_col):
    b, s, d = x.shape
    tm = 512
    nt = s // tm
    ne = w_router_t.shape[0]

    def mod_spec(k):
        return pl.BlockSpec((1, 1, d), lambda bi, i: (bi * N_MOD + k, 0, 0))

    def col_spec(col):
        return pl.BlockSpec((1, tm, d), lambda bi, i: (bi, i, col))

    tile = pl.BlockSpec((1, tm, d), lambda bi, i: (bi, i, 0))
    weight = pl.BlockSpec((d, d), lambda bi, i: (0, 0))
    return pl.pallas_call(
        _merge_kernel,
        grid=(b, nt),
        in_specs=[
            tile, tile, col_spec(PG_C), col_spec(PG_H), tile,
            mod_spec(2), mod_spec(4), mod_spec(3),
            pl.BlockSpec((1, d), lambda bi, i: (0, 0)),
            weight, weight, weight,
            pl.BlockSpec((ne, d), lambda bi, i: (0, 0)),
            pl.BlockSpec((ne, 1), lambda bi, i: (0, 0)),
        ],
        out_specs=[
            tile,
            pl.BlockSpec((tm,) + ROW_TILE, lambda bi, i: (bi * nt + i, 0, 0)),
            pl.BlockSpec((ne, tm), lambda bi, i: (0, bi * nt + i)),
        ],
        out_shape=[
            jax.ShapeDtypeStruct((b, s, d), F32),
            jax.ShapeDtypeStruct((b * s,) + ROW_TILE, F32),
            jax.ShapeDtypeStruct((ne, b * s), F32),
        ],
        compiler_params=_params(("arbitrary", "arbitrary")),
        name="merge_router",
    )(uc, uh, pg, pg, x, mod3, mod3, mod3, g_ffn, w_conv_bf, w_hgrn_bf, w_out_bf, w_router_t,
      b_router_col)


MOE_TL = 1024
MOE_GRP = 16
MOE_CAP = MOE_TL * TOP_K + N_EXPERTS * MOE_GRP
MOE_GPB = MOE_BLOCK // MOE_GRP
RELAY_ROWS = 256
assert MOE_CAP % RELAY_ROWS == 0 and MOE_TL % RELAY_ROWS == 0


def _route_kernel(lg_ref, u_ref, ld_ref, w_ref, cnt_ref):
    tl = MOE_TL
    l = lg_ref[...]
    eio = lax.broadcasted_iota(jnp.int32, l.shape, 0)
    vals, hots = [], []
    for _ in range(TOP_K):
        m = jnp.max(l, axis=0, keepdims=True)
        idx = jnp.min(jnp.where(l == m, eio, N_EXPERTS), axis=0, keepdims=True)
        hot = eio == idx
        vals.append(m)
        hots.append(hot)
        l = jnp.where(hot, -jnp.inf, l)
    ex = [jnp.exp(v - vals[0]) for v in vals]
    den = ex[0] + ex[1] + ex[2] + ex[3]
    cnt = hots[0].astype(F32)
    for k in range(1, TOP_K):
        cnt = cnt + hots[k].astype(F32)
    prefix = jnp.dot(cnt.astype(BF16), u_ref[...], preferred_element_type=F32)
    n_e = jnp.sum(cnt, axis=1, keepdims=True)
    pad_e = jnp.floor((n_e + (MOE_GRP - 1)) * (1.0 / MOE_GRP)) * MOE_GRP
    scan = jnp.broadcast_to(pad_e, (N_EXPERTS, ROW_TILE[1]))
    ei = lax.broadcasted_iota(jnp.int32, scan.shape, 0)
    dist = 1
    while dist < N_EXPERTS:
        scan = scan + jnp.where(ei >= dist, pltpu.roll(scan, dist, axis=0), 0.0)
        dist *= 2
    base = prefix + (scan[:, 0:1] - pad_e)
    for k in range(TOP_K):
        dest = jnp.sum(jnp.where(hots[k], base, 0.0), axis=0, keepdims=True)
        ld_ref[0, :, k * tl:(k + 1) * tl] = (dest * float(ROW_TILE[0])).astype(jnp.int32)
        w_ref[0, :, k * tl:(k + 1) * tl] = ex[k] / den
    cnt_ref[0] = jnp.broadcast_to(n_e, (N_EXPERTS, ROW_TILE[1])).astype(jnp.int32)


def _route(logits_t):
    ne, t = logits_t.shape
    tl = MOE_TL
    n_tiles = t // tl
    upper = jnp.asarray(np.triu(np.ones((tl, tl), np.float32), k=1), dtype=BF16)
    slot = pl.BlockSpec((1, 1, TOP_K * tl), lambda i: (i, 0, 0))
    return pl.pallas_call(
        _route_kernel,
        grid=(n_tiles,),
        in_specs=[pl.BlockSpec((ne, tl), lambda i: (0, i)),
                  pl.BlockSpec((tl, tl), lambda i: (0, 0))],
        out_specs=[slot, slot, pl.BlockSpec((1, ne, ROW_TILE[1]), lambda i: (i, 0, 0))],
        out_shape=[
            jax.ShapeDtypeStruct((n_tiles, 1, TOP_K * tl), jnp.int32),
            jax.ShapeDtypeStruct((n_tiles, 1, TOP_K * tl), F32),
            jax.ShapeDtypeStruct((n_tiles, ne, ROW_TILE[1]), jnp.int32),
        ],
        compiler_params=_params(("arbitrary",)),
        name="route",
    )(logits_t, upper)


def _gather_kernel(ld_ref, h3_ref, xs_ref, xs3):
    tl = MOE_TL
    last = pl.num_programs(0) - 1

    @pl.when((pl.program_id(0) == 0) | (pl.program_id(0) == last))
    def _():
        xs3[...] = jnp.zeros_like(xs3)

    def push(t, carry):
        tile = h3_ref[t]
        for k in range(TOP_K):
            xs3[_tile_rows(ld_ref[0, 0, k * tl + t]), :] = tile
        return carry

    @pl.when(pl.program_id(0) < last)
    def _():
        lax.fori_loop(0, tl, push, 0, unroll=16)

    def relay(r, carry):
        r0 = pl.multiple_of(r * RELAY_ROWS, RELAY_ROWS)
        tiles = xs3[pl.ds(r0 * ROW_TILE[0], RELAY_ROWS * ROW_TILE[0]), :]
        tiles = tiles.reshape((RELAY_ROWS,) + ROW_TILE)
        xs_ref[pl.ds(r0, RELAY_ROWS), :] = _tiles_to_rows(tiles).astype(BF16)
        return carry

    lax.fori_loop(0, MOE_CAP // RELAY_ROWS, relay, 0)


def _gather(ld, h3):
    t = h3.shape[0]
    tl = MOE_TL
    n_tiles = t // tl
    last = n_tiles - 1
    return pl.pallas_call(
        _gather_kernel,
        grid=(n_tiles + 1,),
        in_specs=[
            pl.BlockSpec((1, 1, TOP_K * tl), lambda i: (jnp.minimum(i, last), 0, 0),
                         memory_space=pltpu.SMEM),
            pl.BlockSpec((tl,) + ROW_TILE, lambda i: (jnp.minimum(i, last), 0, 0)),
        ],
        out_specs=pl.BlockSpec((MOE_CAP, D_MODEL), lambda i: (i, 0)),
        out_shape=jax.ShapeDtypeStruct(((n_tiles + 1) * MOE_CAP, D_MODEL), BF16),
        scratch_shapes=[pltpu.VMEM((MOE_CAP * ROW_TILE[0], ROW_TILE[1]), F32)],
        compiler_params=_params(("arbitrary",)),
        name="moe_gather",
    )(ld, h3)


def _table_kernel(cnt_ref, be_ref, src_ref, toff):
    n_tiles = cnt_ref.shape[0]
    n_slots = src_ref.shape[0]
    groups_per_tile = MOE_CAP // MOE_GRP
    step = 16

    for i in range(n_tiles):
        toff[i] = 0

    def per_expert(e, pos):
        def per_tile(i, p):
            g = (cnt_ref[i, e] + (MOE_GRP - 1)) // MOE_GRP
            base = i * groups_per_tile + toff[i]

            def put(j, c):
                for u in range(step):
                    src_ref[p + j * step + u] = base + j * step + u
                return c

            put(0, 0)
            lax.fori_loop(1, (g + (step - 1)) // step, put, 0)
            toff[i] = toff[i] + g
            return p + g

        end = lax.fori_loop(0, n_tiles, per_tile, pos)
        new_pos = (end + (MOE_GPB - 1)) // MOE_GPB * MOE_GPB
        for u in range(MOE_GPB):
            src_ref[end + u] = -1

        def put_e(b, c):
            be_ref[b] = e
            return c

        lax.fori_loop(pos // MOE_GPB, new_pos // MOE_GPB, put_e, 0)
        return new_pos

    used = lax.fori_loop(0, N_EXPERTS, per_expert, 0)

    def tail(b, c):
        be_ref[b] = N_EXPERTS - 1
        for u in range(MOE_GPB):
            src_ref[b * MOE_GPB + u] = -1
        return c

    lax.fori_loop(used // MOE_GPB, n_slots // MOE_GPB, tail, 0)


def _block_table(cnt, n_blocks):
    smem = pl.BlockSpec(memory_space=pltpu.SMEM)
    return pl.pallas_call(
        _table_kernel,
        in_specs=[smem],
        out_specs=[smem, smem],
        out_shape=[jax.ShapeDtypeStruct((n_blocks,), jnp.int32),
                   jax.ShapeDtypeStruct((n_blocks * MOE_GPB,), jnp.int32)],
        scratch_shapes=[pltpu.SMEM((cnt.shape[0],), jnp.int32)],
        name="moe_block_table",
    )(cnt)


def _ffn_kernel(be_ref, src_ref, xs_hbm, w1_ref, b1_ref, w2_ref, b2_ref, ys_hbm,
                xbuf, ybuf, w1s, w2s, sem_in, sem_out, *, spare_grp):
    i = pl.program_id(0)
    nb = pl.num_programs(0)
    slot = i % 2

    def live(blk):
        return src_ref[blk * MOE_GPB] >= 0

    def group_rows(grp):
        return pl.ds(pl.multiple_of(grp * MOE_GRP, MOE_GRP), MOE_GRP)

    def start_in(blk, sl):
        for g in range(MOE_GPB):
            grp = src_ref[blk * MOE_GPB + g]
            grp = jnp.where(grp >= 0, grp, spare_grp)
            pltpu.make_async_copy(xs_hbm.at[group_rows(grp), :],
                                  xbuf.at[sl, pl.ds(g * MOE_GRP, MOE_GRP), :], sem_in.at[sl]).start()

    def start_out(blk, sl):
        for g in range(MOE_GPB):
            grp = src_ref[blk * MOE_GPB + g]
            grp = jnp.where(grp >= 0, grp, spare_grp + 1 + sl * MOE_GPB + g)
            pltpu.make_async_copy(ybuf.at[sl, pl.ds(g * MOE_GRP, MOE_GRP), :],
                                  ys_hbm.at[group_rows(grp), :], sem_out.at[sl]).start()

    def wait_in(sl):
        pltpu.make_async_copy(xs_hbm.at[pl.ds(0, MOE_BLOCK), :], xbuf.at[sl], sem_in.at[sl]).wait()

    def wait_out(sl):
        pltpu.make_async_copy(ybuf.at[sl], ys_hbm.at[pl.ds(0, MOE_BLOCK), :], sem_out.at[sl]).wait()

    @pl.when((i == 0) & live(0))
    def _():
        start_in(0, 0)

    nxt = jnp.minimum(i + 1, nb - 1)

    @pl.when((i + 1 < nb) & live(nxt))
    def _():
        start_in(nxt, 1 - slot)

    @pl.when((i >= 2) & live(jnp.maximum(i - 2, 0)))
    def _():
        wait_out(slot)

    prev = be_ref[jnp.maximum(i - 1, 0)]

    @pl.when((i == 0) | (be_ref[i] != prev))
    def _():
        w1s[...] = w1_ref[0].astype(BF16)
        w2s[...] = w2_ref[0].astype(BF16)

    @pl.when(live(i))
    def _():
        wait_in(slot)
        u = jnp.dot(xbuf[slot], w1s[...], preferred_element_type=F32) + b1_ref[0]
        gl = jnp.minimum(u[:, :D_FF], SWIGLU_LIMIT)
        lin = jnp.clip(u[:, D_FF:], -SWIGLU_LIMIT, SWIGLU_LIMIT)
        act = gl * _sigmoid(SWIGLU_ALPHA * gl) * (lin + 1.0)
        y = jnp.dot(act.astype(BF16), w2s[...], preferred_element_type=F32) + b2_ref[0]
        ybuf[slot] = y.astype(BF16)
        start_out(i, slot)

    @pl.when(i == nb - 1)
    def _():
        @pl.when(live(i))
        def _():
            wait_out(slot)

        @pl.when((i >= 1) & live(jnp.maximum(i - 1, 0)))
        def _():
            wait_out(1 - slot)


def _ffn(block_e, src, xs, w1, b1, w2, b2):
    r, d = xs.shape
    nb = block_e.shape[0]
    spare_grp = (r - MOE_CAP) // MOE_GRP
    assert 1 + 2 * MOE_GPB <= MOE_CAP // MOE_GRP
    grid_spec = pltpu.PrefetchScalarGridSpec(
        num_scalar_prefetch=2,
        grid=(nb,),
        in_specs=[
            pl.BlockSpec(memory_space=pl.ANY),
            pl.BlockSpec((1, d, 2 * D_FF), lambda i, be, sr: (be[i], 0, 0)),
            pl.BlockSpec((1, 1, 2 * D_FF), lambda i, be, sr: (be[i], 0, 0)),
            pl.BlockSpec((1, D_FF, d), lambda i, be, sr: (be[i], 0, 0)),
            pl.BlockSpec((1, 1, d), lambda i, be, sr: (be[i], 0, 0)),
        ],
        out_specs=pl.BlockSpec(memory_space=pl.ANY),
        scratch_shapes=[
            pltpu.VMEM((2, MOE_BLOCK, d), BF16), pltpu.VMEM((2, MOE_BLOCK, d), BF16),
            pltpu.VMEM((d, 2 * D_FF), BF16), pltpu.VMEM((D_FF, d), BF16),
            pltpu.SemaphoreType.DMA((2,)), pltpu.SemaphoreType.DMA((2,)),
        ],
    )
    return pl.pallas_call(
        functools.partial(_ffn_kernel, spare_grp=spare_grp),
        grid_spec=grid_spec,
        out_shape=jax.ShapeDtypeStruct((r, d), BF16),
        input_output_aliases={2: 0},
        compiler_params=_params(("arbitrary",)),
        name="expert_ffn",
    )(block_e, src, xs, w1, b1, w2, b2)


def _combine_kernel(ld_ref, w_ref, ys_ref, x1_ref, ga_ref, g_ref, o_ref, y3, o3, wb):
    tl = MOE_TL
    lanes = ROW_TILE[1]

    w4 = jnp.concatenate([w_ref[0, :, k * tl:(k + 1) * tl] for k in range(TOP_K)], axis=0)
    wt = jnp.concatenate([w4, jnp.zeros((lanes - TOP_K, tl), F32)], axis=0).T
    for k in range(TOP_K):
        wb[k] = jnp.broadcast_to(wt[:, k:k + 1], (tl, lanes))

    def relay(r, carry):
        r0 = pl.multiple_of(r * RELAY_ROWS, RELAY_ROWS)
        tiles = _rows_to_tiles(ys_ref[pl.ds(r0, RELAY_ROWS), :].astype(F32))
        y3[pl.ds(r0 * ROW_TILE[0], RELAY_ROWS * ROW_TILE[0]), :] = tiles.reshape(
            RELAY_ROWS * ROW_TILE[0], ROW_TILE[1])
        return carry

    lax.fori_loop(0, MOE_CAP // RELAY_ROWS, relay, 0)

    def pull(t, carry):
        acc = None
        for k in range(TOP_K):
            wv = jnp.broadcast_to(wb[k, pl.ds(t, 1), :], ROW_TILE)
            term = wv * y3[_tile_rows(ld_ref[0, 0, k * tl + t]), :]
            acc = term if acc is None else acc + term
        o3[t] = acc
        return carry

    lax.fori_loop(0, tl, pull, 0, unroll=32)

    def finish(r, carry):
        r0 = pl.multiple_of(r * RELAY_ROWS, RELAY_ROWS)
        x2 = x1_ref[pl.ds(r0, RELAY_ROWS), :] + ga_ref[0] * _tiles_to_rows(o3[pl.ds(r0, RELAY_ROWS)])
        ms = jnp.mean(x2 * x2, axis=-1, keepdims=True)
        o_ref[pl.ds(r0, RELAY_ROWS), :] = x2 * lax.rsqrt(ms + EPS) * g_ref[...]
        return carry

    lax.fori_loop(0, tl // RELAY_ROWS, finish, 0)


def _combine(ld, wts, ys, x1f, mod3, g_final, tiles_per_batch):
    t, d = x1f.shape
    tl = MOE_TL
    n_tiles = t // tl
    smem = functools.partial(pl.BlockSpec, (1, 1, TOP_K * tl), lambda i: (i, 0, 0),
                             memory_space=pltpu.SMEM)
    return pl.pallas_call(
        _combine_kernel,
        grid=(n_tiles,),
        in_specs=[
            smem(),
            pl.BlockSpec((1, 1, TOP_K * tl), lambda i: (i, 0, 0)),
            pl.BlockSpec((MOE_CAP, d), lambda i: (i, 0)),
            pl.BlockSpec((tl, d), lambda i: (i, 0)),
            pl.BlockSpec((1, 1, d), lambda i: ((i // tiles_per_batch) * N_MOD + 5, 0, 0)),
            pl.BlockSpec((1, d), lambda i: (0, 0)),
        ],
        out_specs=pl.BlockSpec((tl, d), lambda i: (i, 0)),
        out_shape=jax.ShapeDtypeStruct((t, d), F32),
        scratch_shapes=[pltpu.VMEM((MOE_CAP * ROW_TILE[0], ROW_TILE[1]), F32),
                        pltpu.VMEM((tl,) + ROW_TILE, F32),
                        pltpu.VMEM((TOP_K, tl, ROW_TILE[1]), F32)],
        compiler_params=_params(("arbitrary",), COMBINE_VMEM_LIMIT),
        name="moe_combine",
    )(ld, wts, ys, x1f, mod3, g_final)


def kernel(x, c, w_ada, b_ada, g_mix, w_in, conv_dw, conv_dw_bias, conv_ln_g, conv_ln_b,
           w_conv_out, lb_param, hgrn_norm_g, w_hgrn_out, w_out, g_ffn, w_router, b_router,
           w1, b1, w2, b2, g_final):
    b, s, d = x.shape
    assert w_ada.shape[0] == 1, "single-layer block"
    assert s % MOE_TL == 0
    t = b * s
    n_tiles = t // MOE_TL

    c_pad = jnp.zeros((8, d), F32).at[:b].set(c.astype(F32))
    mod, lb = _ada(c_pad, w_ada[0], b_ada, lb_param)
    mod3 = mod[:b].reshape(b * N_MOD, 1, d)

    dw_rows = jnp.broadcast_to(conv_dw[0][:, None, :], (CONV_K, PACK * SUBLANES, d))
    p, pg, uc = _inproj_conv(x, g_mix, mod3, w_in[0].astype(BF16), dw_rows, conv_dw_bias,
                             conv_ln_g, conv_ln_b)
    uh = _hgrn_branch(p, lb, hgrn_norm_g)

    x1, h3, logits_t = _merge(uc, uh, pg, x, mod3, g_ffn, w_conv_out[0].astype(BF16),
                              w_hgrn_out[0].astype(BF16), w_out[0].astype(BF16),
                              w_router[0].T, b_router[0][:, None])

    ld, wts, cnt = _route(logits_t)
    xs = _gather(ld, h3)
    groups_max = t * TOP_K // MOE_GRP + n_tiles * N_EXPERTS
    n_blocks = groups_max // MOE_GPB + N_EXPERTS
    block_e, src = _block_table(cnt[:, :, 0], n_blocks)
    ys = _ffn(block_e, src, xs, w1[0], b1[0][:, None, :], w2[0], b2[0][:, None, :])
    out = _combine(ld, wts, ys, x1.reshape(t, d), mod3, g_final.reshape(1, d), s // MOE_TL)
    return out.reshape(b, s, d)
```
